```python
import math
import jax, jax.numpy as jnp
from jax import lax
import numpy as np

D_MODEL = 1024
BATCH = 8
SEQ = 2048
DEPTH = 1
DEC_BATCH = 128
DEC_SEQ = 1
PAST_LEN = 16384
PAGE_SIZE = 128

POOL_WIDTH = D_MODEL // 2
POOL_GROUPS = 4
POOL_GROUP_DIM = POOL_WIDTH // POOL_GROUPS
POOL_WINDOWS = (2, 4, 8, 16)
POOL_HIST = max(POOL_WINDOWS) - 1
MLSTM_WIDTH = D_MODEL - POOL_WIDTH
MLSTM_HEADS = 4
HEAD_DIM = MLSTM_WIDTH // MLSTM_HEADS
CHUNK = 128
MIX_WIDTH = POOL_WIDTH + MLSTM_WIDTH
IN_COLS = POOL_WIDTH + 4 * MLSTM_WIDTH + 2 * MLSTM_HEADS
N_EXPERTS = 32
TOP_K = 4
D_FF = D_MODEL
SWIGLU_ALPHA = 1.702
SWIGLU_LIMIT = 7.0
MOE_BLOCK = 128
PLE_DIM = 256
DN_ALPHA = (2 * DEPTH) ** 0.25
DN_BETA = (8 * DEPTH) ** -0.25
LN_EPS = 1e-5
F32 = jnp.float32

kernel_name = 'hybrid_pool_mlstm_moe_deepnorm_step'


def layer_norm(x, g, b):
    xf = x.astype(F32)
    mu = jnp.mean(xf, -1, keepdims=True)
    var = jnp.mean(jnp.square(xf - mu), -1, keepdims=True)
    return ((xf - mu) * lax.rsqrt(var + LN_EPS) * g.astype(F32) + b.astype(F32)).astype(x.dtype)


def head_norm(h, g):
    mu = jnp.mean(h, -1, keepdims=True)
    var = jnp.mean(jnp.square(h - mu), -1, keepdims=True)
    return (h - mu) * lax.rsqrt(var + LN_EPS) * g.reshape(MLSTM_HEADS, HEAD_DIM).astype(F32)


def pool_mix(u, hist, pos0, w_pool, pool_scale):
    B, T, _ = u.shape
    ext = jnp.concatenate([hist.astype(u.dtype), u], axis=1)
    cs = jnp.cumsum(ext.astype(F32), axis=1)
    cs = jnp.pad(cs, ((0, 0), (1, 0), (0, 0)))
    cs = cs.reshape(B, POOL_HIST + 1 + T, POOL_GROUPS, POOL_GROUP_DIM)
    pos = pos0 + jnp.arange(T, dtype=jnp.int32)
    base = POOL_HIST + 1
    means = []
    for g, w in enumerate(POOL_WINDOWS):
        s = cs[:, base:, g] - cs[:, base - w:base - w + T, g]
        cnt = jnp.minimum(pos + 1, w).astype(F32)
        means.append(s / cnt[None, :, None])
    mean = jnp.stack(means, axis=2)
    z = mean - u.reshape(B, T, POOL_GROUPS, POOL_GROUP_DIM).astype(F32)
    out = jnp.einsum('btgc,gcd->btgd', z, w_pool.astype(F32)).reshape(B, T, POOL_WIDTH)
    out = out * pool_scale.astype(F32)
    return out.astype(u.dtype), ext[:, -POOL_HIST:]


def mlstm_chunk(carry, inp):
    C, n, m = carry
    q, k, v, ig, lf = inp
    L = q.shape[2]
    F = jnp.cumsum(lf, axis=-1)
    causal = jnp.tril(jnp.ones((L, L), dtype=bool))
    logD = F[..., :, None] - F[..., None, :] + ig[..., None, :]
    logD = jnp.where(causal, logD, -jnp.inf)
    inter = m[..., None] + F
    m_t = jnp.maximum(inter, jnp.max(logD, axis=-1))
    Dw = jnp.exp(logD - m_t[..., None])
    sc = jnp.exp(inter - m_t)
    qk = jnp.einsum('bhtk,bhsk->bhts', q, k) * Dw
    num = jnp.einsum('bhts,bhsv->bhtv', qk, v) + sc[..., None] * jnp.einsum('bhvk,bhtk->bhtv', C, q)
    den = jnp.sum(qk, axis=-1) + sc * jnp.einsum('bhk,bhtk->bht', n, q)
    h = num / jnp.maximum(jnp.abs(den), jnp.exp(-m_t))[..., None]
    m_new = m_t[..., -1]
    wk = jnp.exp(ig + F[..., -1:] - F - m_new[..., None])
    decay = jnp.exp(m + F[..., -1] - m_new)
    C_new = decay[..., None, None] * C + jnp.einsum('bhsv,bhsk->bhvk', v * wk[..., None], k)
    n_new = decay[..., None] * n + jnp.einsum('bhs,bhsk->bhk', wk, k)
    return (C_new, n_new, m_new), h


def mlstm(q, k, v, ig, lf, C0, n0, m0):
    B, T, H, dh = q.shape
    L = min(CHUNK, T)
    nc = T // L
    k = k * (dh ** -0.5)

    def to_chunks(a):
        a = a.astype(F32).reshape((B, nc, L) + a.shape[2:])
        return jnp.swapaxes(jnp.moveaxis(a, 1, 0), 2, 3)

    xs = (to_chunks(q), to_chunks(k), to_chunks(v), to_chunks(ig), to_chunks(lf))
    carry0 = (C0.astype(F32), n0.astype(F32), m0.astype(F32))
    (C, n, m), hs = lax.scan(mlstm_chunk, carry0, xs)
    h = jnp.transpose(hs, (1, 0, 3, 2, 4)).reshape(B, T, H, dh)
    return h, C, n, m


def moe(xf, w_router, b_router, w1, b1, w2, b2):
    N, D = xf.shape
    logits = xf.astype(F32) @ w_router.astype(F32) + b_router.astype(F32)
    top_v, top_e = lax.top_k(logits, TOP_K)
    gates = jax.nn.softmax(top_v, axis=-1)
    A = N * TOP_K
    e_flat = top_e.reshape(A).astype(jnp.int32)
    g_flat = gates.reshape(A)
    tok = jnp.arange(A, dtype=jnp.int32) // TOP_K
    order = jnp.argsort(e_flat)
    e_sorted = e_flat[order]
    counts = jnp.bincount(e_flat, length=N_EXPERTS)
    padded = (counts + MOE_BLOCK - 1) // MOE_BLOCK * MOE_BLOCK
    start = jnp.cumsum(counts) - counts
    ends_p = jnp.cumsum(padded)
    pstart = ends_p - padded
    dest = pstart[e_sorted] + jnp.arange(A, dtype=jnp.int32) - start[e_sorted]
    n_blocks = (A + MOE_BLOCK - 1) // MOE_BLOCK + N_EXPERTS
    P = n_blocks * MOE_BLOCK
    slot_tok = jnp.zeros((P,), jnp.int32).at[dest].set(tok[order])
    slot_gate = jnp.zeros((P,), F32).at[dest].set(g_flat[order])
    block_start = jnp.arange(n_blocks, dtype=jnp.int32) * MOE_BLOCK
    block_exp = jnp.minimum(jnp.searchsorted(ends_p, block_start, side='right'), N_EXPERTS - 1)

    def run_block(args):
        e, toks, g = args
        xb = xf[toks]
        h = xb @ w1[e] + b1[e]
        glu = jnp.minimum(h[:, ::2], SWIGLU_LIMIT)
        lin = jnp.clip(h[:, 1::2], -SWIGLU_LIMIT, SWIGLU_LIMIT)
        a = glu * jax.nn.sigmoid(SWIGLU_ALPHA * glu) * (lin + 1)
        out = a @ w2[e] + b2[e]
        return out * g[:, None].astype(out.dtype)

    outs = lax.map(run_block, (block_exp, slot_tok.reshape(n_blocks, MOE_BLOCK),
                               slot_gate.reshape(n_blocks, MOE_BLOCK)))
    return jnp.zeros_like(xf).at[slot_tok].add(outs.reshape(P, D).astype(xf.dtype))


def decoder_layer(x, p, hist, C0, n0, m0, pos0, w_in, b_i, b_f, w_pool, pool_scale, mh_g, w_out,
                  ln1_g, ln1_b, w_router, b_router, w_mlp1, b_mlp1, w_mlp2, b_mlp2,
                  ln2_g, ln2_b, w_ple, w_ple_gate):
    B, T, D = x.shape
    proj = x @ w_in
    M = MLSTM_WIDTH
    cuts = [POOL_WIDTH, POOL_WIDTH + M, POOL_WIDTH + 2 * M, POOL_WIDTH + 3 * M,
            POOL_WIDTH + 4 * M, POOL_WIDTH + 4 * M + MLSTM_HEADS]
    u, q, k, v, o, gi, gf = jnp.split(proj, cuts, axis=-1)
    pool_out, hist_new = pool_mix(u, hist, pos0, w_pool, pool_scale)
    hs = (B, T, MLSTM_HEADS, HEAD_DIM)
    ig = gi.astype(F32) + b_i.astype(F32)
    lf = jax.nn.log_sigmoid(gf.astype(F32) + b_f.astype(F32))
    h, C, n, m = mlstm(q.reshape(hs), k.reshape(hs), v.reshape(hs), ig, lf, C0, n0, m0)
    h = jax.nn.sigmoid(o.astype(F32)) * head_norm(h, mh_g).reshape(B, T, M)
    mix = jnp.concatenate([pool_out, h.astype(x.dtype)], axis=-1) @ w_out
    x1 = layer_norm(DN_ALPHA * x + mix, ln1_g, ln1_b)
    ff = moe(x1.reshape(B * T, D), w_router, b_router, w_mlp1, b_mlp1, w_mlp2, b_mlp2).reshape(B, T, D)
    x2 = layer_norm(DN_ALPHA * x1 + ff, ln2_g, ln2_b)
    y = x2 + jax.nn.sigmoid(x2 @ w_ple_gate) * (p.astype(x2.dtype) @ w_ple)
    return y, hist_new, C, n, m


def setup_inputs(seed: int = 0) -> dict:
    key = jax.random.key(seed)
    ks = jax.random.split(key, 27)

    def nrm(k, shape, s):
        return s * jax.random.normal(k, shape, F32)

    H, dh = MLSTM_HEADS, HEAD_DIM
    return {
        'x_prompt': nrm(ks[0], (BATCH, SEQ, D_MODEL), 1.0),
        'x_sample': nrm(ks[1], (DEC_BATCH, DEC_SEQ, D_MODEL), 1.0),
        'state_pool': nrm(ks[2], (DEPTH, DEC_BATCH, POOL_HIST, POOL_WIDTH), 1.0),
        'state_mlstm_C': nrm(ks[3], (DEPTH, DEC_BATCH, H, dh, dh), 0.1),
        'state_mlstm_n': nrm(ks[4], (DEPTH, DEC_BATCH, H, dh), 0.1),
        'state_mlstm_m': nrm(ks[5], (DEPTH, DEC_BATCH, H), 0.5),
        'p_prompt': nrm(ks[6], (DEPTH, BATCH, SEQ, PLE_DIM), 1.0),
        'p_sample': nrm(ks[7], (DEPTH, DEC_BATCH, DEC_SEQ, PLE_DIM), 1.0),
        'w_in': nrm(ks[8], (DEPTH, D_MODEL, IN_COLS), D_MODEL ** -0.5),
        'b_i': nrm(ks[9], (DEPTH, H), 0.1),
        'b_f': jax.random.uniform(ks[10], (DEPTH, H), F32, 3.0, 6.0),
        'w_pool': nrm(ks[11], (DEPTH, POOL_GROUPS, POOL_GROUP_DIM, POOL_GROUP_DIM), POOL_GROUP_DIM ** -0.5),
        'pool_scale': 1.0 + nrm(ks[12], (DEPTH, POOL_WIDTH), 0.1),
        'mh_g': 1.0 + nrm(ks[13], (DEPTH, MLSTM_WIDTH), 0.1),
        'w_out': nrm(ks[14], (DEPTH, MIX_WIDTH, D_MODEL), DN_BETA * MIX_WIDTH ** -0.5),
        'ln1_g': 1.0 + nrm(ks[15], (DEPTH, D_MODEL), 0.1),
        'ln1_b': nrm(ks[16], (DEPTH, D_MODEL), 0.02),
        'w_router': nrm(ks[17], (DEPTH, D_MODEL, N_EXPERTS), D_MODEL ** -0.5),
        'b_router': nrm(ks[18], (DEPTH, N_EXPERTS), 0.01),
        'w_mlp1': nrm(ks[19], (DEPTH, N_EXPERTS, D_MODEL, 2 * D_FF), D_MODEL ** -0.5),
        'b_mlp1': nrm(ks[20], (DEPTH, N_EXPERTS, 2 * D_FF), 0.02),
        'w_mlp2': nrm(ks[21], (DEPTH, N_EXPERTS, D_FF, D_MODEL), DN_BETA * D_FF ** -0.5),
        'b_mlp2': nrm(ks[22], (DEPTH, N_EXPERTS, D_MODEL), 0.02),
        'ln2_g': 1.0 + nrm(ks[23], (DEPTH, D_MODEL), 0.1),
        'ln2_b': nrm(ks[24], (DEPTH, D_MODEL), 0.02),
        'w_ple': nrm(ks[25], (DEPTH, PLE_DIM, D_MODEL), DN_BETA * PLE_DIM ** -0.5),
        'w_ple_gate': nrm(ks[26], (DEPTH, D_MODEL, D_MODEL), D_MODEL ** -0.5),
    }


def reference(x_prompt, x_sample, state_pool, state_mlstm_C, state_mlstm_n, state_mlstm_m,
              p_prompt, p_sample, w_in, b_i, b_f, w_pool, pool_scale, mh_g, w_out,
              ln1_g, ln1_b, w_router, b_router, w_mlp1, b_mlp1, w_mlp2, b_mlp2,
              ln2_g, ln2_b, w_ple, w_ple_gate):
    yp, ys = x_prompt, x_sample
    pool_p, C_p, n_p, m_p = [], [], [], []
    pool_s, C_s, n_s, m_s = [], [], [], []
    for i in range(DEPTH):
        wts = (w_in[i], b_i[i], b_f[i], w_pool[i], pool_scale[i], mh_g[i], w_out[i],
               ln1_g[i], ln1_b[i], w_router[i], b_router[i], w_mlp1[i], b_mlp1[i],
               w_mlp2[i], b_mlp2[i], ln2_g[i], ln2_b[i], w_ple[i], w_ple_gate[i])
        hist0 = jnp.zeros((BATCH, POOL_HIST, POOL_WIDTH), yp.dtype)
        C0 = jnp.zeros((BATCH, MLSTM_HEADS, HEAD_DIM, HEAD_DIM), F32)
        n0 = jnp.zeros((BATCH, MLSTM_HEADS, HEAD_DIM), F32)
        m0 = jnp.zeros((BATCH, MLSTM_HEADS), F32)
        yp, hp, Cp, np_, mp = decoder_layer(yp, p_prompt[i], hist0, C0, n0, m0, 0, *wts)
        ys, hs, Cs, ns, ms = decoder_layer(ys, p_sample[i], state_pool[i], state_mlstm_C[i],
                                           state_mlstm_n[i], state_mlstm_m[i], PAST_LEN, *wts)
        pool_p.append(hp); C_p.append(Cp); n_p.append(np_); m_p.append(mp)
        pool_s.append(hs); C_s.append(Cs); n_s.append(ns); m_s.append(ms)
    new_pool_prompt = jnp.stack(pool_p)
    new_C_prompt = jnp.stack(C_p)
    new_n_prompt = jnp.stack(n_p)
    new_m_prompt = jnp.stack(m_p)
    new_pool_sample = jnp.stack(pool_s)
    new_C_sample = jnp.stack(C_s)
    new_n_sample = jnp.stack(n_s)
    new_m_sample = jnp.stack(m_s)
    return (yp, ys, new_pool_prompt, new_C_prompt, new_n_prompt, new_m_prompt,
            new_pool_sample, new_C_sample, new_n_sample, new_m_sample)
```

```python
import jax
import jax.numpy as jnp
from jax import lax
from jax.experimental import pallas as pl
from jax.experimental.pallas import tpu as pltpu

F32 = jnp.float32
BF16 = jnp.bfloat16
I32 = jnp.int32

D_MODEL = 1024
BATCH = 8
SEQ = 2048
DEC_BATCH = 128
PAST_LEN = 16384
POOL_WIDTH = 512
POOL_GROUPS = 4
POOL_GROUP_DIM = 128
POOL_WINDOWS = (2, 4, 8, 16)
POOL_HIST = 15
MLSTM_WIDTH = 512
HEADS = 4
HEAD_DIM = 128
CHUNK = 128
N_EXPERTS = 32
TOP_K = 4
D_FF = 1024
SWIGLU_ALPHA = 1.702
SWIGLU_LIMIT = 7.0
PLE_DIM = 256
DN_ALPHA = 2.0 ** 0.25
LN_EPS = 1e-5

LANES = 128
SUBLANES = 8
VMEM_LIMIT = 56 * 1024 * 1024

MIX_TILE = 256
HIST_PAD = 16
TOK_TILE = 512
N_PROMPT = BATCH * SEQ
N_PROMPT_TILES = N_PROMPT // TOK_TILE
N_TILES = N_PROMPT_TILES + 1
ROW_BLOCK = 256
N_ASSIGN = (N_PROMPT + DEC_BATCH) * TOP_K
N_BLOCKS = -(-N_ASSIGN // ROW_BLOCK) + N_EXPERTS
N_SLOTS = N_BLOCKS * ROW_BLOCK
SAMPLE_BT = 16


def _dot(a, b):
    return jnp.dot(a, b, preferred_element_type=F32)


def _dot_nt(a, b):
    return lax.dot_general(a, b, (((1,), (1,)), ((), ())), preferred_element_type=F32)


def _dot_tn(a, b):
    return lax.dot_general(a, b, (((0,), (0,)), ((), ())), preferred_element_type=F32)


def _split3(a):
    a0 = a.astype(BF16)
    r1 = a - a0.astype(F32)
    a1 = r1.astype(BF16)
    r2 = r1 - a1.astype(F32)
    return a0, a1, r2.astype(BF16)


def _log_sigmoid(x):
    return jnp.minimum(x, 0.0) - jnp.log1p(jnp.exp(-jnp.abs(x)))


def _layer_norm(x, g, b):
    mu = jnp.mean(x, axis=-1, keepdims=True)
    xc = x - mu
    var = jnp.mean(xc * xc, axis=-1, keepdims=True)
    return xc * lax.rsqrt(var + LN_EPS) * g + b


def _gate_values(g, gbias):
    lane = lax.broadcasted_iota(I32, g.shape, 1)
    z = g + gbias
    return jnp.where(lane < HEADS, z, _log_sigmoid(z))


def _head_out(hh, o_h, gain):
    mu = jnp.mean(hh, axis=-1, keepdims=True)
    hc = hh - mu
    var = jnp.mean(hc * hc, axis=-1, keepdims=True)
    return jax.nn.sigmoid(o_h) * (hc * lax.rsqrt(var + LN_EPS) * gain)


def _prompt_mixer_kernel(x_ref, win_ref, wg_ref, gb_ref, wpool_ref, pscale_ref, mhg_ref, wout_ref,
                         ln1g_ref, ln1b_ref,
                         x1_ref, pool_ref, c_out_ref, n_out_ref, m_out_ref,
                         ubuf, mixbuf, c_s, n_s, m_s):
    ti = pl.program_id(1)
    nt = pl.num_programs(1)
    TT = MIX_TILE

    @pl.when(ti == 0)
    def _():
        ubuf[0:HIST_PAD, :] = jnp.zeros((HIST_PAD, POOL_WIDTH), F32)
        c_s[...] = jnp.zeros_like(c_s)
        n_s[...] = jnp.zeros_like(n_s)
        m_s[...] = jnp.zeros_like(m_s)

    x = x_ref[0]
    xb = x.astype(BF16)
    proj = _dot(xb, win_ref[...])
    g = _dot(xb, wg_ref[...])
    u = proj[:, 0:POOL_WIDTH]

    ubuf[HIST_PAD:HIST_PAD + TT, :] = u
    pos = ti * TT + lax.broadcasted_iota(I32, (TT, 1), 0)
    for gi, w in enumerate(POOL_WINDOWS):
        sl = slice(gi * POOL_GROUP_DIM, (gi + 1) * POOL_GROUP_DIM)
        ug = u[:, sl]
        s = ug
        for i in range(1, w):
            s = s + ubuf[HIST_PAD - i:HIST_PAD - i + TT, sl]
        cnt = jnp.minimum(pos + 1, w).astype(F32)
        z = s / cnt - ug
        mixbuf[:, sl] = _dot(z.astype(BF16), wpool_ref[gi]) * pscale_ref[:, sl]

    @pl.when(ti == nt - 1)
    def _():
        pool_ref[0] = ubuf[TT + 1:TT + HIST_PAD, :]

    ubuf[0:HIST_PAD, :] = ubuf[TT:TT + HIST_PAD, :]

    L = CHUNK
    row = lax.broadcasted_iota(I32, (L, L), 0)
    col = lax.broadcasted_iota(I32, (L, L), 1)
    causal = row >= col
    tril = jnp.where(causal, 1.0, 0.0).astype(BF16)
    for c in range(TT // L):
        rs = slice(c * L, (c + 1) * L)
        val = _gate_values(g[rs, :], gb_ref[...])
        v0, v1, v2 = _split3(val)
        cum = _dot(tril, v0) + _dot(tril, v1) + _dot(tril, v2)
        val_t = val.T
        cum_t = cum.T
        for h in range(HEADS):
            hs = slice(h * HEAD_DIM, (h + 1) * HEAD_DIM)
            qf = proj[rs, POOL_WIDTH + h * HEAD_DIM:POOL_WIDTH + (h + 1) * HEAD_DIM]
            kf = proj[rs, 2 * POOL_WIDTH + h * HEAD_DIM:2 * POOL_WIDTH + (h + 1) * HEAD_DIM] * (HEAD_DIM ** -0.5)
            vf = proj[rs, 3 * POOL_WIDTH + h * HEAD_DIM:3 * POOL_WIDTH + (h + 1) * HEAD_DIM]
            of = proj[rs, 4 * POOL_WIDTH + h * HEAD_DIM:4 * POOL_WIDTH + (h + 1) * HEAD_DIM]
            qb = qf.astype(BF16)
            kb = kf.astype(BF16)
            f_col = cum[:, HEADS + h:HEADS + h + 1]
            f_row = cum_t[HEADS + h:HEADS + h + 1, :]
            ig_row = val_t[h:h + 1, :]
            ig_col = val[:, h:h + 1]
            m_prev = m_s[h:h + 1, 0:1]
            c_prev = c_s[h]
            n_prev = n_s[h:h + 1, :]

            log_d = jnp.where(causal, f_col - f_row + ig_row, -jnp.inf)
            inter = m_prev + f_col
            m_t = jnp.maximum(inter, jnp.max(log_d, axis=-1, keepdims=True))
            dw = jnp.exp(log_d - m_t)
            sc = jnp.exp(inter - m_t)
            qk = _dot_nt(qb, kb) * dw
            num = _dot(qk.astype(BF16), vf.astype(BF16)) + sc * _dot_nt(qb, c_prev.astype(BF16))
            den = jnp.sum(qk, axis=-1, keepdims=True) + sc * jnp.sum(qf * n_prev, axis=-1, keepdims=True)
            hh = num / jnp.maximum(jnp.abs(den), jnp.exp(-m_t))

            m_new = m_t[L - 1:L, :]
            f_last = f_col[L - 1:L, :]
            wk = jnp.exp(ig_col + f_last - f_col - m_new)
            decay = jnp.exp(m_prev + f_last - m_new)
            c_s[h] = decay * c_prev + _dot_tn((vf * wk).astype(BF16), kb)
            n_s[h:h + 1, :] = decay * n_prev + jnp.sum(wk * kf, axis=0, keepdims=True)
            m_s[h:h + 1, :] = jnp.broadcast_to(m_new, (1, LANES))

            mixbuf[rs, POOL_WIDTH + h * HEAD_DIM:POOL_WIDTH + (h + 1) * HEAD_DIM] = _head_out(
                hh, of, mhg_ref[:, hs])

    @pl.when(ti == nt - 1)
    def _():
        c_out_ref[0] = c_s[...]
        n_out_ref[0] = n_s[0:HEADS, :]
        m_out_ref[0] = m_s[...]

    mix = _dot(mixbuf[...].astype(BF16), wout_ref[...])
    x1_ref[...] = _layer_norm(DN_ALPHA * x + mix, ln1g_ref[...], ln1b_ref[...])


def _prompt_mixer(x, w_in_b, w_g_b, gbias, w_pool_b, pscale, mhg, w_out_b, ln1g, ln1b):
    nt = SEQ // MIX_TILE
    const2 = lambda b, t: (0, 0)
    const3 = lambda b, t: (0, 0, 0)
    return pl.pallas_call(
        _prompt_mixer_kernel,
        grid=(BATCH, nt),
        in_specs=[
            pl.BlockSpec((1, MIX_TILE, D_MODEL), lambda b, t: (b, t, 0)),
            pl.BlockSpec(w_in_b.shape, const2),
            pl.BlockSpec(w_g_b.shape, const2),
            pl.BlockSpec(gbias.shape, const2),
            pl.BlockSpec(w_pool_b.shape, const3),
            pl.BlockSpec(pscale.shape, const2),
            pl.BlockSpec(mhg.shape, const2),
            pl.BlockSpec(w_out_b.shape, const2),
            pl.BlockSpec(ln1g.shape, const2),
            pl.BlockSpec(ln1b.shape, const2),
        ],
        out_specs=[
            pl.BlockSpec((MIX_TILE, D_MODEL), lambda b, t: (b * nt + t, 0)),
            pl.BlockSpec((1, POOL_HIST, POOL_WIDTH), lambda b, t: (b, 0, 0)),
            pl.BlockSpec((1, HEADS, HEAD_DIM, HEAD_DIM), lambda b, t: (b, 0, 0, 0)),
            pl.BlockSpec((1, HEADS, HEAD_DIM), lambda b, t: (b, 0, 0)),
            pl.BlockSpec((1, SUBLANES, LANES), lambda b, t: (b, 0, 0)),
        ],
        out_shape=[
            jax.ShapeDtypeStruct((N_PROMPT, D_MODEL), F32),
            jax.ShapeDtypeStruct((BATCH, POOL_HIST, POOL_WIDTH), F32),
            jax.ShapeDtypeStruct((BATCH, HEADS, HEAD_DIM, HEAD_DIM), F32),
            jax.ShapeDtypeStruct((BATCH, HEADS, HEAD_DIM), F32),
            jax.ShapeDtypeStruct((BATCH, SUBLANES, LANES), F32),
        ],
        scratch_shapes=[
            pltpu.VMEM((HIST_PAD + MIX_TILE, POOL_WIDTH), F32),
            pltpu.VMEM((MIX_TILE, D_MODEL), F32),
            pltpu.VMEM((HEADS, HEAD_DIM, HEAD_DIM), F32),
            pltpu.VMEM((SUBLANES, HEAD_DIM), F32),
            pltpu.VMEM((SUBLANES, LANES), F32),
        ],
        compiler_params=pltpu.CompilerParams(
            dimension_semantics=("arbitrary", "arbitrary"), vmem_limit_bytes=VMEM_LIMIT),
        name="prompt_mixer",
    )(x, w_in_b, w_g_b, gbias, w_pool_b, pscale, mhg, w_out_b, ln1g, ln1b)


def _sample_mixer_kernel(x_ref, hist_ref, c_ref, n_ref, m_ref, win_ref, wg_ref, gb_ref, wpool_ref,
                         pscale_ref, mhg_ref, wout_ref, ln1g_ref, ln1b_ref,
                         x1_ref, pool_out_ref, c_out_ref, n_out_ref, m_out_ref,
                         q_s, k_s, vw_s, v_s, o_s, mixbuf, h_s, coef_s):
    i = pl.program_id(0)
    nsteps = pl.num_programs(0)
    B = DEC_BATCH

    @pl.when(i == 0)
    def _():
        x = x_ref[...]
        xb = x.astype(BF16)
        proj = _dot(xb, win_ref[...])
        g = _dot(xb, wg_ref[...])
        u = proj[:, 0:POOL_WIDTH]
        for gi, w in enumerate(POOL_WINDOWS):
            sl = slice(gi * POOL_GROUP_DIM, (gi + 1) * POOL_GROUP_DIM)
            ug = u[:, sl]
            s = ug
            for j in range(1, w):
                r = POOL_HIST - j
                s = s + hist_ref[:, r * POOL_WIDTH + gi * POOL_GROUP_DIM:r * POOL_WIDTH + (gi + 1) * POOL_GROUP_DIM]
            cnt = float(min(PAST_LEN + 1, w))
            z = s / cnt - ug
            mixbuf[:, sl] = _dot(z.astype(BF16), wpool_ref[gi]) * pscale_ref[:, sl]
        pool_out_ref[:, 0:(POOL_HIST - 1) * POOL_WIDTH] = hist_ref[:, POOL_WIDTH:POOL_HIST * POOL_WIDTH]
        pool_out_ref[:, (POOL_HIST - 1) * POOL_WIDTH:POOL_HIST * POOL_WIDTH] = u

        val = _gate_values(g, gb_ref[...])
        lane = lax.broadcasted_iota(I32, (B, LANES), 1)
        qk_all = jnp.zeros((B, LANES), F32)
        sc_all = jnp.zeros((B, LANES), F32)
        den_all = jnp.zeros((B, LANES), F32)
        floor_all = jnp.zeros((B, LANES), F32)
        m_all = jnp.zeros((B, LANES), F32)
        for h in range(HEADS):
            hs = slice(h * HEAD_DIM, (h + 1) * HEAD_DIM)
            qf = proj[:, POOL_WIDTH + h * HEAD_DIM:POOL_WIDTH + (h + 1) * HEAD_DIM]
            kf = proj[:, 2 * POOL_WIDTH + h * HEAD_DIM:2 * POOL_WIDTH + (h + 1) * HEAD_DIM] * (HEAD_DIM ** -0.5)
            vf = proj[:, 3 * POOL_WIDTH + h * HEAD_DIM:3 * POOL_WIDTH + (h + 1) * HEAD_DIM]
            ig = val[:, h:h + 1]
            lf = val[:, HEADS + h:HEADS + h + 1]
            m0 = m_ref[:, h:h + 1]
            n0 = n_ref[:, hs]
            inter = m0 + lf
            m_t = jnp.maximum(inter, ig)
            dw = jnp.exp(ig - m_t)
            sc = jnp.exp(inter - m_t)
            qk = jnp.sum(qf * kf, axis=-1, keepdims=True) * dw
            den = qk + sc * jnp.sum(qf * n0, axis=-1, keepdims=True)
            n_out_ref[:, hs] = sc * n0 + dw * kf
            q_s[0:B, hs] = qf
            k_s[0:B, hs] = kf
            v_s[0:B, hs] = vf
            vw_s[0:B, hs] = vf * dw
            sel = lane == h
            qk_all = jnp.where(sel, qk, qk_all)
            sc_all = jnp.where(sel, sc, sc_all)
            den_all = jnp.where(sel, den, den_all)
            floor_all = jnp.where(sel, jnp.exp(-m_t), floor_all)
            m_all = jnp.where(lane == HEADS + h, m_t, m_all)
        o_s[...] = proj[:, 4 * POOL_WIDTH:5 * POOL_WIDTH]
        coef_s[0] = qk_all
        coef_s[1] = sc_all
        coef_s[2] = den_all
        coef_s[3] = floor_all
        m_out_ref[...] = m_all

    rows = pl.ds(pl.multiple_of(i * SAMPLE_BT, SAMPLE_BT), SAMPLE_BT)
    q_t, k_t, v_t, vw_t = q_s[rows, :], k_s[rows, :], v_s[rows, :], vw_s[rows, :]
    qk_t, sc_t, den_t, floor_t = coef_s[0, rows, :], coef_s[1, rows, :], coef_s[2, rows, :], coef_s[3, rows, :]
    h_rows = []
    for bl in range(SAMPLE_BT):
        heads = []
        for h in range(HEADS):
            hs = slice(h * HEAD_DIM, (h + 1) * HEAD_DIM)
            c_prev = c_ref[bl, h]
            q8 = jnp.broadcast_to(q_t[bl:bl + 1, hs], (SUBLANES, HEAD_DIM))
            cq = _dot_nt(q8.astype(BF16), c_prev.astype(BF16))[0:1, :]
            qk = qk_t[bl:bl + 1, h:h + 1]
            sc = sc_t[bl:bl + 1, h:h + 1]
            num = qk * v_t[bl:bl + 1, hs] + sc * cq
            heads.append(num / jnp.maximum(jnp.abs(den_t[bl:bl + 1, h:h + 1]), floor_t[bl:bl + 1, h:h + 1]))
            v_col = jnp.broadcast_to(vw_t[bl:bl + 1, hs], (HEAD_DIM, HEAD_DIM)).T
            c_out_ref[bl, h] = sc * c_prev + v_col * k_t[bl:bl + 1, hs]
        h_rows.append(jnp.concatenate(heads, axis=1))
    h_s[rows, :] = jnp.concatenate(h_rows, axis=0)

    @pl.when(i == nsteps - 1)
    def _():
        for h in range(HEADS):
            hs = slice(h * HEAD_DIM, (h + 1) * HEAD_DIM)
            mixbuf[:, POOL_WIDTH + h * HEAD_DIM:POOL_WIDTH + (h + 1) * HEAD_DIM] = _head_out(
                h_s[:, hs], o_s[:, hs], mhg_ref[:, hs])
        mix = _dot(mixbuf[...].astype(BF16), wout_ref[...])
        x1 = _layer_norm(DN_ALPHA * x_ref[...] + mix, ln1g_ref[...], ln1b_ref[...])
        x1_ref[0:B, :] = x1
        x1_ref[B:TOK_TILE, :] = jnp.zeros((TOK_TILE - B, D_MODEL), F32)


def _sample_mixer(x, hist2, c0, n0, m0, w_in_b, w_g_b, gbias, w_pool_b, pscale, mhg, w_out_b, ln1g, ln1b):
    B = DEC_BATCH
    steps = B // SAMPLE_BT
    full = lambda a: pl.BlockSpec(a.shape, lambda i: (0,) * a.ndim)
    c_spec = pl.BlockSpec((SAMPLE_BT, HEADS, HEAD_DIM, HEAD_DIM), lambda i: (i, 0, 0, 0))
    return pl.pallas_call(
        _sample_mixer_kernel,
        grid=(steps,),
        in_specs=[full(x), full(hist2), c_spec, full(n0), full(m0), full(w_in_b), full(w_g_b), full(gbias),
                  full(w_pool_b), full(pscale), full(mhg), full(w_out_b), full(ln1g), full(ln1b)],
        out_specs=[
            pl.BlockSpec((TOK_TILE, D_MODEL), lambda i: (0, 0)),
            pl.BlockSpec((B, POOL_HIST * POOL_WIDTH), lambda i: (0, 0)),
            c_spec,
            pl.BlockSpec((B, MLSTM_WIDTH), lambda i: (0, 0)),
            pl.BlockSpec((B, LANES), lambda i: (0, 0)),
        ],
        out_shape=[
            jax.ShapeDtypeStruct((TOK_TILE, D_MODEL), F32),
            jax.ShapeDtypeStruct((B, POOL_HIST * POOL_WIDTH), F32),
            jax.ShapeDtypeStruct((B, HEADS, HEAD_DIM, HEAD_DIM), F32),
            jax.ShapeDtypeStruct((B, MLSTM_WIDTH), F32),
            jax.ShapeDtypeStruct((B, LANES), F32),
        ],
        scratch_shapes=[
            pltpu.VMEM((B, MLSTM_WIDTH), F32),
            pltpu.VMEM((B, MLSTM_WIDTH), F32),
            pltpu.VMEM((B, MLSTM_WIDTH), F32),
            pltpu.VMEM((B, MLSTM_WIDTH), F32),
            pltpu.VMEM((B, MLSTM_WIDTH), F32),
            pltpu.VMEM((B, D_MODEL), F32),
            pltpu.VMEM((B, MLSTM_WIDTH), F32),
            pltpu.VMEM((4, B, LANES), F32),
        ],
        compiler_params=pltpu.CompilerParams(
            dimension_semantics=("arbitrary",), vmem_limit_bytes=VMEM_LIMIT),
        name="sample_mixer",
    )(x, hist2, c0, n0, m0, w_in_b, w_g_b, gbias, w_pool_b, pscale, mhg, w_out_b, ln1g, ln1b)


def _pick_tile(i, prompt_ref, sample_ref):
    return jnp.where(i < N_PROMPT_TILES, prompt_ref[...], sample_ref[...])


def _router_kernel(xp_ref, xs_ref, wrt_ref, br_ref, exp_ref, gate_ref, rank_ref, cnt_ref, carry):
    i = pl.program_id(0)
    T = TOK_TILE

    @pl.when(i == 0)
    def _():
        carry[...] = jnp.zeros_like(carry)

    x = _pick_tile(i, xp_ref, xs_ref)
    xh = x.astype(BF16)
    xl = (x - xh.astype(F32)).astype(BF16)
    w = wrt_ref[...]
    wh = w.astype(BF16)
    wl = (w - wh.astype(F32)).astype(BF16)
    logits = _dot_nt(wh, xh) + (_dot_nt(wh, xl) + _dot_nt(wl, xh)) + br_ref[:, 0:1]

    n_valid = jnp.where(i < N_PROMPT_TILES, T, DEC_BATCH)
    valid = lax.broadcasted_iota(I32, (1, T), 1) < n_valid
    erow = lax.broadcasted_iota(I32, (N_EXPERTS, T), 0).astype(F32)
    work = logits
    vals, idxs, sels = [], [], []
    for _ in range(TOP_K):
        mx = jnp.max(work, axis=0, keepdims=True)
        idx = jnp.min(jnp.where(work == mx, erow, float(N_EXPERTS)), axis=0, keepdims=True)
        sel = erow == idx
        work = jnp.where(sel, -jnp.inf, work)
        vals.append(mx)
        idxs.append(idx.astype(I32))
        sels.append(sel)
    chosen = jnp.logical_or(jnp.logical_or(sels[0], sels[1]), jnp.logical_or(sels[2], sels[3]))
    es = [jnp.exp(v - vals[0]) for v in vals]
    tot = es[0] + es[1] + es[2] + es[3]

    onehot = jnp.where(jnp.logical_and(chosen, valid), 1.0, 0.0)
    trow = lax.broadcasted_iota(I32, (T, T), 0)
    tcol = lax.broadcasted_iota(I32, (T, T), 1)
    before = jnp.where(trow < tcol, 1.0, 0.0).astype(BF16)
    running = _dot(onehot.astype(BF16), before) + carry[:, 0:1]
    carry[...] = carry[...] + jnp.sum(onehot, axis=1, keepdims=True)
    cnt_ref[...] = carry[...]

    r8 = lax.broadcasted_iota(I32, (SUBLANES, T), 0)
    e_out = jnp.zeros((SUBLANES, T), I32)
    g_out = jnp.zeros((SUBLANES, T), F32)
    k_out = jnp.zeros((SUBLANES, T), I32)
    for j in range(TOP_K):
        rank_j = jnp.sum(jnp.where(sels[j], running, 0.0), axis=0, keepdims=True).astype(I32)
        e_out = jnp.where(r8 == j, idxs[j], e_out)
        g_out = jnp.where(r8 == j, es[j] / tot, g_out)
        k_out = jnp.where(r8 == j, rank_j, k_out)
    exp_ref[0] = e_out
    gate_ref[0] = g_out
    rank_ref[0] = k_out


def _router(x1p, x1s, w_router_t, b_router_col):
    tile_spec = pl.BlockSpec((1, SUBLANES, TOK_TILE), lambda i: (i, 0, 0))
    return pl.pallas_call(
        _router_kernel,
        grid=(N_TILES,),
        in_specs=[
            pl.BlockSpec((TOK_TILE, D_MODEL), lambda i: (jnp.minimum(i, N_PROMPT_TILES - 1), 0)),
            pl.BlockSpec((TOK_TILE, D_MODEL), lambda i: (0, 0)),
            pl.BlockSpec(w_router_t.shape, lambda i: (0, 0)),
            pl.BlockSpec(b_router_col.shape, lambda i: (0, 0)),
        ],
        out_specs=[tile_spec, tile_spec, tile_spec,
                   pl.BlockSpec((N_EXPERTS, LANES), lambda i: (0, 0))],
        out_shape=[
            jax.ShapeDtypeStruct((N_TILES, SUBLANES, TOK_TILE), I32),
            jax.ShapeDtypeStruct((N_TILES, SUBLANES, TOK_TILE), F32),
            jax.ShapeDtypeStruct((N_TILES, SUBLANES, TOK_TILE), I32),
            jax.ShapeDtypeStruct((N_EXPERTS, LANES), F32),
        ],
        scratch_shapes=[pltpu.VMEM((N_EXPERTS, LANES), F32)],
        compiler_params=pltpu.CompilerParams(
            dimension_semantics=("arbitrary",), vmem_limit_bytes=VMEM_LIMIT),
        name="router",
    )(x1p, x1s, w_router_t, b_router_col)


def _slot_kernel(gstart_ref, exp_ref, rank_ref, dest_ref):
    e = exp_ref[...]
    acc = rank_ref[...]
    for k in range(N_EXPERTS):
        acc = acc + jnp.where(e == k, gstart_ref[k], 0)
    dest_ref[...] = acc


def _slots(group_start, top_e, rank):
    return pl.pallas_call(
        _slot_kernel,
        in_specs=[pl.BlockSpec(memory_space=pltpu.SMEM),
                  pl.BlockSpec(memory_space=pltpu.VMEM),
                  pl.BlockSpec(memory_space=pltpu.VMEM)],
        out_specs=pl.BlockSpec(memory_space=pltpu.VMEM),
        out_shape=jax.ShapeDtypeStruct(top_e.shape, I32),
        name="slots",
    )(group_start, top_e, rank)


ROW_TILE = D_MODEL // LANES


def _to_row_tiles(dst_ref, x):
    for s in range(ROW_TILE):
        dst_ref[pl.ds(s, x.shape[0], stride=ROW_TILE), :] = x[:, s * LANES:(s + 1) * LANES]


def _from_row_tiles(src_ref, n_rows):
    return jnp.concatenate(
        [src_ref[pl.ds(s, n_rows, stride=ROW_TILE), :] for s in range(ROW_TILE)], axis=1)


def _tile_rows(r):
    return pl.ds(pl.multiple_of(r * ROW_TILE, ROW_TILE), ROW_TILE)


def _dispatch_kernel(ztail_ref, nused_ref, dest_ref, xp_ref, xs_ref, slots_hbm, stage, zbuf, sem, zsem):
    i = pl.program_id(0)

    def zero_block(row_start):
        start = pl.multiple_of(row_start * ROW_TILE, ROW_TILE)
        return pltpu.make_async_copy(zbuf, slots_hbm.at[pl.ds(start, ROW_BLOCK * ROW_TILE), :], zsem)

    @pl.when(i == 0)
    def _():
        zbuf[...] = jnp.zeros_like(zbuf)
        for e in range(N_EXPERTS):
            @pl.when(ztail_ref[e] >= 0)
            def _():
                zero_block(ztail_ref[e]).start()

        def start_tail(b, carry):
            zero_block(b * ROW_BLOCK).start()
            return carry

        def wait_tail(b, carry):
            zero_block(b * ROW_BLOCK).wait()
            return carry

        lax.fori_loop(nused_ref[0], N_BLOCKS, start_tail, 0)
        for e in range(N_EXPERTS):
            @pl.when(ztail_ref[e] >= 0)
            def _():
                zero_block(ztail_ref[e]).wait()
        lax.fori_loop(nused_ref[0], N_BLOCKS, wait_tail, 0)

    _to_row_tiles(stage, _pick_tile(i, xp_ref, xs_ref))
    n_valid = jnp.where(i < N_PROMPT_TILES, TOK_TILE, DEC_BATCH)

    def row_copy(t, j):
        return pltpu.make_async_copy(stage.at[_tile_rows(t), :],
                                     slots_hbm.at[_tile_rows(dest_ref[0, j, t]), :], sem)

    def issue(t, carry):
        for j in range(TOP_K):
            row_copy(t, j).start()
        return carry

    def drain(t, carry):
        for j in range(TOP_K):
            row_copy(t, j).wait()
        return carry

    lax.fori_loop(0, n_valid, issue, 0)
    lax.fori_loop(0, n_valid, drain, 0)


def _dispatch(zero_tail, n_used, dest, x1p, x1s):
    grid_spec = pltpu.PrefetchScalarGridSpec(
        num_scalar_prefetch=2,
        grid=(N_TILES,),
        in_specs=[
            pl.BlockSpec((1, SUBLANES, TOK_TILE), lambda i, z, n: (i, 0, 0), memory_space=pltpu.SMEM),
            pl.BlockSpec((TOK_TILE, D_MODEL), lambda i, z, n: (jnp.minimum(i, N_PROMPT_TILES - 1), 0)),
            pl.BlockSpec((TOK_TILE, D_MODEL), lambda i, z, n: (0, 0)),
        ],
        out_specs=pl.BlockSpec(memory_space=pl.ANY),
        scratch_shapes=[
            pltpu.VMEM((TOK_TILE * ROW_TILE, LANES), F32),
            pltpu.VMEM((ROW_BLOCK * ROW_TILE, LANES), F32),
            pltpu.SemaphoreType.DMA(()),
            pltpu.SemaphoreType.DMA(()),
        ],
    )
    return pl.pallas_call(
        _dispatch_kernel,
        grid_spec=grid_spec,
        out_shape=jax.ShapeDtypeStruct((N_SLOTS * ROW_TILE, LANES), F32),
        compiler_params=pltpu.CompilerParams(
            dimension_semantics=("arbitrary",), vmem_limit_bytes=VMEM_LIMIT),
        name="dispatch",
    )(zero_tail, n_used, dest, x1p, x1s)


def _expert_kernel(bexp_ref, nused_ref, xs_ref, w1_ref, b1_ref, w2_ref, b2_ref, out_ref):
    i = pl.program_id(0)

    @pl.when(i < nused_ref[0])
    def _():
        x = _from_row_tiles(xs_ref, ROW_BLOCK)
        h = _dot(x.astype(BF16), w1_ref[0]) + b1_ref[0]
        glu = jnp.minimum(h[:, 0:D_FF], SWIGLU_LIMIT)
        lin = jnp.clip(h[:, D_FF:2 * D_FF], -SWIGLU_LIMIT, SWIGLU_LIMIT)
        a = glu * jax.nn.sigmoid(SWIGLU_ALPHA * glu) * (lin + 1.0)
        _to_row_tiles(out_ref, _dot(a.astype(BF16), w2_ref[0]) + b2_ref[0])

    @pl.when(i >= nused_ref[0])
    def _():
        out_ref[...] = jnp.zeros_like(out_ref)


def _experts(block_expert, n_used, slots, w1p, b1p, w2b, b2):
    row_idx = lambda i, be, nu: (jnp.minimum(i, nu[0] - 1), 0)
    exp_idx = lambda i, be, nu: (be[i], 0, 0)
    grid_spec = pltpu.PrefetchScalarGridSpec(
        num_scalar_prefetch=2,
        grid=(N_BLOCKS,),
        in_specs=[
            pl.BlockSpec((ROW_BLOCK * ROW_TILE, LANES), row_idx),
            pl.BlockSpec((1, D_MODEL, 2 * D_FF), exp_idx),
            pl.BlockSpec((1, 1, 2 * D_FF), exp_idx),
            pl.BlockSpec((1, D_FF, D_MODEL), exp_idx),
            pl.BlockSpec((1, 1, D_MODEL), exp_idx),
        ],
        out_specs=pl.BlockSpec((ROW_BLOCK * ROW_TILE, LANES), lambda i, be, nu: (i, 0)),
    )
    return pl.pallas_call(
        _expert_kernel,
        grid_spec=grid_spec,
        out_shape=jax.ShapeDtypeStruct((N_SLOTS * ROW_TILE, LANES), F32),
        compiler_params=pltpu.CompilerParams(
            dimension_semantics=("arbitrary",), vmem_limit_bytes=VMEM_LIMIT),
        name="experts",
    )(block_expert, n_used, slots, w1p, b1p, w2b, b2)


def _combine_kernel(dest_ref, gate_ref, xp_ref, xs_ref, pp_ref, ps_ref, eo_hbm, ln2g_ref, ln2b_ref,
                    wpg_ref, wple_ref, yp_ref, ys_ref, gbuf, sem):
    i = pl.program_id(0)
    T = TOK_TILE
    n_valid = jnp.where(i < N_PROMPT_TILES, T, DEC_BATCH)

    def row_copy(j, t):
        return pltpu.make_async_copy(eo_hbm.at[_tile_rows(dest_ref[0, j, t]), :],
                                     gbuf.at[j, _tile_rows(t), :], sem)

    def issue(t, carry):
        for j in range(TOP_K):
            row_copy(j, t).start()
        return carry

    def drain(t, carry):
        for j in range(TOP_K):
            row_copy(j, t).wait()
        return carry

    @pl.when(i == N_PROMPT_TILES)
    def _():
        gbuf[...] = jnp.zeros_like(gbuf)

    lax.fori_loop(0, n_valid, issue, 0)
    lax.fori_loop(0, n_valid, drain, 0)

    x1 = _pick_tile(i, xp_ref, xs_ref)
    gates_t = jnp.concatenate([gate_ref[0], jnp.zeros((LANES - SUBLANES, T), F32)], axis=0).T
    ff = gates_t[:, 0:1] * _from_row_tiles(gbuf.at[0], T)
    for j in range(1, TOP_K):
        ff = ff + gates_t[:, j:j + 1] * _from_row_tiles(gbuf.at[j], T)
    x2 = _layer_norm(DN_ALPHA * x1 + ff, ln2g_ref[...], ln2b_ref[...])
    p = _pick_tile(i, pp_ref, ps_ref)
    y = x2 + jax.nn.sigmoid(_dot(x2.astype(BF16), wpg_ref[...])) * _dot(p.astype(BF16), wple_ref[...])

    @pl.when(i < N_PROMPT_TILES)
    def _():
        yp_ref[...] = y

    @pl.when(i == N_PROMPT_TILES)
    def _():
        ys_ref[...] = y[0:DEC_BATCH, :]


def _combine(dest, gates, x1p, x1s, pp, ps, expert_out, ln2g, ln2b, w_pg_b, w_ple_b):
    tile_idx = lambda i: (jnp.minimum(i, N_PROMPT_TILES - 1), 0)
    const2 = lambda i: (0, 0)
    return pl.pallas_call(
        _combine_kernel,
        grid=(N_TILES,),
        in_specs=[
            pl.BlockSpec((1, SUBLANES, TOK_TILE), lambda i: (i, 0, 0), memory_space=pltpu.SMEM),
            pl.BlockSpec((1, SUBLANES, TOK_TILE), lambda i: (i, 0, 0)),
            pl.BlockSpec((TOK_TILE, D_MODEL), tile_idx),
            pl.BlockSpec((TOK_TILE, D_MODEL), const2),
            pl.BlockSpec((TOK_TILE, PLE_DIM), tile_idx),
            pl.BlockSpec((TOK_TILE, PLE_DIM), const2),
            pl.BlockSpec(memory_space=pl.ANY),
            pl.BlockSpec(ln2g.shape, const2),
            pl.BlockSpec(ln2b.shape, const2),
            pl.BlockSpec(w_pg_b.shape, const2),
            pl.BlockSpec(w_ple_b.shape, const2),
        ],
        out_specs=[
            pl.BlockSpec((TOK_TILE, D_MODEL), tile_idx),
            pl.BlockSpec((DEC_BATCH, D_MODEL), const2),
        ],
        out_shape=[
            jax.ShapeDtypeStruct((N_PROMPT, D_MODEL), F32),
            jax.ShapeDtypeStruct((DEC_BATCH, D_MODEL), F32),
        ],
        scratch_shapes=[
            pltpu.VMEM((TOP_K, TOK_TILE * ROW_TILE, LANES), F32),
            pltpu.SemaphoreType.DMA(()),
        ],
        compiler_params=pltpu.CompilerParams(
            dimension_semantics=("arbitrary",), vmem_limit_bytes=VMEM_LIMIT),
        name="combine",
    )(dest, gates, x1p, x1s, pp, ps, expert_out, ln2g, ln2b, w_pg_b, w_ple_b)


def kernel(x_prompt, x_sample, state_pool, state_mlstm_C, state_mlstm_n, state_mlstm_m, p_prompt, p_sample, w_in, b_i, b_f, w_pool, pool_scale, mh_g, w_out, ln1_g, ln1_b, w_router, b_router, w_mlp1, b_mlp1, w_mlp2, b_mlp2, ln2_g, ln2_b, w_ple, w_ple_gate):
    n_main = POOL_WIDTH + 4 * MLSTM_WIDTH
    w_in_b = w_in[0, :, 0:n_main].astype(BF16)
    w_g_b = jnp.pad(w_in[0, :, n_main:], ((0, 0), (0, LANES - 2 * HEADS))).astype(BF16)
    gbias = jnp.pad(jnp.concatenate([b_i[0], b_f[0]]), (0, LANES - 2 * HEADS)).reshape(1, LANES)
    w_pool_b = w_pool[0].astype(BF16)
    pscale = pool_scale[0].reshape(1, POOL_WIDTH)
    mhg = mh_g[0].reshape(1, MLSTM_WIDTH)
    w_out_b = w_out[0].astype(BF16)
    ln1g = ln1_g[0].reshape(1, D_MODEL)
    ln1b = ln1_b[0].reshape(1, D_MODEL)
    ln2g = ln2_g[0].reshape(1, D_MODEL)
    ln2b = ln2_b[0].reshape(1, D_MODEL)
    w_router_t = w_router[0].T
    b_router_col = jnp.broadcast_to(b_router[0].reshape(N_EXPERTS, 1), (N_EXPERTS, LANES))
    w1 = w_mlp1[0]
    w1p = jnp.concatenate([w1[:, :, 0::2], w1[:, :, 1::2]], axis=-1).astype(BF16)
    b1 = b_mlp1[0]
    b1p = jnp.concatenate([b1[:, 0::2], b1[:, 1::2]], axis=-1).reshape(N_EXPERTS, 1, 2 * D_FF)
    w2b = w_mlp2[0].astype(BF16)
    b2 = b_mlp2[0].reshape(N_EXPERTS, 1, D_MODEL)
    w_pg_b = w_ple_gate[0].astype(BF16)
    w_ple_b = w_ple[0].astype(BF16)

    x1p, pool_p, c_p, n_p, m_p = _prompt_mixer(
        x_prompt, w_in_b, w_g_b, gbias, w_pool_b, pscale, mhg, w_out_b, ln1g, ln1b)
    x1s, pool_s, c_s, n_s, m_s = _sample_mixer(
        x_sample.reshape(DEC_BATCH, D_MODEL),
        state_pool[0].reshape(DEC_BATCH, POOL_HIST * POOL_WIDTH),
        state_mlstm_C[0], state_mlstm_n[0].reshape(DEC_BATCH, MLSTM_WIDTH), state_mlstm_m[0],
        w_in_b, w_g_b, gbias, w_pool_b, pscale, mhg, w_out_b, ln1g, ln1b)

    top_e, gates, rank, counts = _router(x1p, x1s, w_router_t, b_router_col)
    cnt = counts[:, 0].astype(I32)
    nblk = (cnt + ROW_BLOCK - 1) // ROW_BLOCK
    blk_end = jnp.cumsum(nblk)
    group_start = (blk_end - nblk) * ROW_BLOCK
    n_used = blk_end[-1:]
    blk = jnp.minimum(jnp.arange(N_BLOCKS, dtype=I32), n_used[0] - 1)
    block_expert = jnp.minimum(jnp.searchsorted(blk_end, blk, side='right'), N_EXPERTS - 1).astype(I32)
    zero_tail = jnp.where(cnt > 0, blk_end * ROW_BLOCK - ROW_BLOCK, -1).astype(I32)
    dest = _slots(group_start.astype(I32), top_e, rank)

    n_used = n_used.astype(I32)
    slots = _dispatch(zero_tail, n_used, dest, x1p, x1s)
    expert_out = _experts(block_expert, n_used, slots, w1p, b1p, w2b, b2)

    pp = p_prompt[0].reshape(N_PROMPT, PLE_DIM)
    ps = jnp.pad(p_sample[0].reshape(DEC_BATCH, PLE_DIM), ((0, TOK_TILE - DEC_BATCH), (0, 0)))
    yp, ys = _combine(dest, gates, x1p, x1s, pp, ps, expert_out, ln2g, ln2b, w_pg_b, w_ple_b)

    return (
        yp.reshape(BATCH, SEQ, D_MODEL),
        ys.reshape(DEC_BATCH, 1, D_MODEL),
        pool_p.reshape(1, BATCH, POOL_HIST, POOL_WIDTH),
        c_p.reshape(1, BATCH, HEADS, HEAD_DIM, HEAD_DIM),
        n_p.reshape(1, BATCH, HEADS, HEAD_DIM),
        m_p[:, 0:HEADS, 0].reshape(1, BATCH, HEADS),
        pool_s.reshape(1, DEC_BATCH, POOL_HIST, POOL_WIDTH),
        c_s.reshape(1, DEC_BATCH, HEADS, HEAD_DIM, HEAD_DIM),
        n_s.reshape(1, DEC_BATCH, HEADS, HEAD_DIM),
        m_s[:, HEADS:2 * HEADS].reshape(1, DEC_BATCH, HEADS),
    )
```

```python
import jax
import jax.numpy as jnp
from jax import lax
from jax.experimental import pallas as pl
from jax.experimental.pallas import tpu as pltpu

F32 = jnp.float32
BF16 = jnp.bfloat16
I32 = jnp.int32

D_MODEL = 1024
BATCH = 8
SEQ = 2048
DEC_BATCH = 128
PAST_LEN = 16384
POOL_WIDTH = 512
POOL_GROUPS = 4
POOL_GROUP_DIM = 128
POOL_WINDOWS = (2, 4, 8, 16)
POOL_HIST = 15
MLSTM_WIDTH = 512
HEADS = 4
HEAD_DIM = 128
CHUNK = 128
N_EXPERTS = 32
TOP_K = 4
D_FF = 1024
SWIGLU_ALPHA = 1.702
SWIGLU_LIMIT = 7.0
PLE_DIM = 256
DN_ALPHA = 2.0 ** 0.25
LN_EPS = 1e-5

LANES = 128
SUBLANES = 8
VMEM_LIMIT = 56 * 1024 * 1024

MIX_TILE = 256
HIST_PAD = 16
TOK_TILE = 512
N_PROMPT = BATCH * SEQ
N_PROMPT_TILES = N_PROMPT // TOK_TILE
N_TILES = N_PROMPT_TILES + 1
ROW_BLOCK = 256
N_ASSIGN = (N_PROMPT + DEC_BATCH) * TOP_K
N_BLOCKS = -(-N_ASSIGN // ROW_BLOCK) + N_EXPERTS
N_SLOTS = N_BLOCKS * ROW_BLOCK
SAMPLE_BT = 16


def _dot(a, b):
    return jnp.dot(a, b, preferred_element_type=F32)


def _dot_nt(a, b):
    return lax.dot_general(a, b, (((1,), (1,)), ((), ())), preferred_element_type=F32)


def _dot_tn(a, b):
    return lax.dot_general(a, b, (((0,), (0,)), ((), ())), preferred_element_type=F32)


def _split3(a):
    a0 = a.astype(BF16)
    r1 = a - a0.astype(F32)
    a1 = r1.astype(BF16)
    r2 = r1 - a1.astype(F32)
    return a0, a1, r2.astype(BF16)


def _log_sigmoid(x):
    return jnp.minimum(x, 0.0) - jnp.log1p(jnp.exp(-jnp.abs(x)))


def _layer_norm(x, g, b):
    mu = jnp.mean(x, axis=-1, keepdims=True)
    xc = x - mu
    var = jnp.mean(xc * xc, axis=-1, keepdims=True)
    return xc * lax.rsqrt(var + LN_EPS) * g + b


def _gate_values(g, gbias):
    lane = lax.broadcasted_iota(I32, g.shape, 1)
    z = g + gbias
    return jnp.where(lane < HEADS, z, _log_sigmoid(z))


def _head_out(hh, o_h, gain):
    mu = jnp.mean(hh, axis=-1, keepdims=True)
    hc = hh - mu
    var = jnp.mean(hc * hc, axis=-1, keepdims=True)
    return jax.nn.sigmoid(o_h) * (hc * lax.rsqrt(var + LN_EPS) * gain)


def _prompt_mixer_kernel(x_ref, win_ref, wg_ref, gb_ref, wpool_ref, pscale_ref, mhg_ref, wout_ref,
                         ln1g_ref, ln1b_ref,
                         x1_ref, pool_ref, c_out_ref, n_out_ref, m_out_ref,
                         ubuf, mixbuf, c_s, n_s, m_s):
    ti = pl.program_id(1)
    nt = pl.num_programs(1)
    TT = MIX_TILE

    @pl.when(ti == 0)
    def _():
        ubuf[0:HIST_PAD, :] = jnp.zeros((HIST_PAD, POOL_WIDTH), F32)
        c_s[...] = jnp.zeros_like(c_s)
        n_s[...] = jnp.zeros_like(n_s)
        m_s[...] = jnp.zeros_like(m_s)

    x = x_ref[0]
    xb = x.astype(BF16)
    proj = _dot(xb, win_ref[...])
    g = _dot(xb, wg_ref[...])
    u = proj[:, 0:POOL_WIDTH]

    ubuf[HIST_PAD:HIST_PAD + TT, :] = u
    pos = ti * TT + lax.broadcasted_iota(I32, (TT, 1), 0)
    for gi, w in enumerate(POOL_WINDOWS):
        sl = slice(gi * POOL_GROUP_DIM, (gi + 1) * POOL_GROUP_DIM)
        ug = u[:, sl]
        s = ug
        for i in range(1, w):
            s = s + ubuf[HIST_PAD - i:HIST_PAD - i + TT, sl]
        cnt = jnp.minimum(pos + 1, w).astype(F32)
        z = s / cnt - ug
        mixbuf[:, sl] = _dot(z.astype(BF16), wpool_ref[gi]) * pscale_ref[:, sl]

    @pl.when(ti == nt - 1)
    def _():
        pool_ref[0] = ubuf[TT + 1:TT + HIST_PAD, :]

    ubuf[0:HIST_PAD, :] = ubuf[TT:TT + HIST_PAD, :]

    L = CHUNK
    row = lax.broadcasted_iota(I32, (L, L), 0)
    col = lax.broadcasted_iota(I32, (L, L), 1)
    causal = row >= col
    tril = jnp.where(causal, 1.0, 0.0).astype(BF16)
    for c in range(TT // L):
        rs = slice(c * L, (c + 1) * L)
        val = _gate_values(g[rs, :], gb_ref[...])
        v0, v1, v2 = _split3(val)
        cum = _dot(tril, v0) + _dot(tril, v1) + _dot(tril, v2)
        val_t = val.T
        cum_t = cum.T
        for h in range(HEADS):
            hs = slice(h * HEAD_DIM, (h + 1) * HEAD_DIM)
            qf = proj[rs, POOL_WIDTH + h * HEAD_DIM:POOL_WIDTH + (h + 1) * HEAD_DIM]
            kf = proj[rs, 2 * POOL_WIDTH + h * HEAD_DIM:2 * POOL_WIDTH + (h + 1) * HEAD_DIM] * (HEAD_DIM ** -0.5)
            vf = proj[rs, 3 * POOL_WIDTH + h * HEAD_DIM:3 * POOL_WIDTH + (h + 1) * HEAD_DIM]
            of = proj[rs, 4 * POOL_WIDTH + h * HEAD_DIM:4 * POOL_WIDTH + (h + 1) * HEAD_DIM]
            qb = qf.astype(BF16)
            kb = kf.astype(BF16)
            f_col = cum[:, HEADS + h:HEADS + h + 1]
            f_row = cum_t[HEADS + h:HEADS + h + 1, :]
            ig_row = val_t[h:h + 1, :]
            ig_col = val[:, h:h + 1]
            m_prev = m_s[h:h + 1, 0:1]
            c_prev = c_s[h]
            n_prev = n_s[h:h + 1, :]

            log_d = jnp.where(causal, f_col - f_row + ig_row, -jnp.inf)
            inter = m_prev + f_col
            m_t = jnp.maximum(inter, jnp.max(log_d, axis=-1, keepdims=True))
            dw = jnp.exp(log_d - m_t)
            sc = jnp.exp(inter - m_t)
            qk = _dot_nt(qb, kb) * dw
            num = _dot(qk.astype(BF16), vf.astype(BF16)) + sc * _dot_nt(qb, c_prev.astype(BF16))
            den = jnp.sum(qk, axis=-1, keepdims=True) + sc * jnp.sum(qf * n_prev, axis=-1, keepdims=True)
            hh = num / jnp.maximum(jnp.abs(den), jnp.exp(-m_t))

            m_new = m_t[L - 1:L, :]
            f_last = f_col[L - 1:L, :]
            wk = jnp.exp(ig_col + f_last - f_col - m_new)
            decay = jnp.exp(m_prev + f_last - m_new)
            c_s[h] = decay * c_prev + _dot_tn((vf * wk).astype(BF16), kb)
            n_s[h:h + 1, :] = decay * n_prev + jnp.sum(wk * kf, axis=0, keepdims=True)
            m_s[h:h + 1, :] = jnp.broadcast_to(m_new, (1, LANES))

            mixbuf[rs, POOL_WIDTH + h * HEAD_DIM:POOL_WIDTH + (h + 1) * HEAD_DIM] = _head_out(
                hh, of, mhg_ref[:, hs])

    @pl.when(ti == nt - 1)
    def _():
        c_out_ref[0] = c_s[...]
        n_out_ref[0] = n_s[0:HEADS, :]
        m_out_ref[0] = m_s[...]

    mix = _dot(mixbuf[...].astype(BF16), wout_ref[...])
    x1_ref[...] = _layer_norm(DN_ALPHA * x + mix, ln1g_ref[...], ln1b_ref[...])


def _prompt_mixer(x, w_in_b, w_g_b, gbias, w_pool_b, pscale, mhg, w_out_b, ln1g, ln1b):
    nt = SEQ // MIX_TILE
    const2 = lambda b, t: (0, 0)
    const3 = lambda b, t: (0, 0, 0)
    return pl.pallas_call(
        _prompt_mixer_kernel,
        grid=(BATCH, nt),
        in_specs=[
            pl.BlockSpec((1, MIX_TILE, D_MODEL), lambda b, t: (b, t, 0)),
            pl.BlockSpec(w_in_b.shape, const2),
            pl.BlockSpec(w_g_b.shape, const2),
            pl.BlockSpec(gbias.shape, const2),
            pl.BlockSpec(w_pool_b.shape, const3),
            pl.BlockSpec(pscale.shape, const2),
            pl.BlockSpec(mhg.shape, const2),
            pl.BlockSpec(w_out_b.shape, const2),
            pl.BlockSpec(ln1g.shape, const2),
            pl.BlockSpec(ln1b.shape, const2),
        ],
        out_specs=[
            pl.BlockSpec((MIX_TILE, D_MODEL), lambda b, t: (b * nt + t, 0)),
            pl.BlockSpec((1, POOL_HIST, POOL_WIDTH), lambda b, t: (b, 0, 0)),
            pl.BlockSpec((1, HEADS, HEAD_DIM, HEAD_DIM), lambda b, t: (b, 0, 0, 0)),
            pl.BlockSpec((1, HEADS, HEAD_DIM), lambda b, t: (b, 0, 0)),
            pl.BlockSpec((1, SUBLANES, LANES), lambda b, t: (b, 0, 0)),
        ],
        out_shape=[
            jax.ShapeDtypeStruct((N_PROMPT, D_MODEL), F32),
            jax.ShapeDtypeStruct((BATCH, POOL_HIST, POOL_WIDTH), F32),
            jax.ShapeDtypeStruct((BATCH, HEADS, HEAD_DIM, HEAD_DIM), F32),
            jax.ShapeDtypeStruct((BATCH, HEADS, HEAD_DIM), F32),
            jax.ShapeDtypeStruct((BATCH, SUBLANES, LANES), F32),
        ],
        scratch_shapes=[
            pltpu.VMEM((HIST_PAD + MIX_TILE, POOL_WIDTH), F32),
            pltpu.VMEM((MIX_TILE, D_MODEL), F32),
            pltpu.VMEM((HEADS, HEAD_DIM, HEAD_DIM), F32),
            pltpu.VMEM((SUBLANES, HEAD_DIM), F32),
            pltpu.VMEM((SUBLANES, LANES), F32),
        ],
        compiler_params=pltpu.CompilerParams(
            dimension_semantics=("arbitrary", "arbitrary"), vmem_limit_bytes=VMEM_LIMIT),
        name="prompt_mixer",
    )(x, w_in_b, w_g_b, gbias, w_pool_b, pscale, mhg, w_out_b, ln1g, ln1b)


def _sample_mixer_kernel(x_ref, hist_ref, c_ref, n_ref, m_ref, win_ref, wg_ref, gb_ref, wpool_ref,
                         pscale_ref, mhg_ref, wout_ref, ln1g_ref, ln1b_ref,
                         x1_ref, pool_out_ref, c_out_ref, n_out_ref, m_out_ref,
                         q_s, k_s, vw_s, v_s, o_s, mixbuf, h_s, coef_s):
    i = pl.program_id(0)
    nsteps = pl.num_programs(0)
    B = DEC_BATCH

    @pl.when(i == 0)
    def _():
        x = x_ref[...]
        xb = x.astype(BF16)
        proj = _dot(xb, win_ref[...])
        g = _dot(xb, wg_ref[...])
        u = proj[:, 0:POOL_WIDTH]
        for gi, w in enumerate(POOL_WINDOWS):
            sl = slice(gi * POOL_GROUP_DIM, (gi + 1) * POOL_GROUP_DIM)
            ug = u[:, sl]
            s = ug
            for j in range(1, w):
                r = POOL_HIST - j
                s = s + hist_ref[:, r * POOL_WIDTH + gi * POOL_GROUP_DIM:r * POOL_WIDTH + (gi + 1) * POOL_GROUP_DIM]
            cnt = float(min(PAST_LEN + 1, w))
            z = s / cnt - ug
            mixbuf[:, sl] = _dot(z.astype(BF16), wpool_ref[gi]) * pscale_ref[:, sl]
        pool_out_ref[:, 0:(POOL_HIST - 1) * POOL_WIDTH] = hist_ref[:, POOL_WIDTH:POOL_HIST * POOL_WIDTH]
        pool_out_ref[:, (POOL_HIST - 1) * POOL_WIDTH:POOL_HIST * POOL_WIDTH] = u

        val = _gate_values(g, gb_ref[...])
        lane = lax.broadcasted_iota(I32, (B, LANES), 1)
        qk_all = jnp.zeros((B, LANES), F32)
        sc_all = jnp.zeros((B, LANES), F32)
        den_all = jnp.zeros((B, LANES), F32)
        floor_all = jnp.zeros((B, LANES), F32)
        m_all = jnp.zeros((B, LANES), F32)
        for h in range(HEADS):
            hs = slice(h * HEAD_DIM, (h + 1) * HEAD_DIM)
            qf = proj[:, POOL_WIDTH + h * HEAD_DIM:POOL_WIDTH + (h + 1) * HEAD_DIM]
            kf = proj[:, 2 * POOL_WIDTH + h * HEAD_DIM:2 * POOL_WIDTH + (h + 1) * HEAD_DIM] * (HEAD_DIM ** -0.5)
            vf = proj[:, 3 * POOL_WIDTH + h * HEAD_DIM:3 * POOL_WIDTH + (h + 1) * HEAD_DIM]
            ig = val[:, h:h + 1]
            lf = val[:, HEADS + h:HEADS + h + 1]
            m0 = m_ref[:, h:h + 1]
            n0 = n_ref[:, hs]
            inter = m0 + lf
            m_t = jnp.maximum(inter, ig)
            dw = jnp.exp(ig - m_t)
            sc = jnp.exp(inter - m_t)
            qk = jnp.sum(qf * kf, axis=-1, keepdims=True) * dw
            den = qk + sc * jnp.sum(qf * n0, axis=-1, keepdims=True)
            n_out_ref[:, hs] = sc * n0 + dw * kf
            q_s[0:B, hs] = qf
            k_s[0:B, hs] = kf
            v_s[0:B, hs] = vf
            vw_s[0:B, hs] = vf * dw
            sel = lane == h
            qk_all = jnp.where(sel, qk, qk_all)
            sc_all = jnp.where(sel, sc, sc_all)
            den_all = jnp.where(sel, den, den_all)
            floor_all = jnp.where(sel, jnp.exp(-m_t), floor_all)
            m_all = jnp.where(lane == HEADS + h, m_t, m_all)
        o_s[...] = proj[:, 4 * POOL_WIDTH:5 * POOL_WIDTH]
        coef_s[0] = qk_all
        coef_s[1] = sc_all
        coef_s[2] = den_all
        coef_s[3] = floor_all
        m_out_ref[...] = m_all

    rows = pl.ds(pl.multiple_of(i * SAMPLE_BT, SAMPLE_BT), SAMPLE_BT)
    q_t, k_t, v_t, vw_t = q_s[rows, :], k_s[rows, :], v_s[rows, :], vw_s[rows, :]
    qk_t, sc_t, den_t, floor_t = coef_s[0, rows, :], coef_s[1, rows, :], coef_s[2, rows, :], coef_s[3, rows, :]
    h_rows = []
    for bl in range(SAMPLE_BT):
        heads = []
        for h in range(HEADS):
            hs = slice(h * HEAD_DIM, (h + 1) * HEAD_DIM)
            c_prev = c_ref[bl, h]
            q8 = jnp.broadcast_to(q_t[bl:bl + 1, hs], (SUBLANES, HEAD_DIM))
            cq = _dot_nt(q8.astype(BF16), c_prev.astype(BF16))[0:1, :]
            qk = qk_t[bl:bl + 1, h:h + 1]
            sc = sc_t[bl:bl + 1, h:h + 1]
            num = qk * v_t[bl:bl + 1, hs] + sc * cq
            heads.append(num / jnp.maximum(jnp.abs(den_t[bl:bl + 1, h:h + 1]), floor_t[bl:bl + 1, h:h + 1]))
            v_col = jnp.broadcast_to(vw_t[bl:bl + 1, hs], (HEAD_DIM, HEAD_DIM)).T
            c_out_ref[bl, h] = sc * c_prev + v_col * k_t[bl:bl + 1, hs]
        h_rows.append(jnp.concatenate(heads, axis=1))
    h_s[rows, :] = jnp.concatenate(h_rows, axis=0)

    @pl.when(i == nsteps - 1)
    def _():
        for h in range(HEADS):
            hs = slice(h * HEAD_DIM, (h + 1) * HEAD_DIM)
            mixbuf[:, POOL_WIDTH + h * HEAD_DIM:POOL_WIDTH + (h + 1) * HEAD_DIM] = _head_out(
                h_s[:, hs], o_s[:, hs], mhg_ref[:, hs])
        mix = _dot(mixbuf[...].astype(BF16), wout_ref[...])
        x1 = _layer_norm(DN_ALPHA * x_ref[...] + mix, ln1g_ref[...], ln1b_ref[...])
        x1_ref[0:B, :] = x1
        x1_ref[B:TOK_TILE, :] = jnp.zeros((TOK_TILE - B, D_MODEL), F32)


def _sample_mixer(x, hist2, c0, n0, m0, w_in_b, w_g_b, gbias, w_pool_b, pscale, mhg, w_out_b, ln1g, ln1b):
    B = DEC_BATCH
    steps = B // SAMPLE_BT
    full = lambda a: pl.BlockSpec(a.shape, lambda i: (0,) * a.ndim)
    c_spec = pl.BlockSpec((SAMPLE_BT, HEADS, HEAD_DIM, HEAD_DIM), lambda i: (i, 0, 0, 0))
    return pl.pallas_call(
        _sample_mixer_kernel,
        grid=(steps,),
        in_specs=[full(x), full(hist2), c_spec, full(n0), full(m0), full(w_in_b), full(w_g_b), full(gbias),
                  full(w_pool_b), full(pscale), full(mhg), full(w_out_b), full(ln1g), full(ln1b)],
        out_specs=[
            pl.BlockSpec((TOK_TILE, D_MODEL), lambda i: (0, 0)),
            pl.BlockSpec((B, POOL_HIST * POOL_WIDTH), lambda i: (0, 0)),
            c_spec,
            pl.BlockSpec((B, MLSTM_WIDTH), lambda i: (0, 0)),
            pl.BlockSpec((B, LANES), lambda i: (0, 0)),
        ],
        out_shape=[
            jax.ShapeDtypeStruct((TOK_TILE, D_MODEL), F32),
            jax.ShapeDtypeStruct((B, POOL_HIST * POOL_WIDTH), F32),
            jax.ShapeDtypeStruct((B, HEADS, HEAD_DIM, HEAD_DIM), F32),
            jax.ShapeDtypeStruct((B, MLSTM_WIDTH), F32),
            jax.ShapeDtypeStruct((B, LANES), F32),
        ],
        scratch_shapes=[
            pltpu.VMEM((B, MLSTM_WIDTH), F32),
            pltpu.VMEM((B, MLSTM_WIDTH), F32),
            pltpu.VMEM((B, MLSTM_WIDTH), F32),
            pltpu.VMEM((B, MLSTM_WIDTH), F32),
            pltpu.VMEM((B, MLSTM_WIDTH), F32),
            pltpu.VMEM((B, D_MODEL), F32),
            pltpu.VMEM((B, MLSTM_WIDTH), F32),
            pltpu.VMEM((4, B, LANES), F32),
        ],
        compiler_params=pltpu.CompilerParams(
            dimension_semantics=("arbitrary",), vmem_limit_bytes=VMEM_LIMIT),
        name="sample_mixer",
    )(x, hist2, c0, n0, m0, w_in_b, w_g_b, gbias, w_pool_b, pscale, mhg, w_out_b, ln1g, ln1b)


def _pick_tile(i, prompt_ref, sample_ref):
    return jnp.where(i < N_PROMPT_TILES, prompt_ref[...], sample_ref[...])


def _router_kernel(xp_ref, xs_ref, wrt_ref, br_ref, exp_ref, gate_ref, rank_ref, cnt_ref, carry):
    i = pl.program_id(0)
    T = TOK_TILE

    @pl.when(i == 0)
    def _():
        carry[...] = jnp.zeros_like(carry)

    x = _pick_tile(i, xp_ref, xs_ref)
    xh = x.astype(BF16)
    xl = (x - xh.astype(F32)).astype(BF16)
    w = wrt_ref[...]
    wh = w.astype(BF16)
    wl = (w - wh.astype(F32)).astype(BF16)
    logits = _dot_nt(wh, xh) + (_dot_nt(wh, xl) + _dot_nt(wl, xh)) + br_ref[:, 0:1]

    n_valid = jnp.where(i < N_PROMPT_TILES, T, DEC_BATCH)
    valid = lax.broadcasted_iota(I32, (1, T), 1) < n_valid
    erow = lax.broadcasted_iota(I32, (N_EXPERTS, T), 0).astype(F32)
    work = logits
    vals, idxs, sels = [], [], []
    for _ in range(TOP_K):
        mx = jnp.max(work, axis=0, keepdims=True)
        idx = jnp.min(jnp.where(work == mx, erow, float(N_EXPERTS)), axis=0, keepdims=True)
        sel = erow == idx
        work = jnp.where(sel, -jnp.inf, work)
        vals.append(mx)
        idxs.append(idx.astype(I32))
        sels.append(sel)
    chosen = jnp.logical_or(jnp.logical_or(sels[0], sels[1]), jnp.logical_or(sels[2], sels[3]))
    es = [jnp.exp(v - vals[0]) for v in vals]
    tot = es[0] + es[1] + es[2] + es[3]

    onehot = jnp.where(jnp.logical_and(chosen, valid), 1.0, 0.0)
    trow = lax.broadcasted_iota(I32, (T, T), 0)
    tcol = lax.broadcasted_iota(I32, (T, T), 1)
    before = jnp.where(trow < tcol, 1.0, 0.0).astype(BF16)
    running = _dot(onehot.astype(BF16), before) + carry[:, 0:1]
    carry[...] = carry[...] + jnp.sum(onehot, axis=1, keepdims=True)
    cnt_ref[...] = carry[...]

    r8 = lax.broadcasted_iota(I32, (SUBLANES, T), 0)
    e_out = jnp.zeros((SUBLANES, T), I32)
    g_out = jnp.zeros((SUBLANES, T), F32)
    k_out = jnp.zeros((SUBLANES, T), I32)
    for j in range(TOP_K):
        rank_j = jnp.sum(jnp.where(sels[j], running, 0.0), axis=0, keepdims=True).astype(I32)
        e_out = jnp.where(r8 == j, idxs[j], e_out)
        g_out = jnp.where(r8 == j, es[j] / tot, g_out)
        k_out = jnp.where(r8 == j, rank_j, k_out)
    exp_ref[0] = e_out
    gate_ref[0] = g_out
    rank_ref[0] = k_out


def _router(x1p, x1s, w_router_t, b_router_col):
    tile_spec = pl.BlockSpec((1, SUBLANES, TOK_TILE), lambda i: (i, 0, 0))
    return pl.pallas_call(
        _router_kernel,
        grid=(N_TILES,),
        in_specs=[
            pl.BlockSpec((TOK_TILE, D_MODEL), lambda i: (jnp.minimum(i, N_PROMPT_TILES - 1), 0)),
            pl.BlockSpec((TOK_TILE, D_MODEL), lambda i: (0, 0)),
            pl.BlockSpec(w_router_t.shape, lambda i: (0, 0)),
            pl.BlockSpec(b_router_col.shape, lambda i: (0, 0)),
        ],
        out_specs=[tile_spec, tile_spec, tile_spec,
                   pl.BlockSpec((N_EXPERTS, LANES), lambda i: (0, 0))],
        out_shape=[
            jax.ShapeDtypeStruct((N_TILES, SUBLANES, TOK_TILE), I32),
            jax.ShapeDtypeStruct((N_TILES, SUBLANES, TOK_TILE), F32),
            jax.ShapeDtypeStruct((N_TILES, SUBLANES, TOK_TILE), I32),
            jax.ShapeDtypeStruct((N_EXPERTS, LANES), F32),
        ],
        scratch_shapes=[pltpu.VMEM((N_EXPERTS, LANES), F32)],
        compiler_params=pltpu.CompilerParams(
            dimension_semantics=("arbitrary",), vmem_limit_bytes=VMEM_LIMIT),
        name="router",
    )(x1p, x1s, w_router_t, b_router_col)


def _slot_kernel(gstart_ref, exp_ref, rank_ref, dest_ref):
    e = exp_ref[...]
    acc = rank_ref[...]
    for k in range(N_EXPERTS):
        acc = acc + jnp.where(e == k, gstart_ref[k], 0)
    dest_ref[...] = acc


def _slots(group_start, top_e, rank):
    return pl.pallas_call(
        _slot_kernel,
        in_specs=[pl.BlockSpec(memory_space=pltpu.SMEM),
                  pl.BlockSpec(memory_space=pltpu.VMEM),
                  pl.BlockSpec(memory_space=pltpu.VMEM)],
        out_specs=pl.BlockSpec(memory_space=pltpu.VMEM),
        out_shape=jax.ShapeDtypeStruct(top_e.shape, I32),
        name="slots",
    )(group_start, top_e, rank)


ROW_TILE = D_MODEL // LANES


def _to_row_tiles(dst_ref, x):
    for s in range(ROW_TILE):
        dst_ref[pl.ds(s, x.shape[0], stride=ROW_TILE), :] = x[:, s * LANES:(s + 1) * LANES]


def _from_row_tiles(src_ref, n_rows):
    return jnp.concatenate(
        [src_ref[pl.ds(s, n_rows, stride=ROW_TILE), :] for s in range(ROW_TILE)], axis=1)


def _tile_rows(r):
    return pl.ds(pl.multiple_of(r * ROW_TILE, ROW_TILE), ROW_TILE)


def _dispatch_kernel(ztail_ref, nused_ref, dest_ref, xp_ref, xs_ref, slots_hbm, stage, zbuf, sem, zsem):
    i = pl.program_id(0)

    def zero_block(row_start):
        start = pl.multiple_of(row_start * ROW_TILE, ROW_TILE)
        return pltpu.make_async_copy(zbuf, slots_hbm.at[pl.ds(start, ROW_BLOCK * ROW_TILE), :], zsem)

    @pl.when(i == 0)
    def _():
        zbuf[...] = jnp.zeros_like(zbuf)
        for e in range(N_EXPERTS):
            @pl.when(ztail_ref[e] >= 0)
            def _():
                zero_block(ztail_ref[e]).start()

        def start_tail(b, carry):
            zero_block(b * ROW_BLOCK).start()
            return carry

        def wait_tail(b, carry):
            zero_block(b * ROW_BLOCK).wait()
            return carry

        lax.fori_loop(nused_ref[0], N_BLOCKS, start_tail, 0)
        for e in range(N_EXPERTS):
            @pl.when(ztail_ref[e] >= 0)
            def _():
                zero_block(ztail_ref[e]).wait()
        lax.fori_loop(nused_ref[0], N_BLOCKS, wait_tail, 0)

    _to_row_tiles(stage, _pick_tile(i, xp_ref, xs_ref))
    n_valid = jnp.where(i < N_PROMPT_TILES, TOK_TILE, DEC_BATCH)

    def row_copy(t, j):
        return pltpu.make_async_copy(stage.at[_tile_rows(t), :],
                                     slots_hbm.at[_tile_rows(dest_ref[0, j, t]), :], sem)

    def issue(t, carry):
        for j in range(TOP_K):
            row_copy(t, j).start()
        return carry

    def drain(t, carry):
        for j in range(TOP_K):
            row_copy(t, j).wait()
        return carry

    lax.fori_loop(0, n_valid, issue, 0)
    lax.fori_loop(0, n_valid, drain, 0)


def _dispatch(zero_tail, n_used, dest, x1p, x1s):
    grid_spec = pltpu.PrefetchScalarGridSpec(
        num_scalar_prefetch=2,
        grid=(N_TILES,),
        in_specs=[
            pl.BlockSpec((1, SUBLANES, TOK_TILE), lambda i, z, n: (i, 0, 0), memory_space=pltpu.SMEM),
            pl.BlockSpec((TOK_TILE, D_MODEL), lambda i, z, n: (jnp.minimum(i, N_PROMPT_TILES - 1), 0)),
            pl.BlockSpec((TOK_TILE, D_MODEL), lambda i, z, n: (0, 0)),
        ],
        out_specs=pl.BlockSpec(memory_space=pl.ANY),
        scratch_shapes=[
            pltpu.VMEM((TOK_TILE * ROW_TILE, LANES), F32),
            pltpu.VMEM((ROW_BLOCK * ROW_TILE, LANES), F32),
            pltpu.SemaphoreType.DMA(()),
            pltpu.SemaphoreType.DMA(()),
        ],
    )
    return pl.pallas_call(
        _dispatch_kernel,
        grid_spec=grid_spec,
        out_shape=jax.ShapeDtypeStruct((N_SLOTS * ROW_TILE, LANES), F32),
        compiler_params=pltpu.CompilerParams(
            dimension_semantics=("arbitrary",), vmem_limit_bytes=VMEM_LIMIT),
        name="dispatch",
    )(zero_tail, n_used, dest, x1p, x1s)


MXU_COLS = 256


def _expert_kernel(bexp_ref, first_ref, next_ref, nused_ref, xs_ref, w1_hbm, b1_ref, w2_hbm, b2_ref, out_ref,
                   w1_stage, w2_stage, w1_b, w2_b, sem):
    i = pl.program_id(0)

    def fetch(e):
        return (pltpu.make_async_copy(w1_hbm.at[e], w1_stage, sem.at[0]),
                pltpu.make_async_copy(w2_hbm.at[e], w2_stage, sem.at[1]))

    @pl.when(i == 0)
    def _():
        for cp in fetch(bexp_ref[0]):
            cp.start()

    @pl.when(first_ref[i] == 1)
    def _():
        for cp in fetch(bexp_ref[i]):
            cp.wait()
        half = MXU_COLS // 2
        k = lax.broadcasted_iota(I32, (MXU_COLS, MXU_COLS), 0)
        j = lax.broadcasted_iota(I32, (MXU_COLS, MXU_COLS), 1)
        src = jnp.where(j < half, 2 * j, 2 * (j - half) + 1)
        perm = jnp.where(k == src, 1.0, 0.0).astype(BF16)
        for c in range(2 * D_FF // MXU_COLS):
            blk = w1_stage[:, c * MXU_COLS:(c + 1) * MXU_COLS].astype(BF16)
            sep = _dot(blk, perm).astype(BF16)
            w1_b[:, c * half:(c + 1) * half] = sep[:, 0:half]
            w1_b[:, D_FF + c * half:D_FF + (c + 1) * half] = sep[:, half:MXU_COLS]
        w2_b[...] = w2_stage[...].astype(BF16)

        @pl.when(next_ref[i] >= 0)
        def _():
            for cp in fetch(next_ref[i]):
                cp.start()

    @pl.when(i < nused_ref[0])
    def _():
        x = _from_row_tiles(xs_ref, ROW_BLOCK)
        h = _dot(x.astype(BF16), w1_b[...]) + b1_ref[0]
        glu = jnp.minimum(h[:, 0:D_FF], SWIGLU_LIMIT)
        lin = jnp.clip(h[:, D_FF:2 * D_FF], -SWIGLU_LIMIT, SWIGLU_LIMIT)
        a = glu * jax.nn.sigmoid(SWIGLU_ALPHA * glu) * (lin + 1.0)
        _to_row_tiles(out_ref, _dot(a.astype(BF16), w2_b[...]) + b2_ref[0])

    @pl.when(i >= nused_ref[0])
    def _():
        out_ref[...] = jnp.zeros_like(out_ref)


def _experts(block_expert, block_first, block_next, n_used, slots, w1, b1p, w2, b2):
    row_idx = lambda i, be, bf, bn, nu: (jnp.minimum(i, nu[0] - 1), 0)
    exp_idx = lambda i, be, bf, bn, nu: (be[i], 0, 0)
    grid_spec = pltpu.PrefetchScalarGridSpec(
        num_scalar_prefetch=4,
        grid=(N_BLOCKS,),
        in_specs=[
            pl.BlockSpec((ROW_BLOCK * ROW_TILE, LANES), row_idx),
            pl.BlockSpec(memory_space=pl.ANY),
            pl.BlockSpec((1, 1, 2 * D_FF), exp_idx),
            pl.BlockSpec(memory_space=pl.ANY),
            pl.BlockSpec((1, 1, D_MODEL), exp_idx),
        ],
        out_specs=pl.BlockSpec((ROW_BLOCK * ROW_TILE, LANES), lambda i, be, bf, bn, nu: (i, 0)),
        scratch_shapes=[
            pltpu.VMEM((D_MODEL, 2 * D_FF), F32),
            pltpu.VMEM((D_FF, D_MODEL), F32),
            pltpu.VMEM((D_MODEL, 2 * D_FF), BF16),
            pltpu.VMEM((D_FF, D_MODEL), BF16),
            pltpu.SemaphoreType.DMA((2,)),
        ],
    )
    return pl.pallas_call(
        _expert_kernel,
        grid_spec=grid_spec,
        out_shape=jax.ShapeDtypeStruct((N_SLOTS * ROW_TILE, LANES), F32),
        compiler_params=pltpu.CompilerParams(
            dimension_semantics=("arbitrary",), vmem_limit_bytes=VMEM_LIMIT),
        name="experts",
    )(block_expert, block_first, block_next, n_used, slots, w1, b1p, w2, b2)


def _combine_kernel(dest_ref, gate_ref, xp_ref, xs_ref, pp_ref, ps_ref, eo_hbm, ln2g_ref, ln2b_ref,
                    wpg_ref, wple_ref, yp_ref, ys_ref, gbuf, sem):
    i = pl.program_id(0)
    T = TOK_TILE
    n_valid = jnp.where(i < N_PROMPT_TILES, T, DEC_BATCH)

    def row_copy(j, t):
        return pltpu.make_async_copy(eo_hbm.at[_tile_rows(dest_ref[0, j, t]), :],
                                     gbuf.at[j, _tile_rows(t), :], sem)

    def issue(t, carry):
        for j in range(TOP_K):
            row_copy(j, t).start()
        return carry

    def drain(t, carry):
        for j in range(TOP_K):
            row_copy(j, t).wait()
        return carry

    @pl.when(i == N_PROMPT_TILES)
    def _():
        gbuf[...] = jnp.zeros_like(gbuf)

    lax.fori_loop(0, n_valid, issue, 0)
    lax.fori_loop(0, n_valid, drain, 0)

    x1 = _pick_tile(i, xp_ref, xs_ref)
    gates_t = jnp.concatenate([gate_ref[0], jnp.zeros((LANES - SUBLANES, T), F32)], axis=0).T
    ff = gates_t[:, 0:1] * _from_row_tiles(gbuf.at[0], T)
    for j in range(1, TOP_K):
        ff = ff + gates_t[:, j:j + 1] * _from_row_tiles(gbuf.at[j], T)
    x2 = _layer_norm(DN_ALPHA * x1 + ff, ln2g_ref[...], ln2b_ref[...])
    p = _pick_tile(i, pp_ref, ps_ref)
    y = x2 + jax.nn.sigmoid(_dot(x2.astype(BF16), wpg_ref[...])) * _dot(p.astype(BF16), wple_ref[...])

    @pl.when(i < N_PROMPT_TILES)
    def _():
        yp_ref[...] = y

    @pl.when(i == N_PROMPT_TILES)
    def _():
        ys_ref[...] = y[0:DEC_BATCH, :]


def _combine(dest, gates, x1p, x1s, pp, ps, expert_out, ln2g, ln2b, w_pg_b, w_ple_b):
    tile_idx = lambda i: (jnp.minimum(i, N_PROMPT_TILES - 1), 0)
    const2 = lambda i: (0, 0)
    return pl.pallas_call(
        _combine_kernel,
        grid=(N_TILES,),
        in_specs=[
            pl.BlockSpec((1, SUBLANES, TOK_TILE), lambda i: (i, 0, 0), memory_space=pltpu.SMEM),
            pl.BlockSpec((1, SUBLANES, TOK_TILE), lambda i: (i, 0, 0)),
            pl.BlockSpec((TOK_TILE, D_MODEL), tile_idx),
            pl.BlockSpec((TOK_TILE, D_MODEL), const2),
            pl.BlockSpec((TOK_TILE, PLE_DIM), tile_idx),
            pl.BlockSpec((TOK_TILE, PLE_DIM), const2),
            pl.BlockSpec(memory_space=pl.ANY),
            pl.BlockSpec(ln2g.shape, const2),
            pl.BlockSpec(ln2b.shape, const2),
            pl.BlockSpec(w_pg_b.shape, const2),
            pl.BlockSpec(w_ple_b.shape, const2),
        ],
        out_specs=[
            pl.BlockSpec((TOK_TILE, D_MODEL), tile_idx),
            pl.BlockSpec((DEC_BATCH, D_MODEL), const2),
        ],
        out_shape=[
            jax.ShapeDtypeStruct((N_PROMPT, D_MODEL), F32),
            jax.ShapeDtypeStruct((DEC_BATCH, D_MODEL), F32),
        ],
        scratch_shapes=[
            pltpu.VMEM((TOP_K, TOK_TILE * ROW_TILE, LANES), F32),
            pltpu.SemaphoreType.DMA(()),
        ],
        compiler_params=pltpu.CompilerParams(
            dimension_semantics=("arbitrary",), vmem_limit_bytes=VMEM_LIMIT),
        name="combine",
    )(dest, gates, x1p, x1s, pp, ps, expert_out, ln2g, ln2b, w_pg_b, w_ple_b)


def kernel(x_prompt, x_sample, state_pool, state_mlstm_C, state_mlstm_n, state_mlstm_m, p_prompt, p_sample, w_in, b_i, b_f, w_pool, pool_scale, mh_g, w_out, ln1_g, ln1_b, w_router, b_router, w_mlp1, b_mlp1, w_mlp2, b_mlp2, ln2_g, ln2_b, w_ple, w_ple_gate):
    n_main = POOL_WIDTH + 4 * MLSTM_WIDTH
    w_in_b = w_in[0, :, 0:n_main].astype(BF16)
    w_g_b = jnp.pad(w_in[0, :, n_main:], ((0, 0), (0, LANES - 2 * HEADS))).astype(BF16)
    gbias = jnp.pad(jnp.concatenate([b_i[0], b_f[0]]), (0, LANES - 2 * HEADS)).reshape(1, LANES)
    w_pool_b = w_pool[0].astype(BF16)
    pscale = pool_scale[0].reshape(1, POOL_WIDTH)
    mhg = mh_g[0].reshape(1, MLSTM_WIDTH)
    w_out_b = w_out[0].astype(BF16)
    ln1g = ln1_g[0].reshape(1, D_MODEL)
    ln1b = ln1_b[0].reshape(1, D_MODEL)
    ln2g = ln2_g[0].reshape(1, D_MODEL)
    ln2b = ln2_b[0].reshape(1, D_MODEL)
    w_router_t = w_router[0].T
    b_router_col = jnp.broadcast_to(b_router[0].reshape(N_EXPERTS, 1), (N_EXPERTS, LANES))
    b1 = b_mlp1[0]
    b1p = jnp.concatenate([b1[:, 0::2], b1[:, 1::2]], axis=-1).reshape(N_EXPERTS, 1, 2 * D_FF)
    b2 = b_mlp2[0].reshape(N_EXPERTS, 1, D_MODEL)
    w_pg_b = w_ple_gate[0].astype(BF16)
    w_ple_b = w_ple[0].astype(BF16)

    x1p, pool_p, c_p, n_p, m_p = _prompt_mixer(
        x_prompt, w_in_b, w_g_b, gbias, w_pool_b, pscale, mhg, w_out_b, ln1g, ln1b)
    x1s, pool_s, c_s, n_s, m_s = _sample_mixer(
        x_sample.reshape(DEC_BATCH, D_MODEL),
        state_pool[0].reshape(DEC_BATCH, POOL_HIST * POOL_WIDTH),
        state_mlstm_C[0], state_mlstm_n[0].reshape(DEC_BATCH, MLSTM_WIDTH), state_mlstm_m[0],
        w_in_b, w_g_b, gbias, w_pool_b, pscale, mhg, w_out_b, ln1g, ln1b)

    top_e, gates, rank, counts = _router(x1p, x1s, w_router_t, b_router_col)
    cnt = counts[:, 0].astype(I32)
    nblk = (cnt + ROW_BLOCK - 1) // ROW_BLOCK
    blk_end = jnp.cumsum(nblk)
    group_start = (blk_end - nblk) * ROW_BLOCK
    n_used = blk_end[-1:].astype(I32)
    blk_ids = jnp.arange(N_BLOCKS, dtype=I32)

    def expert_of(b):
        b = jnp.minimum(b, n_used[0] - 1)
        return jnp.minimum(jnp.sum(blk_end[None, :] <= b[:, None], axis=1), N_EXPERTS - 1).astype(I32)

    block_expert = expert_of(blk_ids)
    in_use = blk_ids < n_used[0]
    is_first = jnp.logical_or(blk_ids == 0, block_expert != expert_of(blk_ids - 1))
    block_first = jnp.logical_and(in_use, is_first).astype(I32)
    next_start = blk_end[block_expert]
    block_next = jnp.where(next_start < n_used[0], expert_of(next_start), -1).astype(I32)
    zero_tail = jnp.where(cnt > 0, blk_end * ROW_BLOCK - ROW_BLOCK, -1).astype(I32)
    dest = _slots(group_start.astype(I32), top_e, rank)

    slots = _dispatch(zero_tail, n_used, dest, x1p, x1s)
    expert_out = _experts(block_expert, block_first, block_next, n_used, slots, w_mlp1[0], b1p, w_mlp2[0], b2)

    pp = p_prompt[0].reshape(N_PROMPT, PLE_DIM)
    ps = jnp.pad(p_sample[0].reshape(DEC_BATCH, PLE_DIM), ((0, TOK_TILE - DEC_BATCH), (0, 0)))
    yp, ys = _combine(dest, gates, x1p, x1s, pp, ps, expert_out, ln2g, ln2b, w_pg_b, w_ple_b)

    return (
        yp.reshape(BATCH, SEQ, D_MODEL),
        ys.reshape(DEC_BATCH, 1, D_MODEL),
        pool_p.reshape(1, BATCH, POOL_HIST, POOL_WIDTH),
        c_p.reshape(1, BATCH, HEADS, HEAD_DIM, HEAD_DIM),
        n_p.reshape(1, BATCH, HEADS, HEAD_DIM),
        m_p[:, 0:HEADS, 0].reshape(1, BATCH, HEADS),
        pool_s.reshape(1, DEC_BATCH, POOL_HIST, POOL_WIDTH),
        c_s.reshape(1, DEC_BATCH, HEADS, HEAD_DIM, HEAD_DIM),
        n_s.reshape(1, DEC_BATCH, HEADS, HEAD_DIM),
        m_s[:, HEADS:2 * HEADS].reshape(1, DEC_BATCH, HEADS),
    )
```

```python
import jax
import jax.numpy as jnp
from jax import lax
from jax.experimental import pallas as pl
from jax.experimental.pallas import tpu as pltpu

F32 = jnp.float32
BF16 = jnp.bfloat16
I32 = jnp.int32

D_MODEL = 1024
BATCH = 8
SEQ = 2048
DEC_BATCH = 128
PAST_LEN = 16384
POOL_WIDTH = 512
POOL_GROUPS = 4
POOL_GROUP_DIM = 128
POOL_WINDOWS = (2, 4, 8, 16)
POOL_HIST = 15
MLSTM_WIDTH = 512
HEADS = 4
HEAD_DIM = 128
CHUNK = 128
N_EXPERTS = 32
TOP_K = 4
D_FF = 1024
SWIGLU_ALPHA = 1.702
SWIGLU_LIMIT = 7.0
PLE_DIM = 256
DN_ALPHA = 2.0 ** 0.25
LN_EPS = 1e-5

LANES = 128
SUBLANES = 8
BF16_ROWS = 16
MXU_COLS = 256
VMEM_LIMIT = 56 * 1024 * 1024

MIX_TILE = 256
HIST_PAD = 16
TOK_TILE = 512
N_PROMPT = BATCH * SEQ
N_PROMPT_TILES = N_PROMPT // TOK_TILE
N_TILES = N_PROMPT_TILES + 1
SAMPLE_BT = 16

CHUNK_ROWS = BF16_ROWS
LOCAL_ROWS = TOK_TILE * TOP_K + N_EXPERTS * CHUNK_ROWS
GROUPS = LOCAL_ROWS // TOK_TILE
ROW_BLOCK = 256
BLOCK_CHUNKS = ROW_BLOCK // CHUNK_ROWS
MAX_CHUNKS = N_TILES * (TOK_TILE * TOP_K // CHUNK_ROWS + N_EXPERTS)
N_BLOCKS = -(-MAX_CHUNKS // BLOCK_CHUNKS) + N_EXPERTS
ZERO_CHUNK_ROW = LOCAL_ROWS - CHUNK_ROWS
FREE_ROWS = LOCAL_ROWS - TOK_TILE * TOP_K


def _dot(a, b):
    return jnp.dot(a, b, preferred_element_type=F32)


def _dot_nt(a, b):
    return lax.dot_general(a, b, (((1,), (1,)), ((), ())), preferred_element_type=F32)


def _dot_tn(a, b):
    return lax.dot_general(a, b, (((0,), (0,)), ((), ())), preferred_element_type=F32)


def _split3(a):
    a0 = a.astype(BF16)
    r1 = a - a0.astype(F32)
    a1 = r1.astype(BF16)
    r2 = r1 - a1.astype(F32)
    return a0, a1, r2.astype(BF16)


def _log_sigmoid(x):
    return jnp.minimum(x, 0.0) - jnp.log1p(jnp.exp(-jnp.abs(x)))


def _layer_norm(x, g, b):
    mu = jnp.mean(x, axis=-1, keepdims=True)
    xc = x - mu
    var = jnp.mean(xc * xc, axis=-1, keepdims=True)
    return xc * lax.rsqrt(var + LN_EPS) * g + b


def _gate_values(g, gbias):
    lane = lax.broadcasted_iota(I32, g.shape, 1)
    z = g + gbias
    return jnp.where(lane < HEADS, z, _log_sigmoid(z))


def _head_out(hh, o_h, gain):
    mu = jnp.mean(hh, axis=-1, keepdims=True)
    hc = hh - mu
    var = jnp.mean(hc * hc, axis=-1, keepdims=True)
    return jax.nn.sigmoid(o_h) * (hc * lax.rsqrt(var + LN_EPS) * gain)


def _prompt_mixer_kernel(x_ref, win_ref, wg_ref, gb_ref, wpool_ref, pscale_ref, mhg_ref, wout_ref,
                         ln1g_ref, ln1b_ref,
                         x1_ref, pool_ref, c_out_ref, n_out_ref, m_out_ref,
                         ubuf, mixbuf, c_s, n_s, m_s):
    ti = pl.program_id(1)
    nt = pl.num_programs(1)
    TT = MIX_TILE

    @pl.when(ti == 0)
    def _():
        ubuf[0:HIST_PAD, :] = jnp.zeros((HIST_PAD, POOL_WIDTH), F32)
        c_s[...] = jnp.zeros_like(c_s)
        n_s[...] = jnp.zeros_like(n_s)
        m_s[...] = jnp.zeros_like(m_s)

    x = x_ref[0]
    xb = x.astype(BF16)
    proj = _dot(xb, win_ref[...])
    g = _dot(xb, wg_ref[...])
    u = proj[:, 0:POOL_WIDTH]

    ubuf[HIST_PAD:HIST_PAD + TT, :] = u
    pos = ti * TT + lax.broadcasted_iota(I32, (TT, 1), 0)
    for gi, w in enumerate(POOL_WINDOWS):
        sl = slice(gi * POOL_GROUP_DIM, (gi + 1) * POOL_GROUP_DIM)
        ug = u[:, sl]
        s = ug
        for i in range(1, w):
            s = s + ubuf[HIST_PAD - i:HIST_PAD - i + TT, sl]
        cnt = jnp.minimum(pos + 1, w).astype(F32)
        z = s / cnt - ug
        mixbuf[:, sl] = _dot(z.astype(BF16), wpool_ref[gi]) * pscale_ref[:, sl]

    @pl.when(ti == nt - 1)
    def _():
        pool_ref[0] = ubuf[TT + 1:TT + HIST_PAD, :]

    ubuf[0:HIST_PAD, :] = ubuf[TT:TT + HIST_PAD, :]

    L = CHUNK
    row = lax.broadcasted_iota(I32, (L, L), 0)
    col = lax.broadcasted_iota(I32, (L, L), 1)
    causal = row >= col
    tril = jnp.where(causal, 1.0, 0.0).astype(BF16)
    for c in range(TT // L):
        rs = slice(c * L, (c + 1) * L)
        val = _gate_values(g[rs, :], gb_ref[...])
        v0, v1, v2 = _split3(val)
        cum = _dot(tril, v0) + _dot(tril, v1) + _dot(tril, v2)
        val_t = val.T
        cum_t = cum.T
        for h in range(HEADS):
            hs = slice(h * HEAD_DIM, (h + 1) * HEAD_DIM)
            qf = proj[rs, POOL_WIDTH + h * HEAD_DIM:POOL_WIDTH + (h + 1) * HEAD_DIM]
            kf = proj[rs, 2 * POOL_WIDTH + h * HEAD_DIM:2 * POOL_WIDTH + (h + 1) * HEAD_DIM] * (HEAD_DIM ** -0.5)
            vf = proj[rs, 3 * POOL_WIDTH + h * HEAD_DIM:3 * POOL_WIDTH + (h + 1) * HEAD_DIM]
            of = proj[rs, 4 * POOL_WIDTH + h * HEAD_DIM:4 * POOL_WIDTH + (h + 1) * HEAD_DIM]
            qb = qf.astype(BF16)
            kb = kf.astype(BF16)
            f_col = cum[:, HEADS + h:HEADS + h + 1]
            f_row = cum_t[HEADS + h:HEADS + h + 1, :]
            ig_row = val_t[h:h + 1, :]
            ig_col = val[:, h:h + 1]
            m_prev = m_s[h:h + 1, 0:1]
            c_prev = c_s[h]
            n_prev = n_s[h:h + 1, :]

            log_d = jnp.where(causal, f_col - f_row + ig_row, -jnp.inf)
            inter = m_prev + f_col
            m_t = jnp.maximum(inter, jnp.max(log_d, axis=-1, keepdims=True))
            dw = jnp.exp(log_d - m_t)
            sc = jnp.exp(inter - m_t)
            qk = _dot_nt(qb, kb) * dw
            num = _dot(qk.astype(BF16), vf.astype(BF16)) + sc * _dot_nt(qb, c_prev.astype(BF16))
            den = jnp.sum(qk, axis=-1, keepdims=True) + sc * jnp.sum(qf * n_prev, axis=-1, keepdims=True)
            hh = num / jnp.maximum(jnp.abs(den), jnp.exp(-m_t))

            m_new = m_t[L - 1:L, :]
            f_last = f_col[L - 1:L, :]
            wk = jnp.exp(ig_col + f_last - f_col - m_new)
            decay = jnp.exp(m_prev + f_last - m_new)
            c_s[h] = decay * c_prev + _dot_tn((vf * wk).astype(BF16), kb)
            n_s[h:h + 1, :] = decay * n_prev + jnp.sum(wk * kf, axis=0, keepdims=True)
            m_s[h:h + 1, :] = jnp.broadcast_to(m_new, (1, LANES))

            mixbuf[rs, POOL_WIDTH + h * HEAD_DIM:POOL_WIDTH + (h + 1) * HEAD_DIM] = _head_out(
                hh, of, mhg_ref[:, hs])

    @pl.when(ti == nt - 1)
    def _():
        c_out_ref[0] = c_s[...]
        n_out_ref[0] = n_s[0:HEADS, :]
        m_out_ref[0] = m_s[...]

    mix = _dot(mixbuf[...].astype(BF16), wout_ref[...])
    x1_ref[...] = _layer_norm(DN_ALPHA * x + mix, ln1g_ref[...], ln1b_ref[...])


def _prompt_mixer(x, w_in_b, w_g_b, gbias, w_pool_b, pscale, mhg, w_out_b, ln1g, ln1b):
    nt = SEQ // MIX_TILE
    const2 = lambda b, t: (0, 0)
    const3 = lambda b, t: (0, 0, 0)
    return pl.pallas_call(
        _prompt_mixer_kernel,
        grid=(BATCH, nt),
        in_specs=[
            pl.BlockSpec((1, MIX_TILE, D_MODEL), lambda b, t: (b, t, 0)),
            pl.BlockSpec(w_in_b.shape, const2),
            pl.BlockSpec(w_g_b.shape, const2),
            pl.BlockSpec(gbias.shape, const2),
            pl.BlockSpec(w_pool_b.shape, const3),
            pl.BlockSpec(pscale.shape, const2),
            pl.BlockSpec(mhg.shape, const2),
            pl.BlockSpec(w_out_b.shape, const2),
            pl.BlockSpec(ln1g.shape, const2),
            pl.BlockSpec(ln1b.shape, const2),
        ],
        out_specs=[
            pl.BlockSpec((MIX_TILE, D_MODEL), lambda b, t: (b * nt + t, 0)),
            pl.BlockSpec((1, POOL_HIST, POOL_WIDTH), lambda b, t: (b, 0, 0)),
            pl.BlockSpec((1, HEADS, HEAD_DIM, HEAD_DIM), lambda b, t: (b, 0, 0, 0)),
            pl.BlockSpec((1, HEADS, HEAD_DIM), lambda b, t: (b, 0, 0)),
            pl.BlockSpec((1, SUBLANES, LANES), lambda b, t: (b, 0, 0)),
        ],
        out_shape=[
            jax.ShapeDtypeStruct((N_PROMPT, D_MODEL), F32),
            jax.ShapeDtypeStruct((BATCH, POOL_HIST, POOL_WIDTH), F32),
            jax.ShapeDtypeStruct((BATCH, HEADS, HEAD_DIM, HEAD_DIM), F32),
            jax.ShapeDtypeStruct((BATCH, HEADS, HEAD_DIM), F32),
            jax.ShapeDtypeStruct((BATCH, SUBLANES, LANES), F32),
        ],
        scratch_shapes=[
            pltpu.VMEM((HIST_PAD + MIX_TILE, POOL_WIDTH), F32),
            pltpu.VMEM((MIX_TILE, D_MODEL), F32),
            pltpu.VMEM((HEADS, HEAD_DIM, HEAD_DIM), F32),
            pltpu.VMEM((SUBLANES, HEAD_DIM), F32),
            pltpu.VMEM((SUBLANES, LANES), F32),
        ],
        compiler_params=pltpu.CompilerParams(
            dimension_semantics=("arbitrary", "arbitrary"), vmem_limit_bytes=VMEM_LIMIT),
        name="prompt_mixer",
    )(x, w_in_b, w_g_b, gbias, w_pool_b, pscale, mhg, w_out_b, ln1g, ln1b)


def _sample_mixer_kernel(x_ref, hist_ref, c_ref, n_ref, m_ref, win_ref, wg_ref, gb_ref, wpool_ref,
                         pscale_ref, mhg_ref, wout_ref, ln1g_ref, ln1b_ref,
                         x1_ref, pool_out_ref, c_out_ref, n_out_ref, m_out_ref,
                         q_s, k_s, vw_s, v_s, o_s, mixbuf, h_s, coef_s):
    i = pl.program_id(0)
    nsteps = pl.num_programs(0)
    B = DEC_BATCH

    @pl.when(i == 0)
    def _():
        x = x_ref[...]
        xb = x.astype(BF16)
        proj = _dot(xb, win_ref[...])
        g = _dot(xb, wg_ref[...])
        u = proj[:, 0:POOL_WIDTH]
        for gi, w in enumerate(POOL_WINDOWS):
            sl = slice(gi * POOL_GROUP_DIM, (gi + 1) * POOL_GROUP_DIM)
            ug = u[:, sl]
            s = ug
            for j in range(1, w):
                r = POOL_HIST - j
                s = s + hist_ref[:, r * POOL_WIDTH + gi * POOL_GROUP_DIM:r * POOL_WIDTH + (gi + 1) * POOL_GROUP_DIM]
            cnt = float(min(PAST_LEN + 1, w))
            z = s / cnt - ug
            mixbuf[:, sl] = _dot(z.astype(BF16), wpool_ref[gi]) * pscale_ref[:, sl]
        pool_out_ref[:, 0:(POOL_HIST - 1) * POOL_WIDTH] = hist_ref[:, POOL_WIDTH:POOL_HIST * POOL_WIDTH]
        pool_out_ref[:, (POOL_HIST - 1) * POOL_WIDTH:POOL_HIST * POOL_WIDTH] = u

        val = _gate_values(g, gb_ref[...])
        lane = lax.broadcasted_iota(I32, (B, LANES), 1)
        qk_all = jnp.zeros((B, LANES), F32)
        sc_all = jnp.zeros((B, LANES), F32)
        den_all = jnp.zeros((B, LANES), F32)
        floor_all = jnp.zeros((B, LANES), F32)
        m_all = jnp.zeros((B, LANES), F32)
        for h in range(HEADS):
            hs = slice(h * HEAD_DIM, (h + 1) * HEAD_DIM)
            qf = proj[:, POOL_WIDTH + h * HEAD_DIM:POOL_WIDTH + (h + 1) * HEAD_DIM]
            kf = proj[:, 2 * POOL_WIDTH + h * HEAD_DIM:2 * POOL_WIDTH + (h + 1) * HEAD_DIM] * (HEAD_DIM ** -0.5)
            vf = proj[:, 3 * POOL_WIDTH + h * HEAD_DIM:3 * POOL_WIDTH + (h + 1) * HEAD_DIM]
            ig = val[:, h:h + 1]
            lf = val[:, HEADS + h:HEADS + h + 1]
            m0 = m_ref[:, h:h + 1]
            n0 = n_ref[:, hs]
            inter = m0 + lf
            m_t = jnp.maximum(inter, ig)
            dw = jnp.exp(ig - m_t)
            sc = jnp.exp(inter - m_t)
            qk = jnp.sum(qf * kf, axis=-1, keepdims=True) * dw
            den = qk + sc * jnp.sum(qf * n0, axis=-1, keepdims=True)
            n_out_ref[:, hs] = sc * n0 + dw * kf
            q_s[0:B, hs] = qf
            k_s[0:B, hs] = kf
            v_s[0:B, hs] = vf
            vw_s[0:B, hs] = vf * dw
            sel = lane == h
            qk_all = jnp.where(sel, qk, qk_all)
            sc_all = jnp.where(sel, sc, sc_all)
            den_all = jnp.where(sel, den, den_all)
            floor_all = jnp.where(sel, jnp.exp(-m_t), floor_all)
            m_all = jnp.where(lane == HEADS + h, m_t, m_all)
        o_s[...] = proj[:, 4 * POOL_WIDTH:5 * POOL_WIDTH]
        coef_s[0] = qk_all
        coef_s[1] = sc_all
        coef_s[2] = den_all
        coef_s[3] = floor_all
        m_out_ref[...] = m_all

    rows = pl.ds(pl.multiple_of(i * SAMPLE_BT, SAMPLE_BT), SAMPLE_BT)
    q_t, k_t, v_t, vw_t = q_s[rows, :], k_s[rows, :], v_s[rows, :], vw_s[rows, :]
    qk_t, sc_t, den_t, floor_t = coef_s[0, rows, :], coef_s[1, rows, :], coef_s[2, rows, :], coef_s[3, rows, :]
    h_rows = []
    for bl in range(SAMPLE_BT):
        heads = []
        for h in range(HEADS):
            hs = slice(h * HEAD_DIM, (h + 1) * HEAD_DIM)
            c_prev = c_ref[bl, h]
            q8 = jnp.broadcast_to(q_t[bl:bl + 1, hs], (SUBLANES, HEAD_DIM))
            cq = _dot_nt(q8.astype(BF16), c_prev.astype(BF16))[0:1, :]
            qk = qk_t[bl:bl + 1, h:h + 1]
            sc = sc_t[bl:bl + 1, h:h + 1]
            num = qk * v_t[bl:bl + 1, hs] + sc * cq
            heads.append(num / jnp.maximum(jnp.abs(den_t[bl:bl + 1, h:h + 1]), floor_t[bl:bl + 1, h:h + 1]))
            v_col = jnp.broadcast_to(vw_t[bl:bl + 1, hs], (HEAD_DIM, HEAD_DIM)).T
            c_out_ref[bl, h] = sc * c_prev + v_col * k_t[bl:bl + 1, hs]
        h_rows.append(jnp.concatenate(heads, axis=1))
    h_s[rows, :] = jnp.concatenate(h_rows, axis=0)

    @pl.when(i == nsteps - 1)
    def _():
        for h in range(HEADS):
            hs = slice(h * HEAD_DIM, (h + 1) * HEAD_DIM)
            mixbuf[:, POOL_WIDTH + h * HEAD_DIM:POOL_WIDTH + (h + 1) * HEAD_DIM] = _head_out(
                h_s[:, hs], o_s[:, hs], mhg_ref[:, hs])
        mix = _dot(mixbuf[...].astype(BF16), wout_ref[...])
        x1 = _layer_norm(DN_ALPHA * x_ref[...] + mix, ln1g_ref[...], ln1b_ref[...])
        x1_ref[0:B, :] = x1
        x1_ref[B:TOK_TILE, :] = jnp.zeros((TOK_TILE - B, D_MODEL), F32)


def _sample_mixer(x, hist2, c0, n0, m0, w_in_b, w_g_b, gbias, w_pool_b, pscale, mhg, w_out_b, ln1g, ln1b):
    B = DEC_BATCH
    steps = B // SAMPLE_BT
    full = lambda a: pl.BlockSpec(a.shape, lambda i: (0,) * a.ndim)
    c_spec = pl.BlockSpec((SAMPLE_BT, HEADS, HEAD_DIM, HEAD_DIM), lambda i: (i, 0, 0, 0))
    return pl.pallas_call(
        _sample_mixer_kernel,
        grid=(steps,),
        in_specs=[full(x), full(hist2), c_spec, full(n0), full(m0), full(w_in_b), full(w_g_b), full(gbias),
                  full(w_pool_b), full(pscale), full(mhg), full(w_out_b), full(ln1g), full(ln1b)],
        out_specs=[
            pl.BlockSpec((TOK_TILE, D_MODEL), lambda i: (0, 0)),
            pl.BlockSpec((B, POOL_HIST * POOL_WIDTH), lambda i: (0, 0)),
            c_spec,
            pl.BlockSpec((B, MLSTM_WIDTH), lambda i: (0, 0)),
            pl.BlockSpec((B, LANES), lambda i: (0, 0)),
        ],
        out_shape=[
            jax.ShapeDtypeStruct((TOK_TILE, D_MODEL), F32),
            jax.ShapeDtypeStruct((B, POOL_HIST * POOL_WIDTH), F32),
            jax.ShapeDtypeStruct((B, HEADS, HEAD_DIM, HEAD_DIM), F32),
            jax.ShapeDtypeStruct((B, MLSTM_WIDTH), F32),
            jax.ShapeDtypeStruct((B, LANES), F32),
        ],
        scratch_shapes=[
            pltpu.VMEM((B, MLSTM_WIDTH), F32),
            pltpu.VMEM((B, MLSTM_WIDTH), F32),
            pltpu.VMEM((B, MLSTM_WIDTH), F32),
            pltpu.VMEM((B, MLSTM_WIDTH), F32),
            pltpu.VMEM((B, MLSTM_WIDTH), F32),
            pltpu.VMEM((B, D_MODEL), F32),
            pltpu.VMEM((B, MLSTM_WIDTH), F32),
            pltpu.VMEM((4, B, LANES), F32),
        ],
        compiler_params=pltpu.CompilerParams(
            dimension_semantics=("arbitrary",), vmem_limit_bytes=VMEM_LIMIT),
        name="sample_mixer",
    )(x, hist2, c0, n0, m0, w_in_b, w_g_b, gbias, w_pool_b, pscale, mhg, w_out_b, ln1g, ln1b)


def _pick_tile(i, prompt_ref, sample_ref):
    return jnp.where(i < N_PROMPT_TILES, prompt_ref[...], sample_ref[...])


def _placement(slot_rows, group):
    r = group * TOK_TILE + lax.broadcasted_iota(I32, (TOK_TILE, TOK_TILE), 0)
    return [r == s for s in slot_rows]


def _route_kernel(xp_ref, xs_ref, wrt_ref, br_ref, slot_ref, gate_ref, nch_ref, sorted_ref):
    i = pl.program_id(0)
    T = TOK_TILE
    E = N_EXPERTS

    x = _pick_tile(i, xp_ref, xs_ref)
    xh = x.astype(BF16)
    xl = (x - xh.astype(F32)).astype(BF16)
    w = wrt_ref[...]
    wh = w.astype(BF16)
    wl = (w - wh.astype(F32)).astype(BF16)
    logits = _dot_nt(wh, xh) + (_dot_nt(wh, xl) + _dot_nt(wl, xh)) + br_ref[:, 0:1]

    erow = lax.broadcasted_iota(I32, (E, T), 0).astype(F32)
    work = logits
    vals, sels = [], []
    for _ in range(TOP_K):
        mx = jnp.max(work, axis=0, keepdims=True)
        idx = jnp.min(jnp.where(work == mx, erow, float(E)), axis=0, keepdims=True)
        sel = erow == idx
        work = jnp.where(sel, -jnp.inf, work)
        vals.append(mx)
        sels.append(sel)
    chosen = jnp.logical_or(jnp.logical_or(sels[0], sels[1]), jnp.logical_or(sels[2], sels[3]))
    es = [jnp.exp(v - vals[0]) for v in vals]
    tot = es[0] + es[1] + es[2] + es[3]

    onehot = jnp.where(chosen, 1.0, 0.0)
    trow = lax.broadcasted_iota(I32, (T, T), 0)
    tcol = lax.broadcasted_iota(I32, (T, T), 1)
    before = jnp.where(trow < tcol, 1.0, 0.0).astype(BF16)
    rank = _dot(onehot.astype(BF16), before)
    cnt = jnp.sum(onehot, axis=1, keepdims=True)
    nch = jnp.floor((cnt + (CHUNK_ROWS - 1)) * (1.0 / CHUNK_ROWS))
    lower = jnp.where(lax.broadcasted_iota(I32, (E, E), 0) > lax.broadcasted_iota(I32, (E, E), 1), 1.0, 0.0)
    nch_b = jnp.broadcast_to(nch, (E, LANES))
    seg_start = _dot(lower.astype(BF16), nch_b.astype(BF16))[:, 0:1] * CHUNK_ROWS
    base = seg_start + rank

    r8 = lax.broadcasted_iota(I32, (SUBLANES, T), 0)
    s_out = jnp.zeros((SUBLANES, T), I32)
    g_out = jnp.zeros((SUBLANES, T), F32)
    slot_rows = []
    for j in range(TOP_K):
        slot_j = jnp.sum(jnp.where(sels[j], base, 0.0), axis=0, keepdims=True).astype(I32)
        slot_rows.append(slot_j)
        s_out = jnp.where(r8 == j, slot_j, s_out)
        g_out = jnp.where(r8 == j, es[j] / tot, g_out)
    slot_ref[0] = s_out
    gate_ref[0] = g_out
    nch_ref[0] = nch_b

    for grp in range(GROUPS):
        m = _placement(slot_rows, grp)
        hit = jnp.logical_or(jnp.logical_or(m[0], m[1]), jnp.logical_or(m[2], m[3]))
        place = jnp.where(hit, 1.0, 0.0).astype(BF16)
        sorted_ref[grp * T:(grp + 1) * T, :] = _dot(place, xh).astype(BF16)


def _route(x1p, x1s, w_router_t, b_router_col):
    tile_spec = pl.BlockSpec((1, SUBLANES, TOK_TILE), lambda i: (i, 0, 0))
    return pl.pallas_call(
        _route_kernel,
        grid=(N_TILES,),
        in_specs=[
            pl.BlockSpec((TOK_TILE, D_MODEL), lambda i: (jnp.minimum(i, N_PROMPT_TILES - 1), 0)),
            pl.BlockSpec((TOK_TILE, D_MODEL), lambda i: (0, 0)),
            pl.BlockSpec(w_router_t.shape, lambda i: (0, 0)),
            pl.BlockSpec(b_router_col.shape, lambda i: (0, 0)),
        ],
        out_specs=[tile_spec, tile_spec,
                   pl.BlockSpec((1, N_EXPERTS, LANES), lambda i: (i, 0, 0)),
                   pl.BlockSpec((LOCAL_ROWS, D_MODEL), lambda i: (i, 0))],
        out_shape=[
            jax.ShapeDtypeStruct((N_TILES, SUBLANES, TOK_TILE), I32),
            jax.ShapeDtypeStruct((N_TILES, SUBLANES, TOK_TILE), F32),
            jax.ShapeDtypeStruct((N_TILES, N_EXPERTS, LANES), F32),
            jax.ShapeDtypeStruct((N_TILES * LOCAL_ROWS, D_MODEL), BF16),
        ],
        compiler_params=pltpu.CompilerParams(
            dimension_semantics=("arbitrary",), vmem_limit_bytes=VMEM_LIMIT),
        name="route",
    )(x1p, x1s, w_router_t, b_router_col)


def _expert_kernel(tbl_ref, bexp_ref, first_ref, next_ref, nused_ref,
                   sorted_hbm, w1_hbm, b1_ref, w2_hbm, b2_ref, out_hbm,
                   w1_stage, w2_stage, w1_b, w2_b, xbuf, obuf, zbuf, wsem, gsem, ssem, zsem):
    i = pl.program_id(0)
    nused = nused_ref[0]

    def fetch(e):
        return (pltpu.make_async_copy(w1_hbm.at[e], w1_stage, wsem.at[0]),
                pltpu.make_async_copy(w2_hbm.at[e], w2_stage, wsem.at[1]))

    def chunk_row(b, q):
        return pl.multiple_of(tbl_ref[b * BLOCK_CHUNKS + q], CHUNK_ROWS)

    def gather(b, q):
        slot = lax.rem(b, 2)
        return pltpu.make_async_copy(sorted_hbm.at[pl.ds(chunk_row(b, q), CHUNK_ROWS), :],
                                     xbuf.at[slot, pl.ds(q * CHUNK_ROWS, CHUNK_ROWS), :], gsem.at[slot])

    def scatter(b, q):
        slot = lax.rem(b, 2)
        return pltpu.make_async_copy(obuf.at[slot, pl.ds(q * CHUNK_ROWS, CHUNK_ROWS), :],
                                     out_hbm.at[pl.ds(chunk_row(b, q), CHUNK_ROWS), :], ssem.at[slot])

    def for_real_chunks(b, fn):
        for q in range(BLOCK_CHUNKS):
            @pl.when(tbl_ref[b * BLOCK_CHUNKS + q] != ZERO_CHUNK_ROW)
            def _():
                fn(scatter(b, q))

    def zero_tail(k):
        start = pl.multiple_of(k * LOCAL_ROWS + TOK_TILE * TOP_K, CHUNK_ROWS)
        return pltpu.make_async_copy(zbuf, out_hbm.at[pl.ds(start, FREE_ROWS), :], zsem)

    @pl.when(i == 0)
    def _():
        zbuf[...] = jnp.zeros_like(zbuf)
        lax.fori_loop(0, N_TILES, lambda k, c: (zero_tail(k).start(), c)[1], 0)
        for cp in fetch(bexp_ref[0]):
            cp.start()
        for q in range(BLOCK_CHUNKS):
            gather(0, q).start()
        lax.fori_loop(0, N_TILES, lambda k, c: (zero_tail(k).wait(), c)[1], 0)

    @pl.when(first_ref[i] == 1)
    def _():
        for cp in fetch(bexp_ref[i]):
            cp.wait()
        half = MXU_COLS // 2
        k = lax.broadcasted_iota(I32, (MXU_COLS, MXU_COLS), 0)
        j = lax.broadcasted_iota(I32, (MXU_COLS, MXU_COLS), 1)
        src = jnp.where(j < half, 2 * j, 2 * (j - half) + 1)
        perm = jnp.where(k == src, 1.0, 0.0).astype(BF16)
        for c in range(2 * D_FF // MXU_COLS):
            blk = w1_stage[:, c * MXU_COLS:(c + 1) * MXU_COLS].astype(BF16)
            sep = _dot(blk, perm).astype(BF16)
            w1_b[:, c * half:(c + 1) * half] = sep[:, 0:half]
            w1_b[:, D_FF + c * half:D_FF + (c + 1) * half] = sep[:, half:MXU_COLS]
        w2_b[...] = w2_stage[...].astype(BF16)

        @pl.when(next_ref[i] >= 0)
        def _():
            for cp in fetch(next_ref[i]):
                cp.start()

    @pl.when(i < nused)
    def _():
        slot = lax.rem(i, 2)
        for q in range(BLOCK_CHUNKS):
            gather(i, q).wait()

        @pl.when(i + 1 < nused)
        def _():
            for q in range(BLOCK_CHUNKS):
                gather(i + 1, q).start()

        @pl.when(i >= 2)
        def _():
            for_real_chunks(i - 2, lambda cp: cp.wait())

        h = _dot(xbuf[slot], w1_b[...]) + b1_ref[0]
        glu = jnp.minimum(h[:, 0:D_FF], SWIGLU_LIMIT)
        lin = jnp.clip(h[:, D_FF:2 * D_FF], -SWIGLU_LIMIT, SWIGLU_LIMIT)
        a = glu * jax.nn.sigmoid(SWIGLU_ALPHA * glu) * (lin + 1.0)
        obuf[slot] = _dot(a.astype(BF16), w2_b[...]) + b2_ref[0]
        for_real_chunks(i, lambda cp: cp.start())

        @pl.when(i == nused - 1)
        def _():
            @pl.when(i >= 1)
            def _():
                for_real_chunks(i - 1, lambda cp: cp.wait())
            for_real_chunks(i, lambda cp: cp.wait())


def _experts(chunk_table, block_expert, block_first, block_next, n_used, sorted_rows, w1, b1p, w2, b2):
    exp_idx = lambda i, tb, be, bf, bn, nu: (be[i], 0, 0)
    grid_spec = pltpu.PrefetchScalarGridSpec(
        num_scalar_prefetch=5,
        grid=(N_BLOCKS,),
        in_specs=[
            pl.BlockSpec(memory_space=pl.ANY),
            pl.BlockSpec(memory_space=pl.ANY),
            pl.BlockSpec((1, 1, 2 * D_FF), exp_idx),
            pl.BlockSpec(memory_space=pl.ANY),
            pl.BlockSpec((1, 1, D_MODEL), exp_idx),
        ],
        out_specs=pl.BlockSpec(memory_space=pl.ANY),
        scratch_shapes=[
            pltpu.VMEM((D_MODEL, 2 * D_FF), F32),
            pltpu.VMEM((D_FF, D_MODEL), F32),
            pltpu.VMEM((D_MODEL, 2 * D_FF), BF16),
            pltpu.VMEM((D_FF, D_MODEL), BF16),
            pltpu.VMEM((2, ROW_BLOCK, D_MODEL), BF16),
            pltpu.VMEM((2, ROW_BLOCK, D_MODEL), F32),
            pltpu.VMEM((FREE_ROWS, D_MODEL), F32),
            pltpu.SemaphoreType.DMA((2,)),
            pltpu.SemaphoreType.DMA((2,)),
            pltpu.SemaphoreType.DMA((2,)),
            pltpu.SemaphoreType.DMA(()),
        ],
    )
    return pl.pallas_call(
        _expert_kernel,
        grid_spec=grid_spec,
        out_shape=jax.ShapeDtypeStruct((N_TILES * LOCAL_ROWS, D_MODEL), F32),
        compiler_params=pltpu.CompilerParams(
            dimension_semantics=("arbitrary",), vmem_limit_bytes=VMEM_LIMIT),
        name="experts",
    )(chunk_table, block_expert, block_first, block_next, n_used, sorted_rows, w1, b1p, w2, b2)


def _combine_kernel(slot_ref, gate_ref, xp_ref, xs_ref, pp_ref, ps_ref, eo_ref, ln2g_ref, ln2b_ref,
                    wpg_ref, wple_ref, yp_ref, ys_ref):
    i = pl.program_id(0)
    T = TOK_TILE

    slot_rows = [slot_ref[0, j:j + 1, :] for j in range(TOP_K)]
    gates_t = jnp.concatenate([gate_ref[0], jnp.zeros((LANES - SUBLANES, T), F32)], axis=0).T
    lane = lax.broadcasted_iota(I32, (T, LANES), 1)
    gate_parts = []
    for j in range(TOP_K):
        g0, g1, g2 = [part.astype(F32) for part in _split3(gates_t[:, j:j + 1])]
        parts = jnp.where(lane == 0, g0, jnp.where(lane == 1, g1, jnp.where(lane == 2, g2, 0.0)))
        gate_parts.append(parts.astype(BF16))

    ff = jnp.zeros((T, D_MODEL), F32)
    for grp in range(GROUPS):
        m = _placement(slot_rows, grp)
        row_gate = jnp.zeros((T, LANES), F32)
        for j in range(TOP_K):
            row_gate = row_gate + _dot(jnp.where(m[j], 1.0, 0.0).astype(BF16), gate_parts[j])
        g_col = row_gate[:, 0:1] + row_gate[:, 1:2] + row_gate[:, 2:3]
        hit = jnp.logical_or(jnp.logical_or(m[0], m[1]), jnp.logical_or(m[2], m[3]))
        place = jnp.where(hit, 1.0, 0.0).astype(BF16)
        z = (eo_ref[grp * T:(grp + 1) * T, :] * g_col).astype(BF16)
        ff = ff + _dot_tn(place, z)

    x1 = _pick_tile(i, xp_ref, xs_ref)
    x2 = _layer_norm(DN_ALPHA * x1 + ff, ln2g_ref[...], ln2b_ref[...])
    p = _pick_tile(i, pp_ref, ps_ref)
    y = x2 + jax.nn.sigmoid(_dot(x2.astype(BF16), wpg_ref[...])) * _dot(p.astype(BF16), wple_ref[...])

    @pl.when(i < N_PROMPT_TILES)
    def _():
        yp_ref[...] = y

    @pl.when(i == N_PROMPT_TILES)
    def _():
        ys_ref[...] = y[0:DEC_BATCH, :]


def _combine(slots, gates, x1p, x1s, pp, ps, expert_out, ln2g, ln2b, w_pg_b, w_ple_b):
    tile_idx = lambda i: (jnp.minimum(i, N_PROMPT_TILES - 1), 0)
    const2 = lambda i: (0, 0)
    return pl.pallas_call(
        _combine_kernel,
        grid=(N_TILES,),
        in_specs=[
            pl.BlockSpec((1, SUBLANES, TOK_TILE), lambda i: (i, 0, 0)),
            pl.BlockSpec((1, SUBLANES, TOK_TILE), lambda i: (i, 0, 0)),
            pl.BlockSpec((TOK_TILE, D_MODEL), tile_idx),
            pl.BlockSpec((TOK_TILE, D_MODEL), const2),
            pl.BlockSpec((TOK_TILE, PLE_DIM), tile_idx),
            pl.BlockSpec((TOK_TILE, PLE_DIM), const2),
            pl.BlockSpec((LOCAL_ROWS, D_MODEL), lambda i: (i, 0)),
            pl.BlockSpec(ln2g.shape, const2),
            pl.BlockSpec(ln2b.shape, const2),
            pl.BlockSpec(w_pg_b.shape, const2),
            pl.BlockSpec(w_ple_b.shape, const2),
        ],
        out_specs=[
            pl.BlockSpec((TOK_TILE, D_MODEL), tile_idx),
            pl.BlockSpec((DEC_BATCH, D_MODEL), const2),
        ],
        out_shape=[
            jax.ShapeDtypeStruct((N_PROMPT, D_MODEL), F32),
            jax.ShapeDtypeStruct((DEC_BATCH, D_MODEL), F32),
        ],
        compiler_params=pltpu.CompilerParams(
            dimension_semantics=("arbitrary",), vmem_limit_bytes=VMEM_LIMIT),
        name="combine",
    )(slots, gates, x1p, x1s, pp, ps, expert_out, ln2g, ln2b, w_pg_b, w_ple_b)


def _block_tables(nch):
    seg_start = (jnp.cumsum(nch, axis=1) - nch) * CHUNK_ROWS
    tot = jnp.sum(nch, axis=0)
    nblk = (tot + BLOCK_CHUNKS - 1) // BLOCK_CHUNKS
    blk_end = jnp.cumsum(nblk)
    blk_start = blk_end - nblk
    n_used = blk_end[-1:].astype(I32)
    blk_ids = jnp.arange(N_BLOCKS, dtype=I32)

    def expert_of(b):
        b = jnp.minimum(b, n_used[0] - 1)
        return jnp.minimum(jnp.sum(blk_end[None, :] <= b[:, None], axis=1), N_EXPERTS - 1).astype(I32)

    block_expert = expert_of(blk_ids)
    in_use = blk_ids < n_used[0]
    is_first = jnp.logical_or(blk_ids == 0, block_expert != expert_of(blk_ids - 1))
    block_first = jnp.logical_and(in_use, is_first).astype(I32)
    next_start = blk_end[block_expert]
    block_next = jnp.where(next_start < n_used[0], expert_of(next_start), -1).astype(I32)

    ent = jnp.arange(N_BLOCKS * BLOCK_CHUNKS, dtype=I32)
    b = ent // BLOCK_CHUNKS
    e = block_expert[b]
    g = (b - blk_start[e]) * BLOCK_CHUNKS + ent % BLOCK_CHUNKS
    cum_tiles = jnp.cumsum(nch, axis=0)
    cum_e = cum_tiles[:, e]
    k = jnp.minimum(jnp.sum(cum_e <= g[None, :], axis=0), N_TILES - 1)
    before = cum_tiles[k, e] - nch[k, e]
    row = k * LOCAL_ROWS + seg_start[k, e] + (g - before) * CHUNK_ROWS
    valid = jnp.logical_and(g < tot[e], b < n_used[0])
    table = jnp.where(valid, row, ZERO_CHUNK_ROW).astype(I32)
    return table, block_expert, block_first, block_next, n_used


def kernel(x_prompt, x_sample, state_pool, state_mlstm_C, state_mlstm_n, state_mlstm_m, p_prompt, p_sample, w_in, b_i, b_f, w_pool, pool_scale, mh_g, w_out, ln1_g, ln1_b, w_router, b_router, w_mlp1, b_mlp1, w_mlp2, b_mlp2, ln2_g, ln2_b, w_ple, w_ple_gate):
    n_main = POOL_WIDTH + 4 * MLSTM_WIDTH
    w_in_b = w_in[0, :, 0:n_main].astype(BF16)
    w_g_b = jnp.pad(w_in[0, :, n_main:], ((0, 0), (0, LANES - 2 * HEADS))).astype(BF16)
    gbias = jnp.pad(jnp.concatenate([b_i[0], b_f[0]]), (0, LANES - 2 * HEADS)).reshape(1, LANES)
    w_pool_b = w_pool[0].astype(BF16)
    pscale = pool_scale[0].reshape(1, POOL_WIDTH)
    mhg = mh_g[0].reshape(1, MLSTM_WIDTH)
    w_out_b = w_out[0].astype(BF16)
    ln1g = ln1_g[0].reshape(1, D_MODEL)
    ln1b = ln1_b[0].reshape(1, D_MODEL)
    ln2g = ln2_g[0].reshape(1, D_MODEL)
    ln2b = ln2_b[0].reshape(1, D_MODEL)
    w_router_t = w_router[0].T
    b_router_col = jnp.broadcast_to(b_router[0].reshape(N_EXPERTS, 1), (N_EXPERTS, LANES))
    b1 = b_mlp1[0]
    b1p = jnp.concatenate([b1[:, 0::2], b1[:, 1::2]], axis=-1).reshape(N_EXPERTS, 1, 2 * D_FF)
    b2 = b_mlp2[0].reshape(N_EXPERTS, 1, D_MODEL)
    w_pg_b = w_ple_gate[0].astype(BF16)
    w_ple_b = w_ple[0].astype(BF16)

    x1p, pool_p, c_p, n_p, m_p = _prompt_mixer(
        x_prompt, w_in_b, w_g_b, gbias, w_pool_b, pscale, mhg, w_out_b, ln1g, ln1b)
    x1s, pool_s, c_s, n_s, m_s = _sample_mixer(
        x_sample.reshape(DEC_BATCH, D_MODEL),
        state_pool[0].reshape(DEC_BATCH, POOL_HIST * POOL_WIDTH),
        state_mlstm_C[0], state_mlstm_n[0].reshape(DEC_BATCH, MLSTM_WIDTH), state_mlstm_m[0],
        w_in_b, w_g_b, gbias, w_pool_b, pscale, mhg, w_out_b, ln1g, ln1b)

    slots, gates, nch, sorted_rows = _route(x1p, x1s, w_router_t, b_router_col)
    tables = _block_tables(nch[:, :, 0].astype(I32))
    expert_out = _experts(*tables, sorted_rows, w_mlp1[0], b1p, w_mlp2[0], b2)

    pp = p_prompt[0].reshape(N_PROMPT, PLE_DIM)
    ps = jnp.pad(p_sample[0].reshape(DEC_BATCH, PLE_DIM), ((0, TOK_TILE - DEC_BATCH), (0, 0)))
    yp, ys = _combine(slots, gates, x1p, x1s, pp, ps, expert_out, ln2g, ln2b, w_pg_b, w_ple_b)

    return (
        yp.reshape(BATCH, SEQ, D_MODEL),
        ys.reshape(DEC_BATCH, 1, D_MODEL),
        pool_p.reshape(1, BATCH, POOL_HIST, POOL_WIDTH),
        c_p.reshape(1, BATCH, HEADS, HEAD_DIM, HEAD_DIM),
        n_p.reshape(1, BATCH, HEADS, HEAD_DIM),
        m_p[:, 0:HEADS, 0].reshape(1, BATCH, HEADS),
        pool_s.reshape(1, DEC_BATCH, POOL_HIST, POOL_WIDTH),
        c_s.reshape(1, DEC_BATCH, HEADS, HEAD_DIM, HEAD_DIM),
        n_s.reshape(1, DEC_BATCH, HEADS, HEAD_DIM),
        m_s[:, HEADS:2 * HEADS].reshape(1, DEC_BATCH, HEADS),
    )
```

```python
import jax
import jax.numpy as jnp
from jax import lax
from jax.experimental import pallas as pl
from jax.experimental.pallas import tpu as pltpu

F32 = jnp.float32
BF16 = jnp.bfloat16
I32 = jnp.int32

D_MODEL = 1024
BATCH = 8
SEQ = 2048
DEC_BATCH = 128
PAST_LEN = 16384
POOL_WIDTH = 512
POOL_GROUPS = 4
POOL_GROUP_DIM = 128
POOL_WINDOWS = (2, 4, 8, 16)
POOL_HIST = 15
MLSTM_WIDTH = 512
HEADS = 4
HEAD_DIM = 128
CHUNK = 128
N_EXPERTS = 32
TOP_K = 4
D_FF = 1024
SWIGLU_ALPHA = 1.702
SWIGLU_LIMIT = 7.0
PLE_DIM = 256
DN_ALPHA = 2.0 ** 0.25
LN_EPS = 1e-5

LANES = 128
SUBLANES = 8
BF16_ROWS = 16
MXU_COLS = 256
VMEM_LIMIT = 56 * 1024 * 1024

MIX_TILE = 256
MIX_SEQS = 2
HIST_PAD = 16
TOK_TILE = 512
N_PROMPT = BATCH * SEQ
N_PROMPT_TILES = N_PROMPT // TOK_TILE
N_TILES = N_PROMPT_TILES + 1
SAMPLE_BT = 16

CHUNK_ROWS = BF16_ROWS
LOCAL_ROWS = TOK_TILE * TOP_K + N_EXPERTS * CHUNK_ROWS
GROUPS = LOCAL_ROWS // TOK_TILE
ROW_BLOCK = 256
BLOCK_CHUNKS = ROW_BLOCK // CHUNK_ROWS
MAX_CHUNKS = N_TILES * (TOK_TILE * TOP_K // CHUNK_ROWS + N_EXPERTS)
N_BLOCKS = -(-MAX_CHUNKS // BLOCK_CHUNKS) + N_EXPERTS
ZERO_CHUNK_ROW = LOCAL_ROWS - CHUNK_ROWS
FREE_ROWS = LOCAL_ROWS - TOK_TILE * TOP_K


def _dot(a, b):
    return jnp.dot(a, b, preferred_element_type=F32)


def _dot_nt(a, b):
    return lax.dot_general(a, b, (((1,), (1,)), ((), ())), preferred_element_type=F32)


def _dot_tn(a, b):
    return lax.dot_general(a, b, (((0,), (0,)), ((), ())), preferred_element_type=F32)


def _split3(a):
    a0 = a.astype(BF16)
    r1 = a - a0.astype(F32)
    a1 = r1.astype(BF16)
    r2 = r1 - a1.astype(F32)
    return a0, a1, r2.astype(BF16)


def _log_sigmoid(x):
    return jnp.minimum(x, 0.0) - jnp.log1p(jnp.exp(-jnp.abs(x)))


def _layer_norm(x, g, b):
    mu = jnp.mean(x, axis=-1, keepdims=True)
    xc = x - mu
    var = jnp.mean(xc * xc, axis=-1, keepdims=True)
    return xc * lax.rsqrt(var + LN_EPS) * g + b


def _gate_values(g, gbias):
    lane = lax.broadcasted_iota(I32, g.shape, 1)
    z = g + gbias
    return jnp.where(lane < HEADS, z, _log_sigmoid(z))


def _head_out(hh, o_h, gain):
    mu = jnp.mean(hh, axis=-1, keepdims=True)
    hc = hh - mu
    var = jnp.mean(hc * hc, axis=-1, keepdims=True)
    return jax.nn.sigmoid(o_h) * (hc * lax.rsqrt(var + LN_EPS) * gain)


def _prompt_mixer_kernel(x_ref, win_ref, wg_ref, gb_ref, wpool_ref, pscale_ref, mhg_ref, wout_ref,
                         ln1g_ref, ln1b_ref,
                         x1_ref, pool_ref, c_out_ref, n_out_ref, m_out_ref,
                         ubuf, mixbuf, c_s, n_s, m_s):
    ti = pl.program_id(1)
    nt = pl.num_programs(1)
    TT = MIX_TILE
    S = MIX_SEQS

    @pl.when(ti == 0)
    def _():
        for s in range(S):
            ubuf[s, 0:HIST_PAD, :] = jnp.zeros((HIST_PAD, POOL_WIDTH), F32)
        c_s[...] = jnp.zeros_like(c_s)
        n_s[...] = jnp.zeros_like(n_s)
        m_s[...] = jnp.zeros_like(m_s)

    x = jnp.concatenate([x_ref[s, 0] for s in range(S)], axis=0)
    xb = x.astype(BF16)
    proj = _dot(xb, win_ref[...])
    g = _dot(xb, wg_ref[...])

    L = CHUNK
    row = lax.broadcasted_iota(I32, (L, L), 0)
    col = lax.broadcasted_iota(I32, (L, L), 1)
    causal = row >= col
    tril = jnp.where(causal, 1.0, 0.0).astype(BF16)
    pos = ti * TT + lax.broadcasted_iota(I32, (TT, 1), 0)

    for s in range(S):
        base = s * TT
        u = proj[base:base + TT, 0:POOL_WIDTH]

        ubuf[s, HIST_PAD:HIST_PAD + TT, :] = u
        for gi, w in enumerate(POOL_WINDOWS):
            sl = slice(gi * POOL_GROUP_DIM, (gi + 1) * POOL_GROUP_DIM)
            ug = u[:, sl]
            acc = ug
            for i in range(1, w):
                acc = acc + ubuf[s, HIST_PAD - i:HIST_PAD - i + TT, sl]
            cnt = jnp.minimum(pos + 1, w).astype(F32)
            z = acc / cnt - ug
            mixbuf[base:base + TT, sl] = _dot(z.astype(BF16), wpool_ref[gi]) * pscale_ref[:, sl]

        @pl.when(ti == nt - 1)
        def _():
            pool_ref[s, 0] = ubuf[s, TT + 1:TT + HIST_PAD, :]

        ubuf[s, 0:HIST_PAD, :] = ubuf[s, TT:TT + HIST_PAD, :]

    NC = TT // L
    chains = [(s, h) for s in range(S) for h in range(HEADS)]
    units = [(s, c, h) for c in range(NC) for s in range(S) for h in range(HEADS)]
    U = range(len(units))

    def rows(s, c):
        return slice(s * TT + c * L, s * TT + (c + 1) * L)

    def head_cols(part, h):
        return slice(part * POOL_WIDTH + h * HEAD_DIM, part * POOL_WIDTH + (h + 1) * HEAD_DIM)

    gate, cum, gate_t, cum_t = {}, {}, {}, {}
    for c in range(NC):
        for s in range(S):
            val = _gate_values(g[rows(s, c), :], gb_ref[...])
            v0, v1, v2 = _split3(val)
            gate[s, c] = val
            cum[s, c] = _dot(tril, v0) + _dot(tril, v1) + _dot(tril, v2)
    for key in gate:
        gate_t[key] = gate[key].T
        cum_t[key] = cum[key].T
    qf = [proj[rows(s, c), head_cols(1, h)] for s, c, h in units]
    kf = [proj[rows(s, c), head_cols(2, h)] * (HEAD_DIM ** -0.5) for s, c, h in units]
    vf = [proj[rows(s, c), head_cols(3, h)] for s, c, h in units]
    qb = [a.astype(BF16) for a in qf]
    kb = [a.astype(BF16) for a in kf]
    f_col = [cum[s, c][:, HEADS + h:HEADS + h + 1] for s, c, h in units]
    log_d = [jnp.where(causal, f_col[u] - cum_t[s, c][HEADS + h:HEADS + h + 1, :] + gate_t[s, c][h:h + 1, :],
                       -jnp.inf) for u, (s, c, h) in enumerate(units)]
    row_max = [jnp.max(log_d[u], axis=-1, keepdims=True) for u in U]
    qk_raw = [_dot_nt(qb[u], kb[u]) for u in U]

    m_prev, m_t, inter = [None] * len(units), [None] * len(units), [None] * len(units)
    m_run = {(s, h): m_s[s, h:h + 1, 0:1] for s, h in chains}
    for u, (s, c, h) in enumerate(units):
        m_prev[u] = m_run[s, h]
        inter[u] = m_prev[u] + f_col[u]
        m_t[u] = jnp.maximum(inter[u], row_max[u])
        m_run[s, h] = m_t[u][L - 1:L, :]
    m_new = [m_t[u][L - 1:L, :] for u in U]

    dw = [jnp.exp(log_d[u] - m_t[u]) for u in U]
    sc = [jnp.exp(inter[u] - m_t[u]) for u in U]
    qk = [qk_raw[u] * dw[u] for u in U]
    intra = [_dot(qk[u].astype(BF16), vf[u].astype(BF16)) for u in U]
    row_sum = [jnp.sum(qk[u], axis=-1, keepdims=True) for u in U]
    floor = [jnp.exp(-m_t[u]) for u in U]
    f_last = [f_col[u][L - 1:L, :] for u in U]
    wk = [jnp.exp(gate[s, c][:, h:h + 1] + f_last[u] - f_col[u] - m_new[u]) for u, (s, c, h) in enumerate(units)]
    decay = [jnp.exp(m_prev[u] + f_last[u] - m_new[u]) for u in U]
    upd = [_dot_tn((vf[u] * wk[u]).astype(BF16), kb[u]) for u in U]
    n_upd = [jnp.sum(wk[u] * kf[u], axis=0, keepdims=True) for u in U]

    c_run = {(s, h): c_s[s, h] for s, h in chains}
    n_run = {(s, h): n_s[s, h:h + 1, :] for s, h in chains}
    hh = [None] * len(units)
    for c in range(NC):
        cu = [u for u in U if units[u][1] == c]
        inter_term = {u: _dot_nt(qb[u], c_run[units[u][0], units[u][2]].astype(BF16)) for u in cu}
        n_term = {u: jnp.sum(qf[u] * n_run[units[u][0], units[u][2]], axis=-1, keepdims=True) for u in cu}
        for u in cu:
            s, _, h = units[u]
            num = intra[u] + sc[u] * inter_term[u]
            den = row_sum[u] + sc[u] * n_term[u]
            hh[u] = num / jnp.maximum(jnp.abs(den), floor[u])
            c_run[s, h] = decay[u] * c_run[s, h] + upd[u]
            n_run[s, h] = decay[u] * n_run[s, h] + n_upd[u]
    for s, h in chains:
        c_s[s, h] = c_run[s, h]
        n_s[s, h:h + 1, :] = n_run[s, h]
        m_s[s, h:h + 1, :] = jnp.broadcast_to(m_run[s, h], (1, LANES))
    for u, (s, c, h) in enumerate(units):
        mixbuf[rows(s, c), head_cols(1, h)] = _head_out(
            hh[u], proj[rows(s, c), head_cols(4, h)], mhg_ref[:, h * HEAD_DIM:(h + 1) * HEAD_DIM])

    @pl.when(ti == nt - 1)
    def _():
        for s in range(S):
            c_out_ref[s, 0] = c_s[s]
            n_out_ref[s, 0] = n_s[s, 0:HEADS, :]
            m_out_ref[s, 0] = m_s[s]

    mix = _dot(mixbuf[...].astype(BF16), wout_ref[...])
    x1 = _layer_norm(DN_ALPHA * x + mix, ln1g_ref[...], ln1b_ref[...])
    for s in range(S):
        x1_ref[s] = x1[s * TT:(s + 1) * TT, :]


def _prompt_mixer(x, w_in_b, w_g_b, gbias, w_pool_b, pscale, mhg, w_out_b, ln1g, ln1b):
    nt = SEQ // MIX_TILE
    S = MIX_SEQS
    G = BATCH // S
    const2 = lambda b, t: (0, 0)
    const3 = lambda b, t: (0, 0, 0)
    outs = pl.pallas_call(
        _prompt_mixer_kernel,
        grid=(G, nt),
        in_specs=[
            pl.BlockSpec((S, 1, MIX_TILE, D_MODEL), lambda b, t: (0, b, t, 0)),
            pl.BlockSpec(w_in_b.shape, const2),
            pl.BlockSpec(w_g_b.shape, const2),
            pl.BlockSpec(gbias.shape, const2),
            pl.BlockSpec(w_pool_b.shape, const3),
            pl.BlockSpec(pscale.shape, const2),
            pl.BlockSpec(mhg.shape, const2),
            pl.BlockSpec(w_out_b.shape, const2),
            pl.BlockSpec(ln1g.shape, const2),
            pl.BlockSpec(ln1b.shape, const2),
        ],
        out_specs=[
            pl.BlockSpec((S, MIX_TILE, D_MODEL), lambda b, t: (0, b * nt + t, 0)),
            pl.BlockSpec((S, 1, POOL_HIST, POOL_WIDTH), lambda b, t: (0, b, 0, 0)),
            pl.BlockSpec((S, 1, HEADS, HEAD_DIM, HEAD_DIM), lambda b, t: (0, b, 0, 0, 0)),
            pl.BlockSpec((S, 1, HEADS, HEAD_DIM), lambda b, t: (0, b, 0, 0)),
            pl.BlockSpec((S, 1, SUBLANES, LANES), lambda b, t: (0, b, 0, 0)),
        ],
        out_shape=[
            jax.ShapeDtypeStruct((S, G * SEQ, D_MODEL), F32),
            jax.ShapeDtypeStruct((S, G, POOL_HIST, POOL_WIDTH), F32),
            jax.ShapeDtypeStruct((S, G, HEADS, HEAD_DIM, HEAD_DIM), F32),
            jax.ShapeDtypeStruct((S, G, HEADS, HEAD_DIM), F32),
            jax.ShapeDtypeStruct((S, G, SUBLANES, LANES), F32),
        ],
        scratch_shapes=[
            pltpu.VMEM((S, HIST_PAD + MIX_TILE, POOL_WIDTH), F32),
            pltpu.VMEM((S * MIX_TILE, D_MODEL), F32),
            pltpu.VMEM((S, HEADS, HEAD_DIM, HEAD_DIM), F32),
            pltpu.VMEM((S, SUBLANES, HEAD_DIM), F32),
            pltpu.VMEM((S, SUBLANES, LANES), F32),
        ],
        compiler_params=pltpu.CompilerParams(
            dimension_semantics=("arbitrary", "arbitrary"), vmem_limit_bytes=VMEM_LIMIT),
        name="prompt_mixer",
    )(x.reshape(S, G, SEQ, D_MODEL), w_in_b, w_g_b, gbias, w_pool_b, pscale, mhg, w_out_b, ln1g, ln1b)
    x1, pool, c, n, m = outs
    return (x1.reshape(N_PROMPT, D_MODEL), pool.reshape(BATCH, POOL_HIST, POOL_WIDTH),
            c.reshape(BATCH, HEADS, HEAD_DIM, HEAD_DIM), n.reshape(BATCH, HEADS, HEAD_DIM),
            m.reshape(BATCH, SUBLANES, LANES))


def _sample_mixer_kernel(x_ref, hist_ref, c_ref, n_ref, m_ref, win_ref, wg_ref, gb_ref, wpool_ref,
                         pscale_ref, mhg_ref, wout_ref, ln1g_ref, ln1b_ref,
                         x1_ref, pool_out_ref, c_out_ref, n_out_ref, m_out_ref,
                         q_s, k_s, vw_s, v_s, o_s, mixbuf, h_s, coef_s):
    i = pl.program_id(0)
    nsteps = pl.num_programs(0)
    B = DEC_BATCH

    @pl.when(i == 0)
    def _():
        x = x_ref[...]
        xb = x.astype(BF16)
        proj = _dot(xb, win_ref[...])
        g = _dot(xb, wg_ref[...])
        u = proj[:, 0:POOL_WIDTH]
        for gi, w in enumerate(POOL_WINDOWS):
            sl = slice(gi * POOL_GROUP_DIM, (gi + 1) * POOL_GROUP_DIM)
            ug = u[:, sl]
            s = ug
            for j in range(1, w):
                r = POOL_HIST - j
                s = s + hist_ref[:, r * POOL_WIDTH + gi * POOL_GROUP_DIM:r * POOL_WIDTH + (gi + 1) * POOL_GROUP_DIM]
            cnt = float(min(PAST_LEN + 1, w))
            z = s / cnt - ug
            mixbuf[:, sl] = _dot(z.astype(BF16), wpool_ref[gi]) * pscale_ref[:, sl]
        pool_out_ref[:, 0:(POOL_HIST - 1) * POOL_WIDTH] = hist_ref[:, POOL_WIDTH:POOL_HIST * POOL_WIDTH]
        pool_out_ref[:, (POOL_HIST - 1) * POOL_WIDTH:POOL_HIST * POOL_WIDTH] = u

        val = _gate_values(g, gb_ref[...])
        lane = lax.broadcasted_iota(I32, (B, LANES), 1)
        qk_all = jnp.zeros((B, LANES), F32)
        sc_all = jnp.zeros((B, LANES), F32)
        den_all = jnp.zeros((B, LANES), F32)
        floor_all = jnp.zeros((B, LANES), F32)
        m_all = jnp.zeros((B, LANES), F32)
        for h in range(HEADS):
            hs = slice(h * HEAD_DIM, (h + 1) * HEAD_DIM)
            qf = proj[:, POOL_WIDTH + h * HEAD_DIM:POOL_WIDTH + (h + 1) * HEAD_DIM]
            kf = proj[:, 2 * POOL_WIDTH + h * HEAD_DIM:2 * POOL_WIDTH + (h + 1) * HEAD_DIM] * (HEAD_DIM ** -0.5)
            vf = proj[:, 3 * POOL_WIDTH + h * HEAD_DIM:3 * POOL_WIDTH + (h + 1) * HEAD_DIM]
            ig = val[:, h:h + 1]
            lf = val[:, HEADS + h:HEADS + h + 1]
            m0 = m_ref[:, h:h + 1]
            n0 = n_ref[:, hs]
            inter = m0 + lf
            m_t = jnp.maximum(inter, ig)
            dw = jnp.exp(ig - m_t)
            sc = jnp.exp(inter - m_t)
            qk = jnp.sum(qf * kf, axis=-1, keepdims=True) * dw
            den = qk + sc * jnp.sum(qf * n0, axis=-1, keepdims=True)
            n_out_ref[:, hs] = sc * n0 + dw * kf
            q_s[0:B, hs] = qf
            k_s[0:B, hs] = kf
            v_s[0:B, hs] = vf
            vw_s[0:B, hs] = vf * dw
            sel = lane == h
            qk_all = jnp.where(sel, qk, qk_all)
            sc_all = jnp.where(sel, sc, sc_all)
            den_all = jnp.where(sel, den, den_all)
            floor_all = jnp.where(sel, jnp.exp(-m_t), floor_all)
            m_all = jnp.where(lane == HEADS + h, m_t, m_all)
        o_s[...] = proj[:, 4 * POOL_WIDTH:5 * POOL_WIDTH]
        coef_s[0] = qk_all
        coef_s[1] = sc_all
        coef_s[2] = den_all
        coef_s[3] = floor_all
        m_out_ref[...] = m_all

    rows = pl.ds(pl.multiple_of(i * SAMPLE_BT, SAMPLE_BT), SAMPLE_BT)
    q_t, k_t, v_t, vw_t = q_s[rows, :], k_s[rows, :], v_s[rows, :], vw_s[rows, :]
    qk_t, sc_t, den_t, floor_t = coef_s[0, rows, :], coef_s[1, rows, :], coef_s[2, rows, :], coef_s[3, rows, :]
    h_rows = []
    for bl in range(SAMPLE_BT):
        heads = []
        for h in range(HEADS):
            hs = slice(h * HEAD_DIM, (h + 1) * HEAD_DIM)
            c_prev = c_ref[bl, h]
            q8 = jnp.broadcast_to(q_t[bl:bl + 1, hs], (SUBLANES, HEAD_DIM))
            cq = _dot_nt(q8.astype(BF16), c_prev.astype(BF16))[0:1, :]
            qk = qk_t[bl:bl + 1, h:h + 1]
            sc = sc_t[bl:bl + 1, h:h + 1]
            num = qk * v_t[bl:bl + 1, hs] + sc * cq
            heads.append(num / jnp.maximum(jnp.abs(den_t[bl:bl + 1, h:h + 1]), floor_t[bl:bl + 1, h:h + 1]))
            v_col = jnp.broadcast_to(vw_t[bl:bl + 1, hs], (HEAD_DIM, HEAD_DIM)).T
            c_out_ref[bl, h] = sc * c_prev + v_col * k_t[bl:bl + 1, hs]
        h_rows.append(jnp.concatenate(heads, axis=1))
    h_s[rows, :] = jnp.concatenate(h_rows, axis=0)

    @pl.when(i == nsteps - 1)
    def _():
        for h in range(HEADS):
            hs = slice(h * HEAD_DIM, (h + 1) * HEAD_DIM)
            mixbuf[:, POOL_WIDTH + h * HEAD_DIM:POOL_WIDTH + (h + 1) * HEAD_DIM] = _head_out(
                h_s[:, hs], o_s[:, hs], mhg_ref[:, hs])
        mix = _dot(mixbuf[...].astype(BF16), wout_ref[...])
        x1 = _layer_norm(DN_ALPHA * x_ref[...] + mix, ln1g_ref[...], ln1b_ref[...])
        x1_ref[0:B, :] = x1
        x1_ref[B:TOK_TILE, :] = jnp.zeros((TOK_TILE - B, D_MODEL), F32)


def _sample_mixer(x, hist2, c0, n0, m0, w_in_b, w_g_b, gbias, w_pool_b, pscale, mhg, w_out_b, ln1g, ln1b):
    B = DEC_BATCH
    steps = B // SAMPLE_BT
    full = lambda a: pl.BlockSpec(a.shape, lambda i: (0,) * a.ndim)
    c_spec = pl.BlockSpec((SAMPLE_BT, HEADS, HEAD_DIM, HEAD_DIM), lambda i: (i, 0, 0, 0))
    return pl.pallas_call(
        _sample_mixer_kernel,
        grid=(steps,),
        in_specs=[full(x), full(hist2), c_spec, full(n0), full(m0), full(w_in_b), full(w_g_b), full(gbias),
                  full(w_pool_b), full(pscale), full(mhg), full(w_out_b), full(ln1g), full(ln1b)],
        out_specs=[
            pl.BlockSpec((TOK_TILE, D_MODEL), lambda i: (0, 0)),
            pl.BlockSpec((B, POOL_HIST * POOL_WIDTH), lambda i: (0, 0)),
            c_spec,
            pl.BlockSpec((B, MLSTM_WIDTH), lambda i: (0, 0)),
            pl.BlockSpec((B, LANES), lambda i: (0, 0)),
        ],
        out_shape=[
            jax.ShapeDtypeStruct((TOK_TILE, D_MODEL), F32),
            jax.ShapeDtypeStruct((B, POOL_HIST * POOL_WIDTH), F32),
            jax.ShapeDtypeStruct((B, HEADS, HEAD_DIM, HEAD_DIM), F32),
            jax.ShapeDtypeStruct((B, MLSTM_WIDTH), F32),
            jax.ShapeDtypeStruct((B, LANES), F32),
        ],
        scratch_shapes=[
            pltpu.VMEM((B, MLSTM_WIDTH), F32),
            pltpu.VMEM((B, MLSTM_WIDTH), F32),
            pltpu.VMEM((B, MLSTM_WIDTH), F32),
            pltpu.VMEM((B, MLSTM_WIDTH), F32),
            pltpu.VMEM((B, MLSTM_WIDTH), F32),
            pltpu.VMEM((B, D_MODEL), F32),
            pltpu.VMEM((B, MLSTM_WIDTH), F32),
            pltpu.VMEM((4, B, LANES), F32),
        ],
        compiler_params=pltpu.CompilerParams(
            dimension_semantics=("arbitrary",), vmem_limit_bytes=VMEM_LIMIT),
        name="sample_mixer",
    )(x, hist2, c0, n0, m0, w_in_b, w_g_b, gbias, w_pool_b, pscale, mhg, w_out_b, ln1g, ln1b)


def _pick_tile(i, prompt_ref, sample_ref):
    return jnp.where(i < N_PROMPT_TILES, prompt_ref[...], sample_ref[...])


def _placement(slot_rows, group):
    r = group * TOK_TILE + lax.broadcasted_iota(I32, (TOK_TILE, TOK_TILE), 0)
    return [r == s for s in slot_rows]


def _route_kernel(xp_ref, xs_ref, wrt_ref, br_ref, slot_ref, gate_ref, nch_ref, sorted_ref):
    i = pl.program_id(0)
    T = TOK_TILE
    E = N_EXPERTS

    x = _pick_tile(i, xp_ref, xs_ref)
    xh = x.astype(BF16)
    xl = (x - xh.astype(F32)).astype(BF16)
    w = wrt_ref[...]
    wh = w.astype(BF16)
    wl = (w - wh.astype(F32)).astype(BF16)
    logits = _dot_nt(wh, xh) + (_dot_nt(wh, xl) + _dot_nt(wl, xh)) + br_ref[:, 0:1]

    erow = lax.broadcasted_iota(I32, (E, T), 0).astype(F32)
    work = logits
    vals, sels = [], []
    for _ in range(TOP_K):
        mx = jnp.max(work, axis=0, keepdims=True)
        idx = jnp.min(jnp.where(work == mx, erow, float(E)), axis=0, keepdims=True)
        sel = erow == idx
        work = jnp.where(sel, -jnp.inf, work)
        vals.append(mx)
        sels.append(sel)
    chosen = jnp.logical_or(jnp.logical_or(sels[0], sels[1]), jnp.logical_or(sels[2], sels[3]))
    es = [jnp.exp(v - vals[0]) for v in vals]
    tot = es[0] + es[1] + es[2] + es[3]

    onehot = jnp.where(chosen, 1.0, 0.0)
    trow = lax.broadcasted_iota(I32, (T, T), 0)
    tcol = lax.broadcasted_iota(I32, (T, T), 1)
    before = jnp.where(trow < tcol, 1.0, 0.0).astype(BF16)
    rank = _dot(onehot.astype(BF16), before)
    cnt = jnp.sum(onehot, axis=1, keepdims=True)
    nch = jnp.floor((cnt + (CHUNK_ROWS - 1)) * (1.0 / CHUNK_ROWS))
    lower = jnp.where(lax.broadcasted_iota(I32, (E, E), 0) > lax.broadcasted_iota(I32, (E, E), 1), 1.0, 0.0)
    nch_b = jnp.broadcast_to(nch, (E, LANES))
    seg_start = _dot(lower.astype(BF16), nch_b.astype(BF16))[:, 0:1] * CHUNK_ROWS
    base = seg_start + rank

    r8 = lax.broadcasted_iota(I32, (SUBLANES, T), 0)
    s_out = jnp.zeros((SUBLANES, T), I32)
    g_out = jnp.zeros((SUBLANES, T), F32)
    slot_rows = []
    for j in range(TOP_K):
        slot_j = jnp.sum(jnp.where(sels[j], base, 0.0), axis=0, keepdims=True).astype(I32)
        slot_rows.append(slot_j)
        s_out = jnp.where(r8 == j, slot_j, s_out)
        g_out = jnp.where(r8 == j, es[j] / tot, g_out)
    slot_ref[0] = s_out
    gate_ref[0] = g_out
    nch_ref[0] = nch_b

    for grp in range(GROUPS):
        m = _placement(slot_rows, grp)
        hit = jnp.logical_or(jnp.logical_or(m[0], m[1]), jnp.logical_or(m[2], m[3]))
        place = jnp.where(hit, 1.0, 0.0).astype(BF16)
        sorted_ref[grp * T:(grp + 1) * T, :] = _dot(place, xh).astype(BF16)


def _route(x1p, x1s, w_router_t, b_router_col):
    tile_spec = pl.BlockSpec((1, SUBLANES, TOK_TILE), lambda i: (i, 0, 0))
    return pl.pallas_call(
        _route_kernel,
        grid=(N_TILES,),
        in_specs=[
            pl.BlockSpec((TOK_TILE, D_MODEL), lambda i: (jnp.minimum(i, N_PROMPT_TILES - 1), 0)),
            pl.BlockSpec((TOK_TILE, D_MODEL), lambda i: (0, 0)),
            pl.BlockSpec(w_router_t.shape, lambda i: (0, 0)),
            pl.BlockSpec(b_router_col.shape, lambda i: (0, 0)),
        ],
        out_specs=[tile_spec, tile_spec,
                   pl.BlockSpec((1, N_EXPERTS, LANES), lambda i: (i, 0, 0)),
                   pl.BlockSpec((LOCAL_ROWS, D_MODEL), lambda i: (i, 0))],
        out_shape=[
            jax.ShapeDtypeStruct((N_TILES, SUBLANES, TOK_TILE), I32),
            jax.ShapeDtypeStruct((N_TILES, SUBLANES, TOK_TILE), F32),
            jax.ShapeDtypeStruct((N_TILES, N_EXPERTS, LANES), F32),
            jax.ShapeDtypeStruct((N_TILES * LOCAL_ROWS, D_MODEL), BF16),
        ],
        compiler_params=pltpu.CompilerParams(
            dimension_semantics=("arbitrary",), vmem_limit_bytes=VMEM_LIMIT),
        name="route",
    )(x1p, x1s, w_router_t, b_router_col)


def _expert_kernel(tbl_ref, bexp_ref, first_ref, next_ref, nused_ref,
                   sorted_hbm, w1_hbm, b1_ref, w2_hbm, b2_ref, out_hbm,
                   w1_stage, w2_stage, w1_b, w2_b, xbuf, obuf, zbuf, wsem, gsem, ssem, zsem):
    i = pl.program_id(0)
    nused = nused_ref[0]

    def fetch(e):
        return (pltpu.make_async_copy(w1_hbm.at[e], w1_stage, wsem.at[0]),
                pltpu.make_async_copy(w2_hbm.at[e], w2_stage, wsem.at[1]))

    def chunk_row(b, q):
        return pl.multiple_of(tbl_ref[b * BLOCK_CHUNKS + q], CHUNK_ROWS)

    def gather(b, q):
        slot = lax.rem(b, 2)
        return pltpu.make_async_copy(sorted_hbm.at[pl.ds(chunk_row(b, q), CHUNK_ROWS), :],
                                     xbuf.at[slot, pl.ds(q * CHUNK_ROWS, CHUNK_ROWS), :], gsem.at[slot])

    def scatter(b, q):
        slot = lax.rem(b, 2)
        return pltpu.make_async_copy(obuf.at[slot, pl.ds(q * CHUNK_ROWS, CHUNK_ROWS), :],
                                     out_hbm.at[pl.ds(chunk_row(b, q), CHUNK_ROWS), :], ssem.at[slot])

    def for_real_chunks(b, fn):
        for q in range(BLOCK_CHUNKS):
            @pl.when(tbl_ref[b * BLOCK_CHUNKS + q] != ZERO_CHUNK_ROW)
            def _():
                fn(scatter(b, q))

    def zero_tail(k):
        start = pl.multiple_of(k * LOCAL_ROWS + TOK_TILE * TOP_K, CHUNK_ROWS)
        return pltpu.make_async_copy(zbuf, out_hbm.at[pl.ds(start, FREE_ROWS), :], zsem)

    @pl.when(i == 0)
    def _():
        zbuf[...] = jnp.zeros_like(zbuf)
        lax.fori_loop(0, N_TILES, lambda k, c: (zero_tail(k).start(), c)[1], 0)
        for cp in fetch(bexp_ref[0]):
            cp.start()
        for q in range(BLOCK_CHUNKS):
            gather(0, q).start()
        lax.fori_loop(0, N_TILES, lambda k, c: (zero_tail(k).wait(), c)[1], 0)

    @pl.when(first_ref[i] == 1)
    def _():
        for cp in fetch(bexp_ref[i]):
            cp.wait()
        half = MXU_COLS // 2
        k = lax.broadcasted_iota(I32, (MXU_COLS, MXU_COLS), 0)
        j = lax.broadcasted_iota(I32, (MXU_COLS, MXU_COLS), 1)
        src = jnp.where(j < half, 2 * j, 2 * (j - half) + 1)
        perm = jnp.where(k == src, 1.0, 0.0).astype(BF16)
        for c in range(2 * D_FF // MXU_COLS):
            blk = w1_stage[:, c * MXU_COLS:(c + 1) * MXU_COLS].astype(BF16)
            sep = _dot(blk, perm).astype(BF16)
            w1_b[:, c * half:(c + 1) * half] = sep[:, 0:half]
            w1_b[:, D_FF + c * half:D_FF + (c + 1) * half] = sep[:, half:MXU_COLS]
        w2_b[...] = w2_stage[...].astype(BF16)

        @pl.when(next_ref[i] >= 0)
        def _():
            for cp in fetch(next_ref[i]):
                cp.start()

    @pl.when(i < nused)
    def _():
        slot = lax.rem(i, 2)
        for q in range(BLOCK_CHUNKS):
            gather(i, q).wait()

        @pl.when(i + 1 < nused)
        def _():
            for q in range(BLOCK_CHUNKS):
                gather(i + 1, q).start()

        @pl.when(i >= 2)
        def _():
            for_real_chunks(i - 2, lambda cp: cp.wait())

        h = _dot(xbuf[slot], w1_b[...]) + b1_ref[0]
        glu = jnp.minimum(h[:, 0:D_FF], SWIGLU_LIMIT)
        lin = jnp.clip(h[:, D_FF:2 * D_FF], -SWIGLU_LIMIT, SWIGLU_LIMIT)
        a = glu * jax.nn.sigmoid(SWIGLU_ALPHA * glu) * (lin + 1.0)
        obuf[slot] = _dot(a.astype(BF16), w2_b[...]) + b2_ref[0]
        for_real_chunks(i, lambda cp: cp.start())

        @pl.when(i == nused - 1)
        def _():
            @pl.when(i >= 1)
            def _():
                for_real_chunks(i - 1, lambda cp: cp.wait())
            for_real_chunks(i, lambda cp: cp.wait())


def _experts(chunk_table, block_expert, block_first, block_next, n_used, sorted_rows, w1, b1p, w2, b2):
    exp_idx = lambda i, tb, be, bf, bn, nu: (be[i], 0, 0)
    grid_spec = pltpu.PrefetchScalarGridSpec(
        num_scalar_prefetch=5,
        grid=(N_BLOCKS,),
        in_specs=[
            pl.BlockSpec(memory_space=pl.ANY),
            pl.BlockSpec(memory_space=pl.ANY),
            pl.BlockSpec((1, 1, 2 * D_FF), exp_idx),
            pl.BlockSpec(memory_space=pl.ANY),
            pl.BlockSpec((1, 1, D_MODEL), exp_idx),
        ],
        out_specs=pl.BlockSpec(memory_space=pl.ANY),
        scratch_shapes=[
            pltpu.VMEM((D_MODEL, 2 * D_FF), F32),
            pltpu.VMEM((D_FF, D_MODEL), F32),
            pltpu.VMEM((D_MODEL, 2 * D_FF), BF16),
            pltpu.VMEM((D_FF, D_MODEL), BF16),
            pltpu.VMEM((2, ROW_BLOCK, D_MODEL), BF16),
            pltpu.VMEM((2, ROW_BLOCK, D_MODEL), F32),
            pltpu.VMEM((FREE_ROWS, D_MODEL), F32),
            pltpu.SemaphoreType.DMA((2,)),
            pltpu.SemaphoreType.DMA((2,)),
            pltpu.SemaphoreType.DMA((2,)),
            pltpu.SemaphoreType.DMA(()),
        ],
    )
    return pl.pallas_call(
        _expert_kernel,
        grid_spec=grid_spec,
        out_shape=jax.ShapeDtypeStruct((N_TILES * LOCAL_ROWS, D_MODEL), F32),
        compiler_params=pltpu.CompilerParams(
            dimension_semantics=("arbitrary",), vmem_limit_bytes=VMEM_LIMIT),
        name="experts",
    )(chunk_table, block_expert, block_first, block_next, n_used, sorted_rows, w1, b1p, w2, b2)


def _combine_kernel(slot_ref, gate_ref, xp_ref, xs_ref, pp_ref, ps_ref, eo_ref, ln2g_ref, ln2b_ref,
                    wpg_ref, wple_ref, yp_ref, ys_ref):
    i = pl.program_id(0)
    T = TOK_TILE

    slot_rows = [slot_ref[0, j:j + 1, :] for j in range(TOP_K)]
    gate_rows = [gate_ref[0, j:j + 1, :] for j in range(TOP_K)]
    pad = jnp.zeros((LANES - SUBLANES, T), F32)
    slots_t = jnp.concatenate([slot_ref[0].astype(F32), pad], axis=0).T
    slot_cols = [slots_t[:, j:j + 1].astype(I32) for j in range(TOP_K)]

    ff = jnp.zeros((T, D_MODEL), F32)
    for grp in range(GROUPS):
        m = _placement(slot_rows, grp)
        weighted = jnp.where(m[0], gate_rows[0], jnp.where(m[1], gate_rows[1], jnp.where(
            m[2], gate_rows[2], jnp.where(m[3], gate_rows[3], 0.0))))
        g_col = jnp.sum(weighted, axis=1, keepdims=True)
        z = (eo_ref[grp * T:(grp + 1) * T, :] * g_col).astype(BF16)
        r = grp * T + lax.broadcasted_iota(I32, (T, T), 1)
        hit = jnp.logical_or(jnp.logical_or(r == slot_cols[0], r == slot_cols[1]),
                             jnp.logical_or(r == slot_cols[2], r == slot_cols[3]))
        ff = ff + _dot(jnp.where(hit, 1.0, 0.0).astype(BF16), z)

    x1 = _pick_tile(i, xp_ref, xs_ref)
    x2 = _layer_norm(DN_ALPHA * x1 + ff, ln2g_ref[...], ln2b_ref[...])
    p = _pick_tile(i, pp_ref, ps_ref)
    y = x2 + jax.nn.sigmoid(_dot(x2.astype(BF16), wpg_ref[...])) * _dot(p.astype(BF16), wple_ref[...])

    @pl.when(i < N_PROMPT_TILES)
    def _():
        yp_ref[...] = y

    @pl.when(i == N_PROMPT_TILES)
    def _():
        ys_ref[...] = y[0:DEC_BATCH, :]


def _combine(slots, gates, x1p, x1s, pp, ps, expert_out, ln2g, ln2b, w_pg_b, w_ple_b):
    tile_idx = lambda i: (jnp.minimum(i, N_PROMPT_TILES - 1), 0)
    const2 = lambda i: (0, 0)
    return pl.pallas_call(
        _combine_kernel,
        grid=(N_TILES,),
        in_specs=[
            pl.BlockSpec((1, SUBLANES, TOK_TILE), lambda i: (i, 0, 0)),
            pl.BlockSpec((1, SUBLANES, TOK_TILE), lambda i: (i, 0, 0)),
            pl.BlockSpec((TOK_TILE, D_MODEL), tile_idx),
            pl.BlockSpec((TOK_TILE, D_MODEL), const2),
            pl.BlockSpec((TOK_TILE, PLE_DIM), tile_idx),
            pl.BlockSpec((TOK_TILE, PLE_DIM), const2),
            pl.BlockSpec((LOCAL_ROWS, D_MODEL), lambda i: (i, 0)),
            pl.BlockSpec(ln2g.shape, const2),
            pl.BlockSpec(ln2b.shape, const2),
            pl.BlockSpec(w_pg_b.shape, const2),
            pl.BlockSpec(w_ple_b.shape, const2),
        ],
        out_specs=[
            pl.BlockSpec((TOK_TILE, D_MODEL), tile_idx),
            pl.BlockSpec((DEC_BATCH, D_MODEL), const2),
        ],
        out_shape=[
            jax.ShapeDtypeStruct((N_PROMPT, D_MODEL), F32),
            jax.ShapeDtypeStruct((DEC_BATCH, D_MODEL), F32),
        ],
        compiler_params=pltpu.CompilerParams(
            dimension_semantics=("arbitrary",), vmem_limit_bytes=VMEM_LIMIT),
        name="combine",
    )(slots, gates, x1p, x1s, pp, ps, expert_out, ln2g, ln2b, w_pg_b, w_ple_b)


def _block_tables(nch):
    seg_start = (jnp.cumsum(nch, axis=1) - nch) * CHUNK_ROWS
    tot = jnp.sum(nch, axis=0)
    nblk = (tot + BLOCK_CHUNKS - 1) // BLOCK_CHUNKS
    blk_end = jnp.cumsum(nblk)
    blk_start = blk_end - nblk
    n_used = blk_end[-1:].astype(I32)
    blk_ids = jnp.arange(N_BLOCKS, dtype=I32)

    def expert_of(b):
        b = jnp.minimum(b, n_used[0] - 1)
        return jnp.minimum(jnp.sum(blk_end[None, :] <= b[:, None], axis=1), N_EXPERTS - 1).astype(I32)

    block_expert = expert_of(blk_ids)
    in_use = blk_ids < n_used[0]
    is_first = jnp.logical_or(blk_ids == 0, block_expert != expert_of(blk_ids - 1))
    block_first = jnp.logical_and(in_use, is_first).astype(I32)
    next_start = blk_end[block_expert]
    block_next = jnp.where(next_start < n_used[0], expert_of(next_start), -1).astype(I32)

    nch_t = nch.T
    seg_first = (blk_start[:, None] * BLOCK_CHUNKS + jnp.cumsum(nch_t, axis=1) - nch_t).reshape(-1)
    seg_count = nch_t.reshape(-1)
    seg_row = (jnp.arange(N_TILES, dtype=I32)[None, :] * LOCAL_ROWS + seg_start.T).reshape(-1)
    ent = jnp.arange(N_BLOCKS * BLOCK_CHUNKS, dtype=I32)
    d = ent[:, None] - seg_first[None, :]
    inside = jnp.logical_and(d >= 0, d < seg_count[None, :])
    row = jnp.sum(jnp.where(inside, seg_row[None, :] + d * CHUNK_ROWS, 0), axis=1)
    table = jnp.where(jnp.any(inside, axis=1), row, ZERO_CHUNK_ROW).astype(I32)
    return table, block_expert, block_first, block_next, n_used


def kernel(x_prompt, x_sample, state_pool, state_mlstm_C, state_mlstm_n, state_mlstm_m, p_prompt, p_sample, w_in, b_i, b_f, w_pool, pool_scale, mh_g, w_out, ln1_g, ln1_b, w_router, b_router, w_mlp1, b_mlp1, w_mlp2, b_mlp2, ln2_g, ln2_b, w_ple, w_ple_gate):
    n_main = POOL_WIDTH + 4 * MLSTM_WIDTH
    w_in_b = w_in[0, :, 0:n_main].astype(BF16)
    w_g_b = jnp.pad(w_in[0, :, n_main:], ((0, 0), (0, LANES - 2 * HEADS))).astype(BF16)
    gbias = jnp.pad(jnp.concatenate([b_i[0], b_f[0]]), (0, LANES - 2 * HEADS)).reshape(1, LANES)
    w_pool_b = w_pool[0].astype(BF16)
    pscale = pool_scale[0].reshape(1, POOL_WIDTH)
    mhg = mh_g[0].reshape(1, MLSTM_WIDTH)
    w_out_b = w_out[0].astype(BF16)
    ln1g = ln1_g[0].reshape(1, D_MODEL)
    ln1b = ln1_b[0].reshape(1, D_MODEL)
    ln2g = ln2_g[0].reshape(1, D_MODEL)
    ln2b = ln2_b[0].reshape(1, D_MODEL)
    w_router_t = w_router[0].T
    b_router_col = jnp.broadcast_to(b_router[0].reshape(N_EXPERTS, 1), (N_EXPERTS, LANES))
    b1 = b_mlp1[0]
    b1p = jnp.concatenate([b1[:, 0::2], b1[:, 1::2]], axis=-1).reshape(N_EXPERTS, 1, 2 * D_FF)
    b2 = b_mlp2[0].reshape(N_EXPERTS, 1, D_MODEL)
    w_pg_b = w_ple_gate[0].astype(BF16)
    w_ple_b = w_ple[0].astype(BF16)

    x1p, pool_p, c_p, n_p, m_p = _prompt_mixer(
        x_prompt, w_in_b, w_g_b, gbias, w_pool_b, pscale, mhg, w_out_b, ln1g, ln1b)
    x1s, pool_s, c_s, n_s, m_s = _sample_mixer(
        x_sample.reshape(DEC_BATCH, D_MODEL),
        state_pool[0].reshape(DEC_BATCH, POOL_HIST * POOL_WIDTH),
        state_mlstm_C[0], state_mlstm_n[0].reshape(DEC_BATCH, MLSTM_WIDTH), state_mlstm_m[0],
        w_in_b, w_g_b, gbias, w_pool_b, pscale, mhg, w_out_b, ln1g, ln1b)

    slots, gates, nch, sorted_rows = _route(x1p, x1s, w_router_t, b_router_col)
    tables = _block_tables(nch[:, :, 0].astype(I32))
    expert_out = _experts(*tables, sorted_rows, w_mlp1[0], b1p, w_mlp2[0], b2)

    pp = p_prompt[0].reshape(N_PROMPT, PLE_DIM)
    ps = jnp.pad(p_sample[0].reshape(DEC_BATCH, PLE_DIM), ((0, TOK_TILE - DEC_BATCH), (0, 0)))
    yp, ys = _combine(slots, gates, x1p, x1s, pp, ps, expert_out, ln2g, ln2b, w_pg_b, w_ple_b)

    return (
        yp.reshape(BATCH, SEQ, D_MODEL),
        ys.reshape(DEC_BATCH, 1, D_MODEL),
        pool_p.reshape(1, BATCH, POOL_HIST, POOL_WIDTH),
        c_p.reshape(1, BATCH, HEADS, HEAD_DIM, HEAD_DIM),
        n_p.reshape(1, BATCH, HEADS, HEAD_DIM),
        m_p[:, 0:HEADS, 0].reshape(1, BATCH, HEADS),
        pool_s.reshape(1, DEC_BATCH, POOL_HIST, POOL_WIDTH),
        c_s.reshape(1, DEC_BATCH, HEADS, HEAD_DIM, HEAD_DIM),
        n_s.reshape(1, DEC_BATCH, HEADS, HEAD_DIM),
        m_s[:, HEADS:2 * HEADS].reshape(1, DEC_BATCH, HEADS),
    )
```

```python
import jax
import jax.numpy as jnp
from jax import lax
from jax.experimental import pallas as pl
from jax.experimental.pallas import tpu as pltpu

F32 = jnp.float32
BF16 = jnp.bfloat16
I32 = jnp.int32

D_MODEL = 1024
BATCH = 8
SEQ = 2048
DEC_BATCH = 128
PAST_LEN = 16384
POOL_WIDTH = 512
POOL_GROUPS = 4
POOL_GROUP_DIM = 128
POOL_WINDOWS = (2, 4, 8, 16)
POOL_HIST = 15
MLSTM_WIDTH = 512
HEADS = 4
HEAD_DIM = 128
CHUNK = 128
N_EXPERTS = 32
TOP_K = 4
D_FF = 1024
SWIGLU_ALPHA = 1.702
SWIGLU_LIMIT = 7.0
PLE_DIM = 256
DN_ALPHA = 2.0 ** 0.25
LN_EPS = 1e-5

LANES = 128
SUBLANES = 8
BF16_ROWS = 16
MXU_COLS = 256
VMEM_LIMIT = 56 * 1024 * 1024

MIX_TILE = 256
MIX_SEQS = 2
HIST_PAD = 16
TOK_TILE = 512
N_PROMPT = BATCH * SEQ
N_PROMPT_TILES = N_PROMPT // TOK_TILE
N_TILES = N_PROMPT_TILES + 1
SAMPLE_BT = 16

CHUNK_ROWS = BF16_ROWS
LOCAL_ROWS = TOK_TILE * TOP_K + N_EXPERTS * CHUNK_ROWS
GROUPS = LOCAL_ROWS // TOK_TILE
ROW_BLOCK = 256
BLOCK_CHUNKS = ROW_BLOCK // CHUNK_ROWS
MAX_CHUNKS = N_TILES * (TOK_TILE * TOP_K // CHUNK_ROWS + N_EXPERTS)
N_BLOCKS = -(-MAX_CHUNKS // BLOCK_CHUNKS) + N_EXPERTS
ZERO_CHUNK_ROW = LOCAL_ROWS - CHUNK_ROWS
FREE_ROWS = LOCAL_ROWS - TOK_TILE * TOP_K
DUMP_BASE = N_TILES * LOCAL_ROWS
DUMP_ROWS = 2 * ROW_BLOCK
assert DUMP_ROWS <= FREE_ROWS


def _dot(a, b):
    return jnp.dot(a, b, preferred_element_type=F32)


def _dot_nt(a, b):
    return lax.dot_general(a, b, (((1,), (1,)), ((), ())), preferred_element_type=F32)


def _dot_tn(a, b):
    return lax.dot_general(a, b, (((0,), (0,)), ((), ())), preferred_element_type=F32)


def _split3(a):
    a0 = a.astype(BF16)
    r1 = a - a0.astype(F32)
    a1 = r1.astype(BF16)
    r2 = r1 - a1.astype(F32)
    return a0, a1, r2.astype(BF16)


def _log_sigmoid(x):
    return jnp.minimum(x, 0.0) - jnp.log1p(jnp.exp(-jnp.abs(x)))


def _layer_norm(x, g, b):
    mu = jnp.mean(x, axis=-1, keepdims=True)
    xc = x - mu
    var = jnp.mean(xc * xc, axis=-1, keepdims=True)
    return xc * lax.rsqrt(var + LN_EPS) * g + b


def _gate_values(g, gbias):
    lane = lax.broadcasted_iota(I32, g.shape, 1)
    z = g + gbias
    return jnp.where(lane < HEADS, z, _log_sigmoid(z))


def _head_out(hh, o_h, gain):
    mu = jnp.mean(hh, axis=-1, keepdims=True)
    hc = hh - mu
    var = jnp.mean(hc * hc, axis=-1, keepdims=True)
    return jax.nn.sigmoid(o_h) * (hc * lax.rsqrt(var + LN_EPS) * gain)


def _prompt_mixer_kernel(x_ref, win_ref, wg_ref, gb_ref, wpool_ref, pscale_ref, mhg_ref, wout_ref,
                         ln1g_ref, ln1b_ref,
                         x1_ref, pool_ref, c_out_ref, n_out_ref, m_out_ref,
                         ubuf, mixbuf, c_s, n_s, m_s):
    ti = pl.program_id(1)
    nt = pl.num_programs(1)
    TT = MIX_TILE
    S = MIX_SEQS

    @pl.when(ti == 0)
    def _():
        for s in range(S):
            ubuf[s, 0:HIST_PAD, :] = jnp.zeros((HIST_PAD, POOL_WIDTH), F32)
        c_s[...] = jnp.zeros_like(c_s)
        n_s[...] = jnp.zeros_like(n_s)
        m_s[...] = jnp.zeros_like(m_s)

    x = jnp.concatenate([x_ref[s, 0] for s in range(S)], axis=0)
    xb = x.astype(BF16)
    proj = _dot(xb, win_ref[...])
    g = _dot(xb, wg_ref[...])

    L = CHUNK
    row = lax.broadcasted_iota(I32, (L, L), 0)
    col = lax.broadcasted_iota(I32, (L, L), 1)
    causal = row >= col
    tril = jnp.where(causal, 1.0, 0.0).astype(BF16)
    pos = ti * TT + lax.broadcasted_iota(I32, (TT, 1), 0)

    for s in range(S):
        base = s * TT
        u = proj[base:base + TT, 0:POOL_WIDTH]

        ubuf[s, HIST_PAD:HIST_PAD + TT, :] = u
        for gi, w in enumerate(POOL_WINDOWS):
            sl = slice(gi * POOL_GROUP_DIM, (gi + 1) * POOL_GROUP_DIM)
            ug = u[:, sl]
            acc = ug
            for i in range(1, w):
                acc = acc + ubuf[s, HIST_PAD - i:HIST_PAD - i + TT, sl]
            cnt = jnp.minimum(pos + 1, w).astype(F32)
            z = acc / cnt - ug
            mixbuf[base:base + TT, sl] = _dot(z.astype(BF16), wpool_ref[gi]) * pscale_ref[:, sl]

        @pl.when(ti == nt - 1)
        def _():
            pool_ref[s, 0] = ubuf[s, TT + 1:TT + HIST_PAD, :]

        ubuf[s, 0:HIST_PAD, :] = ubuf[s, TT:TT + HIST_PAD, :]

    NC = TT // L
    chains = [(s, h) for s in range(S) for h in range(HEADS)]
    units = [(s, c, h) for c in range(NC) for s in range(S) for h in range(HEADS)]
    U = range(len(units))

    def rows(s, c):
        return slice(s * TT + c * L, s * TT + (c + 1) * L)

    def head_cols(part, h):
        return slice(part * POOL_WIDTH + h * HEAD_DIM, part * POOL_WIDTH + (h + 1) * HEAD_DIM)

    gate, cum, gate_t, cum_t = {}, {}, {}, {}
    for c in range(NC):
        for s in range(S):
            val = _gate_values(g[rows(s, c), :], gb_ref[...])
            v0, v1, v2 = _split3(val)
            gate[s, c] = val
            cum[s, c] = _dot(tril, v0) + _dot(tril, v1) + _dot(tril, v2)
    for key in gate:
        gate_t[key] = gate[key].T
        cum_t[key] = cum[key].T
    qf = [proj[rows(s, c), head_cols(1, h)] for s, c, h in units]
    kf = [proj[rows(s, c), head_cols(2, h)] * (HEAD_DIM ** -0.5) for s, c, h in units]
    vf = [proj[rows(s, c), head_cols(3, h)] for s, c, h in units]
    qb = [a.astype(BF16) for a in qf]
    kb = [a.astype(BF16) for a in kf]
    f_col = [cum[s, c][:, HEADS + h:HEADS + h + 1] for s, c, h in units]
    log_d = [jnp.where(causal, f_col[u] - cum_t[s, c][HEADS + h:HEADS + h + 1, :] + gate_t[s, c][h:h + 1, :],
                       -jnp.inf) for u, (s, c, h) in enumerate(units)]
    row_max = [jnp.max(log_d[u], axis=-1, keepdims=True) for u in U]
    qk_raw = [_dot_nt(qb[u], kb[u]) for u in U]

    m_prev, m_t, inter = [None] * len(units), [None] * len(units), [None] * len(units)
    m_run = {(s, h): m_s[s, h:h + 1, 0:1] for s, h in chains}
    for u, (s, c, h) in enumerate(units):
        m_prev[u] = m_run[s, h]
        inter[u] = m_prev[u] + f_col[u]
        m_t[u] = jnp.maximum(inter[u], row_max[u])
        m_run[s, h] = m_t[u][L - 1:L, :]
    m_new = [m_t[u][L - 1:L, :] for u in U]

    dw = [jnp.exp(log_d[u] - m_t[u]) for u in U]
    sc = [jnp.exp(inter[u] - m_t[u]) for u in U]
    qk = [qk_raw[u] * dw[u] for u in U]
    intra = [_dot(qk[u].astype(BF16), vf[u].astype(BF16)) for u in U]
    row_sum = [jnp.sum(qk[u], axis=-1, keepdims=True) for u in U]
    floor = [jnp.exp(-m_t[u]) for u in U]
    f_last = [f_col[u][L - 1:L, :] for u in U]
    wk = [jnp.exp(gate[s, c][:, h:h + 1] + f_last[u] - f_col[u] - m_new[u]) for u, (s, c, h) in enumerate(units)]
    decay = [jnp.exp(m_prev[u] + f_last[u] - m_new[u]) for u in U]
    upd = [_dot_tn((vf[u] * wk[u]).astype(BF16), kb[u]) for u in U]
    n_upd = [jnp.sum(wk[u] * kf[u], axis=0, keepdims=True) for u in U]

    c_run = {(s, h): c_s[s, h] for s, h in chains}
    n_run = {(s, h): n_s[s, h:h + 1, :] for s, h in chains}
    hh = [None] * len(units)
    for c in range(NC):
        cu = [u for u in U if units[u][1] == c]
        inter_term = {u: _dot_nt(qb[u], c_run[units[u][0], units[u][2]].astype(BF16)) for u in cu}
        n_term = {u: jnp.sum(qf[u] * n_run[units[u][0], units[u][2]], axis=-1, keepdims=True) for u in cu}
        for u in cu:
            s, _, h = units[u]
            num = intra[u] + sc[u] * inter_term[u]
            den = row_sum[u] + sc[u] * n_term[u]
            hh[u] = num / jnp.maximum(jnp.abs(den), floor[u])
            c_run[s, h] = decay[u] * c_run[s, h] + upd[u]
            n_run[s, h] = decay[u] * n_run[s, h] + n_upd[u]
    for s, h in chains:
        c_s[s, h] = c_run[s, h]
        n_s[s, h:h + 1, :] = n_run[s, h]
        m_s[s, h:h + 1, :] = jnp.broadcast_to(m_run[s, h], (1, LANES))
    for u, (s, c, h) in enumerate(units):
        mixbuf[rows(s, c), head_cols(1, h)] = _head_out(
            hh[u], proj[rows(s, c), head_cols(4, h)], mhg_ref[:, h * HEAD_DIM:(h + 1) * HEAD_DIM])

    @pl.when(ti == nt - 1)
    def _():
        for s in range(S):
            c_out_ref[s, 0] = c_s[s]
            n_out_ref[s, 0] = n_s[s, 0:HEADS, :]
            m_out_ref[s, 0] = m_s[s]

    mix = _dot(mixbuf[...].astype(BF16), wout_ref[...])
    x1 = _layer_norm(DN_ALPHA * x + mix, ln1g_ref[...], ln1b_ref[...])
    for s in range(S):
        x1_ref[s] = x1[s * TT:(s + 1) * TT, :]


def _prompt_mixer(x, w_in_b, w_g_b, gbias, w_pool_b, pscale, mhg, w_out_b, ln1g, ln1b):
    nt = SEQ // MIX_TILE
    S = MIX_SEQS
    G = BATCH // S
    const2 = lambda b, t: (0, 0)
    const3 = lambda b, t: (0, 0, 0)
    outs = pl.pallas_call(
        _prompt_mixer_kernel,
        grid=(G, nt),
        in_specs=[
            pl.BlockSpec((S, 1, MIX_TILE, D_MODEL), lambda b, t: (0, b, t, 0)),
            pl.BlockSpec(w_in_b.shape, const2),
            pl.BlockSpec(w_g_b.shape, const2),
            pl.BlockSpec(gbias.shape, const2),
            pl.BlockSpec(w_pool_b.shape, const3),
            pl.BlockSpec(pscale.shape, const2),
            pl.BlockSpec(mhg.shape, const2),
            pl.BlockSpec(w_out_b.shape, const2),
            pl.BlockSpec(ln1g.shape, const2),
            pl.BlockSpec(ln1b.shape, const2),
        ],
        out_specs=[
            pl.BlockSpec((S, MIX_TILE, D_MODEL), lambda b, t: (0, b * nt + t, 0)),
            pl.BlockSpec((S, 1, POOL_HIST, POOL_WIDTH), lambda b, t: (0, b, 0, 0)),
            pl.BlockSpec((S, 1, HEADS, HEAD_DIM, HEAD_DIM), lambda b, t: (0, b, 0, 0, 0)),
            pl.BlockSpec((S, 1, HEADS, HEAD_DIM), lambda b, t: (0, b, 0, 0)),
            pl.BlockSpec((S, 1, SUBLANES, LANES), lambda b, t: (0, b, 0, 0)),
        ],
        out_shape=[
            jax.ShapeDtypeStruct((S, G * SEQ, D_MODEL), F32),
            jax.ShapeDtypeStruct((S, G, POOL_HIST, POOL_WIDTH), F32),
            jax.ShapeDtypeStruct((S, G, HEADS, HEAD_DIM, HEAD_DIM), F32),
            jax.ShapeDtypeStruct((S, G, HEADS, HEAD_DIM), F32),
            jax.ShapeDtypeStruct((S, G, SUBLANES, LANES), F32),
        ],
        scratch_shapes=[
            pltpu.VMEM((S, HIST_PAD + MIX_TILE, POOL_WIDTH), F32),
            pltpu.VMEM((S * MIX_TILE, D_MODEL), F32),
            pltpu.VMEM((S, HEADS, HEAD_DIM, HEAD_DIM), F32),
            pltpu.VMEM((S, SUBLANES, HEAD_DIM), F32),
            pltpu.VMEM((S, SUBLANES, LANES), F32),
        ],
        compiler_params=pltpu.CompilerParams(
            dimension_semantics=("arbitrary", "arbitrary"), vmem_limit_bytes=VMEM_LIMIT),
        name="prompt_mixer",
    )(x.reshape(S, G, SEQ, D_MODEL), w_in_b, w_g_b, gbias, w_pool_b, pscale, mhg, w_out_b, ln1g, ln1b)
    x1, pool, c, n, m = outs
    return (x1.reshape(N_PROMPT, D_MODEL), pool.reshape(BATCH, POOL_HIST, POOL_WIDTH),
            c.reshape(BATCH, HEADS, HEAD_DIM, HEAD_DIM), n.reshape(BATCH, HEADS, HEAD_DIM),
            m.reshape(BATCH, SUBLANES, LANES))


def _sample_mixer_kernel(x_ref, hist_ref, c_ref, n_ref, m_ref, win_ref, wg_ref, gb_ref, wpool_ref,
                         pscale_ref, mhg_ref, wout_ref, ln1g_ref, ln1b_ref,
                         x1_ref, pool_out_ref, c_out_ref, n_out_ref, m_out_ref,
                         q_s, k_s, vw_s, v_s, o_s, mixbuf, h_s, coef_s):
    i = pl.program_id(0)
    nsteps = pl.num_programs(0)
    B = DEC_BATCH

    @pl.when(i == 0)
    def _():
        x = x_ref[...]
        xb = x.astype(BF16)
        proj = _dot(xb, win_ref[...])
        g = _dot(xb, wg_ref[...])
        u = proj[:, 0:POOL_WIDTH]
        for gi, w in enumerate(POOL_WINDOWS):
            sl = slice(gi * POOL_GROUP_DIM, (gi + 1) * POOL_GROUP_DIM)
            ug = u[:, sl]
            s = ug
            for j in range(1, w):
                r = POOL_HIST - j
                s = s + hist_ref[:, r * POOL_WIDTH + gi * POOL_GROUP_DIM:r * POOL_WIDTH + (gi + 1) * POOL_GROUP_DIM]
            cnt = float(min(PAST_LEN + 1, w))
            z = s / cnt - ug
            mixbuf[:, sl] = _dot(z.astype(BF16), wpool_ref[gi]) * pscale_ref[:, sl]
        pool_out_ref[:, 0:(POOL_HIST - 1) * POOL_WIDTH] = hist_ref[:, POOL_WIDTH:POOL_HIST * POOL_WIDTH]
        pool_out_ref[:, (POOL_HIST - 1) * POOL_WIDTH:POOL_HIST * POOL_WIDTH] = u

        val = _gate_values(g, gb_ref[...])
        lane = lax.broadcasted_iota(I32, (B, LANES), 1)
        qk_all = jnp.zeros((B, LANES), F32)
        sc_all = jnp.zeros((B, LANES), F32)
        den_all = jnp.zeros((B, LANES), F32)
        floor_all = jnp.zeros((B, LANES), F32)
        m_all = jnp.zeros((B, LANES), F32)
        for h in range(HEADS):
            hs = slice(h * HEAD_DIM, (h + 1) * HEAD_DIM)
            qf = proj[:, POOL_WIDTH + h * HEAD_DIM:POOL_WIDTH + (h + 1) * HEAD_DIM]
            kf = proj[:, 2 * POOL_WIDTH + h * HEAD_DIM:2 * POOL_WIDTH + (h + 1) * HEAD_DIM] * (HEAD_DIM ** -0.5)
            vf = proj[:, 3 * POOL_WIDTH + h * HEAD_DIM:3 * POOL_WIDTH + (h + 1) * HEAD_DIM]
            ig = val[:, h:h + 1]
            lf = val[:, HEADS + h:HEADS + h + 1]
            m0 = m_ref[:, h:h + 1]
            n0 = n_ref[:, hs]
            inter = m0 + lf
            m_t = jnp.maximum(inter, ig)
            dw = jnp.exp(ig - m_t)
            sc = jnp.exp(inter - m_t)
            qk = jnp.sum(qf * kf, axis=-1, keepdims=True) * dw
            den = qk + sc * jnp.sum(qf * n0, axis=-1, keepdims=True)
            n_out_ref[:, hs] = sc * n0 + dw * kf
            q_s[0:B, hs] = qf
            k_s[0:B, hs] = kf
            v_s[0:B, hs] = vf
            vw_s[0:B, hs] = vf * dw
            sel = lane == h
            qk_all = jnp.where(sel, qk, qk_all)
            sc_all = jnp.where(sel, sc, sc_all)
            den_all = jnp.where(sel, den, den_all)
            floor_all = jnp.where(sel, jnp.exp(-m_t), floor_all)
            m_all = jnp.where(lane == HEADS + h, m_t, m_all)
        o_s[...] = proj[:, 4 * POOL_WIDTH:5 * POOL_WIDTH]
        coef_s[0] = qk_all
        coef_s[1] = sc_all
        coef_s[2] = den_all
        coef_s[3] = floor_all
        m_out_ref[...] = m_all

    rows = pl.ds(pl.multiple_of(i * SAMPLE_BT, SAMPLE_BT), SAMPLE_BT)
    q_t, k_t, v_t, vw_t = q_s[rows, :], k_s[rows, :], v_s[rows, :], vw_s[rows, :]
    qk_t, sc_t, den_t, floor_t = coef_s[0, rows, :], coef_s[1, rows, :], coef_s[2, rows, :], coef_s[3, rows, :]
    h_rows = []
    for bl in range(SAMPLE_BT):
        heads = []
        for h in range(HEADS):
            hs = slice(h * HEAD_DIM, (h + 1) * HEAD_DIM)
            c_prev = c_ref[bl, h]
            q8 = jnp.broadcast_to(q_t[bl:bl + 1, hs], (SUBLANES, HEAD_DIM))
            cq = _dot_nt(q8.astype(BF16), c_prev.astype(BF16))[0:1, :]
            qk = qk_t[bl:bl + 1, h:h + 1]
            sc = sc_t[bl:bl + 1, h:h + 1]
            num = qk * v_t[bl:bl + 1, hs] + sc * cq
            heads.append(num / jnp.maximum(jnp.abs(den_t[bl:bl + 1, h:h + 1]), floor_t[bl:bl + 1, h:h + 1]))
            v_col = jnp.broadcast_to(vw_t[bl:bl + 1, hs], (HEAD_DIM, HEAD_DIM)).T
            c_out_ref[bl, h] = sc * c_prev + v_col * k_t[bl:bl + 1, hs]
        h_rows.append(jnp.concatenate(heads, axis=1))
    h_s[rows, :] = jnp.concatenate(h_rows, axis=0)

    @pl.when(i == nsteps - 1)
    def _():
        for h in range(HEADS):
            hs = slice(h * HEAD_DIM, (h + 1) * HEAD_DIM)
            mixbuf[:, POOL_WIDTH + h * HEAD_DIM:POOL_WIDTH + (h + 1) * HEAD_DIM] = _head_out(
                h_s[:, hs], o_s[:, hs], mhg_ref[:, hs])
        mix = _dot(mixbuf[...].astype(BF16), wout_ref[...])
        x1 = _layer_norm(DN_ALPHA * x_ref[...] + mix, ln1g_ref[...], ln1b_ref[...])
        x1_ref[0:B, :] = x1
        x1_ref[B:TOK_TILE, :] = jnp.zeros((TOK_TILE - B, D_MODEL), F32)


def _sample_mixer(x, hist2, c0, n0, m0, w_in_b, w_g_b, gbias, w_pool_b, pscale, mhg, w_out_b, ln1g, ln1b):
    B = DEC_BATCH
    steps = B // SAMPLE_BT
    full = lambda a: pl.BlockSpec(a.shape, lambda i: (0,) * a.ndim)
    c_spec = pl.BlockSpec((SAMPLE_BT, HEADS, HEAD_DIM, HEAD_DIM), lambda i: (i, 0, 0, 0))
    return pl.pallas_call(
        _sample_mixer_kernel,
        grid=(steps,),
        in_specs=[full(x), full(hist2), c_spec, full(n0), full(m0), full(w_in_b), full(w_g_b), full(gbias),
                  full(w_pool_b), full(pscale), full(mhg), full(w_out_b), full(ln1g), full(ln1b)],
        out_specs=[
            pl.BlockSpec((TOK_TILE, D_MODEL), lambda i: (0, 0)),
            pl.BlockSpec((B, POOL_HIST * POOL_WIDTH), lambda i: (0, 0)),
            c_spec,
            pl.BlockSpec((B, MLSTM_WIDTH), lambda i: (0, 0)),
            pl.BlockSpec((B, LANES), lambda i: (0, 0)),
        ],
        out_shape=[
            jax.ShapeDtypeStruct((TOK_TILE, D_MODEL), F32),
            jax.ShapeDtypeStruct((B, POOL_HIST * POOL_WIDTH), F32),
            jax.ShapeDtypeStruct((B, HEADS, HEAD_DIM, HEAD_DIM), F32),
            jax.ShapeDtypeStruct((B, MLSTM_WIDTH), F32),
            jax.ShapeDtypeStruct((B, LANES), F32),
        ],
        scratch_shapes=[
            pltpu.VMEM((B, MLSTM_WIDTH), F32),
            pltpu.VMEM((B, MLSTM_WIDTH), F32),
            pltpu.VMEM((B, MLSTM_WIDTH), F32),
            pltpu.VMEM((B, MLSTM_WIDTH), F32),
            pltpu.VMEM((B, MLSTM_WIDTH), F32),
            pltpu.VMEM((B, D_MODEL), F32),
            pltpu.VMEM((B, MLSTM_WIDTH), F32),
            pltpu.VMEM((4, B, LANES), F32),
        ],
        compiler_params=pltpu.CompilerParams(
            dimension_semantics=("arbitrary",), vmem_limit_bytes=VMEM_LIMIT),
        name="sample_mixer",
    )(x, hist2, c0, n0, m0, w_in_b, w_g_b, gbias, w_pool_b, pscale, mhg, w_out_b, ln1g, ln1b)


def _pick_tile(i, prompt_ref, sample_ref):
    return jnp.where(i < N_PROMPT_TILES, prompt_ref[...], sample_ref[...])


def _placement(slot_rows, group):
    r = group * TOK_TILE + lax.broadcasted_iota(I32, (TOK_TILE, TOK_TILE), 0)
    return [r == s for s in slot_rows]


def _route_kernel(xp_ref, xs_ref, wrt_ref, br_ref, slot_ref, gate_ref, nch_ref, sorted_ref):
    i = pl.program_id(0)
    T = TOK_TILE
    E = N_EXPERTS

    x = _pick_tile(i, xp_ref, xs_ref)
    xh = x.astype(BF16)
    xl = (x - xh.astype(F32)).astype(BF16)
    w = wrt_ref[...]
    wh = w.astype(BF16)
    wl = (w - wh.astype(F32)).astype(BF16)
    logits = _dot_nt(wh, xh) + (_dot_nt(wh, xl) + _dot_nt(wl, xh)) + br_ref[:, 0:1]

    erow = lax.broadcasted_iota(I32, (E, T), 0).astype(F32)
    work = logits
    vals, sels = [], []
    for _ in range(TOP_K):
        mx = jnp.max(work, axis=0, keepdims=True)
        idx = jnp.min(jnp.where(work == mx, erow, float(E)), axis=0, keepdims=True)
        sel = erow == idx
        work = jnp.where(sel, -jnp.inf, work)
        vals.append(mx)
        sels.append(sel)
    chosen = jnp.logical_or(jnp.logical_or(sels[0], sels[1]), jnp.logical_or(sels[2], sels[3]))
    es = [jnp.exp(v - vals[0]) for v in vals]
    tot = es[0] + es[1] + es[2] + es[3]

    onehot = jnp.where(chosen, 1.0, 0.0)
    trow = lax.broadcasted_iota(I32, (T, T), 0)
    tcol = lax.broadcasted_iota(I32, (T, T), 1)
    before = jnp.where(trow < tcol, 1.0, 0.0).astype(BF16)
    rank = _dot(onehot.astype(BF16), before)
    cnt = jnp.sum(onehot, axis=1, keepdims=True)
    nch = jnp.floor((cnt + (CHUNK_ROWS - 1)) * (1.0 / CHUNK_ROWS))
    lower = jnp.where(lax.broadcasted_iota(I32, (E, E), 0) > lax.broadcasted_iota(I32, (E, E), 1), 1.0, 0.0)
    nch_b = jnp.broadcast_to(nch, (E, LANES))
    seg_start = _dot(lower.astype(BF16), nch_b.astype(BF16))[:, 0:1] * CHUNK_ROWS
    base = seg_start + rank

    r8 = lax.broadcasted_iota(I32, (SUBLANES, T), 0)
    s_out = jnp.zeros((SUBLANES, T), I32)
    g_out = jnp.zeros((SUBLANES, T), F32)
    slot_rows = []
    for j in range(TOP_K):
        slot_j = jnp.sum(jnp.where(sels[j], base, 0.0), axis=0, keepdims=True).astype(I32)
        slot_rows.append(slot_j)
        s_out = jnp.where(r8 == j, slot_j, s_out)
        g_out = jnp.where(r8 == j, es[j] / tot, g_out)
    slot_ref[0] = s_out
    gate_ref[0] = g_out
    nch_ref[0] = nch_b

    for grp in range(GROUPS):
        m = _placement(slot_rows, grp)
        hit = jnp.logical_or(jnp.logical_or(m[0], m[1]), jnp.logical_or(m[2], m[3]))
        place = jnp.where(hit, 1.0, 0.0).astype(BF16)
        sorted_ref[grp * T:(grp + 1) * T, :] = _dot(place, xh).astype(BF16)


def _route(x1p, x1s, w_router_t, b_router_col):
    tile_spec = pl.BlockSpec((1, SUBLANES, TOK_TILE), lambda i: (i, 0, 0))
    return pl.pallas_call(
        _route_kernel,
        grid=(N_TILES,),
        in_specs=[
            pl.BlockSpec((TOK_TILE, D_MODEL), lambda i: (jnp.minimum(i, N_PROMPT_TILES - 1), 0)),
            pl.BlockSpec((TOK_TILE, D_MODEL), lambda i: (0, 0)),
            pl.BlockSpec(w_router_t.shape, lambda i: (0, 0)),
            pl.BlockSpec(b_router_col.shape, lambda i: (0, 0)),
        ],
        out_specs=[tile_spec, tile_spec,
                   pl.BlockSpec((1, N_EXPERTS, LANES), lambda i: (i, 0, 0)),
                   pl.BlockSpec((LOCAL_ROWS, D_MODEL), lambda i: (i, 0))],
        out_shape=[
            jax.ShapeDtypeStruct((N_TILES, SUBLANES, TOK_TILE), I32),
            jax.ShapeDtypeStruct((N_TILES, SUBLANES, TOK_TILE), F32),
            jax.ShapeDtypeStruct((N_TILES, N_EXPERTS, LANES), F32),
            jax.ShapeDtypeStruct((N_TILES * LOCAL_ROWS, D_MODEL), BF16),
        ],
        compiler_params=pltpu.CompilerParams(
            dimension_semantics=("arbitrary",), vmem_limit_bytes=VMEM_LIMIT),
        name="route",
    )(x1p, x1s, w_router_t, b_router_col)


def _expert_kernel(src_ref, dst_ref, bexp_ref, first_ref, next_ref, nused_ref,
                   sorted_hbm, w1_hbm, b1_ref, w2_hbm, b2_ref, out_hbm,
                   w1_stage, w2_stage, w1_b, w2_b, xbuf, obuf, zbuf, wsem, gsem, ssem, zsem):
    nused = nused_ref[0]

    def fetch(e):
        return (pltpu.make_async_copy(w1_hbm.at[e], w1_stage, wsem.at[0]),
                pltpu.make_async_copy(w2_hbm.at[e], w2_stage, wsem.at[1]))

    def gather(b, q):
        slot = lax.rem(b, 2)
        row = pl.multiple_of(src_ref[b * BLOCK_CHUNKS + q], CHUNK_ROWS)
        return pltpu.make_async_copy(sorted_hbm.at[pl.ds(row, CHUNK_ROWS), :],
                                     xbuf.at[slot, pl.ds(q * CHUNK_ROWS, CHUNK_ROWS), :], gsem.at[slot])

    def scatter(b, q):
        slot = lax.rem(b, 2)
        row = pl.multiple_of(dst_ref[b * BLOCK_CHUNKS + q], CHUNK_ROWS)
        return pltpu.make_async_copy(obuf.at[slot, pl.ds(q * CHUNK_ROWS, CHUNK_ROWS), :],
                                     out_hbm.at[pl.ds(row, CHUNK_ROWS), :], ssem.at[slot])

    def zero_rows(start, n_rows):
        start = pl.multiple_of(start, CHUNK_ROWS)
        return pltpu.make_async_copy(zbuf.at[pl.ds(0, n_rows), :], out_hbm.at[pl.ds(start, n_rows), :], zsem)

    def zero_tail(k):
        return zero_rows(k * LOCAL_ROWS + TOK_TILE * TOP_K, FREE_ROWS)

    zbuf[...] = jnp.zeros_like(zbuf)
    lax.fori_loop(0, N_TILES, lambda k, c: (zero_tail(k).start(), c)[1], 0)
    zero_rows(DUMP_BASE, DUMP_ROWS).start()
    for cp in fetch(bexp_ref[0]):
        cp.start()
    for q in range(BLOCK_CHUNKS):
        gather(0, q).start()
    lax.fori_loop(0, N_TILES, lambda k, c: (zero_tail(k).wait(), c)[1], 0)
    zero_rows(DUMP_BASE, DUMP_ROWS).wait()

    half = MXU_COLS // 2
    k_io = lax.broadcasted_iota(I32, (MXU_COLS, MXU_COLS), 0)
    j_io = lax.broadcasted_iota(I32, (MXU_COLS, MXU_COLS), 1)
    src_col = jnp.where(j_io < half, 2 * j_io, 2 * (j_io - half) + 1)
    perm = jnp.where(k_io == src_col, 1.0, 0.0).astype(BF16)

    def block(i, carry):
        e = bexp_ref[i]
        slot = lax.rem(i, 2)

        @pl.when(first_ref[i] == 1)
        def _():
            for cp in fetch(e):
                cp.wait()
            for c in range(2 * D_FF // MXU_COLS):
                blk = w1_stage[:, c * MXU_COLS:(c + 1) * MXU_COLS].astype(BF16)
                sep = _dot(blk, perm).astype(BF16)
                w1_b[:, c * half:(c + 1) * half] = sep[:, 0:half]
                w1_b[:, D_FF + c * half:D_FF + (c + 1) * half] = sep[:, half:MXU_COLS]
            w2_b[...] = w2_stage[...].astype(BF16)

            @pl.when(next_ref[i] >= 0)
            def _():
                for cp in fetch(next_ref[i]):
                    cp.start()

        for q in range(BLOCK_CHUNKS):
            gather(i, q).wait()

        @pl.when(i + 1 < nused)
        def _():
            for q in range(BLOCK_CHUNKS):
                gather(i + 1, q).start()

        @pl.when(i >= 2)
        def _():
            for q in range(BLOCK_CHUNKS):
                scatter(i - 2, q).wait()

        h = _dot(xbuf[slot], w1_b[...]) + b1_ref[e]
        glu = jnp.minimum(h[:, 0:D_FF], SWIGLU_LIMIT)
        lin = jnp.clip(h[:, D_FF:2 * D_FF], -SWIGLU_LIMIT, SWIGLU_LIMIT)
        a = glu * jax.nn.sigmoid(SWIGLU_ALPHA * glu) * (lin + 1.0)
        obuf[slot] = _dot(a.astype(BF16), w2_b[...]) + b2_ref[e]
        for q in range(BLOCK_CHUNKS):
            scatter(i, q).start()
        return carry

    lax.fori_loop(0, nused, block, 0)

    @pl.when(nused >= 2)
    def _():
        for q in range(BLOCK_CHUNKS):
            scatter(nused - 2, q).wait()
    for q in range(BLOCK_CHUNKS):
        scatter(nused - 1, q).wait()


def _experts(chunk_src, chunk_dst, block_expert, block_first, block_next, n_used, sorted_rows, w1, b1p, w2, b2):
    whole3 = lambda i, *_: (0, 0, 0)
    grid_spec = pltpu.PrefetchScalarGridSpec(
        num_scalar_prefetch=6,
        grid=(1,),
        in_specs=[
            pl.BlockSpec(memory_space=pl.ANY),
            pl.BlockSpec(memory_space=pl.ANY),
            pl.BlockSpec(b1p.shape, whole3),
            pl.BlockSpec(memory_space=pl.ANY),
            pl.BlockSpec(b2.shape, whole3),
        ],
        out_specs=pl.BlockSpec(memory_space=pl.ANY),
        scratch_shapes=[
            pltpu.VMEM((D_MODEL, 2 * D_FF), F32),
            pltpu.VMEM((D_FF, D_MODEL), F32),
            pltpu.VMEM((D_MODEL, 2 * D_FF), BF16),
            pltpu.VMEM((D_FF, D_MODEL), BF16),
            pltpu.VMEM((2, ROW_BLOCK, D_MODEL), BF16),
            pltpu.VMEM((2, ROW_BLOCK, D_MODEL), F32),
            pltpu.VMEM((FREE_ROWS, D_MODEL), F32),
            pltpu.SemaphoreType.DMA((2,)),
            pltpu.SemaphoreType.DMA((2,)),
            pltpu.SemaphoreType.DMA((2,)),
            pltpu.SemaphoreType.DMA(()),
        ],
    )
    return pl.pallas_call(
        _expert_kernel,
        grid_spec=grid_spec,
        out_shape=jax.ShapeDtypeStruct((DUMP_BASE + DUMP_ROWS, D_MODEL), F32),
        compiler_params=pltpu.CompilerParams(
            dimension_semantics=("arbitrary",), vmem_limit_bytes=VMEM_LIMIT),
        name="experts",
    )(chunk_src, chunk_dst, block_expert, block_first, block_next, n_used, sorted_rows, w1, b1p, w2, b2)


def _combine_kernel(slot_ref, gate_ref, xp_ref, xs_ref, pp_ref, ps_ref, eo_ref, ln2g_ref, ln2b_ref,
                    wpg_ref, wple_ref, yp_ref, ys_ref):
    i = pl.program_id(0)
    T = TOK_TILE

    slot_rows = [slot_ref[0, j:j + 1, :] for j in range(TOP_K)]
    gate_rows = [gate_ref[0, j:j + 1, :] for j in range(TOP_K)]
    pad = jnp.zeros((LANES - SUBLANES, T), F32)
    slots_t = jnp.concatenate([slot_ref[0].astype(F32), pad], axis=0).T
    slot_cols = [slots_t[:, j:j + 1].astype(I32) for j in range(TOP_K)]

    ff = jnp.zeros((T, D_MODEL), F32)
    for grp in range(GROUPS):
        m = _placement(slot_rows, grp)
        weighted = jnp.where(m[0], gate_rows[0], jnp.where(m[1], gate_rows[1], jnp.where(
            m[2], gate_rows[2], jnp.where(m[3], gate_rows[3], 0.0))))
        g_col = jnp.sum(weighted, axis=1, keepdims=True)
        z = (eo_ref[grp * T:(grp + 1) * T, :] * g_col).astype(BF16)
        r = grp * T + lax.broadcasted_iota(I32, (T, T), 1)
        hit = jnp.logical_or(jnp.logical_or(r == slot_cols[0], r == slot_cols[1]),
                             jnp.logical_or(r == slot_cols[2], r == slot_cols[3]))
        ff = ff + _dot(jnp.where(hit, 1.0, 0.0).astype(BF16), z)

    x1 = _pick_tile(i, xp_ref, xs_ref)
    x2 = _layer_norm(DN_ALPHA * x1 + ff, ln2g_ref[...], ln2b_ref[...])
    p = _pick_tile(i, pp_ref, ps_ref)
    y = x2 + jax.nn.sigmoid(_dot(x2.astype(BF16), wpg_ref[...])) * _dot(p.astype(BF16), wple_ref[...])

    @pl.when(i < N_PROMPT_TILES)
    def _():
        yp_ref[...] = y

    @pl.when(i == N_PROMPT_TILES)
    def _():
        ys_ref[...] = y[0:DEC_BATCH, :]


def _combine(slots, gates, x1p, x1s, pp, ps, expert_out, ln2g, ln2b, w_pg_b, w_ple_b):
    tile_idx = lambda i: (jnp.minimum(i, N_PROMPT_TILES - 1), 0)
    const2 = lambda i: (0, 0)
    return pl.pallas_call(
        _combine_kernel,
        grid=(N_TILES,),
        in_specs=[
            pl.BlockSpec((1, SUBLANES, TOK_TILE), lambda i: (i, 0, 0)),
            pl.BlockSpec((1, SUBLANES, TOK_TILE), lambda i: (i, 0, 0)),
            pl.BlockSpec((TOK_TILE, D_MODEL), tile_idx),
            pl.BlockSpec((TOK_TILE, D_MODEL), const2),
            pl.BlockSpec((TOK_TILE, PLE_DIM), tile_idx),
            pl.BlockSpec((TOK_TILE, PLE_DIM), const2),
            pl.BlockSpec((LOCAL_ROWS, D_MODEL), lambda i: (i, 0)),
            pl.BlockSpec(ln2g.shape, const2),
            pl.BlockSpec(ln2b.shape, const2),
            pl.BlockSpec(w_pg_b.shape, const2),
            pl.BlockSpec(w_ple_b.shape, const2),
        ],
        out_specs=[
            pl.BlockSpec((TOK_TILE, D_MODEL), tile_idx),
            pl.BlockSpec((DEC_BATCH, D_MODEL), const2),
        ],
        out_shape=[
            jax.ShapeDtypeStruct((N_PROMPT, D_MODEL), F32),
            jax.ShapeDtypeStruct((DEC_BATCH, D_MODEL), F32),
        ],
        compiler_params=pltpu.CompilerParams(
            dimension_semantics=("arbitrary",), vmem_limit_bytes=VMEM_LIMIT),
        name="combine",
    )(slots, gates, x1p, x1s, pp, ps, expert_out, ln2g, ln2b, w_pg_b, w_ple_b)


def _block_tables(nch):
    seg_start = (jnp.cumsum(nch, axis=1) - nch) * CHUNK_ROWS
    tot = jnp.sum(nch, axis=0)
    nblk = (tot + BLOCK_CHUNKS - 1) // BLOCK_CHUNKS
    blk_end = jnp.cumsum(nblk)
    blk_start = blk_end - nblk
    n_used = blk_end[-1:].astype(I32)
    blk_ids = jnp.arange(N_BLOCKS, dtype=I32)

    def expert_of(b):
        b = jnp.minimum(b, n_used[0] - 1)
        return jnp.minimum(jnp.sum(blk_end[None, :] <= b[:, None], axis=1), N_EXPERTS - 1).astype(I32)

    block_expert = expert_of(blk_ids)
    in_use = blk_ids < n_used[0]
    is_first = jnp.logical_or(blk_ids == 0, block_expert != expert_of(blk_ids - 1))
    block_first = jnp.logical_and(in_use, is_first).astype(I32)
    next_start = blk_end[block_expert]
    block_next = jnp.where(next_start < n_used[0], expert_of(next_start), -1).astype(I32)

    nch_t = nch.T
    seg_first = (blk_start[:, None] * BLOCK_CHUNKS + jnp.cumsum(nch_t, axis=1) - nch_t).reshape(-1)
    seg_count = nch_t.reshape(-1)
    seg_row = (jnp.arange(N_TILES, dtype=I32)[None, :] * LOCAL_ROWS + seg_start.T).reshape(-1)
    ent = jnp.arange(N_BLOCKS * BLOCK_CHUNKS, dtype=I32)
    d = ent[:, None] - seg_first[None, :]
    inside = jnp.logical_and(d >= 0, d < seg_count[None, :])
    row = jnp.sum(jnp.where(inside, seg_row[None, :] + d * CHUNK_ROWS, 0), axis=1)
    real = jnp.any(inside, axis=1)
    dump = DUMP_BASE + (((ent // BLOCK_CHUNKS) % 2) * BLOCK_CHUNKS + ent % BLOCK_CHUNKS) * CHUNK_ROWS
    chunk_src = jnp.where(real, row, ZERO_CHUNK_ROW).astype(I32)
    chunk_dst = jnp.where(real, row, dump).astype(I32)
    return chunk_src, chunk_dst, block_expert, block_first, block_next, n_used


def kernel(x_prompt, x_sample, state_pool, state_mlstm_C, state_mlstm_n, state_mlstm_m, p_prompt, p_sample, w_in, b_i, b_f, w_pool, pool_scale, mh_g, w_out, ln1_g, ln1_b, w_router, b_router, w_mlp1, b_mlp1, w_mlp2, b_mlp2, ln2_g, ln2_b, w_ple, w_ple_gate):
    n_main = POOL_WIDTH + 4 * MLSTM_WIDTH
    w_in_b = w_in[0, :, 0:n_main].astype(BF16)
    w_g_b = jnp.pad(w_in[0, :, n_main:], ((0, 0), (0, LANES - 2 * HEADS))).astype(BF16)
    gbias = jnp.pad(jnp.concatenate([b_i[0], b_f[0]]), (0, LANES - 2 * HEADS)).reshape(1, LANES)
    w_pool_b = w_pool[0].astype(BF16)
    pscale = pool_scale[0].reshape(1, POOL_WIDTH)
    mhg = mh_g[0].reshape(1, MLSTM_WIDTH)
    w_out_b = w_out[0].astype(BF16)
    ln1g = ln1_g[0].reshape(1, D_MODEL)
    ln1b = ln1_b[0].reshape(1, D_MODEL)
    ln2g = ln2_g[0].reshape(1, D_MODEL)
    ln2b = ln2_b[0].reshape(1, D_MODEL)
    w_router_t = w_router[0].T
    b_router_col = jnp.broadcast_to(b_router[0].reshape(N_EXPERTS, 1), (N_EXPERTS, LANES))
    b1 = b_mlp1[0]
    b1p = jnp.concatenate([b1[:, 0::2], b1[:, 1::2]], axis=-1).reshape(N_EXPERTS, 1, 2 * D_FF)
    b2 = b_mlp2[0].reshape(N_EXPERTS, 1, D_MODEL)
    w_pg_b = w_ple_gate[0].astype(BF16)
    w_ple_b = w_ple[0].astype(BF16)

    x1p, pool_p, c_p, n_p, m_p = _prompt_mixer(
        x_prompt, w_in_b, w_g_b, gbias, w_pool_b, pscale, mhg, w_out_b, ln1g, ln1b)
    x1s, pool_s, c_s, n_s, m_s = _sample_mixer(
        x_sample.reshape(DEC_BATCH, D_MODEL),
        state_pool[0].reshape(DEC_BATCH, POOL_HIST * POOL_WIDTH),
        state_mlstm_C[0], state_mlstm_n[0].reshape(DEC_BATCH, MLSTM_WIDTH), state_mlstm_m[0],
        w_in_b, w_g_b, gbias, w_pool_b, pscale, mhg, w_out_b, ln1g, ln1b)

    slots, gates, nch, sorted_rows = _route(x1p, x1s, w_router_t, b_router_col)
    tables = _block_tables(nch[:, :, 0].astype(I32))
    expert_out = _experts(*tables, sorted_rows, w_mlp1[0], b1p, w_mlp2[0], b2)

    pp = p_prompt[0].reshape(N_PROMPT, PLE_DIM)
    ps = jnp.pad(p_sample[0].reshape(DEC_BATCH, PLE_DIM), ((0, TOK_TILE - DEC_BATCH), (0, 0)))
    yp, ys = _combine(slots, gates, x1p, x1s, pp, ps, expert_out, ln2g, ln2b, w_pg_b, w_ple_b)

    return (
        yp.reshape(BATCH, SEQ, D_MODEL),
        ys.reshape(DEC_BATCH, 1, D_MODEL),
        pool_p.reshape(1, BATCH, POOL_HIST, POOL_WIDTH),
        c_p.reshape(1, BATCH, HEADS, HEAD_DIM, HEAD_DIM),
        n_p.reshape(1, BATCH, HEADS, HEAD_DIM),
        m_p[:, 0:HEADS, 0].reshape(1, BATCH, HEADS),
        pool_s.reshape(1, DEC_BATCH, POOL_HIST, POOL_WIDTH),
        c_s.reshape(1, DEC_BATCH, HEADS, HEAD_DIM, HEAD_DIM),
        n_s.reshape(1, DEC_BATCH, HEADS, HEAD_DIM),
        m_s[:, HEADS:2 * HEADS].reshape(1, DEC_BATCH, HEADS),
    )
```

```python
import jax
import jax.numpy as jnp
from jax import lax
from jax.experimental import pallas as pl
from jax.experimental.pallas import tpu as pltpu

F32 = jnp.float32
BF16 = jnp.bfloat16
I32 = jnp.int32

D_MODEL = 1024
BATCH = 8
SEQ = 2048
DEC_BATCH = 128
PAST_LEN = 16384
POOL_WIDTH = 512
POOL_GROUPS = 4
POOL_GROUP_DIM = 128
POOL_WINDOWS = (2, 4, 8, 16)
POOL_HIST = 15
MLSTM_WIDTH = 512
HEADS = 4
HEAD_DIM = 128
CHUNK = 128
N_EXPERTS = 32
TOP_K = 4
D_FF = 1024
SWIGLU_ALPHA = 1.702
SWIGLU_LIMIT = 7.0
PLE_DIM = 256
DN_ALPHA = 2.0 ** 0.25
LN_EPS = 1e-5

LANES = 128
SUBLANES = 8
BF16_ROWS = 16
MXU_COLS = 256
VMEM_LIMIT = 56 * 1024 * 1024

MIX_TILE = 256
MIX_SEQS = 2
HIST_PAD = 16
TOK_TILE = 512
N_PROMPT = BATCH * SEQ
N_PROMPT_TILES = N_PROMPT // TOK_TILE
N_TILES = N_PROMPT_TILES + 1
SAMPLE_BT = 16

CHUNK_ROWS = BF16_ROWS
LOCAL_ROWS = TOK_TILE * TOP_K + N_EXPERTS * CHUNK_ROWS
GROUPS = LOCAL_ROWS // TOK_TILE
ROW_BLOCK = 256
BLOCK_CHUNKS = ROW_BLOCK // CHUNK_ROWS
MAX_CHUNKS = N_TILES * (TOK_TILE * TOP_K // CHUNK_ROWS + N_EXPERTS)
N_BLOCKS = -(-MAX_CHUNKS // BLOCK_CHUNKS) + N_EXPERTS
ZERO_CHUNK_ROW = LOCAL_ROWS - CHUNK_ROWS
FREE_ROWS = LOCAL_ROWS - TOK_TILE * TOP_K
DUMP_BASE = N_TILES * LOCAL_ROWS
DUMP_ROWS = 2 * ROW_BLOCK
assert DUMP_ROWS <= FREE_ROWS
WEIGHT_PIECES = 8
PIECES_PER_BLOCK = 2


def _dot(a, b):
    return jnp.dot(a, b, preferred_element_type=F32)


def _dot_nt(a, b):
    return lax.dot_general(a, b, (((1,), (1,)), ((), ())), preferred_element_type=F32)


def _dot_tn(a, b):
    return lax.dot_general(a, b, (((0,), (0,)), ((), ())), preferred_element_type=F32)


def _split3(a):
    a0 = a.astype(BF16)
    r1 = a - a0.astype(F32)
    a1 = r1.astype(BF16)
    r2 = r1 - a1.astype(F32)
    return a0, a1, r2.astype(BF16)


def _log_sigmoid(x):
    return jnp.minimum(x, 0.0) - jnp.log1p(jnp.exp(-jnp.abs(x)))


def _layer_norm(x, g, b):
    mu = jnp.mean(x, axis=-1, keepdims=True)
    xc = x - mu
    var = jnp.mean(xc * xc, axis=-1, keepdims=True)
    return xc * lax.rsqrt(var + LN_EPS) * g + b


def _gate_values(g, gbias):
    lane = lax.broadcasted_iota(I32, g.shape, 1)
    z = g + gbias
    return jnp.where(lane < HEADS, z, _log_sigmoid(z))


def _head_out(hh, o_h, gain):
    mu = jnp.mean(hh, axis=-1, keepdims=True)
    hc = hh - mu
    var = jnp.mean(hc * hc, axis=-1, keepdims=True)
    return jax.nn.sigmoid(o_h) * (hc * lax.rsqrt(var + LN_EPS) * gain)


def _prompt_mixer_kernel(x_ref, win_ref, wg_ref, gb_ref, wpool_ref, pscale_ref, mhg_ref, wout_ref,
                         ln1g_ref, ln1b_ref,
                         x1_ref, pool_ref, c_out_ref, n_out_ref, m_out_ref,
                         ubuf, mixbuf, c_s, n_s, m_s):
    ti = pl.program_id(1)
    nt = pl.num_programs(1)
    TT = MIX_TILE
    S = MIX_SEQS

    @pl.when(ti == 0)
    def _():
        for s in range(S):
            ubuf[s, 0:HIST_PAD, :] = jnp.zeros((HIST_PAD, POOL_WIDTH), F32)
        c_s[...] = jnp.zeros_like(c_s)
        n_s[...] = jnp.zeros_like(n_s)
        m_s[...] = jnp.zeros_like(m_s)

    x = jnp.concatenate([x_ref[s, 0] for s in range(S)], axis=0)
    xb = x.astype(BF16)
    proj = _dot(xb, win_ref[...])
    g = _dot(xb, wg_ref[...])

    L = CHUNK
    row = lax.broadcasted_iota(I32, (L, L), 0)
    col = lax.broadcasted_iota(I32, (L, L), 1)
    causal = row >= col
    tril = jnp.where(causal, 1.0, 0.0).astype(BF16)
    pos = ti * TT + lax.broadcasted_iota(I32, (TT, 1), 0)

    for s in range(S):
        base = s * TT
        u = proj[base:base + TT, 0:POOL_WIDTH]

        ubuf[s, HIST_PAD:HIST_PAD + TT, :] = u
        for gi, w in enumerate(POOL_WINDOWS):
            sl = slice(gi * POOL_GROUP_DIM, (gi + 1) * POOL_GROUP_DIM)
            ug = u[:, sl]
            acc = ug
            for i in range(1, w):
                acc = acc + ubuf[s, HIST_PAD - i:HIST_PAD - i + TT, sl]
            cnt = jnp.minimum(pos + 1, w).astype(F32)
            z = acc / cnt - ug
            mixbuf[base:base + TT, sl] = _dot(z.astype(BF16), wpool_ref[gi]) * pscale_ref[:, sl]

        @pl.when(ti == nt - 1)
        def _():
            pool_ref[s, 0] = ubuf[s, TT + 1:TT + HIST_PAD, :]

        ubuf[s, 0:HIST_PAD, :] = ubuf[s, TT:TT + HIST_PAD, :]

    NC = TT // L
    chains = [(s, h) for s in range(S) for h in range(HEADS)]
    units = [(s, c, h) for c in range(NC) for s in range(S) for h in range(HEADS)]
    U = range(len(units))

    def rows(s, c):
        return slice(s * TT + c * L, s * TT + (c + 1) * L)

    def head_cols(part, h):
        return slice(part * POOL_WIDTH + h * HEAD_DIM, part * POOL_WIDTH + (h + 1) * HEAD_DIM)

    gate, cum, gate_t, cum_t = {}, {}, {}, {}
    for c in range(NC):
        for s in range(S):
            val = _gate_values(g[rows(s, c), :], gb_ref[...])
            v0, v1, v2 = _split3(val)
            gate[s, c] = val
            cum[s, c] = _dot(tril, v0) + _dot(tril, v1) + _dot(tril, v2)
    for key in gate:
        gate_t[key] = gate[key].T
        cum_t[key] = cum[key].T
    qf = [proj[rows(s, c), head_cols(1, h)] for s, c, h in units]
    kf = [proj[rows(s, c), head_cols(2, h)] * (HEAD_DIM ** -0.5) for s, c, h in units]
    vf = [proj[rows(s, c), head_cols(3, h)] for s, c, h in units]
    qb = [a.astype(BF16) for a in qf]
    kb = [a.astype(BF16) for a in kf]
    f_col = [cum[s, c][:, HEADS + h:HEADS + h + 1] for s, c, h in units]
    log_d = [jnp.where(causal, f_col[u] - cum_t[s, c][HEADS + h:HEADS + h + 1, :] + gate_t[s, c][h:h + 1, :],
                       -jnp.inf) for u, (s, c, h) in enumerate(units)]
    row_max = [jnp.max(log_d[u], axis=-1, keepdims=True) for u in U]
    qk_raw = [_dot_nt(qb[u], kb[u]) for u in U]

    m_prev, m_t, inter = [None] * len(units), [None] * len(units), [None] * len(units)
    m_run = {(s, h): m_s[s, h:h + 1, 0:1] for s, h in chains}
    for u, (s, c, h) in enumerate(units):
        m_prev[u] = m_run[s, h]
        inter[u] = m_prev[u] + f_col[u]
        m_t[u] = jnp.maximum(inter[u], row_max[u])
        m_run[s, h] = m_t[u][L - 1:L, :]
    m_new = [m_t[u][L - 1:L, :] for u in U]

    dw = [jnp.exp(log_d[u] - m_t[u]) for u in U]
    sc = [jnp.exp(inter[u] - m_t[u]) for u in U]
    qk = [qk_raw[u] * dw[u] for u in U]
    intra = [_dot(qk[u].astype(BF16), vf[u].astype(BF16)) for u in U]
    row_sum = [jnp.sum(qk[u], axis=-1, keepdims=True) for u in U]
    floor = [jnp.exp(-m_t[u]) for u in U]
    f_last = [f_col[u][L - 1:L, :] for u in U]
    wk = [jnp.exp(gate[s, c][:, h:h + 1] + f_last[u] - f_col[u] - m_new[u]) for u, (s, c, h) in enumerate(units)]
    decay = [jnp.exp(m_prev[u] + f_last[u] - m_new[u]) for u in U]
    upd = [_dot_tn((vf[u] * wk[u]).astype(BF16), kb[u]) for u in U]
    n_upd = [jnp.sum(wk[u] * kf[u], axis=0, keepdims=True) for u in U]

    c_run = {(s, h): c_s[s, h] for s, h in chains}
    n_run = {(s, h): n_s[s, h:h + 1, :] for s, h in chains}
    hh = [None] * len(units)
    for c in range(NC):
        cu = [u for u in U if units[u][1] == c]
        inter_term = {u: _dot_nt(qb[u], c_run[units[u][0], units[u][2]].astype(BF16)) for u in cu}
        n_term = {u: jnp.sum(qf[u] * n_run[units[u][0], units[u][2]], axis=-1, keepdims=True) for u in cu}
        for u in cu:
            s, _, h = units[u]
            num = intra[u] + sc[u] * inter_term[u]
            den = row_sum[u] + sc[u] * n_term[u]
            hh[u] = num / jnp.maximum(jnp.abs(den), floor[u])
            c_run[s, h] = decay[u] * c_run[s, h] + upd[u]
            n_run[s, h] = decay[u] * n_run[s, h] + n_upd[u]
    for s, h in chains:
        c_s[s, h] = c_run[s, h]
        n_s[s, h:h + 1, :] = n_run[s, h]
        m_s[s, h:h + 1, :] = jnp.broadcast_to(m_run[s, h], (1, LANES))
    for u, (s, c, h) in enumerate(units):
        mixbuf[rows(s, c), head_cols(1, h)] = _head_out(
            hh[u], proj[rows(s, c), head_cols(4, h)], mhg_ref[:, h * HEAD_DIM:(h + 1) * HEAD_DIM])

    @pl.when(ti == nt - 1)
    def _():
        for s in range(S):
            c_out_ref[s, 0] = c_s[s]
            n_out_ref[s, 0] = n_s[s, 0:HEADS, :]
            m_out_ref[s, 0] = m_s[s]

    mix = _dot(mixbuf[...].astype(BF16), wout_ref[...])
    x1 = _layer_norm(DN_ALPHA * x + mix, ln1g_ref[...], ln1b_ref[...])
    for s in range(S):
        x1_ref[s] = x1[s * TT:(s + 1) * TT, :]


def _prompt_mixer(x, w_in_b, w_g_b, gbias, w_pool_b, pscale, mhg, w_out_b, ln1g, ln1b):
    nt = SEQ // MIX_TILE
    S = MIX_SEQS
    G = BATCH // S
    const2 = lambda b, t: (0, 0)
    const3 = lambda b, t: (0, 0, 0)
    outs = pl.pallas_call(
        _prompt_mixer_kernel,
        grid=(G, nt),
        in_specs=[
            pl.BlockSpec((S, 1, MIX_TILE, D_MODEL), lambda b, t: (0, b, t, 0)),
            pl.BlockSpec(w_in_b.shape, const2),
            pl.BlockSpec(w_g_b.shape, const2),
            pl.BlockSpec(gbias.shape, const2),
            pl.BlockSpec(w_pool_b.shape, const3),
            pl.BlockSpec(pscale.shape, const2),
            pl.BlockSpec(mhg.shape, const2),
            pl.BlockSpec(w_out_b.shape, const2),
            pl.BlockSpec(ln1g.shape, const2),
            pl.BlockSpec(ln1b.shape, const2),
        ],
        out_specs=[
            pl.BlockSpec((S, MIX_TILE, D_MODEL), lambda b, t: (0, b * nt + t, 0)),
            pl.BlockSpec((S, 1, POOL_HIST, POOL_WIDTH), lambda b, t: (0, b, 0, 0)),
            pl.BlockSpec((S, 1, HEADS, HEAD_DIM, HEAD_DIM), lambda b, t: (0, b, 0, 0, 0)),
            pl.BlockSpec((S, 1, HEADS, HEAD_DIM), lambda b, t: (0, b, 0, 0)),
            pl.BlockSpec((S, 1, SUBLANES, LANES), lambda b, t: (0, b, 0, 0)),
        ],
        out_shape=[
            jax.ShapeDtypeStruct((S, G * SEQ, D_MODEL), F32),
            jax.ShapeDtypeStruct((S, G, POOL_HIST, POOL_WIDTH), F32),
            jax.ShapeDtypeStruct((S, G, HEADS, HEAD_DIM, HEAD_DIM), F32),
            jax.ShapeDtypeStruct((S, G, HEADS, HEAD_DIM), F32),
            jax.ShapeDtypeStruct((S, G, SUBLANES, LANES), F32),
        ],
        scratch_shapes=[
            pltpu.VMEM((S, HIST_PAD + MIX_TILE, POOL_WIDTH), F32),
            pltpu.VMEM((S * MIX_TILE, D_MODEL), F32),
            pltpu.VMEM((S, HEADS, HEAD_DIM, HEAD_DIM), F32),
            pltpu.VMEM((S, SUBLANES, HEAD_DIM), F32),
            pltpu.VMEM((S, SUBLANES, LANES), F32),
        ],
        compiler_params=pltpu.CompilerParams(
            dimension_semantics=("arbitrary", "arbitrary"), vmem_limit_bytes=VMEM_LIMIT),
        name="prompt_mixer",
    )(x.reshape(S, G, SEQ, D_MODEL), w_in_b, w_g_b, gbias, w_pool_b, pscale, mhg, w_out_b, ln1g, ln1b)
    x1, pool, c, n, m = outs
    return (x1.reshape(N_PROMPT, D_MODEL), pool.reshape(BATCH, POOL_HIST, POOL_WIDTH),
            c.reshape(BATCH, HEADS, HEAD_DIM, HEAD_DIM), n.reshape(BATCH, HEADS, HEAD_DIM),
            m.reshape(BATCH, SUBLANES, LANES))


def _sample_mixer_kernel(x_ref, hist_ref, c_ref, n_ref, m_ref, win_ref, wg_ref, gb_ref, wpool_ref,
                         pscale_ref, mhg_ref, wout_ref, ln1g_ref, ln1b_ref,
                         x1_ref, pool_out_ref, c_out_ref, n_out_ref, m_out_ref,
                         q_s, k_s, vw_s, v_s, o_s, mixbuf, h_s, coef_s):
    i = pl.program_id(0)
    nsteps = pl.num_programs(0)
    B = DEC_BATCH

    @pl.when(i == 0)
    def _():
        x = x_ref[...]
        xb = x.astype(BF16)
        proj = _dot(xb, win_ref[...])
        g = _dot(xb, wg_ref[...])
        u = proj[:, 0:POOL_WIDTH]
        for gi, w in enumerate(POOL_WINDOWS):
            sl = slice(gi * POOL_GROUP_DIM, (gi + 1) * POOL_GROUP_DIM)
            ug = u[:, sl]
            s = ug
            for j in range(1, w):
                r = POOL_HIST - j
                s = s + hist_ref[:, r * POOL_WIDTH + gi * POOL_GROUP_DIM:r * POOL_WIDTH + (gi + 1) * POOL_GROUP_DIM]
            cnt = float(min(PAST_LEN + 1, w))
            z = s / cnt - ug
            mixbuf[:, sl] = _dot(z.astype(BF16), wpool_ref[gi]) * pscale_ref[:, sl]
        pool_out_ref[:, 0:(POOL_HIST - 1) * POOL_WIDTH] = hist_ref[:, POOL_WIDTH:POOL_HIST * POOL_WIDTH]
        pool_out_ref[:, (POOL_HIST - 1) * POOL_WIDTH:POOL_HIST * POOL_WIDTH] = u

        val = _gate_values(g, gb_ref[...])
        lane = lax.broadcasted_iota(I32, (B, LANES), 1)
        qk_all = jnp.zeros((B, LANES), F32)
        sc_all = jnp.zeros((B, LANES), F32)
        den_all = jnp.zeros((B, LANES), F32)
        floor_all = jnp.zeros((B, LANES), F32)
        m_all = jnp.zeros((B, LANES), F32)
        for h in range(HEADS):
            hs = slice(h * HEAD_DIM, (h + 1) * HEAD_DIM)
            qf = proj[:, POOL_WIDTH + h * HEAD_DIM:POOL_WIDTH + (h + 1) * HEAD_DIM]
            kf = proj[:, 2 * POOL_WIDTH + h * HEAD_DIM:2 * POOL_WIDTH + (h + 1) * HEAD_DIM] * (HEAD_DIM ** -0.5)
            vf = proj[:, 3 * POOL_WIDTH + h * HEAD_DIM:3 * POOL_WIDTH + (h + 1) * HEAD_DIM]
            ig = val[:, h:h + 1]
            lf = val[:, HEADS + h:HEADS + h + 1]
            m0 = m_ref[:, h:h + 1]
            n0 = n_ref[:, hs]
            inter = m0 + lf
            m_t = jnp.maximum(inter, ig)
            dw = jnp.exp(ig - m_t)
            sc = jnp.exp(inter - m_t)
            qk = jnp.sum(qf * kf, axis=-1, keepdims=True) * dw
            den = qk + sc * jnp.sum(qf * n0, axis=-1, keepdims=True)
            n_out_ref[:, hs] = sc * n0 + dw * kf
            q_s[0:B, hs] = qf
            k_s[0:B, hs] = kf
            v_s[0:B, hs] = vf
            vw_s[0:B, hs] = vf * dw
            sel = lane == h
            qk_all = jnp.where(sel, qk, qk_all)
            sc_all = jnp.where(sel, sc, sc_all)
            den_all = jnp.where(sel, den, den_all)
            floor_all = jnp.where(sel, jnp.exp(-m_t), floor_all)
            m_all = jnp.where(lane == HEADS + h, m_t, m_all)
        o_s[...] = proj[:, 4 * POOL_WIDTH:5 * POOL_WIDTH]
        coef_s[0] = qk_all
        coef_s[1] = sc_all
        coef_s[2] = den_all
        coef_s[3] = floor_all
        m_out_ref[...] = m_all

    rows = pl.ds(pl.multiple_of(i * SAMPLE_BT, SAMPLE_BT), SAMPLE_BT)
    q_t, k_t, v_t, vw_t = q_s[rows, :], k_s[rows, :], v_s[rows, :], vw_s[rows, :]
    qk_t, sc_t, den_t, floor_t = coef_s[0, rows, :], coef_s[1, rows, :], coef_s[2, rows, :], coef_s[3, rows, :]
    h_rows = []
    for bl in range(SAMPLE_BT):
        heads = []
        for h in range(HEADS):
            hs = slice(h * HEAD_DIM, (h + 1) * HEAD_DIM)
            c_prev = c_ref[bl, h]
            q8 = jnp.broadcast_to(q_t[bl:bl + 1, hs], (SUBLANES, HEAD_DIM))
            cq = _dot_nt(q8.astype(BF16), c_prev.astype(BF16))[0:1, :]
            qk = qk_t[bl:bl + 1, h:h + 1]
            sc = sc_t[bl:bl + 1, h:h + 1]
            num = qk * v_t[bl:bl + 1, hs] + sc * cq
            heads.append(num / jnp.maximum(jnp.abs(den_t[bl:bl + 1, h:h + 1]), floor_t[bl:bl + 1, h:h + 1]))
            v_col = jnp.broadcast_to(vw_t[bl:bl + 1, hs], (HEAD_DIM, HEAD_DIM)).T
            c_out_ref[bl, h] = sc * c_prev + v_col * k_t[bl:bl + 1, hs]
        h_rows.append(jnp.concatenate(heads, axis=1))
    h_s[rows, :] = jnp.concatenate(h_rows, axis=0)

    @pl.when(i == nsteps - 1)
    def _():
        for h in range(HEADS):
            hs = slice(h * HEAD_DIM, (h + 1) * HEAD_DIM)
            mixbuf[:, POOL_WIDTH + h * HEAD_DIM:POOL_WIDTH + (h + 1) * HEAD_DIM] = _head_out(
                h_s[:, hs], o_s[:, hs], mhg_ref[:, hs])
        mix = _dot(mixbuf[...].astype(BF16), wout_ref[...])
        x1 = _layer_norm(DN_ALPHA * x_ref[...] + mix, ln1g_ref[...], ln1b_ref[...])
        x1_ref[0:B, :] = x1
        x1_ref[B:TOK_TILE, :] = jnp.zeros((TOK_TILE - B, D_MODEL), F32)


def _sample_mixer(x, hist2, c0, n0, m0, w_in_b, w_g_b, gbias, w_pool_b, pscale, mhg, w_out_b, ln1g, ln1b):
    B = DEC_BATCH
    steps = B // SAMPLE_BT
    full = lambda a: pl.BlockSpec(a.shape, lambda i: (0,) * a.ndim)
    c_spec = pl.BlockSpec((SAMPLE_BT, HEADS, HEAD_DIM, HEAD_DIM), lambda i: (i, 0, 0, 0))
    return pl.pallas_call(
        _sample_mixer_kernel,
        grid=(steps,),
        in_specs=[full(x), full(hist2), c_spec, full(n0), full(m0), full(w_in_b), full(w_g_b), full(gbias),
                  full(w_pool_b), full(pscale), full(mhg), full(w_out_b), full(ln1g), full(ln1b)],
        out_specs=[
            pl.BlockSpec((TOK_TILE, D_MODEL), lambda i: (0, 0)),
            pl.BlockSpec((B, POOL_HIST * POOL_WIDTH), lambda i: (0, 0)),
            c_spec,
            pl.BlockSpec((B, MLSTM_WIDTH), lambda i: (0, 0)),
            pl.BlockSpec((B, LANES), lambda i: (0, 0)),
        ],
        out_shape=[
            jax.ShapeDtypeStruct((TOK_TILE, D_MODEL), F32),
            jax.ShapeDtypeStruct((B, POOL_HIST * POOL_WIDTH), F32),
            jax.ShapeDtypeStruct((B, HEADS, HEAD_DIM, HEAD_DIM), F32),
            jax.ShapeDtypeStruct((B, MLSTM_WIDTH), F32),
            jax.ShapeDtypeStruct((B, LANES), F32),
        ],
        scratch_shapes=[
            pltpu.VMEM((B, MLSTM_WIDTH), F32),
            pltpu.VMEM((B, MLSTM_WIDTH), F32),
            pltpu.VMEM((B, MLSTM_WIDTH), F32),
            pltpu.VMEM((B, MLSTM_WIDTH), F32),
            pltpu.VMEM((B, MLSTM_WIDTH), F32),
            pltpu.VMEM((B, D_MODEL), F32),
            pltpu.VMEM((B, MLSTM_WIDTH), F32),
            pltpu.VMEM((4, B, LANES), F32),
        ],
        compiler_params=pltpu.CompilerParams(
            dimension_semantics=("arbitrary",), vmem_limit_bytes=VMEM_LIMIT),
        name="sample_mixer",
    )(x, hist2, c0, n0, m0, w_in_b, w_g_b, gbias, w_pool_b, pscale, mhg, w_out_b, ln1g, ln1b)


def _pick_tile(i, prompt_ref, sample_ref):
    return jnp.where(i < N_PROMPT_TILES, prompt_ref[...], sample_ref[...])


def _placement(slot_rows, group):
    r = group * TOK_TILE + lax.broadcasted_iota(I32, (TOK_TILE, TOK_TILE), 0)
    return [r == s for s in slot_rows]


def _route_kernel(xp_ref, xs_ref, wrt_ref, br_ref, slot_ref, gate_ref, nch_ref, sorted_ref):
    i = pl.program_id(0)
    T = TOK_TILE
    E = N_EXPERTS

    x = _pick_tile(i, xp_ref, xs_ref)
    xh = x.astype(BF16)
    xl = (x - xh.astype(F32)).astype(BF16)
    w = wrt_ref[...]
    wh = w.astype(BF16)
    wl = (w - wh.astype(F32)).astype(BF16)
    logits = _dot_nt(wh, xh) + (_dot_nt(wh, xl) + _dot_nt(wl, xh)) + br_ref[:, 0:1]

    erow = lax.broadcasted_iota(I32, (E, T), 0).astype(F32)
    work = logits
    vals, sels = [], []
    for _ in range(TOP_K):
        mx = jnp.max(work, axis=0, keepdims=True)
        idx = jnp.min(jnp.where(work == mx, erow, float(E)), axis=0, keepdims=True)
        sel = erow == idx
        work = jnp.where(sel, -jnp.inf, work)
        vals.append(mx)
        sels.append(sel)
    chosen = jnp.logical_or(jnp.logical_or(sels[0], sels[1]), jnp.logical_or(sels[2], sels[3]))
    es = [jnp.exp(v - vals[0]) for v in vals]
    tot = es[0] + es[1] + es[2] + es[3]

    onehot = jnp.where(chosen, 1.0, 0.0)
    trow = lax.broadcasted_iota(I32, (T, T), 0)
    tcol = lax.broadcasted_iota(I32, (T, T), 1)
    before = jnp.where(trow < tcol, 1.0, 0.0).astype(BF16)
    rank = _dot(onehot.astype(BF16), before)
    cnt = jnp.sum(onehot, axis=1, keepdims=True)
    nch = jnp.floor((cnt + (CHUNK_ROWS - 1)) * (1.0 / CHUNK_ROWS))
    lower = jnp.where(lax.broadcasted_iota(I32, (E, E), 0) > lax.broadcasted_iota(I32, (E, E), 1), 1.0, 0.0)
    nch_b = jnp.broadcast_to(nch, (E, LANES))
    seg_start = _dot(lower.astype(BF16), nch_b.astype(BF16))[:, 0:1] * CHUNK_ROWS
    base = seg_start + rank

    r8 = lax.broadcasted_iota(I32, (SUBLANES, T), 0)
    s_out = jnp.zeros((SUBLANES, T), I32)
    g_out = jnp.zeros((SUBLANES, T), F32)
    slot_rows = []
    for j in range(TOP_K):
        slot_j = jnp.sum(jnp.where(sels[j], base, 0.0), axis=0, keepdims=True).astype(I32)
        slot_rows.append(slot_j)
        s_out = jnp.where(r8 == j, slot_j, s_out)
        g_out = jnp.where(r8 == j, es[j] / tot, g_out)
    slot_ref[0] = s_out
    gate_ref[0] = g_out
    nch_ref[0] = nch_b

    for grp in range(GROUPS):
        m = _placement(slot_rows, grp)
        hit = jnp.logical_or(jnp.logical_or(m[0], m[1]), jnp.logical_or(m[2], m[3]))
        place = jnp.where(hit, 1.0, 0.0).astype(BF16)
        sorted_ref[grp * T:(grp + 1) * T, :] = _dot(place, xh).astype(BF16)


def _route(x1p, x1s, w_router_t, b_router_col):
    tile_spec = pl.BlockSpec((1, SUBLANES, TOK_TILE), lambda i: (i, 0, 0))
    return pl.pallas_call(
        _route_kernel,
        grid=(N_TILES,),
        in_specs=[
            pl.BlockSpec((TOK_TILE, D_MODEL), lambda i: (jnp.minimum(i, N_PROMPT_TILES - 1), 0)),
            pl.BlockSpec((TOK_TILE, D_MODEL), lambda i: (0, 0)),
            pl.BlockSpec(w_router_t.shape, lambda i: (0, 0)),
            pl.BlockSpec(b_router_col.shape, lambda i: (0, 0)),
        ],
        out_specs=[tile_spec, tile_spec,
                   pl.BlockSpec((1, N_EXPERTS, LANES), lambda i: (i, 0, 0)),
                   pl.BlockSpec((LOCAL_ROWS, D_MODEL), lambda i: (i, 0))],
        out_shape=[
            jax.ShapeDtypeStruct((N_TILES, SUBLANES, TOK_TILE), I32),
            jax.ShapeDtypeStruct((N_TILES, SUBLANES, TOK_TILE), F32),
            jax.ShapeDtypeStruct((N_TILES, N_EXPERTS, LANES), F32),
            jax.ShapeDtypeStruct((N_TILES * LOCAL_ROWS, D_MODEL), BF16),
        ],
        compiler_params=pltpu.CompilerParams(
            dimension_semantics=("arbitrary",), vmem_limit_bytes=VMEM_LIMIT),
        name="route",
    )(x1p, x1s, w_router_t, b_router_col)


def _expert_kernel(src_ref, dst_ref, bexp_ref, first_ref, next_ref, nused_ref,
                   sorted_hbm, w1_hbm, b1_ref, w2_hbm, b2_ref, out_hbm,
                   w1_stage, w2_stage, w1_b, w2_b, xbuf, obuf, zbuf, wsem, gsem, ssem, zsem):
    nused = nused_ref[0]

    def fetch_piece(e, p):
        r1 = pl.ds(pl.multiple_of(p * (D_MODEL // WEIGHT_PIECES), SUBLANES), D_MODEL // WEIGHT_PIECES)
        r2 = pl.ds(pl.multiple_of(p * (D_FF // WEIGHT_PIECES), SUBLANES), D_FF // WEIGHT_PIECES)
        return (pltpu.make_async_copy(w1_hbm.at[e, r1, :], w1_stage.at[r1, :], wsem.at[0]),
                pltpu.make_async_copy(w2_hbm.at[e, r2, :], w2_stage.at[r2, :], wsem.at[1]))

    def start_pieces(e, lo, hi):
        def body(p, c):
            for cp in fetch_piece(e, p):
                cp.start()
            return c
        lax.fori_loop(lo, hi, body, 0)

    def gather(b, q):
        slot = lax.rem(b, 2)
        row = pl.multiple_of(src_ref[b * BLOCK_CHUNKS + q], CHUNK_ROWS)
        return pltpu.make_async_copy(sorted_hbm.at[pl.ds(row, CHUNK_ROWS), :],
                                     xbuf.at[slot, pl.ds(q * CHUNK_ROWS, CHUNK_ROWS), :], gsem.at[slot])

    def scatter(b, q):
        slot = lax.rem(b, 2)
        row = pl.multiple_of(dst_ref[b * BLOCK_CHUNKS + q], CHUNK_ROWS)
        return pltpu.make_async_copy(obuf.at[slot, pl.ds(q * CHUNK_ROWS, CHUNK_ROWS), :],
                                     out_hbm.at[pl.ds(row, CHUNK_ROWS), :], ssem.at[slot])

    def zero_rows(start, n_rows):
        start = pl.multiple_of(start, CHUNK_ROWS)
        return pltpu.make_async_copy(zbuf.at[pl.ds(0, n_rows), :], out_hbm.at[pl.ds(start, n_rows), :], zsem)

    def zero_tail(k):
        return zero_rows(k * LOCAL_ROWS + TOK_TILE * TOP_K, FREE_ROWS)

    zbuf[...] = jnp.zeros_like(zbuf)
    lax.fori_loop(0, N_TILES, lambda k, c: (zero_tail(k).start(), c)[1], 0)
    zero_rows(DUMP_BASE, DUMP_ROWS).start()
    for q in range(BLOCK_CHUNKS):
        gather(0, q).start()
    lax.fori_loop(0, N_TILES, lambda k, c: (zero_tail(k).wait(), c)[1], 0)
    zero_rows(DUMP_BASE, DUMP_ROWS).wait()

    half = MXU_COLS // 2
    k_io = lax.broadcasted_iota(I32, (MXU_COLS, MXU_COLS), 0)
    j_io = lax.broadcasted_iota(I32, (MXU_COLS, MXU_COLS), 1)
    src_col = jnp.where(j_io < half, 2 * j_io, 2 * (j_io - half) + 1)
    perm = jnp.where(k_io == src_col, 1.0, 0.0).astype(BF16)

    def block(i, fetched):
        e = bexp_ref[i]
        slot = lax.rem(i, 2)
        is_first = first_ref[i] == 1

        @pl.when(is_first)
        def _():
            start_pieces(e, fetched, WEIGHT_PIECES)

            def wait_piece(p, c):
                for cp in fetch_piece(e, p):
                    cp.wait()
                return c
            lax.fori_loop(0, WEIGHT_PIECES, wait_piece, 0)
            for c in range(2 * D_FF // MXU_COLS):
                blk = w1_stage[:, c * MXU_COLS:(c + 1) * MXU_COLS].astype(BF16)
                sep = _dot(blk, perm).astype(BF16)
                w1_b[:, c * half:(c + 1) * half] = sep[:, 0:half]
                w1_b[:, D_FF + c * half:D_FF + (c + 1) * half] = sep[:, half:MXU_COLS]
            w2_b[...] = w2_stage[...].astype(BF16)

        fetched = jnp.where(is_first, 0, fetched)

        for q in range(BLOCK_CHUNKS):
            gather(i, q).wait()

        @pl.when(i + 1 < nused)
        def _():
            for q in range(BLOCK_CHUNKS):
                gather(i + 1, q).start()

        @pl.when(i >= 2)
        def _():
            for q in range(BLOCK_CHUNKS):
                scatter(i - 2, q).wait()

        h = _dot(xbuf[slot], w1_b[...]) + b1_ref[e]
        glu = jnp.minimum(h[:, 0:D_FF], SWIGLU_LIMIT)
        lin = jnp.clip(h[:, D_FF:2 * D_FF], -SWIGLU_LIMIT, SWIGLU_LIMIT)
        a = glu * jax.nn.sigmoid(SWIGLU_ALPHA * glu) * (lin + 1.0)
        obuf[slot] = _dot(a.astype(BF16), w2_b[...]) + b2_ref[e]
        for q in range(BLOCK_CHUNKS):
            scatter(i, q).start()

        more = jnp.where(next_ref[i] >= 0, jnp.minimum(fetched + PIECES_PER_BLOCK, WEIGHT_PIECES), fetched)
        start_pieces(next_ref[i], fetched, more)
        return more

    lax.fori_loop(0, nused, block, jnp.int32(0))

    @pl.when(nused >= 2)
    def _():
        for q in range(BLOCK_CHUNKS):
            scatter(nused - 2, q).wait()
    for q in range(BLOCK_CHUNKS):
        scatter(nused - 1, q).wait()


def _experts(chunk_src, chunk_dst, block_expert, block_first, block_next, n_used, sorted_rows, w1, b1p, w2, b2):
    whole3 = lambda i, *_: (0, 0, 0)
    grid_spec = pltpu.PrefetchScalarGridSpec(
        num_scalar_prefetch=6,
        grid=(1,),
        in_specs=[
            pl.BlockSpec(memory_space=pl.ANY),
            pl.BlockSpec(memory_space=pl.ANY),
            pl.BlockSpec(b1p.shape, whole3),
            pl.BlockSpec(memory_space=pl.ANY),
            pl.BlockSpec(b2.shape, whole3),
        ],
        out_specs=pl.BlockSpec(memory_space=pl.ANY),
        scratch_shapes=[
            pltpu.VMEM((D_MODEL, 2 * D_FF), F32),
            pltpu.VMEM((D_FF, D_MODEL), F32),
            pltpu.VMEM((D_MODEL, 2 * D_FF), BF16),
            pltpu.VMEM((D_FF, D_MODEL), BF16),
            pltpu.VMEM((2, ROW_BLOCK, D_MODEL), BF16),
            pltpu.VMEM((2, ROW_BLOCK, D_MODEL), F32),
            pltpu.VMEM((FREE_ROWS, D_MODEL), F32),
            pltpu.SemaphoreType.DMA((2,)),
            pltpu.SemaphoreType.DMA((2,)),
            pltpu.SemaphoreType.DMA((2,)),
            pltpu.SemaphoreType.DMA(()),
        ],
    )
    return pl.pallas_call(
        _expert_kernel,
        grid_spec=grid_spec,
        out_shape=jax.ShapeDtypeStruct((DUMP_BASE + DUMP_ROWS, D_MODEL), F32),
        compiler_params=pltpu.CompilerParams(
            dimension_semantics=("arbitrary",), vmem_limit_bytes=VMEM_LIMIT),
        name="experts",
    )(chunk_src, chunk_dst, block_expert, block_first, block_next, n_used, sorted_rows, w1, b1p, w2, b2)


def _combine_kernel(slot_ref, gate_ref, xp_ref, xs_ref, pp_ref, ps_ref, eo_ref, ln2g_ref, ln2b_ref,
                    wpg_ref, wple_ref, yp_ref, ys_ref):
    i = pl.program_id(0)
    T = TOK_TILE

    slot_rows = [slot_ref[0, j:j + 1, :] for j in range(TOP_K)]
    gate_rows = [gate_ref[0, j:j + 1, :] for j in range(TOP_K)]
    pad = jnp.zeros((LANES - SUBLANES, T), F32)
    slots_t = jnp.concatenate([slot_ref[0].astype(F32), pad], axis=0).T
    slot_cols = [slots_t[:, j:j + 1].astype(I32) for j in range(TOP_K)]

    ff = jnp.zeros((T, D_MODEL), F32)
    for grp in range(GROUPS):
        m = _placement(slot_rows, grp)
        weighted = jnp.where(m[0], gate_rows[0], jnp.where(m[1], gate_rows[1], jnp.where(
            m[2], gate_rows[2], jnp.where(m[3], gate_rows[3], 0.0))))
        g_col = jnp.sum(weighted, axis=1, keepdims=True)
        z = (eo_ref[grp * T:(grp + 1) * T, :] * g_col).astype(BF16)
        r = grp * T + lax.broadcasted_iota(I32, (T, T), 1)
        hit = jnp.logical_or(jnp.logical_or(r == slot_cols[0], r == slot_cols[1]),
                             jnp.logical_or(r == slot_cols[2], r == slot_cols[3]))
        ff = ff + _dot(jnp.where(hit, 1.0, 0.0).astype(BF16), z)

    x1 = _pick_tile(i, xp_ref, xs_ref)
    x2 = _layer_norm(DN_ALPHA * x1 + ff, ln2g_ref[...], ln2b_ref[...])
    p = _pick_tile(i, pp_ref, ps_ref)
    y = x2 + jax.nn.sigmoid(_dot(x2.astype(BF16), wpg_ref[...])) * _dot(p.astype(BF16), wple_ref[...])

    @pl.when(i < N_PROMPT_TILES)
    def _():
        yp_ref[...] = y

    @pl.when(i == N_PROMPT_TILES)
    def _():
        ys_ref[...] = y[0:DEC_BATCH, :]


def _combine(slots, gates, x1p, x1s, pp, ps, expert_out, ln2g, ln2b, w_pg_b, w_ple_b):
    tile_idx = lambda i: (jnp.minimum(i, N_PROMPT_TILES - 1), 0)
    const2 = lambda i: (0, 0)
    return pl.pallas_call(
        _combine_kernel,
        grid=(N_TILES,),
        in_specs=[
            pl.BlockSpec((1, SUBLANES, TOK_TILE), lambda i: (i, 0, 0)),
            pl.BlockSpec((1, SUBLANES, TOK_TILE), lambda i: (i, 0, 0)),
            pl.BlockSpec((TOK_TILE, D_MODEL), tile_idx),
            pl.BlockSpec((TOK_TILE, D_MODEL), const2),
            pl.BlockSpec((TOK_TILE, PLE_DIM), tile_idx),
            pl.BlockSpec((TOK_TILE, PLE_DIM), const2),
            pl.BlockSpec((LOCAL_ROWS, D_MODEL), lambda i: (i, 0)),
            pl.BlockSpec(ln2g.shape, const2),
            pl.BlockSpec(ln2b.shape, const2),
            pl.BlockSpec(w_pg_b.shape, const2),
            pl.BlockSpec(w_ple_b.shape, const2),
        ],
        out_specs=[
            pl.BlockSpec((TOK_TILE, D_MODEL), tile_idx),
            pl.BlockSpec((DEC_BATCH, D_MODEL), const2),
        ],
        out_shape=[
            jax.ShapeDtypeStruct((N_PROMPT, D_MODEL), F32),
            jax.ShapeDtypeStruct((DEC_BATCH, D_MODEL), F32),
        ],
        compiler_params=pltpu.CompilerParams(
            dimension_semantics=("arbitrary",), vmem_limit_bytes=VMEM_LIMIT),
        name="combine",
    )(slots, gates, x1p, x1s, pp, ps, expert_out, ln2g, ln2b, w_pg_b, w_ple_b)


def _block_tables(nch):
    seg_start = (jnp.cumsum(nch, axis=1) - nch) * CHUNK_ROWS
    tot = jnp.sum(nch, axis=0)
    nblk = (tot + BLOCK_CHUNKS - 1) // BLOCK_CHUNKS
    blk_end = jnp.cumsum(nblk)
    blk_start = blk_end - nblk
    n_used = blk_end[-1:].astype(I32)
    blk_ids = jnp.arange(N_BLOCKS, dtype=I32)

    def expert_of(b):
        b = jnp.minimum(b, n_used[0] - 1)
        return jnp.minimum(jnp.sum(blk_end[None, :] <= b[:, None], axis=1), N_EXPERTS - 1).astype(I32)

    block_expert = expert_of(blk_ids)
    in_use = blk_ids < n_used[0]
    is_first = jnp.logical_or(blk_ids == 0, block_expert != expert_of(blk_ids - 1))
    block_first = jnp.logical_and(in_use, is_first).astype(I32)
    next_start = blk_end[block_expert]
    block_next = jnp.where(next_start < n_used[0], expert_of(next_start), -1).astype(I32)

    nch_t = nch.T
    seg_first = (blk_start[:, None] * BLOCK_CHUNKS + jnp.cumsum(nch_t, axis=1) - nch_t).reshape(-1)
    seg_count = nch_t.reshape(-1)
    seg_row = (jnp.arange(N_TILES, dtype=I32)[None, :] * LOCAL_ROWS + seg_start.T).reshape(-1)
    ent = jnp.arange(N_BLOCKS * BLOCK_CHUNKS, dtype=I32)
    d = ent[:, None] - seg_first[None, :]
    inside = jnp.logical_and(d >= 0, d < seg_count[None, :])
    row = jnp.sum(jnp.where(inside, seg_row[None, :] + d * CHUNK_ROWS, 0), axis=1)
    real = jnp.any(inside, axis=1)
    dump = DUMP_BASE + (((ent // BLOCK_CHUNKS) % 2) * BLOCK_CHUNKS + ent % BLOCK_CHUNKS) * CHUNK_ROWS
    chunk_src = jnp.where(real, row, ZERO_CHUNK_ROW).astype(I32)
    chunk_dst = jnp.where(real, row, dump).astype(I32)
    return chunk_src, chunk_dst, block_expert, block_first, block_next, n_used


def kernel(x_prompt, x_sample, state_pool, state_mlstm_C, state_mlstm_n, state_mlstm_m, p_prompt, p_sample, w_in, b_i, b_f, w_pool, pool_scale, mh_g, w_out, ln1_g, ln1_b, w_router, b_router, w_mlp1, b_mlp1, w_mlp2, b_mlp2, ln2_g, ln2_b, w_ple, w_ple_gate):
    n_main = POOL_WIDTH + 4 * MLSTM_WIDTH
    w_in_b = w_in[0, :, 0:n_main].astype(BF16)
    w_g_b = jnp.pad(w_in[0, :, n_main:], ((0, 0), (0, LANES - 2 * HEADS))).astype(BF16)
    gbias = jnp.pad(jnp.concatenate([b_i[0], b_f[0]]), (0, LANES - 2 * HEADS)).reshape(1, LANES)
    w_pool_b = w_pool[0].astype(BF16)
    pscale = pool_scale[0].reshape(1, POOL_WIDTH)
    mhg = mh_g[0].reshape(1, MLSTM_WIDTH)
    w_out_b = w_out[0].astype(BF16)
    ln1g = ln1_g[0].reshape(1, D_MODEL)
    ln1b = ln1_b[0].reshape(1, D_MODEL)
    ln2g = ln2_g[0].reshape(1, D_MODEL)
    ln2b = ln2_b[0].reshape(1, D_MODEL)
    w_router_t = w_router[0].T
    b_router_col = jnp.broadcast_to(b_router[0].reshape(N_EXPERTS, 1), (N_EXPERTS, LANES))
    b1 = b_mlp1[0]
    b1p = jnp.concatenate([b1[:, 0::2], b1[:, 1::2]], axis=-1).reshape(N_EXPERTS, 1, 2 * D_FF)
    b2 = b_mlp2[0].reshape(N_EXPERTS, 1, D_MODEL)
    w_pg_b = w_ple_gate[0].astype(BF16)
    w_ple_b = w_ple[0].astype(BF16)

    x1p, pool_p, c_p, n_p, m_p = _prompt_mixer(
        x_prompt, w_in_b, w_g_b, gbias, w_pool_b, pscale, mhg, w_out_b, ln1g, ln1b)
    x1s, pool_s, c_s, n_s, m_s = _sample_mixer(
        x_sample.reshape(DEC_BATCH, D_MODEL),
        state_pool[0].reshape(DEC_BATCH, POOL_HIST * POOL_WIDTH),
        state_mlstm_C[0], state_mlstm_n[0].reshape(DEC_BATCH, MLSTM_WIDTH), state_mlstm_m[0],
        w_in_b, w_g_b, gbias, w_pool_b, pscale, mhg, w_out_b, ln1g, ln1b)

    slots, gates, nch, sorted_rows = _route(x1p, x1s, w_router_t, b_router_col)
    tables = _block_tables(nch[:, :, 0].astype(I32))
    expert_out = _experts(*tables, sorted_rows, w_mlp1[0], b1p, w_mlp2[0], b2)

    pp = p_prompt[0].reshape(N_PROMPT, PLE_DIM)
    ps = jnp.pad(p_sample[0].reshape(DEC_BATCH, PLE_DIM), ((0, TOK_TILE - DEC_BATCH), (0, 0)))
    yp, ys = _combine(slots, gates, x1p, x1s, pp, ps, expert_out, ln2g, ln2b, w_pg_b, w_ple_b)

    return (
        yp.reshape(BATCH, SEQ, D_MODEL),
        ys.reshape(DEC_BATCH, 1, D_MODEL),
        pool_p.reshape(1, BATCH, POOL_HIST, POOL_WIDTH),
        c_p.reshape(1, BATCH, HEADS, HEAD_DIM, HEAD_DIM),
        n_p.reshape(1, BATCH, HEADS, HEAD_DIM),
        m_p[:, 0:HEADS, 0].reshape(1, BATCH, HEADS),
        pool_s.reshape(1, DEC_BATCH, POOL_HIST, POOL_WIDTH),
        c_s.reshape(1, DEC_BATCH, HEADS, HEAD_DIM, HEAD_DIM),
        n_s.reshape(1, DEC_BATCH, HEADS, HEAD_DIM),
        m_s[:, HEADS:2 * HEADS].reshape(1, DEC_BATCH, HEADS),
    )
```

```python
import jax
import jax.numpy as jnp
from jax import lax
from jax.experimental import pallas as pl
from jax.experimental.pallas import tpu as pltpu

F32 = jnp.float32
BF16 = jnp.bfloat16
I32 = jnp.int32

D_MODEL = 1024
BATCH = 8
SEQ = 2048
DEC_BATCH = 128
PAST_LEN = 16384
POOL_WIDTH = 512
POOL_GROUPS = 4
POOL_GROUP_DIM = 128
POOL_WINDOWS = (2, 4, 8, 16)
POOL_HIST = 15
MLSTM_WIDTH = 512
HEADS = 4
HEAD_DIM = 128
CHUNK = 128
N_EXPERTS = 32
TOP_K = 4
D_FF = 1024
SWIGLU_ALPHA = 1.702
SWIGLU_LIMIT = 7.0
PLE_DIM = 256
DN_ALPHA = 2.0 ** 0.25
LN_EPS = 1e-5

LANES = 128
SUBLANES = 8
BF16_ROWS = 16
MXU_COLS = 256
VMEM_LIMIT = 56 * 1024 * 1024

MIX_TILE = 256
MIX_SEQS = 2
HIST_PAD = 16
TOK_TILE = 512
N_PROMPT = BATCH * SEQ
N_PROMPT_TILES = N_PROMPT // TOK_TILE
N_TILES = N_PROMPT_TILES + 1
SAMPLE_BT = 16

CHUNK_ROWS = BF16_ROWS
LOCAL_ROWS = TOK_TILE * TOP_K + N_EXPERTS * CHUNK_ROWS
GROUPS = LOCAL_ROWS // TOK_TILE
ROW_BLOCK = 256
BLOCK_CHUNKS = ROW_BLOCK // CHUNK_ROWS
MAX_CHUNKS = N_TILES * (TOK_TILE * TOP_K // CHUNK_ROWS + N_EXPERTS)
N_BLOCKS = -(-MAX_CHUNKS // BLOCK_CHUNKS) + N_EXPERTS
ZERO_CHUNK_ROW = LOCAL_ROWS - CHUNK_ROWS
FREE_ROWS = LOCAL_ROWS - TOK_TILE * TOP_K
DUMP_BASE = N_TILES * LOCAL_ROWS
DUMP_ROWS = 2 * ROW_BLOCK
assert DUMP_ROWS <= FREE_ROWS
WEIGHT_PIECES = 8
PIECES_PER_BLOCK = 2


def _dot(a, b):
    return jnp.dot(a, b, preferred_element_type=F32)


def _dot_nt(a, b):
    return lax.dot_general(a, b, (((1,), (1,)), ((), ())), preferred_element_type=F32)


def _dot_tn(a, b):
    return lax.dot_general(a, b, (((0,), (0,)), ((), ())), preferred_element_type=F32)


def _split3(a):
    a0 = a.astype(BF16)
    r1 = a - a0.astype(F32)
    a1 = r1.astype(BF16)
    r2 = r1 - a1.astype(F32)
    return a0, a1, r2.astype(BF16)


def _log_sigmoid(x):
    return jnp.minimum(x, 0.0) - jnp.log1p(jnp.exp(-jnp.abs(x)))


def _layer_norm(x, g, b):
    mu = jnp.mean(x, axis=-1, keepdims=True)
    xc = x - mu
    var = jnp.mean(xc * xc, axis=-1, keepdims=True)
    return xc * lax.rsqrt(var + LN_EPS) * g + b


def _gate_values(g, gbias):
    lane = lax.broadcasted_iota(I32, g.shape, 1)
    z = g + gbias
    return jnp.where(lane < HEADS, z, _log_sigmoid(z))


def _head_out(hh, o_h, gain):
    mu = jnp.mean(hh, axis=-1, keepdims=True)
    hc = hh - mu
    var = jnp.mean(hc * hc, axis=-1, keepdims=True)
    return jax.nn.sigmoid(o_h) * (hc * lax.rsqrt(var + LN_EPS) * gain)


def _prompt_mixer_kernel(x_ref, xn_ref, win_ref, wg_ref, gb_ref, wpool_ref, pscale_ref, mhg_ref, wout_ref,
                         ln1g_ref, ln1b_ref,
                         x1_ref, pool_ref, c_out_ref, n_out_ref, m_out_ref,
                         ubuf, mixbuf, pbuf, gbuf, c_s, n_s, m_s):
    ti = pl.program_id(1)
    nt = pl.num_programs(1)
    TT = MIX_TILE
    S = MIX_SEQS

    @pl.when(ti == 0)
    def _():
        for s in range(S):
            ubuf[s, 0:HIST_PAD, :] = jnp.zeros((HIST_PAD, POOL_WIDTH), F32)
        c_s[...] = jnp.zeros_like(c_s)
        n_s[...] = jnp.zeros_like(n_s)
        m_s[...] = jnp.zeros_like(m_s)

    x = jnp.concatenate([x_ref[s, 0] for s in range(S)], axis=0)
    cur = lax.rem(ti, 2)
    nxt = 1 - cur
    n_main = POOL_WIDTH + 4 * MLSTM_WIDTH

    @pl.when(ti == 0)
    def _():
        xb0 = x.astype(BF16)
        pbuf[0] = _dot(xb0, win_ref[...])
        gbuf[0] = _dot(xb0, wg_ref[...])

    xnb = jnp.concatenate([xn_ref[s, 0] for s in range(S)], axis=0).astype(BF16)

    def slab(j):
        def run():
            pbuf[nxt, :, j * MXU_COLS:(j + 1) * MXU_COLS] = _dot(xnb, win_ref[:, j * MXU_COLS:(j + 1) * MXU_COLS])
        return run

    def gate_slab():
        gbuf[nxt] = _dot(xnb, wg_ref[...])

    pending = [slab(j) for j in range(n_main // MXU_COLS)] + [gate_slab]

    def ahead(n=1):
        for _ in range(n):
            if pending:
                pending.pop(0)()

    proj = pbuf.at[cur]
    g = gbuf[cur]

    L = CHUNK
    row = lax.broadcasted_iota(I32, (L, L), 0)
    col = lax.broadcasted_iota(I32, (L, L), 1)
    causal = row >= col
    tril = jnp.where(causal, 1.0, 0.0).astype(BF16)
    pos = ti * TT + lax.broadcasted_iota(I32, (TT, 1), 0)

    for s in range(S):
        base = s * TT
        u = proj[base:base + TT, 0:POOL_WIDTH]

        ubuf[s, HIST_PAD:HIST_PAD + TT, :] = u
        for gi, w in enumerate(POOL_WINDOWS):
            sl = slice(gi * POOL_GROUP_DIM, (gi + 1) * POOL_GROUP_DIM)
            ug = u[:, sl]
            acc = ug
            for i in range(1, w):
                acc = acc + ubuf[s, HIST_PAD - i:HIST_PAD - i + TT, sl]
            cnt = jnp.minimum(pos + 1, w).astype(F32)
            z = acc / cnt - ug
            mixbuf[base:base + TT, sl] = _dot(z.astype(BF16), wpool_ref[gi]) * pscale_ref[:, sl]

        @pl.when(ti == nt - 1)
        def _():
            pool_ref[s, 0] = ubuf[s, TT + 1:TT + HIST_PAD, :]

        ubuf[s, 0:HIST_PAD, :] = ubuf[s, TT:TT + HIST_PAD, :]

    NC = TT // L
    chains = [(s, h) for s in range(S) for h in range(HEADS)]
    units = [(s, c, h) for c in range(NC) for s in range(S) for h in range(HEADS)]
    U = range(len(units))

    def rows(s, c):
        return slice(s * TT + c * L, s * TT + (c + 1) * L)

    def head_cols(part, h):
        return slice(part * POOL_WIDTH + h * HEAD_DIM, part * POOL_WIDTH + (h + 1) * HEAD_DIM)

    gate, cum, gate_t, cum_t = {}, {}, {}, {}
    for c in range(NC):
        for s in range(S):
            val = _gate_values(g[rows(s, c), :], gb_ref[...])
            v0, v1, v2 = _split3(val)
            gate[s, c] = val
            cum[s, c] = _dot(tril, v0) + _dot(tril, v1) + _dot(tril, v2)
    for key in gate:
        gate_t[key] = gate[key].T
        cum_t[key] = cum[key].T
    ahead()
    qf = [proj[rows(s, c), head_cols(1, h)] for s, c, h in units]
    kf = [proj[rows(s, c), head_cols(2, h)] * (HEAD_DIM ** -0.5) for s, c, h in units]
    vf = [proj[rows(s, c), head_cols(3, h)] for s, c, h in units]
    qb = [a.astype(BF16) for a in qf]
    kb = [a.astype(BF16) for a in kf]
    f_col = [cum[s, c][:, HEADS + h:HEADS + h + 1] for s, c, h in units]
    ahead()
    log_d = [jnp.where(causal, f_col[u] - cum_t[s, c][HEADS + h:HEADS + h + 1, :] + gate_t[s, c][h:h + 1, :],
                       -jnp.inf) for u, (s, c, h) in enumerate(units)]
    ahead()
    row_max = [jnp.max(log_d[u], axis=-1, keepdims=True) for u in U]
    ahead()
    qk_raw = [_dot_nt(qb[u], kb[u]) for u in U]

    m_prev, m_t, inter = [None] * len(units), [None] * len(units), [None] * len(units)
    m_run = {(s, h): m_s[s, h:h + 1, 0:1] for s, h in chains}
    for u, (s, c, h) in enumerate(units):
        m_prev[u] = m_run[s, h]
        inter[u] = m_prev[u] + f_col[u]
        m_t[u] = jnp.maximum(inter[u], row_max[u])
        m_run[s, h] = m_t[u][L - 1:L, :]
    m_new = [m_t[u][L - 1:L, :] for u in U]

    ahead()
    dw = [jnp.exp(log_d[u] - m_t[u]) for u in U]
    sc = [jnp.exp(inter[u] - m_t[u]) for u in U]
    ahead()
    qk = [qk_raw[u] * dw[u] for u in U]
    ahead()
    intra = [_dot(qk[u].astype(BF16), vf[u].astype(BF16)) for u in U]
    ahead()
    row_sum = [jnp.sum(qk[u], axis=-1, keepdims=True) for u in U]
    floor = [jnp.exp(-m_t[u]) for u in U]
    f_last = [f_col[u][L - 1:L, :] for u in U]
    ahead()
    wk = [jnp.exp(gate[s, c][:, h:h + 1] + f_last[u] - f_col[u] - m_new[u]) for u, (s, c, h) in enumerate(units)]
    decay = [jnp.exp(m_prev[u] + f_last[u] - m_new[u]) for u in U]
    ahead()
    upd = [_dot_tn((vf[u] * wk[u]).astype(BF16), kb[u]) for u in U]
    ahead()
    n_upd = [jnp.sum(wk[u] * kf[u], axis=0, keepdims=True) for u in U]

    c_run = {(s, h): c_s[s, h] for s, h in chains}
    n_run = {(s, h): n_s[s, h:h + 1, :] for s, h in chains}
    hh = [None] * len(units)
    for c in range(NC):
        cu = [u for u in U if units[u][1] == c]
        inter_term = {u: _dot_nt(qb[u], c_run[units[u][0], units[u][2]].astype(BF16)) for u in cu}
        n_term = {u: jnp.sum(qf[u] * n_run[units[u][0], units[u][2]], axis=-1, keepdims=True) for u in cu}
        for u in cu:
            s, _, h = units[u]
            num = intra[u] + sc[u] * inter_term[u]
            den = row_sum[u] + sc[u] * n_term[u]
            hh[u] = num / jnp.maximum(jnp.abs(den), floor[u])
            c_run[s, h] = decay[u] * c_run[s, h] + upd[u]
            n_run[s, h] = decay[u] * n_run[s, h] + n_upd[u]
    ahead(len(pending))
    for s, h in chains:
        c_s[s, h] = c_run[s, h]
        n_s[s, h:h + 1, :] = n_run[s, h]
        m_s[s, h:h + 1, :] = jnp.broadcast_to(m_run[s, h], (1, LANES))
    for u, (s, c, h) in enumerate(units):
        mixbuf[rows(s, c), head_cols(1, h)] = _head_out(
            hh[u], proj[rows(s, c), head_cols(4, h)], mhg_ref[:, h * HEAD_DIM:(h + 1) * HEAD_DIM])

    @pl.when(ti == nt - 1)
    def _():
        for s in range(S):
            c_out_ref[s, 0] = c_s[s]
            n_out_ref[s, 0] = n_s[s, 0:HEADS, :]
            m_out_ref[s, 0] = m_s[s]

    mix = _dot(mixbuf[...].astype(BF16), wout_ref[...])
    x1 = _layer_norm(DN_ALPHA * x + mix, ln1g_ref[...], ln1b_ref[...])
    for s in range(S):
        x1_ref[s] = x1[s * TT:(s + 1) * TT, :]


def _prompt_mixer(x, w_in_b, w_g_b, gbias, w_pool_b, pscale, mhg, w_out_b, ln1g, ln1b):
    nt = SEQ // MIX_TILE
    S = MIX_SEQS
    G = BATCH // S
    const2 = lambda b, t: (0, 0)
    const3 = lambda b, t: (0, 0, 0)
    outs = pl.pallas_call(
        _prompt_mixer_kernel,
        grid=(G, nt),
        in_specs=[
            pl.BlockSpec((S, 1, MIX_TILE, D_MODEL), lambda b, t: (0, b, t, 0)),
            pl.BlockSpec((S, 1, MIX_TILE, D_MODEL), lambda b, t: (0, b, jnp.minimum(t + 1, nt - 1), 0)),
            pl.BlockSpec(w_in_b.shape, const2),
            pl.BlockSpec(w_g_b.shape, const2),
            pl.BlockSpec(gbias.shape, const2),
            pl.BlockSpec(w_pool_b.shape, const3),
            pl.BlockSpec(pscale.shape, const2),
            pl.BlockSpec(mhg.shape, const2),
            pl.BlockSpec(w_out_b.shape, const2),
            pl.BlockSpec(ln1g.shape, const2),
            pl.BlockSpec(ln1b.shape, const2),
        ],
        out_specs=[
            pl.BlockSpec((S, MIX_TILE, D_MODEL), lambda b, t: (0, b * nt + t, 0)),
            pl.BlockSpec((S, 1, POOL_HIST, POOL_WIDTH), lambda b, t: (0, b, 0, 0)),
            pl.BlockSpec((S, 1, HEADS, HEAD_DIM, HEAD_DIM), lambda b, t: (0, b, 0, 0, 0)),
            pl.BlockSpec((S, 1, HEADS, HEAD_DIM), lambda b, t: (0, b, 0, 0)),
            pl.BlockSpec((S, 1, SUBLANES, LANES), lambda b, t: (0, b, 0, 0)),
        ],
        out_shape=[
            jax.ShapeDtypeStruct((S, G * SEQ, D_MODEL), F32),
            jax.ShapeDtypeStruct((S, G, POOL_HIST, POOL_WIDTH), F32),
            jax.ShapeDtypeStruct((S, G, HEADS, HEAD_DIM, HEAD_DIM), F32),
            jax.ShapeDtypeStruct((S, G, HEADS, HEAD_DIM), F32),
            jax.ShapeDtypeStruct((S, G, SUBLANES, LANES), F32),
        ],
        scratch_shapes=[
            pltpu.VMEM((S, HIST_PAD + MIX_TILE, POOL_WIDTH), F32),
            pltpu.VMEM((S * MIX_TILE, D_MODEL), F32),
            pltpu.VMEM((2, S * MIX_TILE, POOL_WIDTH + 4 * MLSTM_WIDTH), F32),
            pltpu.VMEM((2, S * MIX_TILE, LANES), F32),
            pltpu.VMEM((S, HEADS, HEAD_DIM, HEAD_DIM), F32),
            pltpu.VMEM((S, SUBLANES, HEAD_DIM), F32),
            pltpu.VMEM((S, SUBLANES, LANES), F32),
        ],
        compiler_params=pltpu.CompilerParams(
            dimension_semantics=("arbitrary", "arbitrary"), vmem_limit_bytes=VMEM_LIMIT),
        name="prompt_mixer",
    )(x.reshape(S, G, SEQ, D_MODEL), x.reshape(S, G, SEQ, D_MODEL), w_in_b, w_g_b, gbias, w_pool_b, pscale, mhg, w_out_b, ln1g, ln1b)
    x1, pool, c, n, m = outs
    return (x1.reshape(N_PROMPT, D_MODEL), pool.reshape(BATCH, POOL_HIST, POOL_WIDTH),
            c.reshape(BATCH, HEADS, HEAD_DIM, HEAD_DIM), n.reshape(BATCH, HEADS, HEAD_DIM),
            m.reshape(BATCH, SUBLANES, LANES))


def _sample_mixer_kernel(x_ref, hist_ref, c_ref, n_ref, m_ref, win_ref, wg_ref, gb_ref, wpool_ref,
                         pscale_ref, mhg_ref, wout_ref, ln1g_ref, ln1b_ref,
                         x1_ref, pool_out_ref, c_out_ref, n_out_ref, m_out_ref,
                         q_s, k_s, vw_s, v_s, o_s, mixbuf, h_s, coef_s):
    i = pl.program_id(0)
    nsteps = pl.num_programs(0)
    B = DEC_BATCH

    @pl.when(i == 0)
    def _():
        x = x_ref[...]
        xb = x.astype(BF16)
        proj = _dot(xb, win_ref[...])
        g = _dot(xb, wg_ref[...])
        u = proj[:, 0:POOL_WIDTH]
        for gi, w in enumerate(POOL_WINDOWS):
            sl = slice(gi * POOL_GROUP_DIM, (gi + 1) * POOL_GROUP_DIM)
            ug = u[:, sl]
            s = ug
            for j in range(1, w):
                r = POOL_HIST - j
                s = s + hist_ref[:, r * POOL_WIDTH + gi * POOL_GROUP_DIM:r * POOL_WIDTH + (gi + 1) * POOL_GROUP_DIM]
            cnt = float(min(PAST_LEN + 1, w))
            z = s / cnt - ug
            mixbuf[:, sl] = _dot(z.astype(BF16), wpool_ref[gi]) * pscale_ref[:, sl]
        pool_out_ref[:, 0:(POOL_HIST - 1) * POOL_WIDTH] = hist_ref[:, POOL_WIDTH:POOL_HIST * POOL_WIDTH]
        pool_out_ref[:, (POOL_HIST - 1) * POOL_WIDTH:POOL_HIST * POOL_WIDTH] = u

        val = _gate_values(g, gb_ref[...])
        lane = lax.broadcasted_iota(I32, (B, LANES), 1)
        qk_all = jnp.zeros((B, LANES), F32)
        sc_all = jnp.zeros((B, LANES), F32)
        den_all = jnp.zeros((B, LANES), F32)
        floor_all = jnp.zeros((B, LANES), F32)
        m_all = jnp.zeros((B, LANES), F32)
        for h in range(HEADS):
            hs = slice(h * HEAD_DIM, (h + 1) * HEAD_DIM)
            qf = proj[:, POOL_WIDTH + h * HEAD_DIM:POOL_WIDTH + (h + 1) * HEAD_DIM]
            kf = proj[:, 2 * POOL_WIDTH + h * HEAD_DIM:2 * POOL_WIDTH + (h + 1) * HEAD_DIM] * (HEAD_DIM ** -0.5)
            vf = proj[:, 3 * POOL_WIDTH + h * HEAD_DIM:3 * POOL_WIDTH + (h + 1) * HEAD_DIM]
            ig = val[:, h:h + 1]
            lf = val[:, HEADS + h:HEADS + h + 1]
            m0 = m_ref[:, h:h + 1]
            n0 = n_ref[:, hs]
            inter = m0 + lf
            m_t = jnp.maximum(inter, ig)
            dw = jnp.exp(ig - m_t)
            sc = jnp.exp(inter - m_t)
            qk = jnp.sum(qf * kf, axis=-1, keepdims=True) * dw
            den = qk + sc * jnp.sum(qf * n0, axis=-1, keepdims=True)
            n_out_ref[:, hs] = sc * n0 + dw * kf
            q_s[0:B, hs] = qf
            k_s[0:B, hs] = kf
            v_s[0:B, hs] = vf
            vw_s[0:B, hs] = vf * dw
            sel = lane == h
            qk_all = jnp.where(sel, qk, qk_all)
            sc_all = jnp.where(sel, sc, sc_all)
            den_all = jnp.where(sel, den, den_all)
            floor_all = jnp.where(sel, jnp.exp(-m_t), floor_all)
            m_all = jnp.where(lane == HEADS + h, m_t, m_all)
        o_s[...] = proj[:, 4 * POOL_WIDTH:5 * POOL_WIDTH]
        coef_s[0] = qk_all
        coef_s[1] = sc_all
        coef_s[2] = den_all
        coef_s[3] = floor_all
        m_out_ref[...] = m_all

    rows = pl.ds(pl.multiple_of(i * SAMPLE_BT, SAMPLE_BT), SAMPLE_BT)
    q_t, k_t, v_t, vw_t = q_s[rows, :], k_s[rows, :], v_s[rows, :], vw_s[rows, :]
    qk_t, sc_t, den_t, floor_t = coef_s[0, rows, :], coef_s[1, rows, :], coef_s[2, rows, :], coef_s[3, rows, :]
    h_rows = []
    for bl in range(SAMPLE_BT):
        heads = []
        for h in range(HEADS):
            hs = slice(h * HEAD_DIM, (h + 1) * HEAD_DIM)
            c_prev = c_ref[bl, h]
            q8 = jnp.broadcast_to(q_t[bl:bl + 1, hs], (SUBLANES, HEAD_DIM))
            cq = _dot_nt(q8.astype(BF16), c_prev.astype(BF16))[0:1, :]
            qk = qk_t[bl:bl + 1, h:h + 1]
            sc = sc_t[bl:bl + 1, h:h + 1]
            num = qk * v_t[bl:bl + 1, hs] + sc * cq
            heads.append(num / jnp.maximum(jnp.abs(den_t[bl:bl + 1, h:h + 1]), floor_t[bl:bl + 1, h:h + 1]))
            v_col = jnp.broadcast_to(vw_t[bl:bl + 1, hs], (HEAD_DIM, HEAD_DIM)).T
            c_out_ref[bl, h] = sc * c_prev + v_col * k_t[bl:bl + 1, hs]
        h_rows.append(jnp.concatenate(heads, axis=1))
    h_s[rows, :] = jnp.concatenate(h_rows, axis=0)

    @pl.when(i == nsteps - 1)
    def _():
        for h in range(HEADS):
            hs = slice(h * HEAD_DIM, (h + 1) * HEAD_DIM)
            mixbuf[:, POOL_WIDTH + h * HEAD_DIM:POOL_WIDTH + (h + 1) * HEAD_DIM] = _head_out(
                h_s[:, hs], o_s[:, hs], mhg_ref[:, hs])
        mix = _dot(mixbuf[...].astype(BF16), wout_ref[...])
        x1 = _layer_norm(DN_ALPHA * x_ref[...] + mix, ln1g_ref[...], ln1b_ref[...])
        x1_ref[0:B, :] = x1
        x1_ref[B:TOK_TILE, :] = jnp.zeros((TOK_TILE - B, D_MODEL), F32)


def _sample_mixer(x, hist2, c0, n0, m0, w_in_b, w_g_b, gbias, w_pool_b, pscale, mhg, w_out_b, ln1g, ln1b):
    B = DEC_BATCH
    steps = B // SAMPLE_BT
    full = lambda a: pl.BlockSpec(a.shape, lambda i: (0,) * a.ndim)
    c_spec = pl.BlockSpec((SAMPLE_BT, HEADS, HEAD_DIM, HEAD_DIM), lambda i: (i, 0, 0, 0))
    return pl.pallas_call(
        _sample_mixer_kernel,
        grid=(steps,),
        in_specs=[full(x), full(hist2), c_spec, full(n0), full(m0), full(w_in_b), full(w_g_b), full(gbias),
                  full(w_pool_b), full(pscale), full(mhg), full(w_out_b), full(ln1g), full(ln1b)],
        out_specs=[
            pl.BlockSpec((TOK_TILE, D_MODEL), lambda i: (0, 0)),
            pl.BlockSpec((B, POOL_HIST * POOL_WIDTH), lambda i: (0, 0)),
            c_spec,
            pl.BlockSpec((B, MLSTM_WIDTH), lambda i: (0, 0)),
            pl.BlockSpec((B, LANES), lambda i: (0, 0)),
        ],
        out_shape=[
            jax.ShapeDtypeStruct((TOK_TILE, D_MODEL), F32),
            jax.ShapeDtypeStruct((B, POOL_HIST * POOL_WIDTH), F32),
            jax.ShapeDtypeStruct((B, HEADS, HEAD_DIM, HEAD_DIM), F32),
            jax.ShapeDtypeStruct((B, MLSTM_WIDTH), F32),
            jax.ShapeDtypeStruct((B, LANES), F32),
        ],
        scratch_shapes=[
            pltpu.VMEM((B, MLSTM_WIDTH), F32),
            pltpu.VMEM((B, MLSTM_WIDTH), F32),
            pltpu.VMEM((B, MLSTM_WIDTH), F32),
            pltpu.VMEM((B, MLSTM_WIDTH), F32),
            pltpu.VMEM((B, MLSTM_WIDTH), F32),
            pltpu.VMEM((B, D_MODEL), F32),
            pltpu.VMEM((B, MLSTM_WIDTH), F32),
            pltpu.VMEM((4, B, LANES), F32),
        ],
        compiler_params=pltpu.CompilerParams(
            dimension_semantics=("arbitrary",), vmem_limit_bytes=VMEM_LIMIT),
        name="sample_mixer",
    )(x, hist2, c0, n0, m0, w_in_b, w_g_b, gbias, w_pool_b, pscale, mhg, w_out_b, ln1g, ln1b)


def _pick_tile(i, prompt_ref, sample_ref):
    return jnp.where(i < N_PROMPT_TILES, prompt_ref[...], sample_ref[...])


def _placement(slot_rows, group):
    r = group * TOK_TILE + lax.broadcasted_iota(I32, (TOK_TILE, TOK_TILE), 0)
    return [r == s for s in slot_rows]


def _route_kernel(xp_ref, xs_ref, wrt_ref, br_ref, slot_ref, gate_ref, nch_ref, sorted_ref):
    i = pl.program_id(0)
    T = TOK_TILE
    E = N_EXPERTS

    x = _pick_tile(i, xp_ref, xs_ref)
    xh = x.astype(BF16)
    xl = (x - xh.astype(F32)).astype(BF16)
    w = wrt_ref[...]
    wh = w.astype(BF16)
    wl = (w - wh.astype(F32)).astype(BF16)
    logits = _dot_nt(wh, xh) + (_dot_nt(wh, xl) + _dot_nt(wl, xh)) + br_ref[:, 0:1]

    erow = lax.broadcasted_iota(I32, (E, T), 0).astype(F32)
    work = logits
    vals, sels = [], []
    for _ in range(TOP_K):
        mx = jnp.max(work, axis=0, keepdims=True)
        idx = jnp.min(jnp.where(work == mx, erow, float(E)), axis=0, keepdims=True)
        sel = erow == idx
        work = jnp.where(sel, -jnp.inf, work)
        vals.append(mx)
        sels.append(sel)
    chosen = jnp.logical_or(jnp.logical_or(sels[0], sels[1]), jnp.logical_or(sels[2], sels[3]))
    es = [jnp.exp(v - vals[0]) for v in vals]
    tot = es[0] + es[1] + es[2] + es[3]

    onehot = jnp.where(chosen, 1.0, 0.0)
    trow = lax.broadcasted_iota(I32, (T, T), 0)
    tcol = lax.broadcasted_iota(I32, (T, T), 1)
    before = jnp.where(trow < tcol, 1.0, 0.0).astype(BF16)
    rank = _dot(onehot.astype(BF16), before)
    cnt = jnp.sum(onehot, axis=1, keepdims=True)
    nch = jnp.floor((cnt + (CHUNK_ROWS - 1)) * (1.0 / CHUNK_ROWS))
    lower = jnp.where(lax.broadcasted_iota(I32, (E, E), 0) > lax.broadcasted_iota(I32, (E, E), 1), 1.0, 0.0)
    nch_b = jnp.broadcast_to(nch, (E, LANES))
    seg_start = _dot(lower.astype(BF16), nch_b.astype(BF16))[:, 0:1] * CHUNK_ROWS
    base = seg_start + rank

    r8 = lax.broadcasted_iota(I32, (SUBLANES, T), 0)
    s_out = jnp.zeros((SUBLANES, T), I32)
    g_out = jnp.zeros((SUBLANES, T), F32)
    slot_rows = []
    for j in range(TOP_K):
        slot_j = jnp.sum(jnp.where(sels[j], base, 0.0), axis=0, keepdims=True).astype(I32)
        slot_rows.append(slot_j)
        s_out = jnp.where(r8 == j, slot_j, s_out)
        g_out = jnp.where(r8 == j, es[j] / tot, g_out)
    slot_ref[0] = s_out
    gate_ref[0] = g_out
    nch_ref[0] = nch_b

    for grp in range(GROUPS):
        m = _placement(slot_rows, grp)
        hit = jnp.logical_or(jnp.logical_or(m[0], m[1]), jnp.logical_or(m[2], m[3]))
        place = jnp.where(hit, 1.0, 0.0).astype(BF16)
        sorted_ref[grp * T:(grp + 1) * T, :] = _dot(place, xh).astype(BF16)


def _route(x1p, x1s, w_router_t, b_router_col):
    tile_spec = pl.BlockSpec((1, SUBLANES, TOK_TILE), lambda i: (i, 0, 0))
    return pl.pallas_call(
        _route_kernel,
        grid=(N_TILES,),
        in_specs=[
            pl.BlockSpec((TOK_TILE, D_MODEL), lambda i: (jnp.minimum(i, N_PROMPT_TILES - 1), 0)),
            pl.BlockSpec((TOK_TILE, D_MODEL), lambda i: (0, 0)),
            pl.BlockSpec(w_router_t.shape, lambda i: (0, 0)),
            pl.BlockSpec(b_router_col.shape, lambda i: (0, 0)),
        ],
        out_specs=[tile_spec, tile_spec,
                   pl.BlockSpec((1, N_EXPERTS, LANES), lambda i: (i, 0, 0)),
                   pl.BlockSpec((LOCAL_ROWS, D_MODEL), lambda i: (i, 0))],
        out_shape=[
            jax.ShapeDtypeStruct((N_TILES, SUBLANES, TOK_TILE), I32),
            jax.ShapeDtypeStruct((N_TILES, SUBLANES, TOK_TILE), F32),
            jax.ShapeDtypeStruct((N_TILES, N_EXPERTS, LANES), F32),
            jax.ShapeDtypeStruct((N_TILES * LOCAL_ROWS, D_MODEL), BF16),
        ],
        compiler_params=pltpu.CompilerParams(
            dimension_semantics=("arbitrary",), vmem_limit_bytes=VMEM_LIMIT),
        name="route",
    )(x1p, x1s, w_router_t, b_router_col)


def _expert_kernel(src_ref, dst_ref, bexp_ref, first_ref, next_ref, nused_ref,
                   sorted_hbm, w1_hbm, b1_ref, w2_hbm, b2_ref, out_hbm,
                   w1_stage, w2_stage, w1_b, w2_b, xbuf, obuf, zbuf, wsem, gsem, ssem, zsem):
    nused = nused_ref[0]

    def fetch_piece(e, p):
        r1 = pl.ds(pl.multiple_of(p * (D_MODEL // WEIGHT_PIECES), SUBLANES), D_MODEL // WEIGHT_PIECES)
        r2 = pl.ds(pl.multiple_of(p * (D_FF // WEIGHT_PIECES), SUBLANES), D_FF // WEIGHT_PIECES)
        return (pltpu.make_async_copy(w1_hbm.at[e, r1, :], w1_stage.at[r1, :], wsem.at[0]),
                pltpu.make_async_copy(w2_hbm.at[e, r2, :], w2_stage.at[r2, :], wsem.at[1]))

    def start_pieces(e, lo, hi):
        def body(p, c):
            for cp in fetch_piece(e, p):
                cp.start()
            return c
        lax.fori_loop(lo, hi, body, 0)

    def gather(b, q):
        slot = lax.rem(b, 2)
        row = pl.multiple_of(src_ref[b * BLOCK_CHUNKS + q], CHUNK_ROWS)
        return pltpu.make_async_copy(sorted_hbm.at[pl.ds(row, CHUNK_ROWS), :],
                                     xbuf.at[slot, pl.ds(q * CHUNK_ROWS, CHUNK_ROWS), :], gsem.at[slot])

    def scatter(b, q):
        slot = lax.rem(b, 2)
        row = pl.multiple_of(dst_ref[b * BLOCK_CHUNKS + q], CHUNK_ROWS)
        return pltpu.make_async_copy(obuf.at[slot, pl.ds(q * CHUNK_ROWS, CHUNK_ROWS), :],
                                     out_hbm.at[pl.ds(row, CHUNK_ROWS), :], ssem.at[slot])

    def zero_rows(start, n_rows):
        start = pl.multiple_of(start, CHUNK_ROWS)
        return pltpu.make_async_copy(zbuf.at[pl.ds(0, n_rows), :], out_hbm.at[pl.ds(start, n_rows), :], zsem)

    def zero_tail(k):
        return zero_rows(k * LOCAL_ROWS + TOK_TILE * TOP_K, FREE_ROWS)

    zbuf[...] = jnp.zeros_like(zbuf)
    lax.fori_loop(0, N_TILES, lambda k, c: (zero_tail(k).start(), c)[1], 0)
    zero_rows(DUMP_BASE, DUMP_ROWS).start()
    for q in range(BLOCK_CHUNKS):
        gather(0, q).start()
    lax.fori_loop(0, N_TILES, lambda k, c: (zero_tail(k).wait(), c)[1], 0)
    zero_rows(DUMP_BASE, DUMP_ROWS).wait()

    half = MXU_COLS // 2
    k_io = lax.broadcasted_iota(I32, (MXU_COLS, MXU_COLS), 0)
    j_io = lax.broadcasted_iota(I32, (MXU_COLS, MXU_COLS), 1)
    src_col = jnp.where(j_io < half, 2 * j_io, 2 * (j_io - half) + 1)
    perm = jnp.where(k_io == src_col, 1.0, 0.0).astype(BF16)

    def block(i, fetched):
        e = bexp_ref[i]
        slot = lax.rem(i, 2)
        is_first = first_ref[i] == 1

        @pl.when(is_first)
        def _():
            start_pieces(e, fetched, WEIGHT_PIECES)

            def wait_piece(p, c):
                for cp in fetch_piece(e, p):
                    cp.wait()
                return c
            lax.fori_loop(0, WEIGHT_PIECES, wait_piece, 0)
            for c in range(2 * D_FF // MXU_COLS):
                blk = w1_stage[:, c * MXU_COLS:(c + 1) * MXU_COLS].astype(BF16)
                sep = _dot(blk, perm).astype(BF16)
                w1_b[:, c * half:(c + 1) * half] = sep[:, 0:half]
                w1_b[:, D_FF + c * half:D_FF + (c + 1) * half] = sep[:, half:MXU_COLS]
            w2_b[...] = w2_stage[...].astype(BF16)

        fetched = jnp.where(is_first, 0, fetched)

        for q in range(BLOCK_CHUNKS):
            gather(i, q).wait()

        @pl.when(i + 1 < nused)
        def _():
            for q in range(BLOCK_CHUNKS):
                gather(i + 1, q).start()

        @pl.when(i >= 2)
        def _():
            for q in range(BLOCK_CHUNKS):
                scatter(i - 2, q).wait()

        h = _dot(xbuf[slot], w1_b[...]) + b1_ref[e]
        glu = jnp.minimum(h[:, 0:D_FF], SWIGLU_LIMIT)
        lin = jnp.clip(h[:, D_FF:2 * D_FF], -SWIGLU_LIMIT, SWIGLU_LIMIT)
        a = glu * jax.nn.sigmoid(SWIGLU_ALPHA * glu) * (lin + 1.0)
        obuf[slot] = _dot(a.astype(BF16), w2_b[...]) + b2_ref[e]
        for q in range(BLOCK_CHUNKS):
            scatter(i, q).start()

        more = jnp.where(next_ref[i] >= 0, jnp.minimum(fetched + PIECES_PER_BLOCK, WEIGHT_PIECES), fetched)
        start_pieces(next_ref[i], fetched, more)
        return more

    lax.fori_loop(0, nused, block, jnp.int32(0))

    @pl.when(nused >= 2)
    def _():
        for q in range(BLOCK_CHUNKS):
            scatter(nused - 2, q).wait()
    for q in range(BLOCK_CHUNKS):
        scatter(nused - 1, q).wait()


def _experts(chunk_src, chunk_dst, block_expert, block_first, block_next, n_used, sorted_rows, w1, b1p, w2, b2):
    whole3 = lambda i, *_: (0, 0, 0)
    grid_spec = pltpu.PrefetchScalarGridSpec(
        num_scalar_prefetch=6,
        grid=(1,),
        in_specs=[
            pl.BlockSpec(memory_space=pl.ANY),
            pl.BlockSpec(memory_space=pl.ANY),
            pl.BlockSpec(b1p.shape, whole3),
            pl.BlockSpec(memory_space=pl.ANY),
            pl.BlockSpec(b2.shape, whole3),
        ],
        out_specs=pl.BlockSpec(memory_space=pl.ANY),
        scratch_shapes=[
            pltpu.VMEM((D_MODEL, 2 * D_FF), F32),
            pltpu.VMEM((D_FF, D_MODEL), F32),
            pltpu.VMEM((D_MODEL, 2 * D_FF), BF16),
            pltpu.VMEM((D_FF, D_MODEL), BF16),
            pltpu.VMEM((2, ROW_BLOCK, D_MODEL), BF16),
            pltpu.VMEM((2, ROW_BLOCK, D_MODEL), F32),
            pltpu.VMEM((FREE_ROWS, D_MODEL), F32),
            pltpu.SemaphoreType.DMA((2,)),
            pltpu.SemaphoreType.DMA((2,)),
            pltpu.SemaphoreType.DMA((2,)),
            pltpu.SemaphoreType.DMA(()),
        ],
    )
    return pl.pallas_call(
        _expert_kernel,
        grid_spec=grid_spec,
        out_shape=jax.ShapeDtypeStruct((DUMP_BASE + DUMP_ROWS, D_MODEL), F32),
        compiler_params=pltpu.CompilerParams(
            dimension_semantics=("arbitrary",), vmem_limit_bytes=VMEM_LIMIT),
        name="experts",
    )(chunk_src, chunk_dst, block_expert, block_first, block_next, n_used, sorted_rows, w1, b1p, w2, b2)


def _combine_kernel(slot_ref, gate_ref, xp_ref, xs_ref, pp_ref, ps_ref, eo_ref, ln2g_ref, ln2b_ref,
                    wpg_ref, wple_ref, yp_ref, ys_ref):
    i = pl.program_id(0)
    T = TOK_TILE

    slot_rows = [slot_ref[0, j:j + 1, :] for j in range(TOP_K)]
    gate_rows = [gate_ref[0, j:j + 1, :] for j in range(TOP_K)]
    pad = jnp.zeros((LANES - SUBLANES, T), F32)
    slots_t = jnp.concatenate([slot_ref[0].astype(F32), pad], axis=0).T
    slot_cols = [slots_t[:, j:j + 1].astype(I32) for j in range(TOP_K)]

    ff = jnp.zeros((T, D_MODEL), F32)
    for grp in range(GROUPS):
        m = _placement(slot_rows, grp)
        weighted = jnp.where(m[0], gate_rows[0], jnp.where(m[1], gate_rows[1], jnp.where(
            m[2], gate_rows[2], jnp.where(m[3], gate_rows[3], 0.0))))
        g_col = jnp.sum(weighted, axis=1, keepdims=True)
        z = (eo_ref[grp * T:(grp + 1) * T, :] * g_col).astype(BF16)
        r = grp * T + lax.broadcasted_iota(I32, (T, T), 1)
        hit = jnp.logical_or(jnp.logical_or(r == slot_cols[0], r == slot_cols[1]),
                             jnp.logical_or(r == slot_cols[2], r == slot_cols[3]))
        ff = ff + _dot(jnp.where(hit, 1.0, 0.0).astype(BF16), z)

    x1 = _pick_tile(i, xp_ref, xs_ref)
    x2 = _layer_norm(DN_ALPHA * x1 + ff, ln2g_ref[...], ln2b_ref[...])
    p = _pick_tile(i, pp_ref, ps_ref)
    y = x2 + jax.nn.sigmoid(_dot(x2.astype(BF16), wpg_ref[...])) * _dot(p.astype(BF16), wple_ref[...])

    @pl.when(i < N_PROMPT_TILES)
    def _():
        yp_ref[...] = y

    @pl.when(i == N_PROMPT_TILES)
    def _():
        ys_ref[...] = y[0:DEC_BATCH, :]


def _combine(slots, gates, x1p, x1s, pp, ps, expert_out, ln2g, ln2b, w_pg_b, w_ple_b):
    tile_idx = lambda i: (jnp.minimum(i, N_PROMPT_TILES - 1), 0)
    const2 = lambda i: (0, 0)
    return pl.pallas_call(
        _combine_kernel,
        grid=(N_TILES,),
        in_specs=[
            pl.BlockSpec((1, SUBLANES, TOK_TILE), lambda i: (i, 0, 0)),
            pl.BlockSpec((1, SUBLANES, TOK_TILE), lambda i: (i, 0, 0)),
            pl.BlockSpec((TOK_TILE, D_MODEL), tile_idx),
            pl.BlockSpec((TOK_TILE, D_MODEL), const2),
            pl.BlockSpec((TOK_TILE, PLE_DIM), tile_idx),
            pl.BlockSpec((TOK_TILE, PLE_DIM), const2),
            pl.BlockSpec((LOCAL_ROWS, D_MODEL), lambda i: (i, 0)),
            pl.BlockSpec(ln2g.shape, const2),
            pl.BlockSpec(ln2b.shape, const2),
            pl.BlockSpec(w_pg_b.shape, const2),
            pl.BlockSpec(w_ple_b.shape, const2),
        ],
        out_specs=[
            pl.BlockSpec((TOK_TILE, D_MODEL), tile_idx),
            pl.BlockSpec((DEC_BATCH, D_MODEL), const2),
        ],
        out_shape=[
            jax.ShapeDtypeStruct((N_PROMPT, D_MODEL), F32),
            jax.ShapeDtypeStruct((DEC_BATCH, D_MODEL), F32),
        ],
        compiler_params=pltpu.CompilerParams(
            dimension_semantics=("arbitrary",), vmem_limit_bytes=VMEM_LIMIT),
        name="combine",
    )(slots, gates, x1p, x1s, pp, ps, expert_out, ln2g, ln2b, w_pg_b, w_ple_b)


def _block_tables(nch):
    seg_start = (jnp.cumsum(nch, axis=1) - nch) * CHUNK_ROWS
    tot = jnp.sum(nch, axis=0)
    nblk = (tot + BLOCK_CHUNKS - 1) // BLOCK_CHUNKS
    blk_end = jnp.cumsum(nblk)
    blk_start = blk_end - nblk
    n_used = blk_end[-1:].astype(I32)
    blk_ids = jnp.arange(N_BLOCKS, dtype=I32)

    def expert_of(b):
        b = jnp.minimum(b, n_used[0] - 1)
        return jnp.minimum(jnp.sum(blk_end[None, :] <= b[:, None], axis=1), N_EXPERTS - 1).astype(I32)

    block_expert = expert_of(blk_ids)
    in_use = blk_ids < n_used[0]
    is_first = jnp.logical_or(blk_ids == 0, block_expert != expert_of(blk_ids - 1))
    block_first = jnp.logical_and(in_use, is_first).astype(I32)
    next_start = blk_end[block_expert]
    block_next = jnp.where(next_start < n_used[0], expert_of(next_start), -1).astype(I32)

    nch_t = nch.T
    seg_first = (blk_start[:, None] * BLOCK_CHUNKS + jnp.cumsum(nch_t, axis=1) - nch_t).reshape(-1)
    seg_count = nch_t.reshape(-1)
    seg_row = (jnp.arange(N_TILES, dtype=I32)[None, :] * LOCAL_ROWS + seg_start.T).reshape(-1)
    ent = jnp.arange(N_BLOCKS * BLOCK_CHUNKS, dtype=I32)
    d = ent[:, None] - seg_first[None, :]
    inside = jnp.logical_and(d >= 0, d < seg_count[None, :])
    row = jnp.sum(jnp.where(inside, seg_row[None, :] + d * CHUNK_ROWS, 0), axis=1)
    real = jnp.any(inside, axis=1)
    dump = DUMP_BASE + (((ent // BLOCK_CHUNKS) % 2) * BLOCK_CHUNKS + ent % BLOCK_CHUNKS) * CHUNK_ROWS
    chunk_src = jnp.where(real, row, ZERO_CHUNK_ROW).astype(I32)
    chunk_dst = jnp.where(real, row, dump).astype(I32)
    return chunk_src, chunk_dst, block_expert, block_first, block_next, n_used


def kernel(x_prompt, x_sample, state_pool, state_mlstm_C, state_mlstm_n, state_mlstm_m, p_prompt, p_sample, w_in, b_i, b_f, w_pool, pool_scale, mh_g, w_out, ln1_g, ln1_b, w_router, b_router, w_mlp1, b_mlp1, w_mlp2, b_mlp2, ln2_g, ln2_b, w_ple, w_ple_gate):
    n_main = POOL_WIDTH + 4 * MLSTM_WIDTH
    w_in_b = w_in[0, :, 0:n_main].astype(BF16)
    w_g_b = jnp.pad(w_in[0, :, n_main:], ((0, 0), (0, LANES - 2 * HEADS))).astype(BF16)
    gbias = jnp.pad(jnp.concatenate([b_i[0], b_f[0]]), (0, LANES - 2 * HEADS)).reshape(1, LANES)
    w_pool_b = w_pool[0].astype(BF16)
    pscale = pool_scale[0].reshape(1, POOL_WIDTH)
    mhg = mh_g[0].reshape(1, MLSTM_WIDTH)
    w_out_b = w_out[0].astype(BF16)
    ln1g = ln1_g[0].reshape(1, D_MODEL)
    ln1b = ln1_b[0].reshape(1, D_MODEL)
    ln2g = ln2_g[0].reshape(1, D_MODEL)
    ln2b = ln2_b[0].reshape(1, D_MODEL)
    w_router_t = w_router[0].T
    b_router_col = jnp.broadcast_to(b_router[0].reshape(N_EXPERTS, 1), (N_EXPERTS, LANES))
    b1 = b_mlp1[0]
    b1p = jnp.concatenate([b1[:, 0::2], b1[:, 1::2]], axis=-1).reshape(N_EXPERTS, 1, 2 * D_FF)
    b2 = b_mlp2[0].reshape(N_EXPERTS, 1, D_MODEL)
    w_pg_b = w_ple_gate[0].astype(BF16)
    w_ple_b = w_ple[0].astype(BF16)

    x1p, pool_p, c_p, n_p, m_p = _prompt_mixer(
        x_prompt, w_in_b, w_g_b, gbias, w_pool_b, pscale, mhg, w_out_b, ln1g, ln1b)
    x1s, pool_s, c_s, n_s, m_s = _sample_mixer(
        x_sample.reshape(DEC_BATCH, D_MODEL),
        state_pool[0].reshape(DEC_BATCH, POOL_HIST * POOL_WIDTH),
        state_mlstm_C[0], state_mlstm_n[0].reshape(DEC_BATCH, MLSTM_WIDTH), state_mlstm_m[0],
        w_in_b, w_g_b, gbias, w_pool_b, pscale, mhg, w_out_b, ln1g, ln1b)

    slots, gates, nch, sorted_rows = _route(x1p, x1s, w_router_t, b_router_col)
    tables = _block_tables(nch[:, :, 0].astype(I32))
    expert_out = _experts(*tables, sorted_rows, w_mlp1[0], b1p, w_mlp2[0], b2)

    pp = p_prompt[0].reshape(N_PROMPT, PLE_DIM)
    ps = jnp.pad(p_sample[0].reshape(DEC_BATCH, PLE_DIM), ((0, TOK_TILE - DEC_BATCH), (0, 0)))
    yp, ys = _combine(slots, gates, x1p, x1s, pp, ps, expert_out, ln2g, ln2b, w_pg_b, w_ple_b)

    return (
        yp.reshape(BATCH, SEQ, D_MODEL),
        ys.reshape(DEC_BATCH, 1, D_MODEL),
        pool_p.reshape(1, BATCH, POOL_HIST, POOL_WIDTH),
        c_p.reshape(1, BATCH, HEADS, HEAD_DIM, HEAD_DIM),
        n_p.reshape(1, BATCH, HEADS, HEAD_DIM),
        m_p[:, 0:HEADS, 0].reshape(1, BATCH, HEADS),
        pool_s.reshape(1, DEC_BATCH, POOL_HIST, POOL_WIDTH),
        c_s.reshape(1, DEC_BATCH, HEADS, HEAD_DIM, HEAD_DIM),
        n_s.reshape(1, DEC_BATCH, HEADS, HEAD_DIM),
        m_s[:, HEADS:2 * HEADS].reshape(1, DEC_BATCH, HEADS),
    )
```

```python
import jax
import jax.numpy as jnp
from jax import lax
from jax.experimental import pallas as pl
from jax.experimental.pallas import tpu as pltpu

F32 = jnp.float32
BF16 = jnp.bfloat16
I32 = jnp.int32

D_MODEL = 1024
BATCH = 8
SEQ = 2048
DEC_BATCH = 128
PAST_LEN = 16384
POOL_WIDTH = 512
POOL_GROUPS = 4
POOL_GROUP_DIM = 128
POOL_WINDOWS = (2, 4, 8, 16)
POOL_HIST = 15
MLSTM_WIDTH = 512
HEADS = 4
HEAD_DIM = 128
CHUNK = 128
N_EXPERTS = 32
TOP_K = 4
D_FF = 1024
SWIGLU_ALPHA = 1.702
SWIGLU_LIMIT = 7.0
PLE_DIM = 256
DN_ALPHA = 2.0 ** 0.25
LN_EPS = 1e-5

LANES = 128
SUBLANES = 8
BF16_ROWS = 16
MXU_COLS = 256
VMEM_LIMIT = 56 * 1024 * 1024

MIX_TILE = 256
MIX_SEQS = 2
HIST_PAD = 16
TOK_TILE = 512
N_PROMPT = BATCH * SEQ
N_PROMPT_TILES = N_PROMPT // TOK_TILE
N_TILES = N_PROMPT_TILES + 1
SAMPLE_BT = 16

CHUNK_ROWS = BF16_ROWS
LOCAL_ROWS = TOK_TILE * TOP_K + N_EXPERTS * CHUNK_ROWS
GROUPS = LOCAL_ROWS // TOK_TILE
ROW_BLOCK = 256
BLOCK_CHUNKS = ROW_BLOCK // CHUNK_ROWS
MAX_CHUNKS = N_TILES * (TOK_TILE * TOP_K // CHUNK_ROWS + N_EXPERTS)
N_BLOCKS = -(-MAX_CHUNKS // BLOCK_CHUNKS) + N_EXPERTS
ZERO_CHUNK_ROW = LOCAL_ROWS - CHUNK_ROWS
FREE_ROWS = LOCAL_ROWS - TOK_TILE * TOP_K
DUMP_BASE = N_TILES * LOCAL_ROWS
DUMP_ROWS = 2 * ROW_BLOCK
assert DUMP_ROWS <= FREE_ROWS
WEIGHT_PIECES = 8
PIECES_PER_BLOCK = 2
GATHER_DEPTH = 3


def _dot(a, b):
    return jnp.dot(a, b, preferred_element_type=F32)


def _dot_nt(a, b):
    return lax.dot_general(a, b, (((1,), (1,)), ((), ())), preferred_element_type=F32)


def _dot_tn(a, b):
    return lax.dot_general(a, b, (((0,), (0,)), ((), ())), preferred_element_type=F32)


def _split3(a):
    a0 = a.astype(BF16)
    r1 = a - a0.astype(F32)
    a1 = r1.astype(BF16)
    r2 = r1 - a1.astype(F32)
    return a0, a1, r2.astype(BF16)


def _log_sigmoid(x):
    return jnp.minimum(x, 0.0) - jnp.log1p(jnp.exp(-jnp.abs(x)))


def _layer_norm(x, g, b):
    mu = jnp.mean(x, axis=-1, keepdims=True)
    xc = x - mu
    var = jnp.mean(xc * xc, axis=-1, keepdims=True)
    return xc * lax.rsqrt(var + LN_EPS) * g + b


def _gate_values(g, gbias):
    lane = lax.broadcasted_iota(I32, g.shape, 1)
    z = g + gbias
    return jnp.where(lane < HEADS, z, _log_sigmoid(z))


def _head_out(hh, o_h, gain):
    mu = jnp.mean(hh, axis=-1, keepdims=True)
    hc = hh - mu
    var = jnp.mean(hc * hc, axis=-1, keepdims=True)
    return jax.nn.sigmoid(o_h) * (hc * lax.rsqrt(var + LN_EPS) * gain)


def _prompt_mixer_kernel(x_ref, xn_ref, win_ref, wg_ref, gb_ref, wpool_ref, pscale_ref, mhg_ref, wout_ref,
                         ln1g_ref, ln1b_ref,
                         x1_ref, pool_ref, c_out_ref, n_out_ref, m_out_ref,
                         ubuf, mixbuf, pbuf, gbuf, c_s, n_s, m_s):
    ti = pl.program_id(1)
    nt = pl.num_programs(1)
    TT = MIX_TILE
    S = MIX_SEQS

    @pl.when(ti == 0)
    def _():
        for s in range(S):
            ubuf[s, 0:HIST_PAD, :] = jnp.zeros((HIST_PAD, POOL_WIDTH), F32)
        c_s[...] = jnp.zeros_like(c_s)
        n_s[...] = jnp.zeros_like(n_s)
        m_s[...] = jnp.zeros_like(m_s)

    x = jnp.concatenate([x_ref[s, 0] for s in range(S)], axis=0)
    cur = lax.rem(ti, 2)
    nxt = 1 - cur
    n_main = POOL_WIDTH + 4 * MLSTM_WIDTH

    @pl.when(ti == 0)
    def _():
        xb0 = x.astype(BF16)
        pbuf[0] = _dot(xb0, win_ref[...])
        gbuf[0] = _dot(xb0, wg_ref[...])

    xnb = jnp.concatenate([xn_ref[s, 0] for s in range(S)], axis=0).astype(BF16)

    def slab(j):
        def run():
            pbuf[nxt, :, j * MXU_COLS:(j + 1) * MXU_COLS] = _dot(xnb, win_ref[:, j * MXU_COLS:(j + 1) * MXU_COLS])
        return run

    def gate_slab():
        gbuf[nxt] = _dot(xnb, wg_ref[...])

    pending = [slab(j) for j in range(n_main // MXU_COLS)] + [gate_slab]

    def ahead(n=1):
        for _ in range(n):
            if pending:
                pending.pop(0)()

    proj = pbuf.at[cur]
    g = gbuf[cur]

    L = CHUNK
    row = lax.broadcasted_iota(I32, (L, L), 0)
    col = lax.broadcasted_iota(I32, (L, L), 1)
    causal = row >= col
    tril = jnp.where(causal, 1.0, 0.0).astype(BF16)
    pos = ti * TT + lax.broadcasted_iota(I32, (TT, 1), 0)

    for s in range(S):
        base = s * TT
        u = proj[base:base + TT, 0:POOL_WIDTH]

        ubuf[s, HIST_PAD:HIST_PAD + TT, :] = u
        for gi, w in enumerate(POOL_WINDOWS):
            sl = slice(gi * POOL_GROUP_DIM, (gi + 1) * POOL_GROUP_DIM)
            ug = u[:, sl]
            acc = ug
            for i in range(1, w):
                acc = acc + ubuf[s, HIST_PAD - i:HIST_PAD - i + TT, sl]
            cnt = jnp.minimum(pos + 1, w).astype(F32)
            z = acc / cnt - ug
            mixbuf[base:base + TT, sl] = _dot(z.astype(BF16), wpool_ref[gi]) * pscale_ref[:, sl]

        @pl.when(ti == nt - 1)
        def _():
            pool_ref[s, 0] = ubuf[s, TT + 1:TT + HIST_PAD, :]

        ubuf[s, 0:HIST_PAD, :] = ubuf[s, TT:TT + HIST_PAD, :]

    NC = TT // L
    chains = [(s, h) for s in range(S) for h in range(HEADS)]
    units = [(s, c, h) for c in range(NC) for s in range(S) for h in range(HEADS)]
    U = range(len(units))

    def rows(s, c):
        return slice(s * TT + c * L, s * TT + (c + 1) * L)

    def head_cols(part, h):
        return slice(part * POOL_WIDTH + h * HEAD_DIM, part * POOL_WIDTH + (h + 1) * HEAD_DIM)

    gate, cum, gate_t, cum_t = {}, {}, {}, {}
    for c in range(NC):
        for s in range(S):
            val = _gate_values(g[rows(s, c), :], gb_ref[...])
            v0, v1, v2 = _split3(val)
            gate[s, c] = val
            cum[s, c] = _dot(tril, v0) + _dot(tril, v1) + _dot(tril, v2)
    for key in gate:
        gate_t[key] = gate[key].T
        cum_t[key] = cum[key].T
    ahead()
    qf = [proj[rows(s, c), head_cols(1, h)] for s, c, h in units]
    kf = [proj[rows(s, c), head_cols(2, h)] * (HEAD_DIM ** -0.5) for s, c, h in units]
    vf = [proj[rows(s, c), head_cols(3, h)] for s, c, h in units]
    qb = [a.astype(BF16) for a in qf]
    kb = [a.astype(BF16) for a in kf]
    f_col = [cum[s, c][:, HEADS + h:HEADS + h + 1] for s, c, h in units]
    ahead()
    log_d = [jnp.where(causal, f_col[u] - cum_t[s, c][HEADS + h:HEADS + h + 1, :] + gate_t[s, c][h:h + 1, :],
                       -jnp.inf) for u, (s, c, h) in enumerate(units)]
    ahead()
    row_max = [jnp.max(log_d[u], axis=-1, keepdims=True) for u in U]
    ahead()
    qk_raw = [_dot_nt(qb[u], kb[u]) for u in U]

    m_prev, m_t, inter = [None] * len(units), [None] * len(units), [None] * len(units)
    m_run = {(s, h): m_s[s, h:h + 1, 0:1] for s, h in chains}
    for u, (s, c, h) in enumerate(units):
        m_prev[u] = m_run[s, h]
        inter[u] = m_prev[u] + f_col[u]
        m_t[u] = jnp.maximum(inter[u], row_max[u])
        m_run[s, h] = m_t[u][L - 1:L, :]
    m_new = [m_t[u][L - 1:L, :] for u in U]

    ahead()
    dw = [jnp.exp(log_d[u] - m_t[u]) for u in U]
    sc = [jnp.exp(inter[u] - m_t[u]) for u in U]
    ahead()
    qk = [qk_raw[u] * dw[u] for u in U]
    ahead()
    intra = [_dot(qk[u].astype(BF16), vf[u].astype(BF16)) for u in U]
    ahead()
    row_sum = [jnp.sum(qk[u], axis=-1, keepdims=True) for u in U]
    floor = [jnp.exp(-m_t[u]) for u in U]
    f_last = [f_col[u][L - 1:L, :] for u in U]
    ahead()
    wk = [jnp.exp(gate[s, c][:, h:h + 1] + f_last[u] - f_col[u] - m_new[u]) for u, (s, c, h) in enumerate(units)]
    decay = [jnp.exp(m_prev[u] + f_last[u] - m_new[u]) for u in U]
    ahead()
    upd = [_dot_tn((vf[u] * wk[u]).astype(BF16), kb[u]) for u in U]
    ahead()
    n_upd = [jnp.sum(wk[u] * kf[u], axis=0, keepdims=True) for u in U]

    c_run = {(s, h): c_s[s, h] for s, h in chains}
    n_run = {(s, h): n_s[s, h:h + 1, :] for s, h in chains}
    hh = [None] * len(units)
    for c in range(NC):
        cu = [u for u in U if units[u][1] == c]
        inter_term = {u: _dot_nt(qb[u], c_run[units[u][0], units[u][2]].astype(BF16)) for u in cu}
        n_term = {u: jnp.sum(qf[u] * n_run[units[u][0], units[u][2]], axis=-1, keepdims=True) for u in cu}
        for u in cu:
            s, _, h = units[u]
            num = intra[u] + sc[u] * inter_term[u]
            den = row_sum[u] + sc[u] * n_term[u]
            hh[u] = num / jnp.maximum(jnp.abs(den), floor[u])
            c_run[s, h] = decay[u] * c_run[s, h] + upd[u]
            n_run[s, h] = decay[u] * n_run[s, h] + n_upd[u]
    ahead(len(pending))
    for s, h in chains:
        c_s[s, h] = c_run[s, h]
        n_s[s, h:h + 1, :] = n_run[s, h]
        m_s[s, h:h + 1, :] = jnp.broadcast_to(m_run[s, h], (1, LANES))
    for u, (s, c, h) in enumerate(units):
        mixbuf[rows(s, c), head_cols(1, h)] = _head_out(
            hh[u], proj[rows(s, c), head_cols(4, h)], mhg_ref[:, h * HEAD_DIM:(h + 1) * HEAD_DIM])

    @pl.when(ti == nt - 1)
    def _():
        for s in range(S):
            c_out_ref[s, 0] = c_s[s]
            n_out_ref[s, 0] = n_s[s, 0:HEADS, :]
            m_out_ref[s, 0] = m_s[s]

    mix = _dot(mixbuf[...].astype(BF16), wout_ref[...])
    x1 = _layer_norm(DN_ALPHA * x + mix, ln1g_ref[...], ln1b_ref[...])
    for s in range(S):
        x1_ref[s] = x1[s * TT:(s + 1) * TT, :]


def _prompt_mixer(x, w_in_b, w_g_b, gbias, w_pool_b, pscale, mhg, w_out_b, ln1g, ln1b):
    nt = SEQ // MIX_TILE
    S = MIX_SEQS
    G = BATCH // S
    const2 = lambda b, t: (0, 0)
    const3 = lambda b, t: (0, 0, 0)
    outs = pl.pallas_call(
        _prompt_mixer_kernel,
        grid=(G, nt),
        in_specs=[
            pl.BlockSpec((S, 1, MIX_TILE, D_MODEL), lambda b, t: (0, b, t, 0)),
            pl.BlockSpec((S, 1, MIX_TILE, D_MODEL), lambda b, t: (0, b, jnp.minimum(t + 1, nt - 1), 0)),
            pl.BlockSpec(w_in_b.shape, const2),
            pl.BlockSpec(w_g_b.shape, const2),
            pl.BlockSpec(gbias.shape, const2),
            pl.BlockSpec(w_pool_b.shape, const3),
            pl.BlockSpec(pscale.shape, const2),
            pl.BlockSpec(mhg.shape, const2),
            pl.BlockSpec(w_out_b.shape, const2),
            pl.BlockSpec(ln1g.shape, const2),
            pl.BlockSpec(ln1b.shape, const2),
        ],
        out_specs=[
            pl.BlockSpec((S, MIX_TILE, D_MODEL), lambda b, t: (0, b * nt + t, 0)),
            pl.BlockSpec((S, 1, POOL_HIST, POOL_WIDTH), lambda b, t: (0, b, 0, 0)),
            pl.BlockSpec((S, 1, HEADS, HEAD_DIM, HEAD_DIM), lambda b, t: (0, b, 0, 0, 0)),
            pl.BlockSpec((S, 1, HEADS, HEAD_DIM), lambda b, t: (0, b, 0, 0)),
            pl.BlockSpec((S, 1, SUBLANES, LANES), lambda b, t: (0, b, 0, 0)),
        ],
        out_shape=[
            jax.ShapeDtypeStruct((S, G * SEQ, D_MODEL), F32),
            jax.ShapeDtypeStruct((S, G, POOL_HIST, POOL_WIDTH), F32),
            jax.ShapeDtypeStruct((S, G, HEADS, HEAD_DIM, HEAD_DIM), F32),
            jax.ShapeDtypeStruct((S, G, HEADS, HEAD_DIM), F32),
            jax.ShapeDtypeStruct((S, G, SUBLANES, LANES), F32),
        ],
        scratch_shapes=[
            pltpu.VMEM((S, HIST_PAD + MIX_TILE, POOL_WIDTH), F32),
            pltpu.VMEM((S * MIX_TILE, D_MODEL), F32),
            pltpu.VMEM((2, S * MIX_TILE, POOL_WIDTH + 4 * MLSTM_WIDTH), F32),
            pltpu.VMEM((2, S * MIX_TILE, LANES), F32),
            pltpu.VMEM((S, HEADS, HEAD_DIM, HEAD_DIM), F32),
            pltpu.VMEM((S, SUBLANES, HEAD_DIM), F32),
            pltpu.VMEM((S, SUBLANES, LANES), F32),
        ],
        compiler_params=pltpu.CompilerParams(
            dimension_semantics=("arbitrary", "arbitrary"), vmem_limit_bytes=VMEM_LIMIT),
        name="prompt_mixer",
    )(x.reshape(S, G, SEQ, D_MODEL), x.reshape(S, G, SEQ, D_MODEL), w_in_b, w_g_b, gbias, w_pool_b, pscale, mhg, w_out_b, ln1g, ln1b)
    x1, pool, c, n, m = outs
    return (x1.reshape(N_PROMPT, D_MODEL), pool.reshape(BATCH, POOL_HIST, POOL_WIDTH),
            c.reshape(BATCH, HEADS, HEAD_DIM, HEAD_DIM), n.reshape(BATCH, HEADS, HEAD_DIM),
            m.reshape(BATCH, SUBLANES, LANES))


def _sample_mixer_kernel(x_ref, hist_ref, c_ref, n_ref, m_ref, win_ref, wg_ref, gb_ref, wpool_ref,
                         pscale_ref, mhg_ref, wout_ref, ln1g_ref, ln1b_ref,
                         x1_ref, pool_out_ref, c_out_ref, n_out_ref, m_out_ref,
                         q_s, k_s, vw_s, v_s, o_s, mixbuf, h_s, coef_s):
    i = pl.program_id(0)
    nsteps = pl.num_programs(0)
    B = DEC_BATCH

    @pl.when(i == 0)
    def _():
        x = x_ref[...]
        xb = x.astype(BF16)
        proj = _dot(xb, win_ref[...])
        g = _dot(xb, wg_ref[...])
        u = proj[:, 0:POOL_WIDTH]
        for gi, w in enumerate(POOL_WINDOWS):
            sl = slice(gi * POOL_GROUP_DIM, (gi + 1) * POOL_GROUP_DIM)
            ug = u[:, sl]
            s = ug
            for j in range(1, w):
                r = POOL_HIST - j
                s = s + hist_ref[:, r * POOL_WIDTH + gi * POOL_GROUP_DIM:r * POOL_WIDTH + (gi + 1) * POOL_GROUP_DIM]
            cnt = float(min(PAST_LEN + 1, w))
            z = s / cnt - ug
            mixbuf[:, sl] = _dot(z.astype(BF16), wpool_ref[gi]) * pscale_ref[:, sl]
        pool_out_ref[:, 0:(POOL_HIST - 1) * POOL_WIDTH] = hist_ref[:, POOL_WIDTH:POOL_HIST * POOL_WIDTH]
        pool_out_ref[:, (POOL_HIST - 1) * POOL_WIDTH:POOL_HIST * POOL_WIDTH] = u

        val = _gate_values(g, gb_ref[...])
        lane = lax.broadcasted_iota(I32, (B, LANES), 1)
        qk_all = jnp.zeros((B, LANES), F32)
        sc_all = jnp.zeros((B, LANES), F32)
        den_all = jnp.zeros((B, LANES), F32)
        floor_all = jnp.zeros((B, LANES), F32)
        m_all = jnp.zeros((B, LANES), F32)
        for h in range(HEADS):
            hs = slice(h * HEAD_DIM, (h + 1) * HEAD_DIM)
            qf = proj[:, POOL_WIDTH + h * HEAD_DIM:POOL_WIDTH + (h + 1) * HEAD_DIM]
            kf = proj[:, 2 * POOL_WIDTH + h * HEAD_DIM:2 * POOL_WIDTH + (h + 1) * HEAD_DIM] * (HEAD_DIM ** -0.5)
            vf = proj[:, 3 * POOL_WIDTH + h * HEAD_DIM:3 * POOL_WIDTH + (h + 1) * HEAD_DIM]
            ig = val[:, h:h + 1]
            lf = val[:, HEADS + h:HEADS + h + 1]
            m0 = m_ref[:, h:h + 1]
            n0 = n_ref[:, hs]
            inter = m0 + lf
            m_t = jnp.maximum(inter, ig)
            dw = jnp.exp(ig - m_t)
            sc = jnp.exp(inter - m_t)
            qk = jnp.sum(qf * kf, axis=-1, keepdims=True) * dw
            den = qk + sc * jnp.sum(qf * n0, axis=-1, keepdims=True)
            n_out_ref[:, hs] = sc * n0 + dw * kf
            q_s[0:B, hs] = qf
            k_s[0:B, hs] = kf
            v_s[0:B, hs] = vf
            vw_s[0:B, hs] = vf * dw
            sel = lane == h
            qk_all = jnp.where(sel, qk, qk_all)
            sc_all = jnp.where(sel, sc, sc_all)
            den_all = jnp.where(sel, den, den_all)
            floor_all = jnp.where(sel, jnp.exp(-m_t), floor_all)
            m_all = jnp.where(lane == HEADS + h, m_t, m_all)
        o_s[...] = proj[:, 4 * POOL_WIDTH:5 * POOL_WIDTH]
        coef_s[0] = qk_all
        coef_s[1] = sc_all
        coef_s[2] = den_all
        coef_s[3] = floor_all
        m_out_ref[...] = m_all

    rows = pl.ds(pl.multiple_of(i * SAMPLE_BT, SAMPLE_BT), SAMPLE_BT)
    q_t, k_t, v_t, vw_t = q_s[rows, :], k_s[rows, :], v_s[rows, :], vw_s[rows, :]
    qk_t, sc_t, den_t, floor_t = coef_s[0, rows, :], coef_s[1, rows, :], coef_s[2, rows, :], coef_s[3, rows, :]
    h_rows = []
    for bl in range(SAMPLE_BT):
        heads = []
        for h in range(HEADS):
            hs = slice(h * HEAD_DIM, (h + 1) * HEAD_DIM)
            c_prev = c_ref[bl, h]
            q8 = jnp.broadcast_to(q_t[bl:bl + 1, hs], (SUBLANES, HEAD_DIM))
            cq = _dot_nt(q8.astype(BF16), c_prev.astype(BF16))[0:1, :]
            qk = qk_t[bl:bl + 1, h:h + 1]
            sc = sc_t[bl:bl + 1, h:h + 1]
            num = qk * v_t[bl:bl + 1, hs] + sc * cq
            heads.append(num / jnp.maximum(jnp.abs(den_t[bl:bl + 1, h:h + 1]), floor_t[bl:bl + 1, h:h + 1]))
            v_col = jnp.broadcast_to(vw_t[bl:bl + 1, hs], (HEAD_DIM, HEAD_DIM)).T
            c_out_ref[bl, h] = sc * c_prev + v_col * k_t[bl:bl + 1, hs]
        h_rows.append(jnp.concatenate(heads, axis=1))
    h_s[rows, :] = jnp.concatenate(h_rows, axis=0)

    @pl.when(i == nsteps - 1)
    def _():
        for h in range(HEADS):
            hs = slice(h * HEAD_DIM, (h + 1) * HEAD_DIM)
            mixbuf[:, POOL_WIDTH + h * HEAD_DIM:POOL_WIDTH + (h + 1) * HEAD_DIM] = _head_out(
                h_s[:, hs], o_s[:, hs], mhg_ref[:, hs])
        mix = _dot(mixbuf[...].astype(BF16), wout_ref[...])
        x1 = _layer_norm(DN_ALPHA * x_ref[...] + mix, ln1g_ref[...], ln1b_ref[...])
        x1_ref[0:B, :] = x1
        x1_ref[B:TOK_TILE, :] = jnp.zeros((TOK_TILE - B, D_MODEL), F32)


def _sample_mixer(x, hist2, c0, n0, m0, w_in_b, w_g_b, gbias, w_pool_b, pscale, mhg, w_out_b, ln1g, ln1b):
    B = DEC_BATCH
    steps = B // SAMPLE_BT
    full = lambda a: pl.BlockSpec(a.shape, lambda i: (0,) * a.ndim)
    c_spec = pl.BlockSpec((SAMPLE_BT, HEADS, HEAD_DIM, HEAD_DIM), lambda i: (i, 0, 0, 0))
    return pl.pallas_call(
        _sample_mixer_kernel,
        grid=(steps,),
        in_specs=[full(x), full(hist2), c_spec, full(n0), full(m0), full(w_in_b), full(w_g_b), full(gbias),
                  full(w_pool_b), full(pscale), full(mhg), full(w_out_b), full(ln1g), full(ln1b)],
        out_specs=[
            pl.BlockSpec((TOK_TILE, D_MODEL), lambda i: (0, 0)),
            pl.BlockSpec((B, POOL_HIST * POOL_WIDTH), lambda i: (0, 0)),
            c_spec,
            pl.BlockSpec((B, MLSTM_WIDTH), lambda i: (0, 0)),
            pl.BlockSpec((B, LANES), lambda i: (0, 0)),
        ],
        out_shape=[
            jax.ShapeDtypeStruct((TOK_TILE, D_MODEL), F32),
            jax.ShapeDtypeStruct((B, POOL_HIST * POOL_WIDTH), F32),
            jax.ShapeDtypeStruct((B, HEADS, HEAD_DIM, HEAD_DIM), F32),
            jax.ShapeDtypeStruct((B, MLSTM_WIDTH), F32),
            jax.ShapeDtypeStruct((B, LANES), F32),
        ],
        scratch_shapes=[
            pltpu.VMEM((B, MLSTM_WIDTH), F32),
            pltpu.VMEM((B, MLSTM_WIDTH), F32),
            pltpu.VMEM((B, MLSTM_WIDTH), F32),
            pltpu.VMEM((B, MLSTM_WIDTH), F32),
            pltpu.VMEM((B, MLSTM_WIDTH), F32),
            pltpu.VMEM((B, D_MODEL), F32),
            pltpu.VMEM((B, MLSTM_WIDTH), F32),
            pltpu.VMEM((4, B, LANES), F32),
        ],
        compiler_params=pltpu.CompilerParams(
            dimension_semantics=("arbitrary",), vmem_limit_bytes=VMEM_LIMIT),
        name="sample_mixer",
    )(x, hist2, c0, n0, m0, w_in_b, w_g_b, gbias, w_pool_b, pscale, mhg, w_out_b, ln1g, ln1b)


def _pick_tile(i, prompt_ref, sample_ref):
    return jnp.where(i < N_PROMPT_TILES, prompt_ref[...], sample_ref[...])


def _placement(slot_rows, group):
    r = group * TOK_TILE + lax.broadcasted_iota(I32, (TOK_TILE, TOK_TILE), 0)
    return [r == s for s in slot_rows]


def _route_kernel(xp_ref, xs_ref, wrt_ref, br_ref, slot_ref, gate_ref, nch_ref, sorted_ref):
    i = pl.program_id(0)
    T = TOK_TILE
    E = N_EXPERTS

    x = _pick_tile(i, xp_ref, xs_ref)
    xh = x.astype(BF16)
    xl = (x - xh.astype(F32)).astype(BF16)
    w = wrt_ref[...]
    wh = w.astype(BF16)
    wl = (w - wh.astype(F32)).astype(BF16)
    logits = _dot_nt(wh, xh) + (_dot_nt(wh, xl) + _dot_nt(wl, xh)) + br_ref[:, 0:1]

    erow = lax.broadcasted_iota(I32, (E, T), 0).astype(F32)
    work = logits
    vals, sels = [], []
    for _ in range(TOP_K):
        mx = jnp.max(work, axis=0, keepdims=True)
        idx = jnp.min(jnp.where(work == mx, erow, float(E)), axis=0, keepdims=True)
        sel = erow == idx
        work = jnp.where(sel, -jnp.inf, work)
        vals.append(mx)
        sels.append(sel)
    chosen = jnp.logical_or(jnp.logical_or(sels[0], sels[1]), jnp.logical_or(sels[2], sels[3]))
    es = [jnp.exp(v - vals[0]) for v in vals]
    tot = es[0] + es[1] + es[2] + es[3]

    onehot = jnp.where(chosen, 1.0, 0.0)
    trow = lax.broadcasted_iota(I32, (T, T), 0)
    tcol = lax.broadcasted_iota(I32, (T, T), 1)
    before = jnp.where(trow < tcol, 1.0, 0.0).astype(BF16)
    rank = _dot(onehot.astype(BF16), before)
    cnt = jnp.sum(onehot, axis=1, keepdims=True)
    nch = jnp.floor((cnt + (CHUNK_ROWS - 1)) * (1.0 / CHUNK_ROWS))
    lower = jnp.where(lax.broadcasted_iota(I32, (E, E), 0) > lax.broadcasted_iota(I32, (E, E), 1), 1.0, 0.0)
    nch_b = jnp.broadcast_to(nch, (E, LANES))
    seg_start = _dot(lower.astype(BF16), nch_b.astype(BF16))[:, 0:1] * CHUNK_ROWS
    base = seg_start + rank

    r8 = lax.broadcasted_iota(I32, (SUBLANES, T), 0)
    s_out = jnp.zeros((SUBLANES, T), I32)
    g_out = jnp.zeros((SUBLANES, T), F32)
    slot_rows = []
    for j in range(TOP_K):
        slot_j = jnp.sum(jnp.where(sels[j], base, 0.0), axis=0, keepdims=True).astype(I32)
        slot_rows.append(slot_j)
        s_out = jnp.where(r8 == j, slot_j, s_out)
        g_out = jnp.where(r8 == j, es[j] / tot, g_out)
    slot_ref[0] = s_out
    gate_ref[0] = g_out
    nch_ref[0] = nch_b

    for grp in range(GROUPS):
        m = _placement(slot_rows, grp)
        hit = jnp.logical_or(jnp.logical_or(m[0], m[1]), jnp.logical_or(m[2], m[3]))
        place = jnp.where(hit, 1.0, 0.0).astype(BF16)
        sorted_ref[grp * T:(grp + 1) * T, :] = _dot(place, xh).astype(BF16)


def _route(x1p, x1s, w_router_t, b_router_col):
    tile_spec = pl.BlockSpec((1, SUBLANES, TOK_TILE), lambda i: (i, 0, 0))
    return pl.pallas_call(
        _route_kernel,
        grid=(N_TILES,),
        in_specs=[
            pl.BlockSpec((TOK_TILE, D_MODEL), lambda i: (jnp.minimum(i, N_PROMPT_TILES - 1), 0)),
            pl.BlockSpec((TOK_TILE, D_MODEL), lambda i: (0, 0)),
            pl.BlockSpec(w_router_t.shape, lambda i: (0, 0)),
            pl.BlockSpec(b_router_col.shape, lambda i: (0, 0)),
        ],
        out_specs=[tile_spec, tile_spec,
                   pl.BlockSpec((1, N_EXPERTS, LANES), lambda i: (i, 0, 0)),
                   pl.BlockSpec((LOCAL_ROWS, D_MODEL), lambda i: (i, 0))],
        out_shape=[
            jax.ShapeDtypeStruct((N_TILES, SUBLANES, TOK_TILE), I32),
            jax.ShapeDtypeStruct((N_TILES, SUBLANES, TOK_TILE), F32),
            jax.ShapeDtypeStruct((N_TILES, N_EXPERTS, LANES), F32),
            jax.ShapeDtypeStruct((N_TILES * LOCAL_ROWS, D_MODEL), BF16),
        ],
        compiler_params=pltpu.CompilerParams(
            dimension_semantics=("arbitrary",), vmem_limit_bytes=VMEM_LIMIT),
        name="route",
    )(x1p, x1s, w_router_t, b_router_col)


def _expert_kernel(src_ref, dst_ref, bexp_ref, first_ref, next_ref, nused_ref,
                   sorted_hbm, w1_hbm, b1_ref, w2_hbm, b2_ref, out_hbm,
                   w1_stage, w2_stage, w1_b, w2_b, xbuf, obuf, zbuf, wsem, gsem, ssem, zsem):
    nused = nused_ref[0]

    def fetch_piece(e, p):
        r1 = pl.ds(pl.multiple_of(p * (D_MODEL // WEIGHT_PIECES), SUBLANES), D_MODEL // WEIGHT_PIECES)
        r2 = pl.ds(pl.multiple_of(p * (D_FF // WEIGHT_PIECES), SUBLANES), D_FF // WEIGHT_PIECES)
        return (pltpu.make_async_copy(w1_hbm.at[e, r1, :], w1_stage.at[r1, :], wsem.at[0]),
                pltpu.make_async_copy(w2_hbm.at[e, r2, :], w2_stage.at[r2, :], wsem.at[1]))

    def start_pieces(e, lo, hi):
        def body(p, c):
            for cp in fetch_piece(e, p):
                cp.start()
            return c
        lax.fori_loop(lo, hi, body, 0)

    def gather(b, q):
        slot = lax.rem(b, GATHER_DEPTH)
        row = pl.multiple_of(src_ref[b * BLOCK_CHUNKS + q], CHUNK_ROWS)
        return pltpu.make_async_copy(sorted_hbm.at[pl.ds(row, CHUNK_ROWS), :],
                                     xbuf.at[slot, pl.ds(q * CHUNK_ROWS, CHUNK_ROWS), :], gsem.at[slot])

    def scatter(b, q):
        slot = lax.rem(b, 2)
        row = pl.multiple_of(dst_ref[b * BLOCK_CHUNKS + q], CHUNK_ROWS)
        return pltpu.make_async_copy(obuf.at[slot, pl.ds(q * CHUNK_ROWS, CHUNK_ROWS), :],
                                     out_hbm.at[pl.ds(row, CHUNK_ROWS), :], ssem.at[slot])

    def zero_rows(start, n_rows):
        start = pl.multiple_of(start, CHUNK_ROWS)
        return pltpu.make_async_copy(zbuf.at[pl.ds(0, n_rows), :], out_hbm.at[pl.ds(start, n_rows), :], zsem)

    def zero_tail(k):
        return zero_rows(k * LOCAL_ROWS + TOK_TILE * TOP_K, FREE_ROWS)

    zbuf[...] = jnp.zeros_like(zbuf)
    lax.fori_loop(0, N_TILES, lambda k, c: (zero_tail(k).start(), c)[1], 0)
    zero_rows(DUMP_BASE, DUMP_ROWS).start()
    for ahead in range(GATHER_DEPTH - 1):
        @pl.when(ahead < nused)
        def _():
            for q in range(BLOCK_CHUNKS):
                gather(ahead, q).start()
    lax.fori_loop(0, N_TILES, lambda k, c: (zero_tail(k).wait(), c)[1], 0)
    zero_rows(DUMP_BASE, DUMP_ROWS).wait()

    half = MXU_COLS // 2
    k_io = lax.broadcasted_iota(I32, (MXU_COLS, MXU_COLS), 0)
    j_io = lax.broadcasted_iota(I32, (MXU_COLS, MXU_COLS), 1)
    src_col = jnp.where(j_io < half, 2 * j_io, 2 * (j_io - half) + 1)
    perm = jnp.where(k_io == src_col, 1.0, 0.0).astype(BF16)

    def block(i, fetched):
        e = bexp_ref[i]
        slot = lax.rem(i, 2)
        is_first = first_ref[i] == 1

        @pl.when(is_first)
        def _():
            start_pieces(e, fetched, WEIGHT_PIECES)

            def wait_piece(p, c):
                for cp in fetch_piece(e, p):
                    cp.wait()
                return c
            lax.fori_loop(0, WEIGHT_PIECES, wait_piece, 0)
            for c in range(2 * D_FF // MXU_COLS):
                blk = w1_stage[:, c * MXU_COLS:(c + 1) * MXU_COLS].astype(BF16)
                sep = _dot(blk, perm).astype(BF16)
                w1_b[:, c * half:(c + 1) * half] = sep[:, 0:half]
                w1_b[:, D_FF + c * half:D_FF + (c + 1) * half] = sep[:, half:MXU_COLS]
            w2_b[...] = w2_stage[...].astype(BF16)

        fetched = jnp.where(is_first, 0, fetched)

        for q in range(BLOCK_CHUNKS):
            gather(i, q).wait()

        @pl.when(i + GATHER_DEPTH - 1 < nused)
        def _():
            for q in range(BLOCK_CHUNKS):
                gather(i + GATHER_DEPTH - 1, q).start()

        @pl.when(i >= 2)
        def _():
            for q in range(BLOCK_CHUNKS):
                scatter(i - 2, q).wait()

        h = _dot(xbuf[lax.rem(i, GATHER_DEPTH)], w1_b[...]) + b1_ref[e]
        glu = jnp.minimum(h[:, 0:D_FF], SWIGLU_LIMIT)
        lin = jnp.clip(h[:, D_FF:2 * D_FF], -SWIGLU_LIMIT, SWIGLU_LIMIT)
        a = glu * jax.nn.sigmoid(SWIGLU_ALPHA * glu) * (lin + 1.0)
        obuf[slot] = _dot(a.astype(BF16), w2_b[...]) + b2_ref[e]
        for q in range(BLOCK_CHUNKS):
            scatter(i, q).start()

        more = jnp.where(next_ref[i] >= 0, jnp.minimum(fetched + PIECES_PER_BLOCK, WEIGHT_PIECES), fetched)
        start_pieces(next_ref[i], fetched, more)
        return more

    lax.fori_loop(0, nused, block, jnp.int32(0))

    @pl.when(nused >= 2)
    def _():
        for q in range(BLOCK_CHUNKS):
            scatter(nused - 2, q).wait()
    for q in range(BLOCK_CHUNKS):
        scatter(nused - 1, q).wait()


def _experts(chunk_src, chunk_dst, block_expert, block_first, block_next, n_used, sorted_rows, w1, b1p, w2, b2):
    whole3 = lambda i, *_: (0, 0, 0)
    grid_spec = pltpu.PrefetchScalarGridSpec(
        num_scalar_prefetch=6,
        grid=(1,),
        in_specs=[
            pl.BlockSpec(memory_space=pl.ANY),
            pl.BlockSpec(memory_space=pl.ANY),
            pl.BlockSpec(b1p.shape, whole3),
            pl.BlockSpec(memory_space=pl.ANY),
            pl.BlockSpec(b2.shape, whole3),
        ],
        out_specs=pl.BlockSpec(memory_space=pl.ANY),
        scratch_shapes=[
            pltpu.VMEM((D_MODEL, 2 * D_FF), F32),
            pltpu.VMEM((D_FF, D_MODEL), F32),
            pltpu.VMEM((D_MODEL, 2 * D_FF), BF16),
            pltpu.VMEM((D_FF, D_MODEL), BF16),
            pltpu.VMEM((GATHER_DEPTH, ROW_BLOCK, D_MODEL), BF16),
            pltpu.VMEM((2, ROW_BLOCK, D_MODEL), F32),
            pltpu.VMEM((FREE_ROWS, D_MODEL), F32),
            pltpu.SemaphoreType.DMA((2,)),
            pltpu.SemaphoreType.DMA((GATHER_DEPTH,)),
            pltpu.SemaphoreType.DMA((2,)),
            pltpu.SemaphoreType.DMA(()),
        ],
    )
    return pl.pallas_call(
        _expert_kernel,
        grid_spec=grid_spec,
        out_shape=jax.ShapeDtypeStruct((DUMP_BASE + DUMP_ROWS, D_MODEL), F32),
        compiler_params=pltpu.CompilerParams(
            dimension_semantics=("arbitrary",), vmem_limit_bytes=VMEM_LIMIT),
        name="experts",
    )(chunk_src, chunk_dst, block_expert, block_first, block_next, n_used, sorted_rows, w1, b1p, w2, b2)


def _combine_kernel(slot_ref, gate_ref, xp_ref, xs_ref, pp_ref, ps_ref, eo_ref, ln2g_ref, ln2b_ref,
                    wpg_ref, wple_ref, yp_ref, ys_ref):
    i = pl.program_id(0)
    T = TOK_TILE

    slot_rows = [slot_ref[0, j:j + 1, :] for j in range(TOP_K)]
    gate_rows = [gate_ref[0, j:j + 1, :] for j in range(TOP_K)]
    pad = jnp.zeros((LANES - SUBLANES, T), F32)
    slots_t = jnp.concatenate([slot_ref[0].astype(F32), pad], axis=0).T
    slot_cols = [slots_t[:, j:j + 1].astype(I32) for j in range(TOP_K)]

    ff = jnp.zeros((T, D_MODEL), F32)
    for grp in range(GROUPS):
        m = _placement(slot_rows, grp)
        weighted = jnp.where(m[0], gate_rows[0], jnp.where(m[1], gate_rows[1], jnp.where(
            m[2], gate_rows[2], jnp.where(m[3], gate_rows[3], 0.0))))
        g_col = jnp.sum(weighted, axis=1, keepdims=True)
        z = (eo_ref[grp * T:(grp + 1) * T, :] * g_col).astype(BF16)
        r = grp * T + lax.broadcasted_iota(I32, (T, T), 1)
        hit = jnp.logical_or(jnp.logical_or(r == slot_cols[0], r == slot_cols[1]),
                             jnp.logical_or(r == slot_cols[2], r == slot_cols[3]))
        ff = ff + _dot(jnp.where(hit, 1.0, 0.0).astype(BF16), z)

    x1 = _pick_tile(i, xp_ref, xs_ref)
    x2 = _layer_norm(DN_ALPHA * x1 + ff, ln2g_ref[...], ln2b_ref[...])
    p = _pick_tile(i, pp_ref, ps_ref)
    y = x2 + jax.nn.sigmoid(_dot(x2.astype(BF16), wpg_ref[...])) * _dot(p.astype(BF16), wple_ref[...])

    @pl.when(i < N_PROMPT_TILES)
    def _():
        yp_ref[...] = y

    @pl.when(i == N_PROMPT_TILES)
    def _():
        ys_ref[...] = y[0:DEC_BATCH, :]


def _combine(slots, gates, x1p, x1s, pp, ps, expert_out, ln2g, ln2b, w_pg_b, w_ple_b):
    tile_idx = lambda i: (jnp.minimum(i, N_PROMPT_TILES - 1), 0)
    const2 = lambda i: (0, 0)
    return pl.pallas_call(
        _combine_kernel,
        grid=(N_TILES,),
        in_specs=[
            pl.BlockSpec((1, SUBLANES, TOK_TILE), lambda i: (i, 0, 0)),
            pl.BlockSpec((1, SUBLANES, TOK_TILE), lambda i: (i, 0, 0)),
            pl.BlockSpec((TOK_TILE, D_MODEL), tile_idx),
            pl.BlockSpec((TOK_TILE, D_MODEL), const2),
            pl.BlockSpec((TOK_TILE, PLE_DIM), tile_idx),
            pl.BlockSpec((TOK_TILE, PLE_DIM), const2),
            pl.BlockSpec((LOCAL_ROWS, D_MODEL), lambda i: (i, 0)),
            pl.BlockSpec(ln2g.shape, const2),
            pl.BlockSpec(ln2b.shape, const2),
            pl.BlockSpec(w_pg_b.shape, const2),
            pl.BlockSpec(w_ple_b.shape, const2),
        ],
        out_specs=[
            pl.BlockSpec((TOK_TILE, D_MODEL), tile_idx),
            pl.BlockSpec((DEC_BATCH, D_MODEL), const2),
        ],
        out_shape=[
            jax.ShapeDtypeStruct((N_PROMPT, D_MODEL), F32),
            jax.ShapeDtypeStruct((DEC_BATCH, D_MODEL), F32),
        ],
        compiler_params=pltpu.CompilerParams(
            dimension_semantics=("arbitrary",), vmem_limit_bytes=VMEM_LIMIT),
        name="combine",
    )(slots, gates, x1p, x1s, pp, ps, expert_out, ln2g, ln2b, w_pg_b, w_ple_b)


def _block_tables(nch):
    seg_start = (jnp.cumsum(nch, axis=1) - nch) * CHUNK_ROWS
    tot = jnp.sum(nch, axis=0)
    nblk = (tot + BLOCK_CHUNKS - 1) // BLOCK_CHUNKS
    blk_end = jnp.cumsum(nblk)
    blk_start = blk_end - nblk
    n_used = blk_end[-1:].astype(I32)
    blk_ids = jnp.arange(N_BLOCKS, dtype=I32)

    def expert_of(b):
        b = jnp.minimum(b, n_used[0] - 1)
        return jnp.minimum(jnp.sum(blk_end[None, :] <= b[:, None], axis=1), N_EXPERTS - 1).astype(I32)

    block_expert = expert_of(blk_ids)
    in_use = blk_ids < n_used[0]
    is_first = jnp.logical_or(blk_ids == 0, block_expert != expert_of(blk_ids - 1))
    block_first = jnp.logical_and(in_use, is_first).astype(I32)
    next_start = blk_end[block_expert]
    block_next = jnp.where(next_start < n_used[0], expert_of(next_start), -1).astype(I32)

    nch_t = nch.T
    seg_first = (blk_start[:, None] * BLOCK_CHUNKS + jnp.cumsum(nch_t, axis=1) - nch_t).reshape(-1)
    seg_count = nch_t.reshape(-1)
    seg_row = (jnp.arange(N_TILES, dtype=I32)[None, :] * LOCAL_ROWS + seg_start.T).reshape(-1)
    ent = jnp.arange(N_BLOCKS * BLOCK_CHUNKS, dtype=I32)
    d = ent[:, None] - seg_first[None, :]
    inside = jnp.logical_and(d >= 0, d < seg_count[None, :])
    row = jnp.sum(jnp.where(inside, seg_row[None, :] + d * CHUNK_ROWS, 0), axis=1)
    real = jnp.any(inside, axis=1)
    dump = DUMP_BASE + (((ent // BLOCK_CHUNKS) % 2) * BLOCK_CHUNKS + ent % BLOCK_CHUNKS) * CHUNK_ROWS
    chunk_src = jnp.where(real, row, ZERO_CHUNK_ROW).astype(I32)
    chunk_dst = jnp.where(real, row, dump).astype(I32)
    return chunk_src, chunk_dst, block_expert, block_first, block_next, n_used


def kernel(x_prompt, x_sample, state_pool, state_mlstm_C, state_mlstm_n, state_mlstm_m, p_prompt, p_sample, w_in, b_i, b_f, w_pool, pool_scale, mh_g, w_out, ln1_g, ln1_b, w_router, b_router, w_mlp1, b_mlp1, w_mlp2, b_mlp2, ln2_g, ln2_b, w_ple, w_ple_gate):
    n_main = POOL_WIDTH + 4 * MLSTM_WIDTH
    w_in_b = w_in[0, :, 0:n_main].astype(BF16)
    w_g_b = jnp.pad(w_in[0, :, n_main:], ((0, 0), (0, LANES - 2 * HEADS))).astype(BF16)
    gbias = jnp.pad(jnp.concatenate([b_i[0], b_f[0]]), (0, LANES - 2 * HEADS)).reshape(1, LANES)
    w_pool_b = w_pool[0].astype(BF16)
    pscale = pool_scale[0].reshape(1, POOL_WIDTH)
    mhg = mh_g[0].reshape(1, MLSTM_WIDTH)
    w_out_b = w_out[0].astype(BF16)
    ln1g = ln1_g[0].reshape(1, D_MODEL)
    ln1b = ln1_b[0].reshape(1, D_MODEL)
    ln2g = ln2_g[0].reshape(1, D_MODEL)
    ln2b = ln2_b[0].reshape(1, D_MODEL)
    w_router_t = w_router[0].T
    b_router_col = jnp.broadcast_to(b_router[0].reshape(N_EXPERTS, 1), (N_EXPERTS, LANES))
    b1 = b_mlp1[0]
    b1p = jnp.concatenate([b1[:, 0::2], b1[:, 1::2]], axis=-1).reshape(N_EXPERTS, 1, 2 * D_FF)
    b2 = b_mlp2[0].reshape(N_EXPERTS, 1, D_MODEL)
    w_pg_b = w_ple_gate[0].astype(BF16)
    w_ple_b = w_ple[0].astype(BF16)

    x1p, pool_p, c_p, n_p, m_p = _prompt_mixer(
        x_prompt, w_in_b, w_g_b, gbias, w_pool_b, pscale, mhg, w_out_b, ln1g, ln1b)
    x1s, pool_s, c_s, n_s, m_s = _sample_mixer(
        x_sample.reshape(DEC_BATCH, D_MODEL),
        state_pool[0].reshape(DEC_BATCH, POOL_HIST * POOL_WIDTH),
        state_mlstm_C[0], state_mlstm_n[0].reshape(DEC_BATCH, MLSTM_WIDTH), state_mlstm_m[0],
        w_in_b, w_g_b, gbias, w_pool_b, pscale, mhg, w_out_b, ln1g, ln1b)

    slots, gates, nch, sorted_rows = _route(x1p, x1s, w_router_t, b_router_col)
    tables = _block_tables(nch[:, :, 0].astype(I32))
    expert_out = _experts(*tables, sorted_rows, w_mlp1[0], b1p, w_mlp2[0], b2)

    pp = p_prompt[0].reshape(N_PROMPT, PLE_DIM)
    ps = jnp.pad(p_sample[0].reshape(DEC_BATCH, PLE_DIM), ((0, TOK_TILE - DEC_BATCH), (0, 0)))
    yp, ys = _combine(slots, gates, x1p, x1s, pp, ps, expert_out, ln2g, ln2b, w_pg_b, w_ple_b)

    return (
        yp.reshape(BATCH, SEQ, D_MODEL),
        ys.reshape(DEC_BATCH, 1, D_MODEL),
        pool_p.reshape(1, BATCH, POOL_HIST, POOL_WIDTH),
        c_p.reshape(1, BATCH, HEADS, HEAD_DIM, HEAD_DIM),
        n_p.reshape(1, BATCH, HEADS, HEAD_DIM),
        m_p[:, 0:HEADS, 0].reshape(1, BATCH, HEADS),
        pool_s.reshape(1, DEC_BATCH, POOL_HIST, POOL_WIDTH),
        c_s.reshape(1, DEC_BATCH, HEADS, HEAD_DIM, HEAD_DIM),
        n_s.reshape(1, DEC_BATCH, HEADS, HEAD_DIM),
        m_s[:, HEADS:2 * HEADS].reshape(1, DEC_BATCH, HEADS),
    )
```

```python
import jax
import jax.numpy as jnp
from jax import lax
from jax.experimental import pallas as pl
from jax.experimental.pallas import tpu as pltpu

F32 = jnp.float32
BF16 = jnp.bfloat16
I32 = jnp.int32

D_MODEL = 1024
BATCH = 8
SEQ = 2048
DEC_BATCH = 128
PAST_LEN = 16384
POOL_WIDTH = 512
POOL_GROUPS = 4
POOL_GROUP_DIM = 128
POOL_WINDOWS = (2, 4, 8, 16)
POOL_HIST = 15
MLSTM_WIDTH = 512
HEADS = 4
HEAD_DIM = 128
CHUNK = 128
N_EXPERTS = 32
TOP_K = 4
D_FF = 1024
SWIGLU_ALPHA = 1.702
SWIGLU_LIMIT = 7.0
PLE_DIM = 256
DN_ALPHA = 2.0 ** 0.25
LN_EPS = 1e-5

LANES = 128
SUBLANES = 8
BF16_ROWS = 16
MXU_COLS = 256
VMEM_LIMIT = 56 * 1024 * 1024

MIX_TILE = 256
MIX_SEQS = 2
HIST_PAD = 16
TOK_TILE = 512
N_PROMPT = BATCH * SEQ
N_PROMPT_TILES = N_PROMPT // TOK_TILE
N_TILES = N_PROMPT_TILES + 1
SAMPLE_BT = 16

CHUNK_ROWS = BF16_ROWS
LOCAL_ROWS = TOK_TILE * TOP_K + N_EXPERTS * CHUNK_ROWS
GROUPS = LOCAL_ROWS // TOK_TILE
ROW_BLOCK = 256
BLOCK_CHUNKS = ROW_BLOCK // CHUNK_ROWS
MAX_CHUNKS = N_TILES * (TOK_TILE * TOP_K // CHUNK_ROWS + N_EXPERTS)
N_BLOCKS = -(-MAX_CHUNKS // BLOCK_CHUNKS) + N_EXPERTS
ZERO_CHUNK_ROW = LOCAL_ROWS - CHUNK_ROWS
FREE_ROWS = LOCAL_ROWS - TOK_TILE * TOP_K
DUMP_BASE = N_TILES * LOCAL_ROWS
DUMP_ROWS = 2 * ROW_BLOCK
assert DUMP_ROWS <= FREE_ROWS
WEIGHT_PIECES = 8
PIECES_PER_BLOCK = 2
GATHER_DEPTH = 3


def _dot(a, b):
    return jnp.dot(a, b, preferred_element_type=F32)


def _dot_nt(a, b):
    return lax.dot_general(a, b, (((1,), (1,)), ((), ())), preferred_element_type=F32)


def _dot_tn(a, b):
    return lax.dot_general(a, b, (((0,), (0,)), ((), ())), preferred_element_type=F32)


def _split3(a):
    a0 = a.astype(BF16)
    r1 = a - a0.astype(F32)
    a1 = r1.astype(BF16)
    r2 = r1 - a1.astype(F32)
    return a0, a1, r2.astype(BF16)


def _log_sigmoid(x):
    return jnp.minimum(x, 0.0) - jnp.log1p(jnp.exp(-jnp.abs(x)))


def _layer_norm(x, g, b):
    mu = jnp.mean(x, axis=-1, keepdims=True)
    xc = x - mu
    var = jnp.mean(xc * xc, axis=-1, keepdims=True)
    return xc * lax.rsqrt(var + LN_EPS) * g + b


def _gate_values(g, gbias):
    lane = lax.broadcasted_iota(I32, g.shape, 1)
    z = g + gbias
    return jnp.where(lane < HEADS, z, _log_sigmoid(z))


def _head_out(hh, o_h, gain):
    mu = jnp.mean(hh, axis=-1, keepdims=True)
    hc = hh - mu
    var = jnp.mean(hc * hc, axis=-1, keepdims=True)
    return jax.nn.sigmoid(o_h) * (hc * lax.rsqrt(var + LN_EPS) * gain)


def _prompt_mixer_kernel(x_ref, xn_ref, win_ref, wg_ref, gb_ref, wpool_ref, pscale_ref, mhg_ref, wout_ref,
                         ln1g_ref, ln1b_ref,
                         x1_ref, pool_ref, c_out_ref, n_out_ref, m_out_ref,
                         ubuf, mixbuf, pbuf, gbuf, c_s, n_s, m_s):
    ti = pl.program_id(1)
    nt = pl.num_programs(1)
    TT = MIX_TILE
    S = MIX_SEQS

    @pl.when(ti == 0)
    def _():
        for s in range(S):
            ubuf[s, 0:HIST_PAD, :] = jnp.zeros((HIST_PAD, POOL_WIDTH), F32)
        c_s[...] = jnp.zeros_like(c_s)
        n_s[...] = jnp.zeros_like(n_s)
        m_s[...] = jnp.zeros_like(m_s)

    x = jnp.concatenate([x_ref[s, 0] for s in range(S)], axis=0)
    step = pl.program_id(0) * nt + ti
    cur = lax.rem(step, 2)
    nxt = 1 - cur
    n_main = POOL_WIDTH + 4 * MLSTM_WIDTH

    @pl.when(step == 0)
    def _():
        xb0 = x.astype(BF16)
        pbuf[0] = _dot(xb0, win_ref[...])
        gbuf[0] = _dot(xb0, wg_ref[...])

    xnb = jnp.concatenate([xn_ref[s, 0] for s in range(S)], axis=0).astype(BF16)

    def slab(j):
        def run():
            pbuf[nxt, :, j * MXU_COLS:(j + 1) * MXU_COLS] = _dot(xnb, win_ref[:, j * MXU_COLS:(j + 1) * MXU_COLS])
        return run

    def gate_slab():
        gbuf[nxt] = _dot(xnb, wg_ref[...])

    pending = [slab(j) for j in range(n_main // MXU_COLS)] + [gate_slab]

    def ahead(n=1):
        for _ in range(n):
            if pending:
                pending.pop(0)()

    proj = pbuf.at[cur]
    g = gbuf[cur]

    L = CHUNK
    row = lax.broadcasted_iota(I32, (L, L), 0)
    col = lax.broadcasted_iota(I32, (L, L), 1)
    causal = row >= col
    tril = jnp.where(causal, 1.0, 0.0).astype(BF16)
    pos = ti * TT + lax.broadcasted_iota(I32, (TT, 1), 0)

    for s in range(S):
        base = s * TT
        u = proj[base:base + TT, 0:POOL_WIDTH]

        ubuf[s, HIST_PAD:HIST_PAD + TT, :] = u
        for gi, w in enumerate(POOL_WINDOWS):
            sl = slice(gi * POOL_GROUP_DIM, (gi + 1) * POOL_GROUP_DIM)
            ug = u[:, sl]
            acc = ug
            for i in range(1, w):
                acc = acc + ubuf[s, HIST_PAD - i:HIST_PAD - i + TT, sl]
            cnt = jnp.minimum(pos + 1, w).astype(F32)
            z = acc / cnt - ug
            mixbuf[base:base + TT, sl] = _dot(z.astype(BF16), wpool_ref[gi]) * pscale_ref[:, sl]

        @pl.when(ti == nt - 1)
        def _():
            pool_ref[s, 0] = ubuf[s, TT + 1:TT + HIST_PAD, :]

        ubuf[s, 0:HIST_PAD, :] = ubuf[s, TT:TT + HIST_PAD, :]

    NC = TT // L
    chains = [(s, h) for s in range(S) for h in range(HEADS)]
    units = [(s, c, h) for c in range(NC) for s in range(S) for h in range(HEADS)]
    U = range(len(units))

    def rows(s, c):
        return slice(s * TT + c * L, s * TT + (c + 1) * L)

    def head_cols(part, h):
        return slice(part * POOL_WIDTH + h * HEAD_DIM, part * POOL_WIDTH + (h + 1) * HEAD_DIM)

    gate, cum, gate_t, cum_t = {}, {}, {}, {}
    for c in range(NC):
        for s in range(S):
            val = _gate_values(g[rows(s, c), :], gb_ref[...])
            v0, v1, v2 = _split3(val)
            gate[s, c] = val
            cum[s, c] = _dot(tril, v0) + _dot(tril, v1) + _dot(tril, v2)
    for key in gate:
        gate_t[key] = gate[key].T
        cum_t[key] = cum[key].T
    ahead()
    qf = [proj[rows(s, c), head_cols(1, h)] for s, c, h in units]
    kf = [proj[rows(s, c), head_cols(2, h)] * (HEAD_DIM ** -0.5) for s, c, h in units]
    vf = [proj[rows(s, c), head_cols(3, h)] for s, c, h in units]
    qb = [a.astype(BF16) for a in qf]
    kb = [a.astype(BF16) for a in kf]
    f_col = [cum[s, c][:, HEADS + h:HEADS + h + 1] for s, c, h in units]
    ahead()
    log_d = [jnp.where(causal, f_col[u] - cum_t[s, c][HEADS + h:HEADS + h + 1, :] + gate_t[s, c][h:h + 1, :],
                       -jnp.inf) for u, (s, c, h) in enumerate(units)]
    ahead()
    row_max = [jnp.max(log_d[u], axis=-1, keepdims=True) for u in U]
    ahead()
    qk_raw = [_dot_nt(qb[u], kb[u]) for u in U]

    m_prev, m_t, inter = [None] * len(units), [None] * len(units), [None] * len(units)
    m_run = {(s, h): m_s[s, h:h + 1, 0:1] for s, h in chains}
    for u, (s, c, h) in enumerate(units):
        m_prev[u] = m_run[s, h]
        inter[u] = m_prev[u] + f_col[u]
        m_t[u] = jnp.maximum(inter[u], row_max[u])
        m_run[s, h] = m_t[u][L - 1:L, :]
    m_new = [m_t[u][L - 1:L, :] for u in U]

    ahead()
    dw = [jnp.exp(log_d[u] - m_t[u]) for u in U]
    sc = [jnp.exp(inter[u] - m_t[u]) for u in U]
    ahead()
    qk = [qk_raw[u] * dw[u] for u in U]
    ahead()
    intra = [_dot(qk[u].astype(BF16), vf[u].astype(BF16)) for u in U]
    ahead()
    row_sum = [jnp.sum(qk[u], axis=-1, keepdims=True) for u in U]
    floor = [jnp.exp(-m_t[u]) for u in U]
    f_last = [f_col[u][L - 1:L, :] for u in U]
    ahead()
    wk = [jnp.exp(gate[s, c][:, h:h + 1] + f_last[u] - f_col[u] - m_new[u]) for u, (s, c, h) in enumerate(units)]
    decay = [jnp.exp(m_prev[u] + f_last[u] - m_new[u]) for u in U]
    ahead()
    upd = [_dot_tn((vf[u] * wk[u]).astype(BF16), kb[u]) for u in U]
    ahead()
    n_upd = [jnp.sum(wk[u] * kf[u], axis=0, keepdims=True) for u in U]

    c_run = {(s, h): c_s[s, h] for s, h in chains}
    n_run = {(s, h): n_s[s, h:h + 1, :] for s, h in chains}
    hh = [None] * len(units)
    for c in range(NC):
        cu = [u for u in U if units[u][1] == c]
        inter_term = {u: _dot_nt(qb[u], c_run[units[u][0], units[u][2]].astype(BF16)) for u in cu}
        n_term = {u: jnp.sum(qf[u] * n_run[units[u][0], units[u][2]], axis=-1, keepdims=True) for u in cu}
        for u in cu:
            s, _, h = units[u]
            num = intra[u] + sc[u] * inter_term[u]
            den = row_sum[u] + sc[u] * n_term[u]
            hh[u] = num / jnp.maximum(jnp.abs(den), floor[u])
            c_run[s, h] = decay[u] * c_run[s, h] + upd[u]
            n_run[s, h] = decay[u] * n_run[s, h] + n_upd[u]
    ahead(len(pending))
    for s, h in chains:
        c_s[s, h] = c_run[s, h]
        n_s[s, h:h + 1, :] = n_run[s, h]
        m_s[s, h:h + 1, :] = jnp.broadcast_to(m_run[s, h], (1, LANES))
    for u, (s, c, h) in enumerate(units):
        mixbuf[rows(s, c), head_cols(1, h)] = _head_out(
            hh[u], proj[rows(s, c), head_cols(4, h)], mhg_ref[:, h * HEAD_DIM:(h + 1) * HEAD_DIM])

    @pl.when(ti == nt - 1)
    def _():
        for s in range(S):
            c_out_ref[s, 0] = c_s[s]
            n_out_ref[s, 0] = n_s[s, 0:HEADS, :]
            m_out_ref[s, 0] = m_s[s]

    mix = _dot(mixbuf[...].astype(BF16), wout_ref[...])
    x1 = _layer_norm(DN_ALPHA * x + mix, ln1g_ref[...], ln1b_ref[...])
    for s in range(S):
        x1_ref[s] = x1[s * TT:(s + 1) * TT, :]


def _prompt_mixer(x, w_in_b, w_g_b, gbias, w_pool_b, pscale, mhg, w_out_b, ln1g, ln1b):
    nt = SEQ // MIX_TILE
    S = MIX_SEQS
    G = BATCH // S
    const2 = lambda b, t: (0, 0)
    const3 = lambda b, t: (0, 0, 0)

    def next_tile(b, t):
        nxt = jnp.minimum(b * nt + t + 1, G * nt - 1)
        return (0, nxt // nt, nxt % nt, 0)

    outs = pl.pallas_call(
        _prompt_mixer_kernel,
        grid=(G, nt),
        in_specs=[
            pl.BlockSpec((S, 1, MIX_TILE, D_MODEL), lambda b, t: (0, b, t, 0)),
            pl.BlockSpec((S, 1, MIX_TILE, D_MODEL), next_tile),
            pl.BlockSpec(w_in_b.shape, const2),
            pl.BlockSpec(w_g_b.shape, const2),
            pl.BlockSpec(gbias.shape, const2),
            pl.BlockSpec(w_pool_b.shape, const3),
            pl.BlockSpec(pscale.shape, const2),
            pl.BlockSpec(mhg.shape, const2),
            pl.BlockSpec(w_out_b.shape, const2),
            pl.BlockSpec(ln1g.shape, const2),
            pl.BlockSpec(ln1b.shape, const2),
        ],
        out_specs=[
            pl.BlockSpec((S, MIX_TILE, D_MODEL), lambda b, t: (0, b * nt + t, 0)),
            pl.BlockSpec((S, 1, POOL_HIST, POOL_WIDTH), lambda b, t: (0, b, 0, 0)),
            pl.BlockSpec((S, 1, HEADS, HEAD_DIM, HEAD_DIM), lambda b, t: (0, b, 0, 0, 0)),
            pl.BlockSpec((S, 1, HEADS, HEAD_DIM), lambda b, t: (0, b, 0, 0)),
            pl.BlockSpec((S, 1, SUBLANES, LANES), lambda b, t: (0, b, 0, 0)),
        ],
        out_shape=[
            jax.ShapeDtypeStruct((S, G * SEQ, D_MODEL), F32),
            jax.ShapeDtypeStruct((S, G, POOL_HIST, POOL_WIDTH), F32),
            jax.ShapeDtypeStruct((S, G, HEADS, HEAD_DIM, HEAD_DIM), F32),
            jax.ShapeDtypeStruct((S, G, HEADS, HEAD_DIM), F32),
            jax.ShapeDtypeStruct((S, G, SUBLANES, LANES), F32),
        ],
        scratch_shapes=[
            pltpu.VMEM((S, HIST_PAD + MIX_TILE, POOL_WIDTH), F32),
            pltpu.VMEM((S * MIX_TILE, D_MODEL), F32),
            pltpu.VMEM((2, S * MIX_TILE, POOL_WIDTH + 4 * MLSTM_WIDTH), F32),
            pltpu.VMEM((2, S * MIX_TILE, LANES), F32),
            pltpu.VMEM((S, HEADS, HEAD_DIM, HEAD_DIM), F32),
            pltpu.VMEM((S, SUBLANES, HEAD_DIM), F32),
            pltpu.VMEM((S, SUBLANES, LANES), F32),
        ],
        compiler_params=pltpu.CompilerParams(
            dimension_semantics=("arbitrary", "arbitrary"), vmem_limit_bytes=VMEM_LIMIT),
        name="prompt_mixer",
    )(x.reshape(S, G, SEQ, D_MODEL), x.reshape(S, G, SEQ, D_MODEL), w_in_b, w_g_b, gbias, w_pool_b, pscale, mhg, w_out_b, ln1g, ln1b)
    x1, pool, c, n, m = outs
    return (x1.reshape(N_PROMPT, D_MODEL), pool.reshape(BATCH, POOL_HIST, POOL_WIDTH),
            c.reshape(BATCH, HEADS, HEAD_DIM, HEAD_DIM), n.reshape(BATCH, HEADS, HEAD_DIM),
            m.reshape(BATCH, SUBLANES, LANES))


def _sample_mixer_kernel(x_ref, hist_ref, c_ref, n_ref, m_ref, win_ref, wg_ref, gb_ref, wpool_ref,
                         pscale_ref, mhg_ref, wout_ref, ln1g_ref, ln1b_ref,
                         x1_ref, pool_out_ref, c_out_ref, n_out_ref, m_out_ref,
                         q_s, k_s, vw_s, v_s, o_s, mixbuf, h_s, coef_s):
    i = pl.program_id(0)
    nsteps = pl.num_programs(0)
    B = DEC_BATCH

    @pl.when(i == 0)
    def _():
        x = x_ref[...]
        xb = x.astype(BF16)
        proj = _dot(xb, win_ref[...])
        g = _dot(xb, wg_ref[...])
        u = proj[:, 0:POOL_WIDTH]
        for gi, w in enumerate(POOL_WINDOWS):
            sl = slice(gi * POOL_GROUP_DIM, (gi + 1) * POOL_GROUP_DIM)
            ug = u[:, sl]
            s = ug
            for j in range(1, w):
                r = POOL_HIST - j
                s = s + hist_ref[:, r * POOL_WIDTH + gi * POOL_GROUP_DIM:r * POOL_WIDTH + (gi + 1) * POOL_GROUP_DIM]
            cnt = float(min(PAST_LEN + 1, w))
            z = s / cnt - ug
            mixbuf[:, sl] = _dot(z.astype(BF16), wpool_ref[gi]) * pscale_ref[:, sl]
        pool_out_ref[:, 0:(POOL_HIST - 1) * POOL_WIDTH] = hist_ref[:, POOL_WIDTH:POOL_HIST * POOL_WIDTH]
        pool_out_ref[:, (POOL_HIST - 1) * POOL_WIDTH:POOL_HIST * POOL_WIDTH] = u

        val = _gate_values(g, gb_ref[...])
        lane = lax.broadcasted_iota(I32, (B, LANES), 1)
        qk_all = jnp.zeros((B, LANES), F32)
        sc_all = jnp.zeros((B, LANES), F32)
        den_all = jnp.zeros((B, LANES), F32)
        floor_all = jnp.zeros((B, LANES), F32)
        m_all = jnp.zeros((B, LANES), F32)
        for h in range(HEADS):
            hs = slice(h * HEAD_DIM, (h + 1) * HEAD_DIM)
            qf = proj[:, POOL_WIDTH + h * HEAD_DIM:POOL_WIDTH + (h + 1) * HEAD_DIM]
            kf = proj[:, 2 * POOL_WIDTH + h * HEAD_DIM:2 * POOL_WIDTH + (h + 1) * HEAD_DIM] * (HEAD_DIM ** -0.5)
            vf = proj[:, 3 * POOL_WIDTH + h * HEAD_DIM:3 * POOL_WIDTH + (h + 1) * HEAD_DIM]
            ig = val[:, h:h + 1]
            lf = val[:, HEADS + h:HEADS + h + 1]
            m0 = m_ref[:, h:h + 1]
            n0 = n_ref[:, hs]
            inter = m0 + lf
            m_t = jnp.maximum(inter, ig)
            dw = jnp.exp(ig - m_t)
            sc = jnp.exp(inter - m_t)
            qk = jnp.sum(qf * kf, axis=-1, keepdims=True) * dw
            den = qk + sc * jnp.sum(qf * n0, axis=-1, keepdims=True)
            n_out_ref[:, hs] = sc * n0 + dw * kf
            q_s[0:B, hs] = qf
            k_s[0:B, hs] = kf
            v_s[0:B, hs] = vf
            vw_s[0:B, hs] = vf * dw
            sel = lane == h
            qk_all = jnp.where(sel, qk, qk_all)
            sc_all = jnp.where(sel, sc, sc_all)
            den_all = jnp.where(sel, den, den_all)
            floor_all = jnp.where(sel, jnp.exp(-m_t), floor_all)
            m_all = jnp.where(lane == HEADS + h, m_t, m_all)
        o_s[...] = proj[:, 4 * POOL_WIDTH:5 * POOL_WIDTH]
        coef_s[0] = qk_all
        coef_s[1] = sc_all
        coef_s[2] = den_all
        coef_s[3] = floor_all
        m_out_ref[...] = m_all

    rows = pl.ds(pl.multiple_of(i * SAMPLE_BT, SAMPLE_BT), SAMPLE_BT)
    q_t, k_t, v_t, vw_t = q_s[rows, :], k_s[rows, :], v_s[rows, :], vw_s[rows, :]
    qk_t, sc_t, den_t, floor_t = coef_s[0, rows, :], coef_s[1, rows, :], coef_s[2, rows, :], coef_s[3, rows, :]
    h_rows = []
    for bl in range(SAMPLE_BT):
        heads = []
        for h in range(HEADS):
            hs = slice(h * HEAD_DIM, (h + 1) * HEAD_DIM)
            c_prev = c_ref[bl, h]
            q8 = jnp.broadcast_to(q_t[bl:bl + 1, hs], (SUBLANES, HEAD_DIM))
            cq = _dot_nt(q8.astype(BF16), c_prev.astype(BF16))[0:1, :]
            qk = qk_t[bl:bl + 1, h:h + 1]
            sc = sc_t[bl:bl + 1, h:h + 1]
            num = qk * v_t[bl:bl + 1, hs] + sc * cq
            heads.append(num / jnp.maximum(jnp.abs(den_t[bl:bl + 1, h:h + 1]), floor_t[bl:bl + 1, h:h + 1]))
            v_col = jnp.broadcast_to(vw_t[bl:bl + 1, hs], (HEAD_DIM, HEAD_DIM)).T
            c_out_ref[bl, h] = sc * c_prev + v_col * k_t[bl:bl + 1, hs]
        h_rows.append(jnp.concatenate(heads, axis=1))
    h_s[rows, :] = jnp.concatenate(h_rows, axis=0)

    @pl.when(i == nsteps - 1)
    def _():
        for h in range(HEADS):
            hs = slice(h * HEAD_DIM, (h + 1) * HEAD_DIM)
            mixbuf[:, POOL_WIDTH + h * HEAD_DIM:POOL_WIDTH + (h + 1) * HEAD_DIM] = _head_out(
                h_s[:, hs], o_s[:, hs], mhg_ref[:, hs])
        mix = _dot(mixbuf[...].astype(BF16), wout_ref[...])
        x1 = _layer_norm(DN_ALPHA * x_ref[...] + mix, ln1g_ref[...], ln1b_ref[...])
        x1_ref[0:B, :] = x1
        x1_ref[B:TOK_TILE, :] = jnp.zeros((TOK_TILE - B, D_MODEL), F32)


def _sample_mixer(x, hist2, c0, n0, m0, w_in_b, w_g_b, gbias, w_pool_b, pscale, mhg, w_out_b, ln1g, ln1b):
    B = DEC_BATCH
    steps = B // SAMPLE_BT
    full = lambda a: pl.BlockSpec(a.shape, lambda i: (0,) * a.ndim)
    c_spec = pl.BlockSpec((SAMPLE_BT, HEADS, HEAD_DIM, HEAD_DIM), lambda i: (i, 0, 0, 0))
    return pl.pallas_call(
        _sample_mixer_kernel,
        grid=(steps,),
        in_specs=[full(x), full(hist2), c_spec, full(n0), full(m0), full(w_in_b), full(w_g_b), full(gbias),
                  full(w_pool_b), full(pscale), full(mhg), full(w_out_b), full(ln1g), full(ln1b)],
        out_specs=[
            pl.BlockSpec((TOK_TILE, D_MODEL), lambda i: (0, 0)),
            pl.BlockSpec((B, POOL_HIST * POOL_WIDTH), lambda i: (0, 0)),
            c_spec,
            pl.BlockSpec((B, MLSTM_WIDTH), lambda i: (0, 0)),
            pl.BlockSpec((B, LANES), lambda i: (0, 0)),
        ],
        out_shape=[
            jax.ShapeDtypeStruct((TOK_TILE, D_MODEL), F32),
            jax.ShapeDtypeStruct((B, POOL_HIST * POOL_WIDTH), F32),
            jax.ShapeDtypeStruct((B, HEADS, HEAD_DIM, HEAD_DIM), F32),
            jax.ShapeDtypeStruct((B, MLSTM_WIDTH), F32),
            jax.ShapeDtypeStruct((B, LANES), F32),
        ],
        scratch_shapes=[
            pltpu.VMEM((B, MLSTM_WIDTH), F32),
            pltpu.VMEM((B, MLSTM_WIDTH), F32),
            pltpu.VMEM((B, MLSTM_WIDTH), F32),
            pltpu.VMEM((B, MLSTM_WIDTH), F32),
            pltpu.VMEM((B, MLSTM_WIDTH), F32),
            pltpu.VMEM((B, D_MODEL), F32),
            pltpu.VMEM((B, MLSTM_WIDTH), F32),
            pltpu.VMEM((4, B, LANES), F32),
        ],
        compiler_params=pltpu.CompilerParams(
            dimension_semantics=("arbitrary",), vmem_limit_bytes=VMEM_LIMIT),
        name="sample_mixer",
    )(x, hist2, c0, n0, m0, w_in_b, w_g_b, gbias, w_pool_b, pscale, mhg, w_out_b, ln1g, ln1b)


def _pick_tile(i, prompt_ref, sample_ref):
    return jnp.where(i < N_PROMPT_TILES, prompt_ref[...], sample_ref[...])


def _placement(slot_rows, group):
    r = group * TOK_TILE + lax.broadcasted_iota(I32, (TOK_TILE, TOK_TILE), 0)
    return [r == s for s in slot_rows]


def _route_kernel(xp_ref, xs_ref, wrt_ref, br_ref, slot_ref, gate_ref, nch_ref, sorted_ref):
    i = pl.program_id(0)
    T = TOK_TILE
    E = N_EXPERTS

    x = _pick_tile(i, xp_ref, xs_ref)
    xh = x.astype(BF16)
    xl = (x - xh.astype(F32)).astype(BF16)
    w = wrt_ref[...]
    wh = w.astype(BF16)
    wl = (w - wh.astype(F32)).astype(BF16)
    logits = _dot_nt(wh, xh) + (_dot_nt(wh, xl) + _dot_nt(wl, xh)) + br_ref[:, 0:1]

    erow = lax.broadcasted_iota(I32, (E, T), 0).astype(F32)
    work = logits
    vals, sels = [], []
    for _ in range(TOP_K):
        mx = jnp.max(work, axis=0, keepdims=True)
        idx = jnp.min(jnp.where(work == mx, erow, float(E)), axis=0, keepdims=True)
        sel = erow == idx
        work = jnp.where(sel, -jnp.inf, work)
        vals.append(mx)
        sels.append(sel)
    chosen = jnp.logical_or(jnp.logical_or(sels[0], sels[1]), jnp.logical_or(sels[2], sels[3]))
    es = [jnp.exp(v - vals[0]) for v in vals]
    tot = es[0] + es[1] + es[2] + es[3]

    onehot = jnp.where(chosen, 1.0, 0.0)
    trow = lax.broadcasted_iota(I32, (T, T), 0)
    tcol = lax.broadcasted_iota(I32, (T, T), 1)
    before = jnp.where(trow < tcol, 1.0, 0.0).astype(BF16)
    rank = _dot(onehot.astype(BF16), before)
    cnt = jnp.sum(onehot, axis=1, keepdims=True)
    nch = jnp.floor((cnt + (CHUNK_ROWS - 1)) * (1.0 / CHUNK_ROWS))
    lower = jnp.where(lax.broadcasted_iota(I32, (E, E), 0) > lax.broadcasted_iota(I32, (E, E), 1), 1.0, 0.0)
    nch_b = jnp.broadcast_to(nch, (E, LANES))
    seg_start = _dot(lower.astype(BF16), nch_b.astype(BF16))[:, 0:1] * CHUNK_ROWS
    base = seg_start + rank

    r8 = lax.broadcasted_iota(I32, (SUBLANES, T), 0)
    s_out = jnp.zeros((SUBLANES, T), I32)
    g_out = jnp.zeros((SUBLANES, T), F32)
    slot_rows = []
    for j in range(TOP_K):
        slot_j = jnp.sum(jnp.where(sels[j], base, 0.0), axis=0, keepdims=True).astype(I32)
        slot_rows.append(slot_j)
        s_out = jnp.where(r8 == j, slot_j, s_out)
        g_out = jnp.where(r8 == j, es[j] / tot, g_out)
    slot_ref[0] = s_out
    gate_ref[0] = g_out
    nch_ref[0] = nch_b

    for grp in range(GROUPS):
        m = _placement(slot_rows, grp)
        hit = jnp.logical_or(jnp.logical_or(m[0], m[1]), jnp.logical_or(m[2], m[3]))
        place = jnp.where(hit, 1.0, 0.0).astype(BF16)
        sorted_ref[grp * T:(grp + 1) * T, :] = _dot(place, xh).astype(BF16)


def _route(x1p, x1s, w_router_t, b_router_col):
    tile_spec = pl.BlockSpec((1, SUBLANES, TOK_TILE), lambda i: (i, 0, 0))
    return pl.pallas_call(
        _route_kernel,
        grid=(N_TILES,),
        in_specs=[
            pl.BlockSpec((TOK_TILE, D_MODEL), lambda i: (jnp.minimum(i, N_PROMPT_TILES - 1), 0)),
            pl.BlockSpec((TOK_TILE, D_MODEL), lambda i: (0, 0)),
            pl.BlockSpec(w_router_t.shape, lambda i: (0, 0)),
            pl.BlockSpec(b_router_col.shape, lambda i: (0, 0)),
        ],
        out_specs=[tile_spec, tile_spec,
                   pl.BlockSpec((1, N_EXPERTS, LANES), lambda i: (i, 0, 0)),
                   pl.BlockSpec((LOCAL_ROWS, D_MODEL), lambda i: (i, 0))],
        out_shape=[
            jax.ShapeDtypeStruct((N_TILES, SUBLANES, TOK_TILE), I32),
            jax.ShapeDtypeStruct((N_TILES, SUBLANES, TOK_TILE), F32),
            jax.ShapeDtypeStruct((N_TILES, N_EXPERTS, LANES), F32),
            jax.ShapeDtypeStruct((N_TILES * LOCAL_ROWS, D_MODEL), BF16),
        ],
        compiler_params=pltpu.CompilerParams(
            dimension_semantics=("arbitrary",), vmem_limit_bytes=VMEM_LIMIT),
        name="route",
    )(x1p, x1s, w_router_t, b_router_col)


def _expert_kernel(src_ref, dst_ref, bexp_ref, first_ref, next_ref, nused_ref,
                   sorted_hbm, w1_hbm, b1_ref, w2_hbm, b2_ref, out_hbm,
                   w1_stage, w2_stage, w1_b, w2_b, xbuf, obuf, zbuf, wsem, gsem, ssem, zsem):
    nused = nused_ref[0]

    def fetch_piece(e, p):
        r1 = pl.ds(pl.multiple_of(p * (D_MODEL // WEIGHT_PIECES), SUBLANES), D_MODEL // WEIGHT_PIECES)
        r2 = pl.ds(pl.multiple_of(p * (D_FF // WEIGHT_PIECES), SUBLANES), D_FF // WEIGHT_PIECES)
        return (pltpu.make_async_copy(w1_hbm.at[e, r1, :], w1_stage.at[r1, :], wsem.at[0]),
                pltpu.make_async_copy(w2_hbm.at[e, r2, :], w2_stage.at[r2, :], wsem.at[1]))

    def start_pieces(e, lo, hi):
        def body(p, c):
            for cp in fetch_piece(e, p):
                cp.start()
            return c
        lax.fori_loop(lo, hi, body, 0)

    def gather(b, q):
        slot = lax.rem(b, GATHER_DEPTH)
        row = pl.multiple_of(src_ref[b * BLOCK_CHUNKS + q], CHUNK_ROWS)
        return pltpu.make_async_copy(sorted_hbm.at[pl.ds(row, CHUNK_ROWS), :],
                                     xbuf.at[slot, pl.ds(q * CHUNK_ROWS, CHUNK_ROWS), :], gsem.at[slot])

    def scatter(b, q):
        slot = lax.rem(b, 2)
        row = pl.multiple_of(dst_ref[b * BLOCK_CHUNKS + q], CHUNK_ROWS)
        return pltpu.make_async_copy(obuf.at[slot, pl.ds(q * CHUNK_ROWS, CHUNK_ROWS), :],
                                     out_hbm.at[pl.ds(row, CHUNK_ROWS), :], ssem.at[slot])

    def zero_rows(start, n_rows):
        start = pl.multiple_of(start, CHUNK_ROWS)
        return pltpu.make_async_copy(zbuf.at[pl.ds(0, n_rows), :], out_hbm.at[pl.ds(start, n_rows), :], zsem)

    def zero_tail(k):
        return zero_rows(k * LOCAL_ROWS + TOK_TILE * TOP_K, FREE_ROWS)

    zbuf[...] = jnp.zeros_like(zbuf)
    lax.fori_loop(0, N_TILES, lambda k, c: (zero_tail(k).start(), c)[1], 0)
    zero_rows(DUMP_BASE, DUMP_ROWS).start()
    for ahead in range(GATHER_DEPTH - 1):
        @pl.when(ahead < nused)
        def _():
            for q in range(BLOCK_CHUNKS):
                gather(ahead, q).start()
    lax.fori_loop(0, N_TILES, lambda k, c: (zero_tail(k).wait(), c)[1], 0)
    zero_rows(DUMP_BASE, DUMP_ROWS).wait()

    half = MXU_COLS // 2
    k_io = lax.broadcasted_iota(I32, (MXU_COLS, MXU_COLS), 0)
    j_io = lax.broadcasted_iota(I32, (MXU_COLS, MXU_COLS), 1)
    src_col = jnp.where(j_io < half, 2 * j_io, 2 * (j_io - half) + 1)
    perm = jnp.where(k_io == src_col, 1.0, 0.0).astype(BF16)

    def block(i, fetched):
        e = bexp_ref[i]
        slot = lax.rem(i, 2)
        is_first = first_ref[i] == 1

        @pl.when(is_first)
        def _():
            start_pieces(e, fetched, WEIGHT_PIECES)

            def wait_piece(p, c):
                for cp in fetch_piece(e, p):
                    cp.wait()
                return c
            lax.fori_loop(0, WEIGHT_PIECES, wait_piece, 0)
            for c in range(2 * D_FF // MXU_COLS):
                blk = w1_stage[:, c * MXU_COLS:(c + 1) * MXU_COLS].astype(BF16)
                sep = _dot(blk, perm).astype(BF16)
                w1_b[:, c * half:(c + 1) * half] = sep[:, 0:half]
                w1_b[:, D_FF + c * half:D_FF + (c + 1) * half] = sep[:, half:MXU_COLS]
            w2_b[...] = w2_stage[...].astype(BF16)

        fetched = jnp.where(is_first, 0, fetched)

        for q in range(BLOCK_CHUNKS):
            gather(i, q).wait()

        @pl.when(i + GATHER_DEPTH - 1 < nused)
        def _():
            for q in range(BLOCK_CHUNKS):
                gather(i + GATHER_DEPTH - 1, q).start()

        @pl.when(i >= 2)
        def _():
            for q in range(BLOCK_CHUNKS):
                scatter(i - 2, q).wait()

        h = _dot(xbuf[lax.rem(i, GATHER_DEPTH)], w1_b[...]) + b1_ref[e]
        glu = jnp.minimum(h[:, 0:D_FF], SWIGLU_LIMIT)
        lin = jnp.clip(h[:, D_FF:2 * D_FF], -SWIGLU_LIMIT, SWIGLU_LIMIT)
        a = glu * jax.nn.sigmoid(SWIGLU_ALPHA * glu) * (lin + 1.0)
        obuf[slot] = _dot(a.astype(BF16), w2_b[...]) + b2_ref[e]
        for q in range(BLOCK_CHUNKS):
            scatter(i, q).start()

        more = jnp.where(next_ref[i] >= 0, jnp.minimum(fetched + PIECES_PER_BLOCK, WEIGHT_PIECES), fetched)
        start_pieces(next_ref[i], fetched, more)
        return more

    lax.fori_loop(0, nused, block, jnp.int32(0))

    @pl.when(nused >= 2)
    def _():
        for q in range(BLOCK_CHUNKS):
            scatter(nused - 2, q).wait()
    for q in range(BLOCK_CHUNKS):
        scatter(nused - 1, q).wait()


def _experts(chunk_src, chunk_dst, block_expert, block_first, block_next, n_used, sorted_rows, w1, b1p, w2, b2):
    whole3 = lambda i, *_: (0, 0, 0)
    grid_spec = pltpu.PrefetchScalarGridSpec(
        num_scalar_prefetch=6,
        grid=(1,),
        in_specs=[
            pl.BlockSpec(memory_space=pl.ANY),
            pl.BlockSpec(memory_space=pl.ANY),
            pl.BlockSpec(b1p.shape, whole3),
            pl.BlockSpec(memory_space=pl.ANY),
            pl.BlockSpec(b2.shape, whole3),
        ],
        out_specs=pl.BlockSpec(memory_space=pl.ANY),
        scratch_shapes=[
            pltpu.VMEM((D_MODEL, 2 * D_FF), F32),
            pltpu.VMEM((D_FF, D_MODEL), F32),
            pltpu.VMEM((D_MODEL, 2 * D_FF), BF16),
            pltpu.VMEM((D_FF, D_MODEL), BF16),
            pltpu.VMEM((GATHER_DEPTH, ROW_BLOCK, D_MODEL), BF16),
            pltpu.VMEM((2, ROW_BLOCK, D_MODEL), F32),
            pltpu.VMEM((FREE_ROWS, D_MODEL), F32),
            pltpu.SemaphoreType.DMA((2,)),
            pltpu.SemaphoreType.DMA((GATHER_DEPTH,)),
            pltpu.SemaphoreType.DMA((2,)),
            pltpu.SemaphoreType.DMA(()),
        ],
    )
    return pl.pallas_call(
        _expert_kernel,
        grid_spec=grid_spec,
        out_shape=jax.ShapeDtypeStruct((DUMP_BASE + DUMP_ROWS, D_MODEL), F32),
        compiler_params=pltpu.CompilerParams(
            dimension_semantics=("arbitrary",), vmem_limit_bytes=VMEM_LIMIT),
        name="experts",
    )(chunk_src, chunk_dst, block_expert, block_first, block_next, n_used, sorted_rows, w1, b1p, w2, b2)


def _combine_kernel(slot_ref, gate_ref, xp_ref, xs_ref, pp_ref, ps_ref, eo_ref, ln2g_ref, ln2b_ref,
                    wpg_ref, wple_ref, yp_ref, ys_ref):
    i = pl.program_id(0)
    T = TOK_TILE

    slot_rows = [slot_ref[0, j:j + 1, :] for j in range(TOP_K)]
    gate_rows = [gate_ref[0, j:j + 1, :] for j in range(TOP_K)]
    pad = jnp.zeros((LANES - SUBLANES, T), F32)
    slots_t = jnp.concatenate([slot_ref[0].astype(F32), pad], axis=0).T
    slot_cols = [slots_t[:, j:j + 1].astype(I32) for j in range(TOP_K)]

    ff = jnp.zeros((T, D_MODEL), F32)
    for grp in range(GROUPS):
        m = _placement(slot_rows, grp)
        weighted = jnp.where(m[0], gate_rows[0], jnp.where(m[1], gate_rows[1], jnp.where(
            m[2], gate_rows[2], jnp.where(m[3], gate_rows[3], 0.0))))
        g_col = jnp.sum(weighted, axis=1, keepdims=True)
        z = (eo_ref[grp * T:(grp + 1) * T, :] * g_col).astype(BF16)
        r = grp * T + lax.broadcasted_iota(I32, (T, T), 1)
        hit = jnp.logical_or(jnp.logical_or(r == slot_cols[0], r == slot_cols[1]),
                             jnp.logical_or(r == slot_cols[2], r == slot_cols[3]))
        ff = ff + _dot(jnp.where(hit, 1.0, 0.0).astype(BF16), z)

    x1 = _pick_tile(i, xp_ref, xs_ref)
    x2 = _layer_norm(DN_ALPHA * x1 + ff, ln2g_ref[...], ln2b_ref[...])
    p = _pick_tile(i, pp_ref, ps_ref)
    y = x2 + jax.nn.sigmoid(_dot(x2.astype(BF16), wpg_ref[...])) * _dot(p.astype(BF16), wple_ref[...])

    @pl.when(i < N_PROMPT_TILES)
    def _():
        yp_ref[...] = y

    @pl.when(i == N_PROMPT_TILES)
    def _():
        ys_ref[...] = y[0:DEC_BATCH, :]


def _combine(slots, gates, x1p, x1s, pp, ps, expert_out, ln2g, ln2b, w_pg_b, w_ple_b):
    tile_idx = lambda i: (jnp.minimum(i, N_PROMPT_TILES - 1), 0)
    const2 = lambda i: (0, 0)
    return pl.pallas_call(
        _combine_kernel,
        grid=(N_TILES,),
        in_specs=[
            pl.BlockSpec((1, SUBLANES, TOK_TILE), lambda i: (i, 0, 0)),
            pl.BlockSpec((1, SUBLANES, TOK_TILE), lambda i: (i, 0, 0)),
            pl.BlockSpec((TOK_TILE, D_MODEL), tile_idx),
            pl.BlockSpec((TOK_TILE, D_MODEL), const2),
            pl.BlockSpec((TOK_TILE, PLE_DIM), tile_idx),
            pl.BlockSpec((TOK_TILE, PLE_DIM), const2),
            pl.BlockSpec((LOCAL_ROWS, D_MODEL), lambda i: (i, 0)),
            pl.BlockSpec(ln2g.shape, const2),
            pl.BlockSpec(ln2b.shape, const2),
            pl.BlockSpec(w_pg_b.shape, const2),
            pl.BlockSpec(w_ple_b.shape, const2),
        ],
        out_specs=[
            pl.BlockSpec((TOK_TILE, D_MODEL), tile_idx),
            pl.BlockSpec((DEC_BATCH, D_MODEL), const2),
        ],
        out_shape=[
            jax.ShapeDtypeStruct((N_PROMPT, D_MODEL), F32),
            jax.ShapeDtypeStruct((DEC_BATCH, D_MODEL), F32),
        ],
        compiler_params=pltpu.CompilerParams(
            dimension_semantics=("arbitrary",), vmem_limit_bytes=VMEM_LIMIT),
        name="combine",
    )(slots, gates, x1p, x1s, pp, ps, expert_out, ln2g, ln2b, w_pg_b, w_ple_b)


def _block_tables(nch):
    seg_start = (jnp.cumsum(nch, axis=1) - nch) * CHUNK_ROWS
    tot = jnp.sum(nch, axis=0)
    nblk = (tot + BLOCK_CHUNKS - 1) // BLOCK_CHUNKS
    blk_end = jnp.cumsum(nblk)
    blk_start = blk_end - nblk
    n_used = blk_end[-1:].astype(I32)
    blk_ids = jnp.arange(N_BLOCKS, dtype=I32)

    def expert_of(b):
        b = jnp.minimum(b, n_used[0] - 1)
        return jnp.minimum(jnp.sum(blk_end[None, :] <= b[:, None], axis=1), N_EXPERTS - 1).astype(I32)

    block_expert = expert_of(blk_ids)
    in_use = blk_ids < n_used[0]
    is_first = jnp.logical_or(blk_ids == 0, block_expert != expert_of(blk_ids - 1))
    block_first = jnp.logical_and(in_use, is_first).astype(I32)
    next_start = blk_end[block_expert]
    block_next = jnp.where(next_start < n_used[0], expert_of(next_start), -1).astype(I32)

    nch_t = nch.T
    seg_first = (blk_start[:, None] * BLOCK_CHUNKS + jnp.cumsum(nch_t, axis=1) - nch_t).reshape(-1)
    seg_count = nch_t.reshape(-1)
    seg_row = (jnp.arange(N_TILES, dtype=I32)[None, :] * LOCAL_ROWS + seg_start.T).reshape(-1)
    ent = jnp.arange(N_BLOCKS * BLOCK_CHUNKS, dtype=I32)
    d = ent[:, None] - seg_first[None, :]
    inside = jnp.logical_and(d >= 0, d < seg_count[None, :])
    row = jnp.sum(jnp.where(inside, seg_row[None, :] + d * CHUNK_ROWS, 0), axis=1)
    real = jnp.any(inside, axis=1)
    dump = DUMP_BASE + (((ent // BLOCK_CHUNKS) % 2) * BLOCK_CHUNKS + ent % BLOCK_CHUNKS) * CHUNK_ROWS
    chunk_src = jnp.where(real, row, ZERO_CHUNK_ROW).astype(I32)
    chunk_dst = jnp.where(real, row, dump).astype(I32)
    return chunk_src, chunk_dst, block_expert, block_first, block_next, n_used


def kernel(x_prompt, x_sample, state_pool, state_mlstm_C, state_mlstm_n, state_mlstm_m, p_prompt, p_sample, w_in, b_i, b_f, w_pool, pool_scale, mh_g, w_out, ln1_g, ln1_b, w_router, b_router, w_mlp1, b_mlp1, w_mlp2, b_mlp2, ln2_g, ln2_b, w_ple, w_ple_gate):
    n_main = POOL_WIDTH + 4 * MLSTM_WIDTH
    w_in_b = w_in[0, :, 0:n_main].astype(BF16)
    w_g_b = jnp.pad(w_in[0, :, n_main:], ((0, 0), (0, LANES - 2 * HEADS))).astype(BF16)
    gbias = jnp.pad(jnp.concatenate([b_i[0], b_f[0]]), (0, LANES - 2 * HEADS)).reshape(1, LANES)
    w_pool_b = w_pool[0].astype(BF16)
    pscale = pool_scale[0].reshape(1, POOL_WIDTH)
    mhg = mh_g[0].reshape(1, MLSTM_WIDTH)
    w_out_b = w_out[0].astype(BF16)
    ln1g = ln1_g[0].reshape(1, D_MODEL)
    ln1b = ln1_b[0].reshape(1, D_MODEL)
    ln2g = ln2_g[0].reshape(1, D_MODEL)
    ln2b = ln2_b[0].reshape(1, D_MODEL)
    w_router_t = w_router[0].T
    b_router_col = jnp.broadcast_to(b_router[0].reshape(N_EXPERTS, 1), (N_EXPERTS, LANES))
    b1 = b_mlp1[0]
    b1p = jnp.concatenate([b1[:, 0::2], b1[:, 1::2]], axis=-1).reshape(N_EXPERTS, 1, 2 * D_FF)
    b2 = b_mlp2[0].reshape(N_EXPERTS, 1, D_MODEL)
    w_pg_b = w_ple_gate[0].astype(BF16)
    w_ple_b = w_ple[0].astype(BF16)

    x1p, pool_p, c_p, n_p, m_p = _prompt_mixer(
        x_prompt, w_in_b, w_g_b, gbias, w_pool_b, pscale, mhg, w_out_b, ln1g, ln1b)
    x1s, pool_s, c_s, n_s, m_s = _sample_mixer(
        x_sample.reshape(DEC_BATCH, D_MODEL),
        state_pool[0].reshape(DEC_BATCH, POOL_HIST * POOL_WIDTH),
        state_mlstm_C[0], state_mlstm_n[0].reshape(DEC_BATCH, MLSTM_WIDTH), state_mlstm_m[0],
        w_in_b, w_g_b, gbias, w_pool_b, pscale, mhg, w_out_b, ln1g, ln1b)

    slots, gates, nch, sorted_rows = _route(x1p, x1s, w_router_t, b_router_col)
    tables = _block_tables(nch[:, :, 0].astype(I32))
    expert_out = _experts(*tables, sorted_rows, w_mlp1[0], b1p, w_mlp2[0], b2)

    pp = p_prompt[0].reshape(N_PROMPT, PLE_DIM)
    ps = jnp.pad(p_sample[0].reshape(DEC_BATCH, PLE_DIM), ((0, TOK_TILE - DEC_BATCH), (0, 0)))
    yp, ys = _combine(slots, gates, x1p, x1s, pp, ps, expert_out, ln2g, ln2b, w_pg_b, w_ple_b)

    return (
        yp.reshape(BATCH, SEQ, D_MODEL),
        ys.reshape(DEC_BATCH, 1, D_MODEL),
        pool_p.reshape(1, BATCH, POOL_HIST, POOL_WIDTH),
        c_p.reshape(1, BATCH, HEADS, HEAD_DIM, HEAD_DIM),
        n_p.reshape(1, BATCH, HEADS, HEAD_DIM),
        m_p[:, 0:HEADS, 0].reshape(1, BATCH, HEADS),
        pool_s.reshape(1, DEC_BATCH, POOL_HIST, POOL_WIDTH),
        c_s.reshape(1, DEC_BATCH, HEADS, HEAD_DIM, HEAD_DIM),
        n_s.reshape(1, DEC_BATCH, HEADS, HEAD_DIM),
        m_s[:, HEADS:2 * HEADS].reshape(1, DEC_BATCH, HEADS),
    )
```

```python
import jax
import jax.numpy as jnp
from jax import lax
from jax.experimental import pallas as pl
from jax.experimental.pallas import tpu as pltpu

F32 = jnp.float32
BF16 = jnp.bfloat16
I32 = jnp.int32

D_MODEL = 1024
BATCH = 8
SEQ = 2048
DEC_BATCH = 128
PAST_LEN = 16384
POOL_WIDTH = 512
POOL_GROUPS = 4
POOL_GROUP_DIM = 128
POOL_WINDOWS = (2, 4, 8, 16)
POOL_HIST = 15
MLSTM_WIDTH = 512
HEADS = 4
HEAD_DIM = 128
CHUNK = 128
N_EXPERTS = 32
TOP_K = 4
D_FF = 1024
SWIGLU_ALPHA = 1.702
SWIGLU_LIMIT = 7.0
PLE_DIM = 256
DN_ALPHA = 2.0 ** 0.25
LN_EPS = 1e-5

LANES = 128
SUBLANES = 8
BF16_ROWS = 16
MXU_COLS = 256
VMEM_LIMIT = 56 * 1024 * 1024

MIX_TILE = 256
MIX_SEQS = 2
HIST_PAD = 16
TOK_TILE = 512
N_PROMPT = BATCH * SEQ
N_PROMPT_TILES = N_PROMPT // TOK_TILE
N_TILES = N_PROMPT_TILES + 1
SAMPLE_BT = 16

CHUNK_ROWS = BF16_ROWS
LOCAL_ROWS = TOK_TILE * TOP_K + N_EXPERTS * CHUNK_ROWS
GROUPS = LOCAL_ROWS // TOK_TILE
ROW_BLOCK = 256
BLOCK_CHUNKS = ROW_BLOCK // CHUNK_ROWS
MAX_CHUNKS = N_TILES * (TOK_TILE * TOP_K // CHUNK_ROWS + N_EXPERTS)
N_BLOCKS = -(-MAX_CHUNKS // BLOCK_CHUNKS) + N_EXPERTS
ZERO_CHUNK_ROW = LOCAL_ROWS - CHUNK_ROWS
FREE_ROWS = LOCAL_ROWS - TOK_TILE * TOP_K
DUMP_BASE = N_TILES * LOCAL_ROWS
DUMP_ROWS = 2 * ROW_BLOCK
assert DUMP_ROWS <= FREE_ROWS
WEIGHT_PIECES = 8
PIECES_PER_BLOCK = 2
GATHER_DEPTH = 3


def _dot(a, b):
    return jnp.dot(a, b, preferred_element_type=F32)


def _dot_nt(a, b):
    return lax.dot_general(a, b, (((1,), (1,)), ((), ())), preferred_element_type=F32)


def _dot_tn(a, b):
    return lax.dot_general(a, b, (((0,), (0,)), ((), ())), preferred_element_type=F32)


def _split3(a):
    a0 = a.astype(BF16)
    r1 = a - a0.astype(F32)
    a1 = r1.astype(BF16)
    r2 = r1 - a1.astype(F32)
    return a0, a1, r2.astype(BF16)


def _log_sigmoid(x):
    return jnp.minimum(x, 0.0) - jnp.log1p(jnp.exp(-jnp.abs(x)))


def _layer_norm(x, g, b):
    mu = jnp.mean(x, axis=-1, keepdims=True)
    xc = x - mu
    var = jnp.mean(xc * xc, axis=-1, keepdims=True)
    return xc * lax.rsqrt(var + LN_EPS) * g + b


def _gate_values(g, gbias):
    lane = lax.broadcasted_iota(I32, g.shape, 1)
    z = g + gbias
    return jnp.where(lane < HEADS, z, _log_sigmoid(z))


def _head_out(hh, o_h, gain):
    mu = jnp.mean(hh, axis=-1, keepdims=True)
    hc = hh - mu
    var = jnp.mean(hc * hc, axis=-1, keepdims=True)
    return jax.nn.sigmoid(o_h) * (hc * lax.rsqrt(var + LN_EPS) * gain)


def _prompt_mixer_kernel(x_ref, xn_ref, win_ref, wg_ref, gb_ref, wpool_ref, pscale_ref, mhg_ref, wout_ref,
                         ln1g_ref, ln1b_ref,
                         x1_ref, pool_ref, c_out_ref, n_out_ref, m_out_ref,
                         ubuf, mixbuf, pbuf, gbuf, c_s, n_s, m_s):
    ti = pl.program_id(1)
    nt = pl.num_programs(1)
    TT = MIX_TILE
    S = MIX_SEQS

    @pl.when(ti == 0)
    def _():
        for s in range(S):
            ubuf[s, 0:HIST_PAD, :] = jnp.zeros((HIST_PAD, POOL_WIDTH), F32)
        c_s[...] = jnp.zeros_like(c_s)
        n_s[...] = jnp.zeros_like(n_s)
        m_s[...] = jnp.zeros_like(m_s)

    x = jnp.concatenate([x_ref[s, 0] for s in range(S)], axis=0)
    step = pl.program_id(0) * nt + ti
    cur = lax.rem(step, 2)
    nxt = 1 - cur
    n_main = POOL_WIDTH + 4 * MLSTM_WIDTH

    @pl.when(step == 0)
    def _():
        xb0 = x.astype(BF16)
        pbuf[0] = _dot(xb0, win_ref[...])
        gbuf[0] = _dot(xb0, wg_ref[...])

    xnb = jnp.concatenate([xn_ref[s, 0] for s in range(S)], axis=0).astype(BF16)

    def slab(j):
        def run():
            pbuf[nxt, :, j * MXU_COLS:(j + 1) * MXU_COLS] = _dot(xnb, win_ref[:, j * MXU_COLS:(j + 1) * MXU_COLS])
        return run

    def gate_slab():
        gbuf[nxt] = _dot(xnb, wg_ref[...])

    pending = [slab(j) for j in range(n_main // MXU_COLS)] + [gate_slab]

    def ahead(n=1):
        for _ in range(n):
            if pending:
                pending.pop(0)()

    proj = pbuf.at[cur]
    g = gbuf[cur]

    L = CHUNK
    row = lax.broadcasted_iota(I32, (L, L), 0)
    col = lax.broadcasted_iota(I32, (L, L), 1)
    causal = row >= col
    tril = jnp.where(causal, 1.0, 0.0).astype(BF16)
    pos = ti * TT + lax.broadcasted_iota(I32, (TT, 1), 0)

    for s in range(S):
        base = s * TT
        u = proj[base:base + TT, 0:POOL_WIDTH]

        ubuf[s, HIST_PAD:HIST_PAD + TT, :] = u
        for gi, w in enumerate(POOL_WINDOWS):
            sl = slice(gi * POOL_GROUP_DIM, (gi + 1) * POOL_GROUP_DIM)
            ug = u[:, sl]
            acc = ug
            for i in range(1, w):
                acc = acc + ubuf[s, HIST_PAD - i:HIST_PAD - i + TT, sl]
            cnt = jnp.minimum(pos + 1, w).astype(F32)
            z = acc / cnt - ug
            mixbuf[base:base + TT, sl] = _dot(z.astype(BF16), wpool_ref[gi]) * pscale_ref[:, sl]

        @pl.when(ti == nt - 1)
        def _():
            pool_ref[s, 0] = ubuf[s, TT + 1:TT + HIST_PAD, :]

        ubuf[s, 0:HIST_PAD, :] = ubuf[s, TT:TT + HIST_PAD, :]

    NC = TT // L
    chains = [(s, h) for s in range(S) for h in range(HEADS)]
    units = [(s, c, h) for c in range(NC) for s in range(S) for h in range(HEADS)]
    U = range(len(units))

    def rows(s, c):
        return slice(s * TT + c * L, s * TT + (c + 1) * L)

    def head_cols(part, h):
        return slice(part * POOL_WIDTH + h * HEAD_DIM, part * POOL_WIDTH + (h + 1) * HEAD_DIM)

    gate, cum, gate_t, cum_t = {}, {}, {}, {}
    for c in range(NC):
        for s in range(S):
            val = _gate_values(g[rows(s, c), :], gb_ref[...])
            v0, v1, v2 = _split3(val)
            gate[s, c] = val
            cum[s, c] = _dot(tril, v0) + _dot(tril, v1) + _dot(tril, v2)
    for key in gate:
        gate_t[key] = gate[key].T
        cum_t[key] = cum[key].T
    ahead()
    qf = [proj[rows(s, c), head_cols(1, h)] for s, c, h in units]
    kf = [proj[rows(s, c), head_cols(2, h)] * (HEAD_DIM ** -0.5) for s, c, h in units]
    vf = [proj[rows(s, c), head_cols(3, h)] for s, c, h in units]
    qb = [a.astype(BF16) for a in qf]
    kb = [a.astype(BF16) for a in kf]
    f_col = [cum[s, c][:, HEADS + h:HEADS + h + 1] for s, c, h in units]
    ahead()
    log_d = [jnp.where(causal, f_col[u] - cum_t[s, c][HEADS + h:HEADS + h + 1, :] + gate_t[s, c][h:h + 1, :],
                       -jnp.inf) for u, (s, c, h) in enumerate(units)]
    ahead()
    row_max = [jnp.max(log_d[u], axis=-1, keepdims=True) for u in U]
    ahead()
    qk_raw = [_dot_nt(qb[u], kb[u]) for u in U]

    m_prev, m_t, inter = [None] * len(units), [None] * len(units), [None] * len(units)
    m_run = {(s, h): m_s[s, h:h + 1, 0:1] for s, h in chains}
    for u, (s, c, h) in enumerate(units):
        m_prev[u] = m_run[s, h]
        inter[u] = m_prev[u] + f_col[u]
        m_t[u] = jnp.maximum(inter[u], row_max[u])
        m_run[s, h] = m_t[u][L - 1:L, :]
    m_new = [m_t[u][L - 1:L, :] for u in U]

    ahead()
    dw = [jnp.exp(log_d[u] - m_t[u]) for u in U]
    sc = [jnp.exp(inter[u] - m_t[u]) for u in U]
    ahead()
    qk = [qk_raw[u] * dw[u] for u in U]
    ahead()
    intra = [_dot(qk[u].astype(BF16), vf[u].astype(BF16)) for u in U]
    ahead()
    row_sum = [jnp.sum(qk[u], axis=-1, keepdims=True) for u in U]
    floor = [jnp.exp(-m_t[u]) for u in U]
    f_last = [f_col[u][L - 1:L, :] for u in U]
    ahead()
    wk = [jnp.exp(gate[s, c][:, h:h + 1] + f_last[u] - f_col[u] - m_new[u]) for u, (s, c, h) in enumerate(units)]
    decay = [jnp.exp(m_prev[u] + f_last[u] - m_new[u]) for u in U]
    ahead()
    upd = [_dot_tn((vf[u] * wk[u]).astype(BF16), kb[u]) for u in U]
    ahead()
    n_upd = [jnp.sum(wk[u] * kf[u], axis=0, keepdims=True) for u in U]

    c_run = {(s, h): c_s[s, h] for s, h in chains}
    n_run = {(s, h): n_s[s, h:h + 1, :] for s, h in chains}
    hh = [None] * len(units)
    for c in range(NC):
        cu = [u for u in U if units[u][1] == c]
        inter_term = {u: _dot_nt(qb[u], c_run[units[u][0], units[u][2]].astype(BF16)) for u in cu}
        n_term = {u: jnp.sum(qf[u] * n_run[units[u][0], units[u][2]], axis=-1, keepdims=True) for u in cu}
        for u in cu:
            s, _, h = units[u]
            num = intra[u] + sc[u] * inter_term[u]
            den = row_sum[u] + sc[u] * n_term[u]
            hh[u] = num / jnp.maximum(jnp.abs(den), floor[u])
            c_run[s, h] = decay[u] * c_run[s, h] + upd[u]
            n_run[s, h] = decay[u] * n_run[s, h] + n_upd[u]
    ahead(len(pending))
    for s, h in chains:
        c_s[s, h] = c_run[s, h]
        n_s[s, h:h + 1, :] = n_run[s, h]
        m_s[s, h:h + 1, :] = jnp.broadcast_to(m_run[s, h], (1, LANES))
    for u, (s, c, h) in enumerate(units):
        mixbuf[rows(s, c), head_cols(1, h)] = _head_out(
            hh[u], proj[rows(s, c), head_cols(4, h)], mhg_ref[:, h * HEAD_DIM:(h + 1) * HEAD_DIM])

    @pl.when(ti == nt - 1)
    def _():
        for s in range(S):
            c_out_ref[s, 0] = c_s[s]
            n_out_ref[s, 0] = n_s[s, 0:HEADS, :]
            m_out_ref[s, 0] = m_s[s]

    mix = _dot(mixbuf[...].astype(BF16), wout_ref[...])
    x1 = _layer_norm(DN_ALPHA * x + mix, ln1g_ref[...], ln1b_ref[...])
    for s in range(S):
        x1_ref[s] = x1[s * TT:(s + 1) * TT, :]


def _prompt_mixer(x, w_in_b, w_g_b, gbias, w_pool_b, pscale, mhg, w_out_b, ln1g, ln1b):
    nt = SEQ // MIX_TILE
    S = MIX_SEQS
    G = BATCH // S
    const2 = lambda b, t: (0, 0)
    const3 = lambda b, t: (0, 0, 0)

    def next_tile(b, t):
        nxt = jnp.minimum(b * nt + t + 1, G * nt - 1)
        return (0, nxt // nt, nxt % nt, 0)

    outs = pl.pallas_call(
        _prompt_mixer_kernel,
        grid=(G, nt),
        in_specs=[
            pl.BlockSpec((S, 1, MIX_TILE, D_MODEL), lambda b, t: (0, b, t, 0)),
            pl.BlockSpec((S, 1, MIX_TILE, D_MODEL), next_tile),
            pl.BlockSpec(w_in_b.shape, const2),
            pl.BlockSpec(w_g_b.shape, const2),
            pl.BlockSpec(gbias.shape, const2),
            pl.BlockSpec(w_pool_b.shape, const3),
            pl.BlockSpec(pscale.shape, const2),
            pl.BlockSpec(mhg.shape, const2),
            pl.BlockSpec(w_out_b.shape, const2),
            pl.BlockSpec(ln1g.shape, const2),
            pl.BlockSpec(ln1b.shape, const2),
        ],
        out_specs=[
            pl.BlockSpec((S, MIX_TILE, D_MODEL), lambda b, t: (0, b * nt + t, 0)),
            pl.BlockSpec((S, 1, POOL_HIST, POOL_WIDTH), lambda b, t: (0, b, 0, 0)),
            pl.BlockSpec((S, 1, HEADS, HEAD_DIM, HEAD_DIM), lambda b, t: (0, b, 0, 0, 0)),
            pl.BlockSpec((S, 1, HEADS, HEAD_DIM), lambda b, t: (0, b, 0, 0)),
            pl.BlockSpec((S, 1, SUBLANES, LANES), lambda b, t: (0, b, 0, 0)),
        ],
        out_shape=[
            jax.ShapeDtypeStruct((S, G * SEQ, D_MODEL), F32),
            jax.ShapeDtypeStruct((S, G, POOL_HIST, POOL_WIDTH), F32),
            jax.ShapeDtypeStruct((S, G, HEADS, HEAD_DIM, HEAD_DIM), F32),
            jax.ShapeDtypeStruct((S, G, HEADS, HEAD_DIM), F32),
            jax.ShapeDtypeStruct((S, G, SUBLANES, LANES), F32),
        ],
        scratch_shapes=[
            pltpu.VMEM((S, HIST_PAD + MIX_TILE, POOL_WIDTH), F32),
            pltpu.VMEM((S * MIX_TILE, D_MODEL), F32),
            pltpu.VMEM((2, S * MIX_TILE, POOL_WIDTH + 4 * MLSTM_WIDTH), F32),
            pltpu.VMEM((2, S * MIX_TILE, LANES), F32),
            pltpu.VMEM((S, HEADS, HEAD_DIM, HEAD_DIM), F32),
            pltpu.VMEM((S, SUBLANES, HEAD_DIM), F32),
            pltpu.VMEM((S, SUBLANES, LANES), F32),
        ],
        compiler_params=pltpu.CompilerParams(
            dimension_semantics=("arbitrary", "arbitrary"), vmem_limit_bytes=VMEM_LIMIT),
        name="prompt_mixer",
    )(x.reshape(S, G, SEQ, D_MODEL), x.reshape(S, G, SEQ, D_MODEL), w_in_b, w_g_b, gbias, w_pool_b, pscale, mhg, w_out_b, ln1g, ln1b)
    x1, pool, c, n, m = outs
    return (x1.reshape(N_PROMPT, D_MODEL), pool.reshape(BATCH, POOL_HIST, POOL_WIDTH),
            c.reshape(BATCH, HEADS, HEAD_DIM, HEAD_DIM), n.reshape(BATCH, HEADS, HEAD_DIM),
            m.reshape(BATCH, SUBLANES, LANES))


def _sample_mixer_kernel(x_ref, hist_ref, c_ref, n_ref, m_ref, win_ref, wg_ref, gb_ref, wpool_ref,
                         pscale_ref, mhg_ref, wout_ref, ln1g_ref, ln1b_ref,
                         x1_ref, pool_out_ref, c_out_ref, n_out_ref, m_out_ref,
                         q_s, k_s, vw_s, v_s, o_s, mixbuf, h_s, coef_s):
    i = pl.program_id(0)
    nsteps = pl.num_programs(0)
    B = DEC_BATCH

    @pl.when(i == 0)
    def _():
        x = x_ref[...]
        xb = x.astype(BF16)
        proj = _dot(xb, win_ref[...])
        g = _dot(xb, wg_ref[...])
        u = proj[:, 0:POOL_WIDTH]
        for gi, w in enumerate(POOL_WINDOWS):
            sl = slice(gi * POOL_GROUP_DIM, (gi + 1) * POOL_GROUP_DIM)
            ug = u[:, sl]
            s = ug
            for j in range(1, w):
                r = POOL_HIST - j
                s = s + hist_ref[:, r * POOL_WIDTH + gi * POOL_GROUP_DIM:r * POOL_WIDTH + (gi + 1) * POOL_GROUP_DIM]
            cnt = float(min(PAST_LEN + 1, w))
            z = s / cnt - ug
            mixbuf[:, sl] = _dot(z.astype(BF16), wpool_ref[gi]) * pscale_ref[:, sl]
        pool_out_ref[:, 0:(POOL_HIST - 1) * POOL_WIDTH] = hist_ref[:, POOL_WIDTH:POOL_HIST * POOL_WIDTH]
        pool_out_ref[:, (POOL_HIST - 1) * POOL_WIDTH:POOL_HIST * POOL_WIDTH] = u

        val = _gate_values(g, gb_ref[...])
        lane = lax.broadcasted_iota(I32, (B, LANES), 1)
        qk_all = jnp.zeros((B, LANES), F32)
        sc_all = jnp.zeros((B, LANES), F32)
        den_all = jnp.zeros((B, LANES), F32)
        floor_all = jnp.zeros((B, LANES), F32)
        m_all = jnp.zeros((B, LANES), F32)
        for h in range(HEADS):
            hs = slice(h * HEAD_DIM, (h + 1) * HEAD_DIM)
            qf = proj[:, POOL_WIDTH + h * HEAD_DIM:POOL_WIDTH + (h + 1) * HEAD_DIM]
            kf = proj[:, 2 * POOL_WIDTH + h * HEAD_DIM:2 * POOL_WIDTH + (h + 1) * HEAD_DIM] * (HEAD_DIM ** -0.5)
            vf = proj[:, 3 * POOL_WIDTH + h * HEAD_DIM:3 * POOL_WIDTH + (h + 1) * HEAD_DIM]
            ig = val[:, h:h + 1]
            lf = val[:, HEADS + h:HEADS + h + 1]
            m0 = m_ref[:, h:h + 1]
            n0 = n_ref[:, hs]
            inter = m0 + lf
            m_t = jnp.maximum(inter, ig)
            dw = jnp.exp(ig - m_t)
            sc = jnp.exp(inter - m_t)
            qk = jnp.sum(qf * kf, axis=-1, keepdims=True) * dw
            den = qk + sc * jnp.sum(qf * n0, axis=-1, keepdims=True)
            n_out_ref[:, hs] = sc * n0 + dw * kf
            q_s[0:B, hs] = qf
            k_s[0:B, hs] = kf
            v_s[0:B, hs] = vf
            vw_s[0:B, hs] = vf * dw
            sel = lane == h
            qk_all = jnp.where(sel, qk, qk_all)
            sc_all = jnp.where(sel, sc, sc_all)
            den_all = jnp.where(sel, den, den_all)
            floor_all = jnp.where(sel, jnp.exp(-m_t), floor_all)
            m_all = jnp.where(lane == HEADS + h, m_t, m_all)
        o_s[...] = proj[:, 4 * POOL_WIDTH:5 * POOL_WIDTH]
        coef_s[0] = qk_all
        coef_s[1] = sc_all
        coef_s[2] = den_all
        coef_s[3] = floor_all
        m_out_ref[...] = m_all

    rows = pl.ds(pl.multiple_of(i * SAMPLE_BT, SAMPLE_BT), SAMPLE_BT)
    q_t, k_t, v_t, vw_t = q_s[rows, :], k_s[rows, :], v_s[rows, :], vw_s[rows, :]
    qk_t, sc_t, den_t, floor_t = coef_s[0, rows, :], coef_s[1, rows, :], coef_s[2, rows, :], coef_s[3, rows, :]
    h_rows = []
    for bl in range(SAMPLE_BT):
        heads = []
        for h in range(HEADS):
            hs = slice(h * HEAD_DIM, (h + 1) * HEAD_DIM)
            c_prev = c_ref[bl, h]
            q8 = jnp.broadcast_to(q_t[bl:bl + 1, hs], (SUBLANES, HEAD_DIM))
            cq = _dot_nt(q8.astype(BF16), c_prev.astype(BF16))[0:1, :]
            qk = qk_t[bl:bl + 1, h:h + 1]
            sc = sc_t[bl:bl + 1, h:h + 1]
            num = qk * v_t[bl:bl + 1, hs] + sc * cq
            heads.append(num / jnp.maximum(jnp.abs(den_t[bl:bl + 1, h:h + 1]), floor_t[bl:bl + 1, h:h + 1]))
            v_col = jnp.broadcast_to(vw_t[bl:bl + 1, hs], (HEAD_DIM, HEAD_DIM)).T
            c_out_ref[bl, h] = sc * c_prev + v_col * k_t[bl:bl + 1, hs]
        h_rows.append(jnp.concatenate(heads, axis=1))
    h_s[rows, :] = jnp.concatenate(h_rows, axis=0)

    @pl.when(i == nsteps - 1)
    def _():
        for h in range(HEADS):
            hs = slice(h * HEAD_DIM, (h + 1) * HEAD_DIM)
            mixbuf[:, POOL_WIDTH + h * HEAD_DIM:POOL_WIDTH + (h + 1) * HEAD_DIM] = _head_out(
                h_s[:, hs], o_s[:, hs], mhg_ref[:, hs])
        mix = _dot(mixbuf[...].astype(BF16), wout_ref[...])
        x1 = _layer_norm(DN_ALPHA * x_ref[...] + mix, ln1g_ref[...], ln1b_ref[...])
        x1_ref[0:B, :] = x1
        x1_ref[B:TOK_TILE, :] = jnp.zeros((TOK_TILE - B, D_MODEL), F32)


def _sample_mixer(x, hist2, c0, n0, m0, w_in_b, w_g_b, gbias, w_pool_b, pscale, mhg, w_out_b, ln1g, ln1b):
    B = DEC_BATCH
    steps = B // SAMPLE_BT
    full = lambda a: pl.BlockSpec(a.shape, lambda i: (0,) * a.ndim)
    c_spec = pl.BlockSpec((SAMPLE_BT, HEADS, HEAD_DIM, HEAD_DIM), lambda i: (i, 0, 0, 0))
    return pl.pallas_call(
        _sample_mixer_kernel,
        grid=(steps,),
        in_specs=[full(x), full(hist2), c_spec, full(n0), full(m0), full(w_in_b), full(w_g_b), full(gbias),
                  full(w_pool_b), full(pscale), full(mhg), full(w_out_b), full(ln1g), full(ln1b)],
        out_specs=[
            pl.BlockSpec((TOK_TILE, D_MODEL), lambda i: (0, 0)),
            pl.BlockSpec((B, POOL_HIST * POOL_WIDTH), lambda i: (0, 0)),
            c_spec,
            pl.BlockSpec((B, MLSTM_WIDTH), lambda i: (0, 0)),
            pl.BlockSpec((B, LANES), lambda i: (0, 0)),
        ],
        out_shape=[
            jax.ShapeDtypeStruct((TOK_TILE, D_MODEL), F32),
            jax.ShapeDtypeStruct((B, POOL_HIST * POOL_WIDTH), F32),
            jax.ShapeDtypeStruct((B, HEADS, HEAD_DIM, HEAD_DIM), F32),
            jax.ShapeDtypeStruct((B, MLSTM_WIDTH), F32),
            jax.ShapeDtypeStruct((B, LANES), F32),
        ],
        scratch_shapes=[
            pltpu.VMEM((B, MLSTM_WIDTH), F32),
            pltpu.VMEM((B, MLSTM_WIDTH), F32),
            pltpu.VMEM((B, MLSTM_WIDTH), F32),
            pltpu.VMEM((B, MLSTM_WIDTH), F32),
            pltpu.VMEM((B, MLSTM_WIDTH), F32),
            pltpu.VMEM((B, D_MODEL), F32),
            pltpu.VMEM((B, MLSTM_WIDTH), F32),
            pltpu.VMEM((4, B, LANES), F32),
        ],
        compiler_params=pltpu.CompilerParams(
            dimension_semantics=("arbitrary",), vmem_limit_bytes=VMEM_LIMIT),
        name="sample_mixer",
    )(x, hist2, c0, n0, m0, w_in_b, w_g_b, gbias, w_pool_b, pscale, mhg, w_out_b, ln1g, ln1b)


def _pick_tile(i, prompt_ref, sample_ref):
    return jnp.where(i < N_PROMPT_TILES, prompt_ref[...], sample_ref[...])


def _placement(slot_rows, group):
    r = group * TOK_TILE + lax.broadcasted_iota(I32, (TOK_TILE, TOK_TILE), 0)
    return [r == s for s in slot_rows]


def _route_kernel(xp_ref, xs_ref, wrt_ref, br_ref, slot_ref, gate_ref, nch_ref, sorted_ref):
    i = pl.program_id(0)
    T = TOK_TILE
    E = N_EXPERTS

    x = _pick_tile(i, xp_ref, xs_ref)
    xh = x.astype(BF16)
    xl = (x - xh.astype(F32)).astype(BF16)
    w = wrt_ref[...]
    wh = w.astype(BF16)
    wl = (w - wh.astype(F32)).astype(BF16)
    logits = _dot_nt(wh, xh) + (_dot_nt(wh, xl) + _dot_nt(wl, xh)) + br_ref[:, 0:1]

    erow = lax.broadcasted_iota(I32, (E, T), 0).astype(F32)
    work = logits
    vals, sels = [], []
    for _ in range(TOP_K):
        mx = jnp.max(work, axis=0, keepdims=True)
        idx = jnp.min(jnp.where(work == mx, erow, float(E)), axis=0, keepdims=True)
        sel = erow == idx
        work = jnp.where(sel, -jnp.inf, work)
        vals.append(mx)
        sels.append(sel)
    chosen = jnp.logical_or(jnp.logical_or(sels[0], sels[1]), jnp.logical_or(sels[2], sels[3]))
    es = [jnp.exp(v - vals[0]) for v in vals]
    tot = es[0] + es[1] + es[2] + es[3]

    onehot = jnp.where(chosen, 1.0, 0.0)
    trow = lax.broadcasted_iota(I32, (T, T), 0)
    tcol = lax.broadcasted_iota(I32, (T, T), 1)
    before = jnp.where(trow < tcol, 1.0, 0.0).astype(BF16)
    rank = _dot(onehot.astype(BF16), before)
    cnt = jnp.sum(onehot, axis=1, keepdims=True)
    nch = jnp.floor((cnt + (CHUNK_ROWS - 1)) * (1.0 / CHUNK_ROWS))
    lower = jnp.where(lax.broadcasted_iota(I32, (E, E), 0) > lax.broadcasted_iota(I32, (E, E), 1), 1.0, 0.0)
    nch_b = jnp.broadcast_to(nch, (E, LANES))
    seg_start = _dot(lower.astype(BF16), nch_b.astype(BF16))[:, 0:1] * CHUNK_ROWS
    base = seg_start + rank

    r8 = lax.broadcasted_iota(I32, (SUBLANES, T), 0)
    s_out = jnp.zeros((SUBLANES, T), I32)
    g_out = jnp.zeros((SUBLANES, T), F32)
    slot_rows = []
    for j in range(TOP_K):
        slot_j = jnp.sum(jnp.where(sels[j], base, 0.0), axis=0, keepdims=True).astype(I32)
        slot_rows.append(slot_j)
        s_out = jnp.where(r8 == j, slot_j, s_out)
        g_out = jnp.where(r8 == j, es[j] / tot, g_out)
    slot_ref[0] = s_out
    gate_ref[0] = g_out
    nch_ref[0] = nch_b

    for grp in range(GROUPS):
        m = _placement(slot_rows, grp)
        hit = jnp.logical_or(jnp.logical_or(m[0], m[1]), jnp.logical_or(m[2], m[3]))
        place = jnp.where(hit, 1.0, 0.0).astype(BF16)
        sorted_ref[grp * T:(grp + 1) * T, :] = _dot(place, xh).astype(BF16)


def _route(x1p, x1s, w_router_t, b_router_col):
    tile_spec = pl.BlockSpec((1, SUBLANES, TOK_TILE), lambda i: (i, 0, 0))
    return pl.pallas_call(
        _route_kernel,
        grid=(N_TILES,),
        in_specs=[
            pl.BlockSpec((TOK_TILE, D_MODEL), lambda i: (jnp.minimum(i, N_PROMPT_TILES - 1), 0)),
            pl.BlockSpec((TOK_TILE, D_MODEL), lambda i: (0, 0)),
            pl.BlockSpec(w_router_t.shape, lambda i: (0, 0)),
            pl.BlockSpec(b_router_col.shape, lambda i: (0, 0)),
        ],
        out_specs=[tile_spec, tile_spec,
                   pl.BlockSpec((1, N_EXPERTS, LANES), lambda i: (i, 0, 0)),
                   pl.BlockSpec((LOCAL_ROWS, D_MODEL), lambda i: (i, 0))],
        out_shape=[
            jax.ShapeDtypeStruct((N_TILES, SUBLANES, TOK_TILE), I32),
            jax.ShapeDtypeStruct((N_TILES, SUBLANES, TOK_TILE), F32),
            jax.ShapeDtypeStruct((N_TILES, N_EXPERTS, LANES), F32),
            jax.ShapeDtypeStruct((N_TILES * LOCAL_ROWS, D_MODEL), BF16),
        ],
        compiler_params=pltpu.CompilerParams(
            dimension_semantics=("arbitrary",), vmem_limit_bytes=VMEM_LIMIT),
        name="route",
    )(x1p, x1s, w_router_t, b_router_col)


def _expert_kernel(src_ref, dst_ref, bexp_ref, first_ref, next_ref, nused_ref,
                   sorted_hbm, w1_hbm, b1_ref, w2_hbm, b2_ref, out_hbm,
                   w1_stage, w2_stage, w1_b, w2_b, xbuf, obuf, zbuf, wsem, gsem, ssem, zsem):
    nused = nused_ref[0]

    def fetch_piece(e, p):
        r1 = pl.ds(pl.multiple_of(p * (D_MODEL // WEIGHT_PIECES), SUBLANES), D_MODEL // WEIGHT_PIECES)
        r2 = pl.ds(pl.multiple_of(p * (D_FF // WEIGHT_PIECES), SUBLANES), D_FF // WEIGHT_PIECES)
        return (pltpu.make_async_copy(w1_hbm.at[e, r1, :], w1_stage.at[r1, :], wsem.at[0]),
                pltpu.make_async_copy(w2_hbm.at[e, r2, :], w2_stage.at[r2, :], wsem.at[1]))

    def start_pieces(e, lo, hi):
        def body(p, c):
            for cp in fetch_piece(e, p):
                cp.start()
            return c
        lax.fori_loop(lo, hi, body, 0)

    def gather(b, q):
        slot = lax.rem(b, GATHER_DEPTH)
        row = pl.multiple_of(src_ref[b * BLOCK_CHUNKS + q], CHUNK_ROWS)
        return pltpu.make_async_copy(sorted_hbm.at[pl.ds(row, CHUNK_ROWS), :],
                                     xbuf.at[slot, pl.ds(q * CHUNK_ROWS, CHUNK_ROWS), :], gsem.at[slot])

    def scatter(b, q):
        slot = lax.rem(b, 2)
        row = pl.multiple_of(dst_ref[b * BLOCK_CHUNKS + q], CHUNK_ROWS)
        return pltpu.make_async_copy(obuf.at[slot, pl.ds(q * CHUNK_ROWS, CHUNK_ROWS), :],
                                     out_hbm.at[pl.ds(row, CHUNK_ROWS), :], ssem.at[slot])

    def zero_rows(start, n_rows):
        start = pl.multiple_of(start, CHUNK_ROWS)
        return pltpu.make_async_copy(zbuf.at[pl.ds(0, n_rows), :], out_hbm.at[pl.ds(start, n_rows), :], zsem)

    def zero_tail(k):
        return zero_rows(k * LOCAL_ROWS + TOK_TILE * TOP_K, FREE_ROWS)

    zbuf[...] = jnp.zeros_like(zbuf)
    lax.fori_loop(0, N_TILES, lambda k, c: (zero_tail(k).start(), c)[1], 0)
    zero_rows(DUMP_BASE, DUMP_ROWS).start()
    for ahead in range(GATHER_DEPTH - 1):
        @pl.when(ahead < nused)
        def _():
            for q in range(BLOCK_CHUNKS):
                gather(ahead, q).start()
    lax.fori_loop(0, N_TILES, lambda k, c: (zero_tail(k).wait(), c)[1], 0)
    zero_rows(DUMP_BASE, DUMP_ROWS).wait()

    half = MXU_COLS // 2
    k_io = lax.broadcasted_iota(I32, (MXU_COLS, MXU_COLS), 0)
    j_io = lax.broadcasted_iota(I32, (MXU_COLS, MXU_COLS), 1)
    src_col = jnp.where(j_io < half, 2 * j_io, 2 * (j_io - half) + 1)
    perm = jnp.where(k_io == src_col, 1.0, 0.0).astype(BF16)

    def block(i, fetched):
        e = bexp_ref[i]
        slot = lax.rem(i, 2)
        is_first = first_ref[i] == 1

        @pl.when(is_first)
        def _():
            start_pieces(e, fetched, WEIGHT_PIECES)

            def wait_piece(p, c):
                for cp in fetch_piece(e, p):
                    cp.wait()
                return c
            lax.fori_loop(0, WEIGHT_PIECES, wait_piece, 0)
            for c in range(2 * D_FF // MXU_COLS):
                blk = w1_stage[:, c * MXU_COLS:(c + 1) * MXU_COLS].astype(BF16)
                sep = _dot(blk, perm).astype(BF16)
                w1_b[:, c * half:(c + 1) * half] = sep[:, 0:half]
                w1_b[:, D_FF + c * half:D_FF + (c + 1) * half] = sep[:, half:MXU_COLS]
            w2_b[...] = w2_stage[...].astype(BF16)

        fetched = jnp.where(is_first, 0, fetched)

        for q in range(BLOCK_CHUNKS):
            gather(i, q).wait()

        @pl.when(i + GATHER_DEPTH - 1 < nused)
        def _():
            for q in range(BLOCK_CHUNKS):
                gather(i + GATHER_DEPTH - 1, q).start()

        @pl.when(i >= 2)
        def _():
            for q in range(BLOCK_CHUNKS):
                scatter(i - 2, q).wait()

        h = _dot(xbuf[lax.rem(i, GATHER_DEPTH)], w1_b[...]) + b1_ref[e]
        glu = jnp.minimum(h[:, 0:D_FF], SWIGLU_LIMIT)
        lin = jnp.clip(h[:, D_FF:2 * D_FF], -SWIGLU_LIMIT, SWIGLU_LIMIT)
        a = glu * jax.nn.sigmoid(SWIGLU_ALPHA * glu) * (lin + 1.0)
        obuf[slot] = (_dot(a.astype(BF16), w2_b[...]) + b2_ref[e]).astype(BF16)
        for q in range(BLOCK_CHUNKS):
            scatter(i, q).start()

        more = jnp.where(next_ref[i] >= 0, jnp.minimum(fetched + PIECES_PER_BLOCK, WEIGHT_PIECES), fetched)
        start_pieces(next_ref[i], fetched, more)
        return more

    lax.fori_loop(0, nused, block, jnp.int32(0))

    @pl.when(nused >= 2)
    def _():
        for q in range(BLOCK_CHUNKS):
            scatter(nused - 2, q).wait()
    for q in range(BLOCK_CHUNKS):
        scatter(nused - 1, q).wait()


def _experts(chunk_src, chunk_dst, block_expert, block_first, block_next, n_used, sorted_rows, w1, b1p, w2, b2):
    whole3 = lambda i, *_: (0, 0, 0)
    grid_spec = pltpu.PrefetchScalarGridSpec(
        num_scalar_prefetch=6,
        grid=(1,),
        in_specs=[
            pl.BlockSpec(memory_space=pl.ANY),
            pl.BlockSpec(memory_space=pl.ANY),
            pl.BlockSpec(b1p.shape, whole3),
            pl.BlockSpec(memory_space=pl.ANY),
            pl.BlockSpec(b2.shape, whole3),
        ],
        out_specs=pl.BlockSpec(memory_space=pl.ANY),
        scratch_shapes=[
            pltpu.VMEM((D_MODEL, 2 * D_FF), F32),
            pltpu.VMEM((D_FF, D_MODEL), F32),
            pltpu.VMEM((D_MODEL, 2 * D_FF), BF16),
            pltpu.VMEM((D_FF, D_MODEL), BF16),
            pltpu.VMEM((GATHER_DEPTH, ROW_BLOCK, D_MODEL), BF16),
            pltpu.VMEM((2, ROW_BLOCK, D_MODEL), BF16),
            pltpu.VMEM((FREE_ROWS, D_MODEL), BF16),
            pltpu.SemaphoreType.DMA((2,)),
            pltpu.SemaphoreType.DMA((GATHER_DEPTH,)),
            pltpu.SemaphoreType.DMA((2,)),
            pltpu.SemaphoreType.DMA(()),
        ],
    )
    return pl.pallas_call(
        _expert_kernel,
        grid_spec=grid_spec,
        out_shape=jax.ShapeDtypeStruct((DUMP_BASE + DUMP_ROWS, D_MODEL), BF16),
        compiler_params=pltpu.CompilerParams(
            dimension_semantics=("arbitrary",), vmem_limit_bytes=VMEM_LIMIT),
        name="experts",
    )(chunk_src, chunk_dst, block_expert, block_first, block_next, n_used, sorted_rows, w1, b1p, w2, b2)


def _combine_kernel(slot_ref, gate_ref, xp_ref, xs_ref, pp_ref, ps_ref, eo_ref, ln2g_ref, ln2b_ref,
                    wpg_ref, wple_ref, yp_ref, ys_ref):
    i = pl.program_id(0)
    T = TOK_TILE

    slot_rows = [slot_ref[0, j:j + 1, :] for j in range(TOP_K)]
    gate_rows = [gate_ref[0, j:j + 1, :] for j in range(TOP_K)]
    pad = jnp.zeros((LANES - SUBLANES, T), F32)
    slots_t = jnp.concatenate([slot_ref[0].astype(F32), pad], axis=0).T
    slot_cols = [slots_t[:, j:j + 1].astype(I32) for j in range(TOP_K)]

    ff = jnp.zeros((T, D_MODEL), F32)
    for grp in range(GROUPS):
        m = _placement(slot_rows, grp)
        weighted = jnp.where(m[0], gate_rows[0], jnp.where(m[1], gate_rows[1], jnp.where(
            m[2], gate_rows[2], jnp.where(m[3], gate_rows[3], 0.0))))
        g_col = jnp.sum(weighted, axis=1, keepdims=True)
        z = (eo_ref[grp * T:(grp + 1) * T, :].astype(F32) * g_col).astype(BF16)
        r = grp * T + lax.broadcasted_iota(I32, (T, T), 1)
        hit = jnp.logical_or(jnp.logical_or(r == slot_cols[0], r == slot_cols[1]),
                             jnp.logical_or(r == slot_cols[2], r == slot_cols[3]))
        ff = ff + _dot(jnp.where(hit, 1.0, 0.0).astype(BF16), z)

    x1 = _pick_tile(i, xp_ref, xs_ref)
    x2 = _layer_norm(DN_ALPHA * x1 + ff, ln2g_ref[...], ln2b_ref[...])
    p = _pick_tile(i, pp_ref, ps_ref)
    y = x2 + jax.nn.sigmoid(_dot(x2.astype(BF16), wpg_ref[...])) * _dot(p.astype(BF16), wple_ref[...])

    @pl.when(i < N_PROMPT_TILES)
    def _():
        yp_ref[...] = y

    @pl.when(i == N_PROMPT_TILES)
    def _():
        ys_ref[...] = y[0:DEC_BATCH, :]


def _combine(slots, gates, x1p, x1s, pp, ps, expert_out, ln2g, ln2b, w_pg_b, w_ple_b):
    tile_idx = lambda i: (jnp.minimum(i, N_PROMPT_TILES - 1), 0)
    const2 = lambda i: (0, 0)
    return pl.pallas_call(
        _combine_kernel,
        grid=(N_TILES,),
        in_specs=[
            pl.BlockSpec((1, SUBLANES, TOK_TILE), lambda i: (i, 0, 0)),
            pl.BlockSpec((1, SUBLANES, TOK_TILE), lambda i: (i, 0, 0)),
            pl.BlockSpec((TOK_TILE, D_MODEL), tile_idx),
            pl.BlockSpec((TOK_TILE, D_MODEL), const2),
            pl.BlockSpec((TOK_TILE, PLE_DIM), tile_idx),
            pl.BlockSpec((TOK_TILE, PLE_DIM), const2),
            pl.BlockSpec((LOCAL_ROWS, D_MODEL), lambda i: (i, 0)),
            pl.BlockSpec(ln2g.shape, const2),
            pl.BlockSpec(ln2b.shape, const2),
            pl.BlockSpec(w_pg_b.shape, const2),
            pl.BlockSpec(w_ple_b.shape, const2),
        ],
        out_specs=[
            pl.BlockSpec((TOK_TILE, D_MODEL), tile_idx),
            pl.BlockSpec((DEC_BATCH, D_MODEL), const2),
        ],
        out_shape=[
            jax.ShapeDtypeStruct((N_PROMPT, D_MODEL), F32),
            jax.ShapeDtypeStruct((DEC_BATCH, D_MODEL), F32),
        ],
        compiler_params=pltpu.CompilerParams(
            dimension_semantics=("arbitrary",), vmem_limit_bytes=VMEM_LIMIT),
        name="combine",
    )(slots, gates, x1p, x1s, pp, ps, expert_out, ln2g, ln2b, w_pg_b, w_ple_b)


def _block_tables(nch):
    seg_start = (jnp.cumsum(nch, axis=1) - nch) * CHUNK_ROWS
    tot = jnp.sum(nch, axis=0)
    nblk = (tot + BLOCK_CHUNKS - 1) // BLOCK_CHUNKS
    blk_end = jnp.cumsum(nblk)
    blk_start = blk_end - nblk
    n_used = blk_end[-1:].astype(I32)
    blk_ids = jnp.arange(N_BLOCKS, dtype=I32)

    def expert_of(b):
        b = jnp.minimum(b, n_used[0] - 1)
        return jnp.minimum(jnp.sum(blk_end[None, :] <= b[:, None], axis=1), N_EXPERTS - 1).astype(I32)

    block_expert = expert_of(blk_ids)
    in_use = blk_ids < n_used[0]
    is_first = jnp.logical_or(blk_ids == 0, block_expert != expert_of(blk_ids - 1))
    block_first = jnp.logical_and(in_use, is_first).astype(I32)
    next_start = blk_end[block_expert]
    block_next = jnp.where(next_start < n_used[0], expert_of(next_start), -1).astype(I32)

    nch_t = nch.T
    seg_first = (blk_start[:, None] * BLOCK_CHUNKS + jnp.cumsum(nch_t, axis=1) - nch_t).reshape(-1)
    seg_count = nch_t.reshape(-1)
    seg_row = (jnp.arange(N_TILES, dtype=I32)[None, :] * LOCAL_ROWS + seg_start.T).reshape(-1)
    ent = jnp.arange(N_BLOCKS * BLOCK_CHUNKS, dtype=I32)
    d = ent[:, None] - seg_first[None, :]
    inside = jnp.logical_and(d >= 0, d < seg_count[None, :])
    row_plus_1 = jnp.sum(jnp.where(inside, seg_row[None, :] + d * CHUNK_ROWS + 1, 0), axis=1)
    real = row_plus_1 > 0
    row = row_plus_1 - 1
    dump = DUMP_BASE + (((ent // BLOCK_CHUNKS) % 2) * BLOCK_CHUNKS + ent % BLOCK_CHUNKS) * CHUNK_ROWS
    chunk_src = jnp.where(real, row, ZERO_CHUNK_ROW).astype(I32)
    chunk_dst = jnp.where(real, row, dump).astype(I32)
    return chunk_src, chunk_dst, block_expert, block_first, block_next, n_used


def kernel(x_prompt, x_sample, state_pool, state_mlstm_C, state_mlstm_n, state_mlstm_m, p_prompt, p_sample, w_in, b_i, b_f, w_pool, pool_scale, mh_g, w_out, ln1_g, ln1_b, w_router, b_router, w_mlp1, b_mlp1, w_mlp2, b_mlp2, ln2_g, ln2_b, w_ple, w_ple_gate):
    n_main = POOL_WIDTH + 4 * MLSTM_WIDTH
    w_in_b = w_in[0, :, 0:n_main].astype(BF16)
    w_g_b = jnp.pad(w_in[0, :, n_main:], ((0, 0), (0, LANES - 2 * HEADS))).astype(BF16)
    gbias = jnp.pad(jnp.concatenate([b_i[0], b_f[0]]), (0, LANES - 2 * HEADS)).reshape(1, LANES)
    w_pool_b = w_pool[0].astype(BF16)
    pscale = pool_scale[0].reshape(1, POOL_WIDTH)
    mhg = mh_g[0].reshape(1, MLSTM_WIDTH)
    w_out_b = w_out[0].astype(BF16)
    ln1g = ln1_g[0].reshape(1, D_MODEL)
    ln1b = ln1_b[0].reshape(1, D_MODEL)
    ln2g = ln2_g[0].reshape(1, D_MODEL)
    ln2b = ln2_b[0].reshape(1, D_MODEL)
    w_router_t = w_router[0].T
    b_router_col = jnp.broadcast_to(b_router[0].reshape(N_EXPERTS, 1), (N_EXPERTS, LANES))
    b1 = b_mlp1[0]
    b1p = jnp.concatenate([b1[:, 0::2], b1[:, 1::2]], axis=-1).reshape(N_EXPERTS, 1, 2 * D_FF)
    b2 = b_mlp2[0].reshape(N_EXPERTS, 1, D_MODEL)
    w_pg_b = w_ple_gate[0].astype(BF16)
    w_ple_b = w_ple[0].astype(BF16)

    x1p, pool_p, c_p, n_p, m_p = _prompt_mixer(
        x_prompt, w_in_b, w_g_b, gbias, w_pool_b, pscale, mhg, w_out_b, ln1g, ln1b)
    x1s, pool_s, c_s, n_s, m_s = _sample_mixer(
        x_sample.reshape(DEC_BATCH, D_MODEL),
        state_pool[0].reshape(DEC_BATCH, POOL_HIST * POOL_WIDTH),
        state_mlstm_C[0], state_mlstm_n[0].reshape(DEC_BATCH, MLSTM_WIDTH), state_mlstm_m[0],
        w_in_b, w_g_b, gbias, w_pool_b, pscale, mhg, w_out_b, ln1g, ln1b)

    slots, gates, nch, sorted_rows = _route(x1p, x1s, w_router_t, b_router_col)
    tables = _block_tables(nch[:, :, 0].astype(I32))
    expert_out = _experts(*tables, sorted_rows, w_mlp1[0], b1p, w_mlp2[0], b2)

    pp = p_prompt[0].reshape(N_PROMPT, PLE_DIM)
    ps = jnp.pad(p_sample[0].reshape(DEC_BATCH, PLE_DIM), ((0, TOK_TILE - DEC_BATCH), (0, 0)))
    yp, ys = _combine(slots, gates, x1p, x1s, pp, ps, expert_out, ln2g, ln2b, w_pg_b, w_ple_b)

    return (
        yp.reshape(BATCH, SEQ, D_MODEL),
        ys.reshape(DEC_BATCH, 1, D_MODEL),
        pool_p.reshape(1, BATCH, POOL_HIST, POOL_WIDTH),
        c_p.reshape(1, BATCH, HEADS, HEAD_DIM, HEAD_DIM),
        n_p.reshape(1, BATCH, HEADS, HEAD_DIM),
        m_p[:, 0:HEADS, 0].reshape(1, BATCH, HEADS),
        pool_s.reshape(1, DEC_BATCH, POOL_HIST, POOL_WIDTH),
        c_s.reshape(1, DEC_BATCH, HEADS, HEAD_DIM, HEAD_DIM),
        n_s.reshape(1, DEC_BATCH, HEADS, HEAD_DIM),
        m_s[:, HEADS:2 * HEADS].reshape(1, DEC_BATCH, HEADS),
    )
```

```python
import jax
import jax.numpy as jnp
from jax import lax
from jax.experimental import pallas as pl
from jax.experimental.pallas import tpu as pltpu

F32 = jnp.float32
BF16 = jnp.bfloat16
I32 = jnp.int32

D_MODEL = 1024
BATCH = 8
SEQ = 2048
DEC_BATCH = 128
PAST_LEN = 16384
POOL_WIDTH = 512
POOL_GROUPS = 4
POOL_GROUP_DIM = 128
POOL_WINDOWS = (2, 4, 8, 16)
POOL_HIST = 15
MLSTM_WIDTH = 512
HEADS = 4
HEAD_DIM = 128
CHUNK = 128
N_EXPERTS = 32
TOP_K = 4
D_FF = 1024
SWIGLU_ALPHA = 1.702
SWIGLU_LIMIT = 7.0
PLE_DIM = 256
DN_ALPHA = 2.0 ** 0.25
LN_EPS = 1e-5

LANES = 128
SUBLANES = 8
BF16_ROWS = 16
MXU_COLS = 256
VMEM_LIMIT = 56 * 1024 * 1024

MIX_TILE = 256
MIX_SEQS = 2
HIST_PAD = 16
TOK_TILE = 512
N_PROMPT = BATCH * SEQ
N_PROMPT_TILES = N_PROMPT // TOK_TILE
N_TILES = N_PROMPT_TILES + 1
SAMPLE_BT = 16

CHUNK_ROWS = BF16_ROWS
LOCAL_ROWS = TOK_TILE * TOP_K + N_EXPERTS * CHUNK_ROWS
GROUPS = LOCAL_ROWS // TOK_TILE
ROW_BLOCK = 256
BLOCK_CHUNKS = ROW_BLOCK // CHUNK_ROWS
MAX_CHUNKS = N_TILES * (TOK_TILE * TOP_K // CHUNK_ROWS + N_EXPERTS)
N_BLOCKS = -(-MAX_CHUNKS // BLOCK_CHUNKS) + N_EXPERTS
ZERO_CHUNK_ROW = LOCAL_ROWS - CHUNK_ROWS
FREE_ROWS = LOCAL_ROWS - TOK_TILE * TOP_K
DUMP_BASE = N_TILES * LOCAL_ROWS
DUMP_ROWS = 2 * ROW_BLOCK
assert DUMP_ROWS <= FREE_ROWS
WEIGHT_PIECES = 8
PIECES_PER_BLOCK = 2
GATHER_DEPTH = 3


def _dot(a, b):
    return jnp.dot(a, b, preferred_element_type=F32)


def _dot_nt(a, b):
    return lax.dot_general(a, b, (((1,), (1,)), ((), ())), preferred_element_type=F32)


def _dot_tn(a, b):
    return lax.dot_general(a, b, (((0,), (0,)), ((), ())), preferred_element_type=F32)


def _split3(a):
    a0 = a.astype(BF16)
    r1 = a - a0.astype(F32)
    a1 = r1.astype(BF16)
    r2 = r1 - a1.astype(F32)
    return a0, a1, r2.astype(BF16)


def _log_sigmoid(x):
    return jnp.minimum(x, 0.0) - jnp.log1p(jnp.exp(-jnp.abs(x)))


def _layer_norm(x, g, b):
    mu = jnp.mean(x, axis=-1, keepdims=True)
    xc = x - mu
    var = jnp.mean(xc * xc, axis=-1, keepdims=True)
    return xc * lax.rsqrt(var + LN_EPS) * g + b


def _gate_values(g, gbias):
    lane = lax.broadcasted_iota(I32, g.shape, 1)
    z = g + gbias
    return jnp.where(lane < HEADS, z, _log_sigmoid(z))


def _head_out(hh, o_h, gain):
    mu = jnp.mean(hh, axis=-1, keepdims=True)
    hc = hh - mu
    var = jnp.mean(hc * hc, axis=-1, keepdims=True)
    return jax.nn.sigmoid(o_h) * (hc * lax.rsqrt(var + LN_EPS) * gain)


def _prompt_mixer_kernel(x_ref, xn_ref, win_ref, wg_ref, gb_ref, wpool_ref, pscale_ref, mhg_ref, wout_ref,
                         ln1g_ref, ln1b_ref,
                         x1_ref, pool_ref, c_out_ref, n_out_ref, m_out_ref,
                         ubuf, mixbuf, pbuf, gbuf, c_s, n_s, m_s):
    ti = pl.program_id(1)
    nt = pl.num_programs(1)
    TT = MIX_TILE
    S = MIX_SEQS

    @pl.when(ti == 0)
    def _():
        for s in range(S):
            ubuf[s, 0:HIST_PAD, :] = jnp.zeros((HIST_PAD, POOL_WIDTH), F32)
        c_s[...] = jnp.zeros_like(c_s)
        n_s[...] = jnp.zeros_like(n_s)
        m_s[...] = jnp.zeros_like(m_s)

    x = jnp.concatenate([x_ref[s, 0] for s in range(S)], axis=0)
    step = pl.program_id(0) * nt + ti
    cur = lax.rem(step, 2)
    nxt = 1 - cur
    n_main = POOL_WIDTH + 4 * MLSTM_WIDTH

    @pl.when(step == 0)
    def _():
        xb0 = x.astype(BF16)
        pbuf[0] = _dot(xb0, win_ref[...])
        gbuf[0] = _dot(xb0, wg_ref[...])

    xnb = jnp.concatenate([xn_ref[s, 0] for s in range(S)], axis=0).astype(BF16)

    def slab(j):
        def run():
            pbuf[nxt, :, j * MXU_COLS:(j + 1) * MXU_COLS] = _dot(xnb, win_ref[:, j * MXU_COLS:(j + 1) * MXU_COLS])
        return run

    def gate_slab():
        gbuf[nxt] = _dot(xnb, wg_ref[...])

    pending = [slab(j) for j in range(n_main // MXU_COLS)] + [gate_slab]

    def ahead(n=1):
        for _ in range(n):
            if pending:
                pending.pop(0)()

    proj = pbuf.at[cur]
    g = gbuf[cur]

    L = CHUNK
    row = lax.broadcasted_iota(I32, (L, L), 0)
    col = lax.broadcasted_iota(I32, (L, L), 1)
    causal = row >= col
    tril = jnp.where(causal, 1.0, 0.0).astype(BF16)
    pos = ti * TT + lax.broadcasted_iota(I32, (TT, 1), 0)

    for s in range(S):
        base = s * TT
        u = proj[base:base + TT, 0:POOL_WIDTH]

        ubuf[s, HIST_PAD:HIST_PAD + TT, :] = u
        for gi, w in enumerate(POOL_WINDOWS):
            sl = slice(gi * POOL_GROUP_DIM, (gi + 1) * POOL_GROUP_DIM)
            ug = u[:, sl]
            acc = ug
            for i in range(1, w):
                acc = acc + ubuf[s, HIST_PAD - i:HIST_PAD - i + TT, sl]
            cnt = jnp.minimum(pos + 1, w).astype(F32)
            z = acc / cnt - ug
            mixbuf[base:base + TT, sl] = _dot(z.astype(BF16), wpool_ref[gi]) * pscale_ref[:, sl]

        @pl.when(ti == nt - 1)
        def _():
            pool_ref[s, 0] = ubuf[s, TT + 1:TT + HIST_PAD, :]

        ubuf[s, 0:HIST_PAD, :] = ubuf[s, TT:TT + HIST_PAD, :]

    NC = TT // L
    chains = [(s, h) for s in range(S) for h in range(HEADS)]
    units = [(s, c, h) for c in range(NC) for s in range(S) for h in range(HEADS)]
    U = range(len(units))

    def rows(s, c):
        return slice(s * TT + c * L, s * TT + (c + 1) * L)

    def head_cols(part, h):
        return slice(part * POOL_WIDTH + h * HEAD_DIM, part * POOL_WIDTH + (h + 1) * HEAD_DIM)

    gate, cum, gate_t, cum_t = {}, {}, {}, {}
    for c in range(NC):
        for s in range(S):
            val = _gate_values(g[rows(s, c), :], gb_ref[...])
            v0, v1, v2 = _split3(val)
            gate[s, c] = val
            cum[s, c] = _dot(tril, v0) + _dot(tril, v1) + _dot(tril, v2)
    for key in gate:
        gate_t[key] = gate[key].T
        cum_t[key] = cum[key].T
    ahead()
    qf = [proj[rows(s, c), head_cols(1, h)] for s, c, h in units]
    kf = [proj[rows(s, c), head_cols(2, h)] * (HEAD_DIM ** -0.5) for s, c, h in units]
    vf = [proj[rows(s, c), head_cols(3, h)] for s, c, h in units]
    qb = [a.astype(BF16) for a in qf]
    kb = [a.astype(BF16) for a in kf]
    f_col = [cum[s, c][:, HEADS + h:HEADS + h + 1] for s, c, h in units]
    ahead()
    log_d = [jnp.where(causal, f_col[u] - cum_t[s, c][HEADS + h:HEADS + h + 1, :] + gate_t[s, c][h:h + 1, :],
                       -jnp.inf) for u, (s, c, h) in enumerate(units)]
    ahead()
    row_max = [jnp.max(log_d[u], axis=-1, keepdims=True) for u in U]
    ahead()
    qk_raw = [_dot_nt(qb[u], kb[u]) for u in U]

    m_prev, m_t, inter = [None] * len(units), [None] * len(units), [None] * len(units)
    m_run = {(s, h): m_s[s, h:h + 1, 0:1] for s, h in chains}
    for u, (s, c, h) in enumerate(units):
        m_prev[u] = m_run[s, h]
        inter[u] = m_prev[u] + f_col[u]
        m_t[u] = jnp.maximum(inter[u], row_max[u])
        m_run[s, h] = m_t[u][L - 1:L, :]
    m_new = [m_t[u][L - 1:L, :] for u in U]

    ahead()
    dw = [jnp.exp(log_d[u] - m_t[u]) for u in U]
    sc = [jnp.exp(inter[u] - m_t[u]) for u in U]
    ahead()
    qk = [qk_raw[u] * dw[u] for u in U]
    ahead()
    intra = [_dot(qk[u].astype(BF16), vf[u].astype(BF16)) for u in U]
    ahead()
    row_sum = [jnp.sum(qk[u], axis=-1, keepdims=True) for u in U]
    floor = [jnp.exp(-m_t[u]) for u in U]
    f_last = [f_col[u][L - 1:L, :] for u in U]
    ahead()
    wk = [jnp.exp(gate[s, c][:, h:h + 1] + f_last[u] - f_col[u] - m_new[u]) for u, (s, c, h) in enumerate(units)]
    decay = [jnp.exp(m_prev[u] + f_last[u] - m_new[u]) for u in U]
    ahead()
    upd = [_dot_tn((vf[u] * wk[u]).astype(BF16), kb[u]) for u in U]
    ahead()
    n_upd = [jnp.sum(wk[u] * kf[u], axis=0, keepdims=True) for u in U]

    c_run = {(s, h): c_s[s, h] for s, h in chains}
    n_run = {(s, h): n_s[s, h:h + 1, :] for s, h in chains}
    hh = [None] * len(units)
    for c in range(NC):
        cu = [u for u in U if units[u][1] == c]
        inter_term = {u: _dot_nt(qb[u], c_run[units[u][0], units[u][2]].astype(BF16)) for u in cu}
        n_term = {u: jnp.sum(qf[u] * n_run[units[u][0], units[u][2]], axis=-1, keepdims=True) for u in cu}
        for u in cu:
            s, _, h = units[u]
            num = intra[u] + sc[u] * inter_term[u]
            den = row_sum[u] + sc[u] * n_term[u]
            hh[u] = num / jnp.maximum(jnp.abs(den), floor[u])
            c_run[s, h] = decay[u] * c_run[s, h] + upd[u]
            n_run[s, h] = decay[u] * n_run[s, h] + n_upd[u]
    ahead(len(pending))
    for s, h in chains:
        c_s[s, h] = c_run[s, h]
        n_s[s, h:h + 1, :] = n_run[s, h]
        m_s[s, h:h + 1, :] = jnp.broadcast_to(m_run[s, h], (1, LANES))
    for u, (s, c, h) in enumerate(units):
        mixbuf[rows(s, c), head_cols(1, h)] = _head_out(
            hh[u], proj[rows(s, c), head_cols(4, h)], mhg_ref[:, h * HEAD_DIM:(h + 1) * HEAD_DIM])

    @pl.when(ti == nt - 1)
    def _():
        for s in range(S):
            c_out_ref[s, 0] = c_s[s]
            n_out_ref[s, 0] = n_s[s, 0:HEADS, :]
            m_out_ref[s, 0] = m_s[s]

    mix = _dot(mixbuf[...].astype(BF16), wout_ref[...])
    x1 = _layer_norm(DN_ALPHA * x + mix, ln1g_ref[...], ln1b_ref[...])
    for s in range(S):
        x1_ref[s] = x1[s * TT:(s + 1) * TT, :]


def _prompt_mixer(x, w_in_b, w_g_b, gbias, w_pool_b, pscale, mhg, w_out_b, ln1g, ln1b):
    nt = SEQ // MIX_TILE
    S = MIX_SEQS
    G = BATCH // S
    const2 = lambda b, t: (0, 0)
    const3 = lambda b, t: (0, 0, 0)

    def next_tile(b, t):
        nxt = jnp.minimum(b * nt + t + 1, G * nt - 1)
        return (0, nxt // nt, nxt % nt, 0)

    outs = pl.pallas_call(
        _prompt_mixer_kernel,
        grid=(G, nt),
        in_specs=[
            pl.BlockSpec((S, 1, MIX_TILE, D_MODEL), lambda b, t: (0, b, t, 0)),
            pl.BlockSpec((S, 1, MIX_TILE, D_MODEL), next_tile),
            pl.BlockSpec(w_in_b.shape, const2),
            pl.BlockSpec(w_g_b.shape, const2),
            pl.BlockSpec(gbias.shape, const2),
            pl.BlockSpec(w_pool_b.shape, const3),
            pl.BlockSpec(pscale.shape, const2),
            pl.BlockSpec(mhg.shape, const2),
            pl.BlockSpec(w_out_b.shape, const2),
            pl.BlockSpec(ln1g.shape, const2),
            pl.BlockSpec(ln1b.shape, const2),
        ],
        out_specs=[
            pl.BlockSpec((S, MIX_TILE, D_MODEL), lambda b, t: (0, b * nt + t, 0)),
            pl.BlockSpec((S, 1, POOL_HIST, POOL_WIDTH), lambda b, t: (0, b, 0, 0)),
            pl.BlockSpec((S, 1, HEADS, HEAD_DIM, HEAD_DIM), lambda b, t: (0, b, 0, 0, 0)),
            pl.BlockSpec((S, 1, HEADS, HEAD_DIM), lambda b, t: (0, b, 0, 0)),
            pl.BlockSpec((S, 1, SUBLANES, LANES), lambda b, t: (0, b, 0, 0)),
        ],
        out_shape=[
            jax.ShapeDtypeStruct((S, G * SEQ, D_MODEL), F32),
            jax.ShapeDtypeStruct((S, G, POOL_HIST, POOL_WIDTH), F32),
            jax.ShapeDtypeStruct((S, G, HEADS, HEAD_DIM, HEAD_DIM), F32),
            jax.ShapeDtypeStruct((S, G, HEADS, HEAD_DIM), F32),
            jax.ShapeDtypeStruct((S, G, SUBLANES, LANES), F32),
        ],
        scratch_shapes=[
            pltpu.VMEM((S, HIST_PAD + MIX_TILE, POOL_WIDTH), F32),
            pltpu.VMEM((S * MIX_TILE, D_MODEL), F32),
            pltpu.VMEM((2, S * MIX_TILE, POOL_WIDTH + 4 * MLSTM_WIDTH), F32),
            pltpu.VMEM((2, S * MIX_TILE, LANES), F32),
            pltpu.VMEM((S, HEADS, HEAD_DIM, HEAD_DIM), F32),
            pltpu.VMEM((S, SUBLANES, HEAD_DIM), F32),
            pltpu.VMEM((S, SUBLANES, LANES), F32),
        ],
        compiler_params=pltpu.CompilerParams(
            dimension_semantics=("arbitrary", "arbitrary"), vmem_limit_bytes=VMEM_LIMIT),
        name="prompt_mixer",
    )(x.reshape(S, G, SEQ, D_MODEL), x.reshape(S, G, SEQ, D_MODEL), w_in_b, w_g_b, gbias, w_pool_b, pscale, mhg, w_out_b, ln1g, ln1b)
    x1, pool, c, n, m = outs
    return (x1.reshape(N_PROMPT, D_MODEL), pool.reshape(BATCH, POOL_HIST, POOL_WIDTH),
            c.reshape(BATCH, HEADS, HEAD_DIM, HEAD_DIM), n.reshape(BATCH, HEADS, HEAD_DIM),
            m.reshape(BATCH, SUBLANES, LANES))


def _sample_mixer_kernel(x_ref, hist_ref, c_ref, n_ref, m_ref, win_ref, wg_ref, gb_ref, wpool_ref,
                         pscale_ref, mhg_ref, wout_ref, ln1g_ref, ln1b_ref,
                         x1_ref, pool_out_ref, c_out_ref, n_out_ref, m_out_ref,
                         q_s, k_s, vw_s, v_s, o_s, mixbuf, h_s, coef_s):
    i = pl.program_id(0)
    nsteps = pl.num_programs(0)
    B = DEC_BATCH

    @pl.when(i == 0)
    def _():
        x = x_ref[...]
        xb = x.astype(BF16)
        proj = _dot(xb, win_ref[...])
        g = _dot(xb, wg_ref[...])
        u = proj[:, 0:POOL_WIDTH]
        for gi, w in enumerate(POOL_WINDOWS):
            sl = slice(gi * POOL_GROUP_DIM, (gi + 1) * POOL_GROUP_DIM)
            ug = u[:, sl]
            s = ug
            for j in range(1, w):
                r = POOL_HIST - j
                s = s + hist_ref[:, r * POOL_WIDTH + gi * POOL_GROUP_DIM:r * POOL_WIDTH + (gi + 1) * POOL_GROUP_DIM]
            cnt = float(min(PAST_LEN + 1, w))
            z = s / cnt - ug
            mixbuf[:, sl] = _dot(z.astype(BF16), wpool_ref[gi]) * pscale_ref[:, sl]
        pool_out_ref[:, 0:(POOL_HIST - 1) * POOL_WIDTH] = hist_ref[:, POOL_WIDTH:POOL_HIST * POOL_WIDTH]
        pool_out_ref[:, (POOL_HIST - 1) * POOL_WIDTH:POOL_HIST * POOL_WIDTH] = u

        val = _gate_values(g, gb_ref[...])
        lane = lax.broadcasted_iota(I32, (B, LANES), 1)
        qk_all = jnp.zeros((B, LANES), F32)
        sc_all = jnp.zeros((B, LANES), F32)
        den_all = jnp.zeros((B, LANES), F32)
        floor_all = jnp.zeros((B, LANES), F32)
        m_all = jnp.zeros((B, LANES), F32)
        for h in range(HEADS):
            hs = slice(h * HEAD_DIM, (h + 1) * HEAD_DIM)
            qf = proj[:, POOL_WIDTH + h * HEAD_DIM:POOL_WIDTH + (h + 1) * HEAD_DIM]
            kf = proj[:, 2 * POOL_WIDTH + h * HEAD_DIM:2 * POOL_WIDTH + (h + 1) * HEAD_DIM] * (HEAD_DIM ** -0.5)
            vf = proj[:, 3 * POOL_WIDTH + h * HEAD_DIM:3 * POOL_WIDTH + (h + 1) * HEAD_DIM]
            ig = val[:, h:h + 1]
            lf = val[:, HEADS + h:HEADS + h + 1]
            m0 = m_ref[:, h:h + 1]
            n0 = n_ref[:, hs]
            inter = m0 + lf
            m_t = jnp.maximum(inter, ig)
            dw = jnp.exp(ig - m_t)
            sc = jnp.exp(inter - m_t)
            qk = jnp.sum(qf * kf, axis=-1, keepdims=True) * dw
            den = qk + sc * jnp.sum(qf * n0, axis=-1, keepdims=True)
            n_out_ref[:, hs] = sc * n0 + dw * kf
            q_s[0:B, hs] = qf
            k_s[0:B, hs] = kf
            v_s[0:B, hs] = vf
            vw_s[0:B, hs] = vf * dw
            sel = lane == h
            qk_all = jnp.where(sel, qk, qk_all)
            sc_all = jnp.where(sel, sc, sc_all)
            den_all = jnp.where(sel, den, den_all)
            floor_all = jnp.where(sel, jnp.exp(-m_t), floor_all)
            m_all = jnp.where(lane == HEADS + h, m_t, m_all)
        o_s[...] = proj[:, 4 * POOL_WIDTH:5 * POOL_WIDTH]
        coef_s[0] = qk_all
        coef_s[1] = sc_all
        coef_s[2] = den_all
        coef_s[3] = floor_all
        m_out_ref[...] = m_all

    rows = pl.ds(pl.multiple_of(i * SAMPLE_BT, SAMPLE_BT), SAMPLE_BT)
    q_t, k_t, v_t, vw_t = q_s[rows, :], k_s[rows, :], v_s[rows, :], vw_s[rows, :]
    qk_t, sc_t, den_t, floor_t = coef_s[0, rows, :], coef_s[1, rows, :], coef_s[2, rows, :], coef_s[3, rows, :]
    h_rows = []
    for bl in range(SAMPLE_BT):
        heads = []
        for h in range(HEADS):
            hs = slice(h * HEAD_DIM, (h + 1) * HEAD_DIM)
            c_prev = c_ref[bl, h]
            q8 = jnp.broadcast_to(q_t[bl:bl + 1, hs], (SUBLANES, HEAD_DIM))
            cq = _dot_nt(q8.astype(BF16), c_prev.astype(BF16))[0:1, :]
            qk = qk_t[bl:bl + 1, h:h + 1]
            sc = sc_t[bl:bl + 1, h:h + 1]
            num = qk * v_t[bl:bl + 1, hs] + sc * cq
            heads.append(num / jnp.maximum(jnp.abs(den_t[bl:bl + 1, h:h + 1]), floor_t[bl:bl + 1, h:h + 1]))
            v_col = jnp.broadcast_to(vw_t[bl:bl + 1, hs], (HEAD_DIM, HEAD_DIM)).T
            c_out_ref[bl, h] = sc * c_prev + v_col * k_t[bl:bl + 1, hs]
        h_rows.append(jnp.concatenate(heads, axis=1))
    h_s[rows, :] = jnp.concatenate(h_rows, axis=0)

    @pl.when(i == nsteps - 1)
    def _():
        for h in range(HEADS):
            hs = slice(h * HEAD_DIM, (h + 1) * HEAD_DIM)
            mixbuf[:, POOL_WIDTH + h * HEAD_DIM:POOL_WIDTH + (h + 1) * HEAD_DIM] = _head_out(
                h_s[:, hs], o_s[:, hs], mhg_ref[:, hs])
        mix = _dot(mixbuf[...].astype(BF16), wout_ref[...])
        x1 = _layer_norm(DN_ALPHA * x_ref[...] + mix, ln1g_ref[...], ln1b_ref[...])
        x1_ref[0:B, :] = x1
        x1_ref[B:TOK_TILE, :] = jnp.zeros((TOK_TILE - B, D_MODEL), F32)


def _sample_mixer(x, hist2, c0, n0, m0, w_in_b, w_g_b, gbias, w_pool_b, pscale, mhg, w_out_b, ln1g, ln1b):
    B = DEC_BATCH
    steps = B // SAMPLE_BT
    full = lambda a: pl.BlockSpec(a.shape, lambda i: (0,) * a.ndim)
    c_spec = pl.BlockSpec((SAMPLE_BT, HEADS, HEAD_DIM, HEAD_DIM), lambda i: (i, 0, 0, 0))
    return pl.pallas_call(
        _sample_mixer_kernel,
        grid=(steps,),
        in_specs=[full(x), full(hist2), c_spec, full(n0), full(m0), full(w_in_b), full(w_g_b), full(gbias),
                  full(w_pool_b), full(pscale), full(mhg), full(w_out_b), full(ln1g), full(ln1b)],
        out_specs=[
            pl.BlockSpec((TOK_TILE, D_MODEL), lambda i: (0, 0)),
            pl.BlockSpec((B, POOL_HIST * POOL_WIDTH), lambda i: (0, 0)),
            c_spec,
            pl.BlockSpec((B, MLSTM_WIDTH), lambda i: (0, 0)),
            pl.BlockSpec((B, LANES), lambda i: (0, 0)),
        ],
        out_shape=[
            jax.ShapeDtypeStruct((TOK_TILE, D_MODEL), F32),
            jax.ShapeDtypeStruct((B, POOL_HIST * POOL_WIDTH), F32),
            jax.ShapeDtypeStruct((B, HEADS, HEAD_DIM, HEAD_DIM), F32),
            jax.ShapeDtypeStruct((B, MLSTM_WIDTH), F32),
            jax.ShapeDtypeStruct((B, LANES), F32),
        ],
        scratch_shapes=[
            pltpu.VMEM((B, MLSTM_WIDTH), F32),
            pltpu.VMEM((B, MLSTM_WIDTH), F32),
            pltpu.VMEM((B, MLSTM_WIDTH), F32),
            pltpu.VMEM((B, MLSTM_WIDTH), F32),
            pltpu.VMEM((B, MLSTM_WIDTH), F32),
            pltpu.VMEM((B, D_MODEL), F32),
            pltpu.VMEM((B, MLSTM_WIDTH), F32),
            pltpu.VMEM((4, B, LANES), F32),
        ],
        compiler_params=pltpu.CompilerParams(
            dimension_semantics=("arbitrary",), vmem_limit_bytes=VMEM_LIMIT),
        name="sample_mixer",
    )(x, hist2, c0, n0, m0, w_in_b, w_g_b, gbias, w_pool_b, pscale, mhg, w_out_b, ln1g, ln1b)


def _pick_tile(i, prompt_ref, sample_ref):
    return jnp.where(i < N_PROMPT_TILES, prompt_ref[...], sample_ref[...])


def _placement(slot_rows, group):
    r = group * TOK_TILE + lax.broadcasted_iota(I32, (TOK_TILE, TOK_TILE), 0)
    return [r == s for s in slot_rows]


def _route_kernel(xp_ref, xs_ref, wrt_ref, br_ref, slot_ref, gate_ref, nch_ref, sorted_ref):
    i = pl.program_id(0)
    T = TOK_TILE
    E = N_EXPERTS

    x = _pick_tile(i, xp_ref, xs_ref)
    xh = x.astype(BF16)
    xl = (x - xh.astype(F32)).astype(BF16)
    w = wrt_ref[...]
    wh = w.astype(BF16)
    wl = (w - wh.astype(F32)).astype(BF16)
    logits = _dot_nt(wh, xh) + (_dot_nt(wh, xl) + _dot_nt(wl, xh)) + br_ref[:, 0:1]

    erow = lax.broadcasted_iota(I32, (E, T), 0).astype(F32)
    work = logits
    vals, sels = [], []
    for _ in range(TOP_K):
        mx = jnp.max(work, axis=0, keepdims=True)
        idx = jnp.min(jnp.where(work == mx, erow, float(E)), axis=0, keepdims=True)
        sel = erow == idx
        work = jnp.where(sel, -jnp.inf, work)
        vals.append(mx)
        sels.append(sel)
    chosen = jnp.logical_or(jnp.logical_or(sels[0], sels[1]), jnp.logical_or(sels[2], sels[3]))
    es = [jnp.exp(v - vals[0]) for v in vals]
    tot = es[0] + es[1] + es[2] + es[3]

    onehot = jnp.where(chosen, 1.0, 0.0)
    trow = lax.broadcasted_iota(I32, (T, T), 0)
    tcol = lax.broadcasted_iota(I32, (T, T), 1)
    before = jnp.where(trow < tcol, 1.0, 0.0).astype(BF16)
    rank = _dot(onehot.astype(BF16), before)
    cnt = jnp.sum(onehot, axis=1, keepdims=True)
    nch = jnp.floor((cnt + (CHUNK_ROWS - 1)) * (1.0 / CHUNK_ROWS))
    lower = jnp.where(lax.broadcasted_iota(I32, (E, E), 0) > lax.broadcasted_iota(I32, (E, E), 1), 1.0, 0.0)
    nch_b = jnp.broadcast_to(nch, (E, LANES))
    seg_start = _dot(lower.astype(BF16), nch_b.astype(BF16))[:, 0:1] * CHUNK_ROWS
    base = seg_start + rank

    r8 = lax.broadcasted_iota(I32, (SUBLANES, T), 0)
    s_out = jnp.zeros((SUBLANES, T), I32)
    g_out = jnp.zeros((SUBLANES, T), F32)
    slot_rows = []
    for j in range(TOP_K):
        slot_j = jnp.sum(jnp.where(sels[j], base, 0.0), axis=0, keepdims=True).astype(I32)
        slot_rows.append(slot_j)
        s_out = jnp.where(r8 == j, slot_j, s_out)
        g_out = jnp.where(r8 == j, es[j] / tot, g_out)
    slot_ref[0] = s_out
    gate_ref[0] = g_out
    nch_ref[0] = nch_b

    for grp in range(GROUPS):
        m = _placement(slot_rows, grp)
        hit = jnp.logical_or(jnp.logical_or(m[0], m[1]), jnp.logical_or(m[2], m[3]))
        place = jnp.where(hit, 1.0, 0.0).astype(BF16)
        sorted_ref[grp * T:(grp + 1) * T, :] = _dot(place, xh).astype(BF16)


def _route(x1p, x1s, w_router_t, b_router_col):
    tile_spec = pl.BlockSpec((1, SUBLANES, TOK_TILE), lambda i: (i, 0, 0))
    return pl.pallas_call(
        _route_kernel,
        grid=(N_TILES,),
        in_specs=[
            pl.BlockSpec((TOK_TILE, D_MODEL), lambda i: (jnp.minimum(i, N_PROMPT_TILES - 1), 0)),
            pl.BlockSpec((TOK_TILE, D_MODEL), lambda i: (0, 0)),
            pl.BlockSpec(w_router_t.shape, lambda i: (0, 0)),
            pl.BlockSpec(b_router_col.shape, lambda i: (0, 0)),
        ],
        out_specs=[tile_spec, tile_spec,
                   pl.BlockSpec((1, N_EXPERTS, LANES), lambda i: (i, 0, 0)),
                   pl.BlockSpec((LOCAL_ROWS, D_MODEL), lambda i: (i, 0))],
        out_shape=[
            jax.ShapeDtypeStruct((N_TILES, SUBLANES, TOK_TILE), I32),
            jax.ShapeDtypeStruct((N_TILES, SUBLANES, TOK_TILE), F32),
            jax.ShapeDtypeStruct((N_TILES, N_EXPERTS, LANES), F32),
            jax.ShapeDtypeStruct((N_TILES * LOCAL_ROWS, D_MODEL), BF16),
        ],
        compiler_params=pltpu.CompilerParams(
            dimension_semantics=("arbitrary",), vmem_limit_bytes=VMEM_LIMIT),
        name="route",
    )(x1p, x1s, w_router_t, b_router_col)


def _expert_kernel(src_ref, dst_ref, bexp_ref, first_ref, next_ref, nused_ref,
                   sorted_hbm, w1_hbm, b1_ref, w2_hbm, b2_ref, out_hbm,
                   w1_stage, w2_stage, w1_b, w2_b, xbuf, obuf, zbuf, wsem, gsem, ssem, zsem):
    nused = nused_ref[0]

    def fetch_piece(e, p):
        r1 = pl.ds(pl.multiple_of(p * (D_MODEL // WEIGHT_PIECES), SUBLANES), D_MODEL // WEIGHT_PIECES)
        r2 = pl.ds(pl.multiple_of(p * (D_FF // WEIGHT_PIECES), SUBLANES), D_FF // WEIGHT_PIECES)
        return (pltpu.make_async_copy(w1_hbm.at[e, r1, :], w1_stage.at[r1, :], wsem.at[0]),
                pltpu.make_async_copy(w2_hbm.at[e, r2, :], w2_stage.at[r2, :], wsem.at[1]))

    def start_pieces(e, lo, hi):
        def body(p, c):
            for cp in fetch_piece(e, p):
                cp.start()
            return c
        lax.fori_loop(lo, hi, body, 0)

    def gather(b, q):
        slot = lax.rem(b, GATHER_DEPTH)
        row = pl.multiple_of(src_ref[b * BLOCK_CHUNKS + q], CHUNK_ROWS)
        return pltpu.make_async_copy(sorted_hbm.at[pl.ds(row, CHUNK_ROWS), :],
                                     xbuf.at[slot, pl.ds(q * CHUNK_ROWS, CHUNK_ROWS), :], gsem.at[slot])

    def scatter(b, q):
        slot = lax.rem(b, 2)
        row = pl.multiple_of(dst_ref[b * BLOCK_CHUNKS + q], CHUNK_ROWS)
        return pltpu.make_async_copy(obuf.at[slot, pl.ds(q * CHUNK_ROWS, CHUNK_ROWS), :],
                                     out_hbm.at[pl.ds(row, CHUNK_ROWS), :], ssem.at[slot])

    def zero_rows(start, n_rows):
        start = pl.multiple_of(start, CHUNK_ROWS)
        return pltpu.make_async_copy(zbuf.at[pl.ds(0, n_rows), :], out_hbm.at[pl.ds(start, n_rows), :], zsem)

    def zero_tail(k):
        return zero_rows(k * LOCAL_ROWS + TOK_TILE * TOP_K, FREE_ROWS)

    zbuf[...] = jnp.zeros_like(zbuf)
    lax.fori_loop(0, N_TILES, lambda k, c: (zero_tail(k).start(), c)[1], 0)
    zero_rows(DUMP_BASE, DUMP_ROWS).start()
    for ahead in range(GATHER_DEPTH - 1):
        @pl.when(ahead < nused)
        def _():
            for q in range(BLOCK_CHUNKS):
                gather(ahead, q).start()
    lax.fori_loop(0, N_TILES, lambda k, c: (zero_tail(k).wait(), c)[1], 0)
    zero_rows(DUMP_BASE, DUMP_ROWS).wait()

    half = MXU_COLS // 2
    k_io = lax.broadcasted_iota(I32, (MXU_COLS, MXU_COLS), 0)
    j_io = lax.broadcasted_iota(I32, (MXU_COLS, MXU_COLS), 1)
    src_col = jnp.where(j_io < half, 2 * j_io, 2 * (j_io - half) + 1)
    perm = jnp.where(k_io == src_col, 1.0, 0.0).astype(BF16)

    def block(i, fetched):
        e = bexp_ref[i]
        slot = lax.rem(i, 2)
        is_first = first_ref[i] == 1

        @pl.when(is_first)
        def _():
            start_pieces(e, fetched, WEIGHT_PIECES)

            def wait_piece(p, c):
                for cp in fetch_piece(e, p):
                    cp.wait()
                return c
            lax.fori_loop(0, WEIGHT_PIECES, wait_piece, 0)
            for c in range(2 * D_FF // MXU_COLS):
                blk = w1_stage[:, c * MXU_COLS:(c + 1) * MXU_COLS].astype(BF16)
                sep = _dot(blk, perm).astype(BF16)
                w1_b[:, c * half:(c + 1) * half] = sep[:, 0:half]
                w1_b[:, D_FF + c * half:D_FF + (c + 1) * half] = sep[:, half:MXU_COLS]
            w2_b[...] = w2_stage[...].astype(BF16)

        fetched = jnp.where(is_first, 0, fetched)

        for q in range(BLOCK_CHUNKS):
            gather(i, q).wait()

        @pl.when(i + GATHER_DEPTH - 1 < nused)
        def _():
            for q in range(BLOCK_CHUNKS):
                gather(i + GATHER_DEPTH - 1, q).start()

        @pl.when(i >= 2)
        def _():
            for q in range(BLOCK_CHUNKS):
                scatter(i - 2, q).wait()

        h = _dot(xbuf[lax.rem(i, GATHER_DEPTH)], w1_b[...]) + b1_ref[e]
        glu = jnp.minimum(h[:, 0:D_FF], SWIGLU_LIMIT)
        lin = jnp.clip(h[:, D_FF:2 * D_FF], -SWIGLU_LIMIT, SWIGLU_LIMIT)
        a = glu * jax.nn.sigmoid(SWIGLU_ALPHA * glu) * (lin + 1.0)
        obuf[slot] = (_dot(a.astype(BF16), w2_b[...]) + b2_ref[e]).astype(BF16)
        for q in range(BLOCK_CHUNKS):
            scatter(i, q).start()

        more = jnp.where(next_ref[i] >= 0, jnp.minimum(fetched + PIECES_PER_BLOCK, WEIGHT_PIECES), fetched)
        start_pieces(next_ref[i], fetched, more)
        return more

    lax.fori_loop(0, nused, block, jnp.int32(0))

    @pl.when(nused >= 2)
    def _():
        for q in range(BLOCK_CHUNKS):
            scatter(nused - 2, q).wait()
    for q in range(BLOCK_CHUNKS):
        scatter(nused - 1, q).wait()


def _experts(chunk_src, chunk_dst, block_expert, block_first, block_next, n_used, sorted_rows, w1, b1p, w2, b2):
    whole3 = lambda i, *_: (0, 0, 0)
    grid_spec = pltpu.PrefetchScalarGridSpec(
        num_scalar_prefetch=6,
        grid=(1,),
        in_specs=[
            pl.BlockSpec(memory_space=pl.ANY),
            pl.BlockSpec(memory_space=pl.ANY),
            pl.BlockSpec(b1p.shape, whole3),
            pl.BlockSpec(memory_space=pl.ANY),
            pl.BlockSpec(b2.shape, whole3),
        ],
        out_specs=pl.BlockSpec(memory_space=pl.ANY),
        scratch_shapes=[
            pltpu.VMEM((D_MODEL, 2 * D_FF), F32),
            pltpu.VMEM((D_FF, D_MODEL), F32),
            pltpu.VMEM((D_MODEL, 2 * D_FF), BF16),
            pltpu.VMEM((D_FF, D_MODEL), BF16),
            pltpu.VMEM((GATHER_DEPTH, ROW_BLOCK, D_MODEL), BF16),
            pltpu.VMEM((2, ROW_BLOCK, D_MODEL), BF16),
            pltpu.VMEM((FREE_ROWS, D_MODEL), BF16),
            pltpu.SemaphoreType.DMA((2,)),
            pltpu.SemaphoreType.DMA((GATHER_DEPTH,)),
            pltpu.SemaphoreType.DMA((2,)),
            pltpu.SemaphoreType.DMA(()),
        ],
    )
    return pl.pallas_call(
        _expert_kernel,
        grid_spec=grid_spec,
        out_shape=jax.ShapeDtypeStruct((DUMP_BASE + DUMP_ROWS, D_MODEL), BF16),
        compiler_params=pltpu.CompilerParams(
            dimension_semantics=("arbitrary",), vmem_limit_bytes=VMEM_LIMIT),
        name="experts",
    )(chunk_src, chunk_dst, block_expert, block_first, block_next, n_used, sorted_rows, w1, b1p, w2, b2)


def _combine_kernel(slot_ref, gate_ref, xp_ref, xs_ref, pp_ref, ps_ref, eo_ref, ln2g_ref, ln2b_ref,
                    wpg_ref, wple_ref, yp_ref, ys_ref):
    i = pl.program_id(0)
    T = TOK_TILE

    slot_rows = [slot_ref[0, j:j + 1, :] for j in range(TOP_K)]
    gate_rows = [gate_ref[0, j:j + 1, :] for j in range(TOP_K)]
    pad = jnp.zeros((LANES - SUBLANES, T), F32)
    slots_t = jnp.concatenate([slot_ref[0].astype(F32), pad], axis=0).T
    slot_cols = [slots_t[:, j:j + 1].astype(I32) for j in range(TOP_K)]

    ff = jnp.zeros((T, D_MODEL), F32)
    for grp in range(GROUPS):
        m = _placement(slot_rows, grp)
        weighted = jnp.where(m[0], gate_rows[0], jnp.where(m[1], gate_rows[1], jnp.where(
            m[2], gate_rows[2], jnp.where(m[3], gate_rows[3], 0.0))))
        g_col = jnp.sum(weighted, axis=1, keepdims=True)
        z = (eo_ref[grp * T:(grp + 1) * T, :].astype(F32) * g_col).astype(BF16)
        r = grp * T + lax.broadcasted_iota(I32, (T, T), 1)
        hit = jnp.logical_or(jnp.logical_or(r == slot_cols[0], r == slot_cols[1]),
                             jnp.logical_or(r == slot_cols[2], r == slot_cols[3]))
        ff = ff + _dot(jnp.where(hit, 1.0, 0.0).astype(BF16), z)

    x1 = _pick_tile(i, xp_ref, xs_ref)
    x2 = _layer_norm(DN_ALPHA * x1 + ff, ln2g_ref[...], ln2b_ref[...])
    p = _pick_tile(i, pp_ref, ps_ref)
    y = x2 + jax.nn.sigmoid(_dot(x2.astype(BF16), wpg_ref[...])) * _dot(p.astype(BF16), wple_ref[...])

    @pl.when(i < N_PROMPT_TILES)
    def _():
        yp_ref[...] = y

    @pl.when(i == N_PROMPT_TILES)
    def _():
        ys_ref[...] = y[0:DEC_BATCH, :]


def _combine(slots, gates, x1p, x1s, pp, ps, expert_out, ln2g, ln2b, w_pg_b, w_ple_b):
    tile_idx = lambda i: (jnp.minimum(i, N_PROMPT_TILES - 1), 0)
    const2 = lambda i: (0, 0)
    return pl.pallas_call(
        _combine_kernel,
        grid=(N_TILES,),
        in_specs=[
            pl.BlockSpec((1, SUBLANES, TOK_TILE), lambda i: (i, 0, 0)),
            pl.BlockSpec((1, SUBLANES, TOK_TILE), lambda i: (i, 0, 0)),
            pl.BlockSpec((TOK_TILE, D_MODEL), tile_idx),
            pl.BlockSpec((TOK_TILE, D_MODEL), const2),
            pl.BlockSpec((TOK_TILE, PLE_DIM), tile_idx),
            pl.BlockSpec((TOK_TILE, PLE_DIM), const2),
            pl.BlockSpec((LOCAL_ROWS, D_MODEL), lambda i: (i, 0)),
            pl.BlockSpec(ln2g.shape, const2),
            pl.BlockSpec(ln2b.shape, const2),
            pl.BlockSpec(w_pg_b.shape, const2),
            pl.BlockSpec(w_ple_b.shape, const2),
        ],
        out_specs=[
            pl.BlockSpec((TOK_TILE, D_MODEL), tile_idx),
            pl.BlockSpec((DEC_BATCH, D_MODEL), const2),
        ],
        out_shape=[
            jax.ShapeDtypeStruct((N_PROMPT, D_MODEL), F32),
            jax.ShapeDtypeStruct((DEC_BATCH, D_MODEL), F32),
        ],
        compiler_params=pltpu.CompilerParams(
            dimension_semantics=("arbitrary",), vmem_limit_bytes=VMEM_LIMIT),
        name="combine",
    )(slots, gates, x1p, x1s, pp, ps, expert_out, ln2g, ln2b, w_pg_b, w_ple_b)


def _block_tables(nch):
    seg_start = (jnp.cumsum(nch, axis=1) - nch) * CHUNK_ROWS
    tot = jnp.sum(nch, axis=0)
    nblk = (tot + BLOCK_CHUNKS - 1) // BLOCK_CHUNKS
    blk_end = jnp.cumsum(nblk)
    blk_start = blk_end - nblk
    n_used = blk_end[-1:].astype(I32)
    blk_ids = jnp.arange(N_BLOCKS, dtype=I32)

    e_ids = jnp.arange(N_EXPERTS, dtype=I32)
    used = nblk > 0
    blk_clamped = jnp.minimum(blk_ids, n_used[0] - 1)
    block_expert = jnp.minimum(jnp.sum(blk_end[:, None] <= blk_clamped[None, :], axis=0), N_EXPERTS - 1).astype(I32)
    starts_here = jnp.logical_and(blk_start[:, None] == blk_ids[None, :], used[:, None])
    block_first = jnp.any(starts_here, axis=0).astype(I32)
    later_used = jnp.logical_and(e_ids[None, :] > e_ids[:, None], used[None, :])
    next_used = jnp.min(jnp.where(later_used, e_ids[None, :], N_EXPERTS), axis=1)
    next_used = jnp.where(next_used < N_EXPERTS, next_used, -1)
    block_next = jnp.sum(jnp.where(block_expert[None, :] == e_ids[:, None], next_used[:, None], 0), axis=0).astype(I32)

    nch_t = nch.T
    seg_first = (blk_start[:, None] * BLOCK_CHUNKS + jnp.cumsum(nch_t, axis=1) - nch_t).reshape(-1)
    seg_count = nch_t.reshape(-1)
    seg_row = (jnp.arange(N_TILES, dtype=I32)[None, :] * LOCAL_ROWS + seg_start.T).reshape(-1)
    ent = jnp.arange(N_BLOCKS * BLOCK_CHUNKS, dtype=I32)
    d = ent[None, :] - seg_first[:, None]
    inside = jnp.logical_and(d >= 0, d < seg_count[:, None])
    row_plus_1 = jnp.sum(jnp.where(inside, seg_row[:, None] + d * CHUNK_ROWS + 1, 0), axis=0)
    real = row_plus_1 > 0
    row = row_plus_1 - 1
    dump = DUMP_BASE + (((ent // BLOCK_CHUNKS) % 2) * BLOCK_CHUNKS + ent % BLOCK_CHUNKS) * CHUNK_ROWS
    chunk_src = jnp.where(real, row, ZERO_CHUNK_ROW).astype(I32)
    chunk_dst = jnp.where(real, row, dump).astype(I32)
    return chunk_src, chunk_dst, block_expert, block_first, block_next, n_used


def kernel(x_prompt, x_sample, state_pool, state_mlstm_C, state_mlstm_n, state_mlstm_m, p_prompt, p_sample, w_in, b_i, b_f, w_pool, pool_scale, mh_g, w_out, ln1_g, ln1_b, w_router, b_router, w_mlp1, b_mlp1, w_mlp2, b_mlp2, ln2_g, ln2_b, w_ple, w_ple_gate):
    n_main = POOL_WIDTH + 4 * MLSTM_WIDTH
    w_in_b = w_in[0, :, 0:n_main].astype(BF16)
    w_g_b = jnp.pad(w_in[0, :, n_main:], ((0, 0), (0, LANES - 2 * HEADS))).astype(BF16)
    gbias = jnp.pad(jnp.concatenate([b_i[0], b_f[0]]), (0, LANES - 2 * HEADS)).reshape(1, LANES)
    w_pool_b = w_pool[0].astype(BF16)
    pscale = pool_scale[0].reshape(1, POOL_WIDTH)
    mhg = mh_g[0].reshape(1, MLSTM_WIDTH)
    w_out_b = w_out[0].astype(BF16)
    ln1g = ln1_g[0].reshape(1, D_MODEL)
    ln1b = ln1_b[0].reshape(1, D_MODEL)
    ln2g = ln2_g[0].reshape(1, D_MODEL)
    ln2b = ln2_b[0].reshape(1, D_MODEL)
    w_router_t = w_router[0].T
    b_router_col = jnp.broadcast_to(b_router[0].reshape(N_EXPERTS, 1), (N_EXPERTS, LANES))
    b1 = b_mlp1[0]
    b1p = jnp.concatenate([b1[:, 0::2], b1[:, 1::2]], axis=-1).reshape(N_EXPERTS, 1, 2 * D_FF)
    b2 = b_mlp2[0].reshape(N_EXPERTS, 1, D_MODEL)
    w_pg_b = w_ple_gate[0].astype(BF16)
    w_ple_b = w_ple[0].astype(BF16)

    x1p, pool_p, c_p, n_p, m_p = _prompt_mixer(
        x_prompt, w_in_b, w_g_b, gbias, w_pool_b, pscale, mhg, w_out_b, ln1g, ln1b)
    x1s, pool_s, c_s, n_s, m_s = _sample_mixer(
        x_sample.reshape(DEC_BATCH, D_MODEL),
        state_pool[0].reshape(DEC_BATCH, POOL_HIST * POOL_WIDTH),
        state_mlstm_C[0], state_mlstm_n[0].reshape(DEC_BATCH, MLSTM_WIDTH), state_mlstm_m[0],
        w_in_b, w_g_b, gbias, w_pool_b, pscale, mhg, w_out_b, ln1g, ln1b)

    slots, gates, nch, sorted_rows = _route(x1p, x1s, w_router_t, b_router_col)
    tables = _block_tables(nch[:, :, 0].astype(I32))
    expert_out = _experts(*tables, sorted_rows, w_mlp1[0], b1p, w_mlp2[0], b2)

    pp = p_prompt[0].reshape(N_PROMPT, PLE_DIM)
    ps = jnp.pad(p_sample[0].reshape(DEC_BATCH, PLE_DIM), ((0, TOK_TILE - DEC_BATCH), (0, 0)))
    yp, ys = _combine(slots, gates, x1p, x1s, pp, ps, expert_out, ln2g, ln2b, w_pg_b, w_ple_b)

    return (
        yp.reshape(BATCH, SEQ, D_MODEL),
        ys.reshape(DEC_BATCH, 1, D_MODEL),
        pool_p.reshape(1, BATCH, POOL_HIST, POOL_WIDTH),
        c_p.reshape(1, BATCH, HEADS, HEAD_DIM, HEAD_DIM),
        n_p.reshape(1, BATCH, HEADS, HEAD_DIM),
        m_p[:, 0:HEADS, 0].reshape(1, BATCH, HEADS),
        pool_s.reshape(1, DEC_BATCH, POOL_HIST, POOL_WIDTH),
        c_s.reshape(1, DEC_BATCH, HEADS, HEAD_DIM, HEAD_DIM),
        n_s.reshape(1, DEC_BATCH, HEADS, HEAD_DIM),
        m_s[:, HEADS:2 * HEADS].reshape(1, DEC_BATCH, HEADS),
    )
```

```python
import jax
import jax.numpy as jnp
from jax import lax
from jax.experimental import pallas as pl
from jax.experimental.pallas import tpu as pltpu

F32 = jnp.float32
BF16 = jnp.bfloat16
I32 = jnp.int32

D_MODEL = 1024
BATCH = 8
SEQ = 2048
DEC_BATCH = 128
PAST_LEN = 16384
POOL_WIDTH = 512
POOL_GROUP_DIM = 128
POOL_WINDOWS = (2, 4, 8, 16)
POOL_HIST = 15
MLSTM_WIDTH = 512
HEADS = 4
HEAD_DIM = 128
CHUNK = 128
N_EXPERTS = 32
TOP_K = 4
D_FF = 1024
SWIGLU_ALPHA = 1.702
SWIGLU_LIMIT = 7.0
PLE_DIM = 256
DN_ALPHA = 2.0 ** 0.25
LN_EPS = 1e-5

LANES = 128
SUBLANES = 8
BF16_ROWS = 16
MXU_COLS = 256
VMEM_LIMIT = 56 * 1024 * 1024

MIX_TILE = 256
MIX_SEQS = 2
HIST_PAD = 16
TOK_TILE = 512
N_PROMPT = BATCH * SEQ
N_PROMPT_TILES = N_PROMPT // TOK_TILE
N_TILES = N_PROMPT_TILES + 1
SAMPLE_BT = 16

CHUNK_ROWS = BF16_ROWS
LOCAL_ROWS = TOK_TILE * TOP_K + N_EXPERTS * CHUNK_ROWS
GROUPS = LOCAL_ROWS // TOK_TILE
ROW_BLOCK = 256
BLOCK_CHUNKS = ROW_BLOCK // CHUNK_ROWS
MAX_CHUNKS = N_TILES * (TOK_TILE * TOP_K // CHUNK_ROWS + N_EXPERTS)
N_BLOCKS = -(-MAX_CHUNKS // BLOCK_CHUNKS) + N_EXPERTS
ZERO_CHUNK_ROW = LOCAL_ROWS - CHUNK_ROWS
FREE_ROWS = LOCAL_ROWS - TOK_TILE * TOP_K
DUMP_BASE = N_TILES * LOCAL_ROWS
DUMP_ROWS = 2 * ROW_BLOCK
assert DUMP_ROWS <= FREE_ROWS
WEIGHT_PIECES = 8
PIECES_PER_BLOCK = 2
GATHER_DEPTH = 3


def _dot(a, b):
    return jnp.dot(a, b, preferred_element_type=F32)


def _dot_nt(a, b):
    return lax.dot_general(a, b, (((1,), (1,)), ((), ())), preferred_element_type=F32)


def _dot_tn(a, b):
    return lax.dot_general(a, b, (((0,), (0,)), ((), ())), preferred_element_type=F32)


def _split3(a):
    a0 = a.astype(BF16)
    r1 = a - a0.astype(F32)
    a1 = r1.astype(BF16)
    r2 = r1 - a1.astype(F32)
    return a0, a1, r2.astype(BF16)


def _log_sigmoid(x):
    return jnp.minimum(x, 0.0) - jnp.log1p(jnp.exp(-jnp.abs(x)))


def _layer_norm(x, g, b):
    mu = jnp.mean(x, axis=-1, keepdims=True)
    xc = x - mu
    var = jnp.mean(xc * xc, axis=-1, keepdims=True)
    return xc * lax.rsqrt(var + LN_EPS) * g + b


def _gate_values(g, gbias):
    lane = lax.broadcasted_iota(I32, g.shape, 1)
    z = g + gbias
    return jnp.where(lane < HEADS, z, _log_sigmoid(z))


def _head_out(hh, o_h, gain):
    mu = jnp.mean(hh, axis=-1, keepdims=True)
    hc = hh - mu
    var = jnp.mean(hc * hc, axis=-1, keepdims=True)
    return jax.nn.sigmoid(o_h) * (hc * lax.rsqrt(var + LN_EPS) * gain)


def _prompt_mixer_kernel(x_ref, xn_ref, win_ref, wg_ref, gb_ref, wpool_ref, pscale_ref, mhg_ref, wout_ref,
                         ln1g_ref, ln1b_ref,
                         x1_ref, pool_ref, c_out_ref, n_out_ref, m_out_ref,
                         ubuf, mixbuf, pbuf, gbuf, c_s, n_s, m_s):
    ti = pl.program_id(1)
    nt = pl.num_programs(1)
    TT = MIX_TILE
    S = MIX_SEQS

    @pl.when(ti == 0)
    def _():
        for s in range(S):
            ubuf[s, 0:HIST_PAD, :] = jnp.zeros((HIST_PAD, POOL_WIDTH), F32)
        c_s[...] = jnp.zeros_like(c_s)
        n_s[...] = jnp.zeros_like(n_s)
        m_s[...] = jnp.zeros_like(m_s)

    x = jnp.concatenate([x_ref[s, 0] for s in range(S)], axis=0)
    step = pl.program_id(0) * nt + ti
    cur = lax.rem(step, 2)
    nxt = 1 - cur
    n_main = POOL_WIDTH + 4 * MLSTM_WIDTH

    @pl.when(step == 0)
    def _():
        xb0 = x.astype(BF16)
        pbuf[0] = _dot(xb0, win_ref[...])
        gbuf[0] = _dot(xb0, wg_ref[...])

    xnb = jnp.concatenate([xn_ref[s, 0] for s in range(S)], axis=0).astype(BF16)

    def slab(j):
        def run():
            pbuf[nxt, :, j * MXU_COLS:(j + 1) * MXU_COLS] = _dot(xnb, win_ref[:, j * MXU_COLS:(j + 1) * MXU_COLS])
        return run

    def gate_slab():
        gbuf[nxt] = _dot(xnb, wg_ref[...])

    pending = [slab(j) for j in range(n_main // MXU_COLS)] + [gate_slab]

    def ahead(n=1):
        for _ in range(n):
            if pending:
                pending.pop(0)()

    proj = pbuf.at[cur]
    g = gbuf[cur]

    L = CHUNK
    row = lax.broadcasted_iota(I32, (L, L), 0)
    col = lax.broadcasted_iota(I32, (L, L), 1)
    causal = row >= col
    tril = jnp.where(causal, 1.0, 0.0).astype(BF16)
    pos = ti * TT + lax.broadcasted_iota(I32, (TT, 1), 0)

    for s in range(S):
        base = s * TT
        u = proj[base:base + TT, 0:POOL_WIDTH]

        ubuf[s, HIST_PAD:HIST_PAD + TT, :] = u
        for gi, w in enumerate(POOL_WINDOWS):
            sl = slice(gi * POOL_GROUP_DIM, (gi + 1) * POOL_GROUP_DIM)
            ug = u[:, sl]
            acc = ug
            for i in range(1, w):
                acc = acc + ubuf[s, HIST_PAD - i:HIST_PAD - i + TT, sl]
            cnt = jnp.minimum(pos + 1, w).astype(F32)
            z = acc / cnt - ug
            mixbuf[base:base + TT, sl] = _dot(z.astype(BF16), wpool_ref[gi]) * pscale_ref[:, sl]

        @pl.when(ti == nt - 1)
        def _():
            pool_ref[s, 0] = ubuf[s, TT + 1:TT + HIST_PAD, :]

        ubuf[s, 0:HIST_PAD, :] = ubuf[s, TT:TT + HIST_PAD, :]

    NC = TT // L
    chains = [(s, h) for s in range(S) for h in range(HEADS)]
    units = [(s, c, h) for c in range(NC) for s in range(S) for h in range(HEADS)]
    U = range(len(units))

    def rows(s, c):
        return slice(s * TT + c * L, s * TT + (c + 1) * L)

    def head_cols(part, h):
        return slice(part * POOL_WIDTH + h * HEAD_DIM, part * POOL_WIDTH + (h + 1) * HEAD_DIM)

    gate, cum, gate_t, cum_t = {}, {}, {}, {}
    for c in range(NC):
        for s in range(S):
            val = _gate_values(g[rows(s, c), :], gb_ref[...])
            v0, v1, v2 = _split3(val)
            gate[s, c] = val
            cum[s, c] = _dot(tril, v0) + _dot(tril, v1) + _dot(tril, v2)
    for key in gate:
        gate_t[key] = gate[key].T
        cum_t[key] = cum[key].T
    ahead()
    qf = [proj[rows(s, c), head_cols(1, h)] for s, c, h in units]
    kf = [proj[rows(s, c), head_cols(2, h)] * (HEAD_DIM ** -0.5) for s, c, h in units]
    vf = [proj[rows(s, c), head_cols(3, h)] for s, c, h in units]
    qb = [a.astype(BF16) for a in qf]
    kb = [a.astype(BF16) for a in kf]
    f_col = [cum[s, c][:, HEADS + h:HEADS + h + 1] for s, c, h in units]
    ahead()
    log_d = [jnp.where(causal, f_col[u] - cum_t[s, c][HEADS + h:HEADS + h + 1, :] + gate_t[s, c][h:h + 1, :],
                       -jnp.inf) for u, (s, c, h) in enumerate(units)]
    ahead()
    row_max = [jnp.max(log_d[u], axis=-1, keepdims=True) for u in U]
    ahead()
    qk_raw = [_dot_nt(qb[u], kb[u]) for u in U]

    m_prev, m_t, inter = [None] * len(units), [None] * len(units), [None] * len(units)
    m_run = {(s, h): m_s[s, h:h + 1, 0:1] for s, h in chains}
    for u, (s, c, h) in enumerate(units):
        m_prev[u] = m_run[s, h]
        inter[u] = m_prev[u] + f_col[u]
        m_t[u] = jnp.maximum(inter[u], row_max[u])
        m_run[s, h] = m_t[u][L - 1:L, :]
    m_new = [m_t[u][L - 1:L, :] for u in U]

    ahead()
    dw = [jnp.exp(log_d[u] - m_t[u]) for u in U]
    sc = [jnp.exp(inter[u] - m_t[u]) for u in U]
    ahead()
    qk = [qk_raw[u] * dw[u] for u in U]
    ahead()
    intra = [_dot(qk[u].astype(BF16), vf[u].astype(BF16)) for u in U]
    ahead()
    row_sum = [jnp.sum(qk[u], axis=-1, keepdims=True) for u in U]
    floor = [jnp.exp(-m_t[u]) for u in U]
    f_last = [f_col[u][L - 1:L, :] for u in U]
    ahead()
    wk = [jnp.exp(gate[s, c][:, h:h + 1] + f_last[u] - f_col[u] - m_new[u]) for u, (s, c, h) in enumerate(units)]
    decay = [jnp.exp(m_prev[u] + f_last[u] - m_new[u]) for u in U]
    ahead()
    upd = [_dot_tn((vf[u] * wk[u]).astype(BF16), kb[u]) for u in U]
    ahead()
    n_upd = [jnp.sum(wk[u] * kf[u], axis=0, keepdims=True) for u in U]

    c_run = {(s, h): c_s[s, h] for s, h in chains}
    n_run = {(s, h): n_s[s, h:h + 1, :] for s, h in chains}
    hh = [None] * len(units)
    for c in range(NC):
        cu = [u for u in U if units[u][1] == c]
        inter_term = {u: _dot_nt(qb[u], c_run[units[u][0], units[u][2]].astype(BF16)) for u in cu}
        n_term = {u: jnp.sum(qf[u] * n_run[units[u][0], units[u][2]], axis=-1, keepdims=True) for u in cu}
        for u in cu:
            s, _, h = units[u]
            num = intra[u] + sc[u] * inter_term[u]
            den = row_sum[u] + sc[u] * n_term[u]
            hh[u] = num / jnp.maximum(jnp.abs(den), floor[u])
            c_run[s, h] = decay[u] * c_run[s, h] + upd[u]
            n_run[s, h] = decay[u] * n_run[s, h] + n_upd[u]
    ahead(len(pending))
    for s, h in chains:
        c_s[s, h] = c_run[s, h]
        n_s[s, h:h + 1, :] = n_run[s, h]
        m_s[s, h:h + 1, :] = jnp.broadcast_to(m_run[s, h], (1, LANES))
    for u, (s, c, h) in enumerate(units):
        mixbuf[rows(s, c), head_cols(1, h)] = _head_out(
            hh[u], proj[rows(s, c), head_cols(4, h)], mhg_ref[:, h * HEAD_DIM:(h + 1) * HEAD_DIM])

    @pl.when(ti == nt - 1)
    def _():
        for s in range(S):
            c_out_ref[s, 0] = c_s[s]
            n_out_ref[s, 0] = n_s[s, 0:HEADS, :]
            m_out_ref[s, 0] = m_s[s]

    mix = _dot(mixbuf[...].astype(BF16), wout_ref[...])
    x1 = _layer_norm(DN_ALPHA * x + mix, ln1g_ref[...], ln1b_ref[...])
    for s in range(S):
        x1_ref[s] = x1[s * TT:(s + 1) * TT, :]


def _prompt_mixer(x, w_in_b, w_g_b, gbias, w_pool_b, pscale, mhg, w_out_b, ln1g, ln1b):
    nt = SEQ // MIX_TILE
    S = MIX_SEQS
    G = BATCH // S
    const2 = lambda b, t: (0, 0)
    const3 = lambda b, t: (0, 0, 0)

    def next_tile(b, t):
        nxt = jnp.minimum(b * nt + t + 1, G * nt - 1)
        return (0, nxt // nt, nxt % nt, 0)

    outs = pl.pallas_call(
        _prompt_mixer_kernel,
        grid=(G, nt),
        in_specs=[
            pl.BlockSpec((S, 1, MIX_TILE, D_MODEL), lambda b, t: (0, b, t, 0)),
            pl.BlockSpec((S, 1, MIX_TILE, D_MODEL), next_tile),
            pl.BlockSpec(w_in_b.shape, const2),
            pl.BlockSpec(w_g_b.shape, const2),
            pl.BlockSpec(gbias.shape, const2),
            pl.BlockSpec(w_pool_b.shape, const3),
            pl.BlockSpec(pscale.shape, const2),
            pl.BlockSpec(mhg.shape, const2),
            pl.BlockSpec(w_out_b.shape, const2),
            pl.BlockSpec(ln1g.shape, const2),
            pl.BlockSpec(ln1b.shape, const2),
        ],
        out_specs=[
            pl.BlockSpec((S, MIX_TILE, D_MODEL), lambda b, t: (0, b * nt + t, 0)),
            pl.BlockSpec((S, 1, POOL_HIST, POOL_WIDTH), lambda b, t: (0, b, 0, 0)),
            pl.BlockSpec((S, 1, HEADS, HEAD_DIM, HEAD_DIM), lambda b, t: (0, b, 0, 0, 0)),
            pl.BlockSpec((S, 1, HEADS, HEAD_DIM), lambda b, t: (0, b, 0, 0)),
            pl.BlockSpec((S, 1, SUBLANES, LANES), lambda b, t: (0, b, 0, 0)),
        ],
        out_shape=[
            jax.ShapeDtypeStruct((S, G * SEQ, D_MODEL), F32),
            jax.ShapeDtypeStruct((S, G, POOL_HIST, POOL_WIDTH), F32),
            jax.ShapeDtypeStruct((S, G, HEADS, HEAD_DIM, HEAD_DIM), F32),
            jax.ShapeDtypeStruct((S, G, HEADS, HEAD_DIM), F32),
            jax.ShapeDtypeStruct((S, G, SUBLANES, LANES), F32),
        ],
        scratch_shapes=[
            pltpu.VMEM((S, HIST_PAD + MIX_TILE, POOL_WIDTH), F32),
            pltpu.VMEM((S * MIX_TILE, D_MODEL), F32),
            pltpu.VMEM((2, S * MIX_TILE, POOL_WIDTH + 4 * MLSTM_WIDTH), F32),
            pltpu.VMEM((2, S * MIX_TILE, LANES), F32),
            pltpu.VMEM((S, HEADS, HEAD_DIM, HEAD_DIM), F32),
            pltpu.VMEM((S, SUBLANES, HEAD_DIM), F32),
            pltpu.VMEM((S, SUBLANES, LANES), F32),
        ],
        compiler_params=pltpu.CompilerParams(
            dimension_semantics=("arbitrary", "arbitrary"), vmem_limit_bytes=VMEM_LIMIT),
        name="prompt_mixer",
    )(x.reshape(S, G, SEQ, D_MODEL), x.reshape(S, G, SEQ, D_MODEL), w_in_b, w_g_b, gbias, w_pool_b, pscale, mhg, w_out_b, ln1g, ln1b)
    x1, pool, c, n, m = outs
    return (x1.reshape(N_PROMPT, D_MODEL), pool.reshape(BATCH, POOL_HIST, POOL_WIDTH),
            c.reshape(BATCH, HEADS, HEAD_DIM, HEAD_DIM), n.reshape(BATCH, HEADS, HEAD_DIM),
            m.reshape(BATCH, SUBLANES, LANES))


def _sample_mixer_kernel(x_ref, hist_ref, c_ref, n_ref, m_ref, win_ref, wg_ref, gb_ref, wpool_ref,
                         pscale_ref, mhg_ref, wout_ref, ln1g_ref, ln1b_ref,
                         x1_ref, pool_out_ref, c_out_ref, n_out_ref, m_out_ref,
                         q_s, k_s, vw_s, v_s, o_s, mixbuf, h_s, coef_s):
    i = pl.program_id(0)
    nsteps = pl.num_programs(0)
    B = DEC_BATCH

    @pl.when(i == 0)
    def _():
        x = x_ref[...]
        xb = x.astype(BF16)
        proj = _dot(xb, win_ref[...])
        g = _dot(xb, wg_ref[...])
        u = proj[:, 0:POOL_WIDTH]
        for gi, w in enumerate(POOL_WINDOWS):
            sl = slice(gi * POOL_GROUP_DIM, (gi + 1) * POOL_GROUP_DIM)
            ug = u[:, sl]
            s = ug
            for j in range(1, w):
                r = POOL_HIST - j
                s = s + hist_ref[:, r * POOL_WIDTH + gi * POOL_GROUP_DIM:r * POOL_WIDTH + (gi + 1) * POOL_GROUP_DIM]
            cnt = float(min(PAST_LEN + 1, w))
            z = s / cnt - ug
            mixbuf[:, sl] = _dot(z.astype(BF16), wpool_ref[gi]) * pscale_ref[:, sl]
        pool_out_ref[:, 0:(POOL_HIST - 1) * POOL_WIDTH] = hist_ref[:, POOL_WIDTH:POOL_HIST * POOL_WIDTH]
        pool_out_ref[:, (POOL_HIST - 1) * POOL_WIDTH:POOL_HIST * POOL_WIDTH] = u

        val = _gate_values(g, gb_ref[...])
        lane = lax.broadcasted_iota(I32, (B, LANES), 1)
        qk_all = jnp.zeros((B, LANES), F32)
        sc_all = jnp.zeros((B, LANES), F32)
        den_all = jnp.zeros((B, LANES), F32)
        floor_all = jnp.zeros((B, LANES), F32)
        m_all = jnp.zeros((B, LANES), F32)
        for h in range(HEADS):
            hs = slice(h * HEAD_DIM, (h + 1) * HEAD_DIM)
            qf = proj[:, POOL_WIDTH + h * HEAD_DIM:POOL_WIDTH + (h + 1) * HEAD_DIM]
            kf = proj[:, 2 * POOL_WIDTH + h * HEAD_DIM:2 * POOL_WIDTH + (h + 1) * HEAD_DIM] * (HEAD_DIM ** -0.5)
            vf = proj[:, 3 * POOL_WIDTH + h * HEAD_DIM:3 * POOL_WIDTH + (h + 1) * HEAD_DIM]
            ig = val[:, h:h + 1]
            lf = val[:, HEADS + h:HEADS + h + 1]
            m0 = m_ref[:, h:h + 1]
            n0 = n_ref[:, hs]
            inter = m0 + lf
            m_t = jnp.maximum(inter, ig)
            dw = jnp.exp(ig - m_t)
            sc = jnp.exp(inter - m_t)
            qk = jnp.sum(qf * kf, axis=-1, keepdims=True) * dw
            den = qk + sc * jnp.sum(qf * n0, axis=-1, keepdims=True)
            n_out_ref[:, hs] = sc * n0 + dw * kf
            q_s[0:B, hs] = qf
            k_s[0:B, hs] = kf
            v_s[0:B, hs] = vf
            vw_s[0:B, hs] = vf * dw
            sel = lane == h
            qk_all = jnp.where(sel, qk, qk_all)
            sc_all = jnp.where(sel, sc, sc_all)
            den_all = jnp.where(sel, den, den_all)
            floor_all = jnp.where(sel, jnp.exp(-m_t), floor_all)
            m_all = jnp.where(lane == HEADS + h, m_t, m_all)
        o_s[...] = proj[:, 4 * POOL_WIDTH:5 * POOL_WIDTH]
        coef_s[0] = qk_all
        coef_s[1] = sc_all
        coef_s[2] = den_all
        coef_s[3] = floor_all
        m_out_ref[...] = m_all

    rows = pl.ds(pl.multiple_of(i * SAMPLE_BT, SAMPLE_BT), SAMPLE_BT)
    q_t, k_t, v_t, vw_t = q_s[rows, :], k_s[rows, :], v_s[rows, :], vw_s[rows, :]
    qk_t, sc_t, den_t, floor_t = coef_s[0, rows, :], coef_s[1, rows, :], coef_s[2, rows, :], coef_s[3, rows, :]
    h_rows = []
    for bl in range(SAMPLE_BT):
        heads = []
        for h in range(HEADS):
            hs = slice(h * HEAD_DIM, (h + 1) * HEAD_DIM)
            c_prev = c_ref[bl, h]
            q8 = jnp.broadcast_to(q_t[bl:bl + 1, hs], (SUBLANES, HEAD_DIM))
            cq = _dot_nt(q8.astype(BF16), c_prev.astype(BF16))[0:1, :]
            qk = qk_t[bl:bl + 1, h:h + 1]
            sc = sc_t[bl:bl + 1, h:h + 1]
            num = qk * v_t[bl:bl + 1, hs] + sc * cq
            heads.append(num / jnp.maximum(jnp.abs(den_t[bl:bl + 1, h:h + 1]), floor_t[bl:bl + 1, h:h + 1]))
            v_col = jnp.broadcast_to(vw_t[bl:bl + 1, hs], (HEAD_DIM, HEAD_DIM)).T
            c_out_ref[bl, h] = sc * c_prev + v_col * k_t[bl:bl + 1, hs]
        h_rows.append(jnp.concatenate(heads, axis=1))
    h_s[rows, :] = jnp.concatenate(h_rows, axis=0)

    @pl.when(i == nsteps - 1)
    def _():
        for h in range(HEADS):
            hs = slice(h * HEAD_DIM, (h + 1) * HEAD_DIM)
            mixbuf[:, POOL_WIDTH + h * HEAD_DIM:POOL_WIDTH + (h + 1) * HEAD_DIM] = _head_out(
                h_s[:, hs], o_s[:, hs], mhg_ref[:, hs])
        mix = _dot(mixbuf[...].astype(BF16), wout_ref[...])
        x1 = _layer_norm(DN_ALPHA * x_ref[...] + mix, ln1g_ref[...], ln1b_ref[...])
        x1_ref[0:B, :] = x1
        x1_ref[B:TOK_TILE, :] = jnp.zeros((TOK_TILE - B, D_MODEL), F32)


def _sample_mixer(x, hist2, c0, n0, m0, w_in_b, w_g_b, gbias, w_pool_b, pscale, mhg, w_out_b, ln1g, ln1b):
    B = DEC_BATCH
    steps = B // SAMPLE_BT
    full = lambda a: pl.BlockSpec(a.shape, lambda i: (0,) * a.ndim)
    c_spec = pl.BlockSpec((SAMPLE_BT, HEADS, HEAD_DIM, HEAD_DIM), lambda i: (i, 0, 0, 0))
    return pl.pallas_call(
        _sample_mixer_kernel,
        grid=(steps,),
        in_specs=[full(x), full(hist2), c_spec, full(n0), full(m0), full(w_in_b), full(w_g_b), full(gbias),
                  full(w_pool_b), full(pscale), full(mhg), full(w_out_b), full(ln1g), full(ln1b)],
        out_specs=[
            pl.BlockSpec((TOK_TILE, D_MODEL), lambda i: (0, 0)),
            pl.BlockSpec((B, POOL_HIST * POOL_WIDTH), lambda i: (0, 0)),
            c_spec,
            pl.BlockSpec((B, MLSTM_WIDTH), lambda i: (0, 0)),
            pl.BlockSpec((B, LANES), lambda i: (0, 0)),
        ],
        out_shape=[
            jax.ShapeDtypeStruct((TOK_TILE, D_MODEL), F32),
            jax.ShapeDtypeStruct((B, POOL_HIST * POOL_WIDTH), F32),
            jax.ShapeDtypeStruct((B, HEADS, HEAD_DIM, HEAD_DIM), F32),
            jax.ShapeDtypeStruct((B, MLSTM_WIDTH), F32),
            jax.ShapeDtypeStruct((B, LANES), F32),
        ],
        scratch_shapes=[
            pltpu.VMEM((B, MLSTM_WIDTH), F32),
            pltpu.VMEM((B, MLSTM_WIDTH), F32),
            pltpu.VMEM((B, MLSTM_WIDTH), F32),
            pltpu.VMEM((B, MLSTM_WIDTH), F32),
            pltpu.VMEM((B, MLSTM_WIDTH), F32),
            pltpu.VMEM((B, D_MODEL), F32),
            pltpu.VMEM((B, MLSTM_WIDTH), F32),
            pltpu.VMEM((4, B, LANES), F32),
        ],
        compiler_params=pltpu.CompilerParams(
            dimension_semantics=("arbitrary",), vmem_limit_bytes=VMEM_LIMIT),
        name="sample_mixer",
    )(x, hist2, c0, n0, m0, w_in_b, w_g_b, gbias, w_pool_b, pscale, mhg, w_out_b, ln1g, ln1b)


def _pick_tile(i, prompt_ref, sample_ref):
    return jnp.where(i < N_PROMPT_TILES, prompt_ref[...], sample_ref[...])


def _placement(slot_rows, group):
    r = group * TOK_TILE + lax.broadcasted_iota(I32, (TOK_TILE, TOK_TILE), 0)
    return [r == s for s in slot_rows]


def _route_kernel(xp_ref, xs_ref, wrt_ref, br_ref, slot_ref, gate_ref, nch_ref, sorted_ref):
    i = pl.program_id(0)
    T = TOK_TILE
    E = N_EXPERTS

    x = _pick_tile(i, xp_ref, xs_ref)
    xh = x.astype(BF16)
    xl = (x - xh.astype(F32)).astype(BF16)
    w = wrt_ref[...]
    wh = w.astype(BF16)
    wl = (w - wh.astype(F32)).astype(BF16)
    logits = _dot_nt(wh, xh) + (_dot_nt(wh, xl) + _dot_nt(wl, xh)) + br_ref[:, 0:1]

    erow = lax.broadcasted_iota(I32, (E, T), 0).astype(F32)
    work = logits
    vals, sels = [], []
    for _ in range(TOP_K):
        mx = jnp.max(work, axis=0, keepdims=True)
        idx = jnp.min(jnp.where(work == mx, erow, float(E)), axis=0, keepdims=True)
        sel = erow == idx
        work = jnp.where(sel, -jnp.inf, work)
        vals.append(mx)
        sels.append(sel)
    chosen = jnp.logical_or(jnp.logical_or(sels[0], sels[1]), jnp.logical_or(sels[2], sels[3]))
    es = [jnp.exp(v - vals[0]) for v in vals]
    tot = es[0] + es[1] + es[2] + es[3]

    n_valid = jnp.where(i < N_PROMPT_TILES, T, DEC_BATCH)
    valid = lax.broadcasted_iota(I32, (1, T), 1) < n_valid
    onehot = jnp.where(jnp.logical_and(chosen, valid), 1.0, 0.0)
    trow = lax.broadcasted_iota(I32, (T, T), 0)
    tcol = lax.broadcasted_iota(I32, (T, T), 1)
    before = jnp.where(trow < tcol, 1.0, 0.0).astype(BF16)
    rank = _dot(onehot.astype(BF16), before)
    cnt = jnp.sum(onehot, axis=1, keepdims=True)
    nch = jnp.floor((cnt + (CHUNK_ROWS - 1)) * (1.0 / CHUNK_ROWS))
    lower = jnp.where(lax.broadcasted_iota(I32, (E, E), 0) > lax.broadcasted_iota(I32, (E, E), 1), 1.0, 0.0)
    nch_b = jnp.broadcast_to(nch, (E, LANES))
    seg_start = _dot(lower.astype(BF16), nch_b.astype(BF16))[:, 0:1] * CHUNK_ROWS
    base = seg_start + rank

    r8 = lax.broadcasted_iota(I32, (SUBLANES, T), 0)
    s_out = jnp.zeros((SUBLANES, T), I32)
    g_out = jnp.zeros((SUBLANES, T), F32)
    slot_rows = []
    for j in range(TOP_K):
        slot_j = jnp.sum(jnp.where(sels[j], base, 0.0), axis=0, keepdims=True).astype(I32)
        slot_j = jnp.where(valid, slot_j, -1)
        slot_rows.append(slot_j)
        s_out = jnp.where(r8 == j, slot_j, s_out)
        g_out = jnp.where(r8 == j, es[j] / tot, g_out)
    slot_ref[0] = s_out
    gate_ref[0] = g_out
    nch_ref[0] = nch_b

    for grp in range(GROUPS):
        m = _placement(slot_rows, grp)
        hit = jnp.logical_or(jnp.logical_or(m[0], m[1]), jnp.logical_or(m[2], m[3]))
        place = jnp.where(hit, 1.0, 0.0).astype(BF16)
        sorted_ref[grp * T:(grp + 1) * T, :] = _dot(place, xh).astype(BF16)


def _route(x1p, x1s, w_router_t, b_router_col):
    tile_spec = pl.BlockSpec((1, SUBLANES, TOK_TILE), lambda i: (i, 0, 0))
    return pl.pallas_call(
        _route_kernel,
        grid=(N_TILES,),
        in_specs=[
            pl.BlockSpec((TOK_TILE, D_MODEL), lambda i: (jnp.minimum(i, N_PROMPT_TILES - 1), 0)),
            pl.BlockSpec((TOK_TILE, D_MODEL), lambda i: (0, 0)),
            pl.BlockSpec(w_router_t.shape, lambda i: (0, 0)),
            pl.BlockSpec(b_router_col.shape, lambda i: (0, 0)),
        ],
        out_specs=[tile_spec, tile_spec,
                   pl.BlockSpec((1, N_EXPERTS, LANES), lambda i: (i, 0, 0)),
                   pl.BlockSpec((LOCAL_ROWS, D_MODEL), lambda i: (i, 0))],
        out_shape=[
            jax.ShapeDtypeStruct((N_TILES, SUBLANES, TOK_TILE), I32),
            jax.ShapeDtypeStruct((N_TILES, SUBLANES, TOK_TILE), F32),
            jax.ShapeDtypeStruct((N_TILES, N_EXPERTS, LANES), F32),
            jax.ShapeDtypeStruct((N_TILES * LOCAL_ROWS, D_MODEL), BF16),
        ],
        compiler_params=pltpu.CompilerParams(
            dimension_semantics=("arbitrary",), vmem_limit_bytes=VMEM_LIMIT),
        name="route",
    )(x1p, x1s, w_router_t, b_router_col)


def _expert_kernel(src_ref, dst_ref, bexp_ref, first_ref, next_ref, nused_ref,
                   sorted_hbm, w1_hbm, b1_ref, w2_hbm, b2_ref, out_hbm,
                   w1_stage, w2_stage, w1_b, w2_b, xbuf, obuf, zbuf, wsem, gsem, ssem, zsem):
    nused = nused_ref[0]

    def fetch_piece(e, p):
        r1 = pl.ds(pl.multiple_of(p * (D_MODEL // WEIGHT_PIECES), SUBLANES), D_MODEL // WEIGHT_PIECES)
        r2 = pl.ds(pl.multiple_of(p * (D_FF // WEIGHT_PIECES), SUBLANES), D_FF // WEIGHT_PIECES)
        return (pltpu.make_async_copy(w1_hbm.at[e, r1, :], w1_stage.at[r1, :], wsem.at[0]),
                pltpu.make_async_copy(w2_hbm.at[e, r2, :], w2_stage.at[r2, :], wsem.at[1]))

    def start_pieces(e, lo, hi):
        def body(p, c):
            for cp in fetch_piece(e, p):
                cp.start()
            return c
        lax.fori_loop(lo, hi, body, 0)

    def gather(b, q):
        slot = lax.rem(b, GATHER_DEPTH)
        row = pl.multiple_of(src_ref[b * BLOCK_CHUNKS + q], CHUNK_ROWS)
        return pltpu.make_async_copy(sorted_hbm.at[pl.ds(row, CHUNK_ROWS), :],
                                     xbuf.at[slot, pl.ds(q * CHUNK_ROWS, CHUNK_ROWS), :], gsem.at[slot])

    def scatter(b, q):
        slot = lax.rem(b, 2)
        row = pl.multiple_of(dst_ref[b * BLOCK_CHUNKS + q], CHUNK_ROWS)
        return pltpu.make_async_copy(obuf.at[slot, pl.ds(q * CHUNK_ROWS, CHUNK_ROWS), :],
                                     out_hbm.at[pl.ds(row, CHUNK_ROWS), :], ssem.at[slot])

    def zero_rows(start, n_rows):
        start = pl.multiple_of(start, CHUNK_ROWS)
        return pltpu.make_async_copy(zbuf.at[pl.ds(0, n_rows), :], out_hbm.at[pl.ds(start, n_rows), :], zsem)

    def zero_tail(k):
        return zero_rows(k * LOCAL_ROWS + TOK_TILE * TOP_K, FREE_ROWS)

    def zero_last_tile(part):
        return zero_rows((N_TILES - 1) * LOCAL_ROWS + part * FREE_ROWS, FREE_ROWS)

    zbuf[...] = jnp.zeros_like(zbuf)
    lax.fori_loop(0, N_TILES, lambda k, c: (zero_tail(k).start(), c)[1], 0)
    for part in range(TOK_TILE * TOP_K // FREE_ROWS):
        zero_last_tile(part).start()
    zero_rows(DUMP_BASE, DUMP_ROWS).start()
    for ahead in range(GATHER_DEPTH - 1):
        @pl.when(ahead < nused)
        def _():
            for q in range(BLOCK_CHUNKS):
                gather(ahead, q).start()
    lax.fori_loop(0, N_TILES, lambda k, c: (zero_tail(k).wait(), c)[1], 0)
    for part in range(TOK_TILE * TOP_K // FREE_ROWS):
        zero_last_tile(part).wait()
    zero_rows(DUMP_BASE, DUMP_ROWS).wait()

    half = MXU_COLS // 2
    k_io = lax.broadcasted_iota(I32, (MXU_COLS, MXU_COLS), 0)
    j_io = lax.broadcasted_iota(I32, (MXU_COLS, MXU_COLS), 1)
    src_col = jnp.where(j_io < half, 2 * j_io, 2 * (j_io - half) + 1)
    perm = jnp.where(k_io == src_col, 1.0, 0.0).astype(BF16)

    def block(i, fetched):
        e = bexp_ref[i]
        slot = lax.rem(i, 2)
        is_first = first_ref[i] == 1

        @pl.when(is_first)
        def _():
            start_pieces(e, fetched, WEIGHT_PIECES)

            def wait_piece(p, c):
                for cp in fetch_piece(e, p):
                    cp.wait()
                return c
            lax.fori_loop(0, WEIGHT_PIECES, wait_piece, 0)
            for c in range(2 * D_FF // MXU_COLS):
                blk = w1_stage[:, c * MXU_COLS:(c + 1) * MXU_COLS].astype(BF16)
                sep = _dot(blk, perm).astype(BF16)
                w1_b[:, c * half:(c + 1) * half] = sep[:, 0:half]
                w1_b[:, D_FF + c * half:D_FF + (c + 1) * half] = sep[:, half:MXU_COLS]
            w2_b[...] = w2_stage[...].astype(BF16)

        fetched = jnp.where(is_first, 0, fetched)

        for q in range(BLOCK_CHUNKS):
            gather(i, q).wait()

        @pl.when(i + GATHER_DEPTH - 1 < nused)
        def _():
            for q in range(BLOCK_CHUNKS):
                gather(i + GATHER_DEPTH - 1, q).start()

        @pl.when(i >= 2)
        def _():
            for q in range(BLOCK_CHUNKS):
                scatter(i - 2, q).wait()

        h = _dot(xbuf[lax.rem(i, GATHER_DEPTH)], w1_b[...]) + b1_ref[e]
        glu = jnp.minimum(h[:, 0:D_FF], SWIGLU_LIMIT)
        lin = jnp.clip(h[:, D_FF:2 * D_FF], -SWIGLU_LIMIT, SWIGLU_LIMIT)
        a = glu * jax.nn.sigmoid(SWIGLU_ALPHA * glu) * (lin + 1.0)
        obuf[slot] = (_dot(a.astype(BF16), w2_b[...]) + b2_ref[e]).astype(BF16)
        for q in range(BLOCK_CHUNKS):
            scatter(i, q).start()

        more = jnp.where(next_ref[i] >= 0, jnp.minimum(fetched + PIECES_PER_BLOCK, WEIGHT_PIECES), fetched)
        start_pieces(next_ref[i], fetched, more)
        return more

    lax.fori_loop(0, nused, block, jnp.int32(0))

    @pl.when(nused >= 2)
    def _():
        for q in range(BLOCK_CHUNKS):
            scatter(nused - 2, q).wait()
    for q in range(BLOCK_CHUNKS):
        scatter(nused - 1, q).wait()


def _experts(chunk_src, chunk_dst, block_expert, block_first, block_next, n_used, sorted_rows, w1, b1p, w2, b2):
    whole3 = lambda i, *_: (0, 0, 0)
    grid_spec = pltpu.PrefetchScalarGridSpec(
        num_scalar_prefetch=6,
        grid=(1,),
        in_specs=[
            pl.BlockSpec(memory_space=pl.ANY),
            pl.BlockSpec(memory_space=pl.ANY),
            pl.BlockSpec(b1p.shape, whole3),
            pl.BlockSpec(memory_space=pl.ANY),
            pl.BlockSpec(b2.shape, whole3),
        ],
        out_specs=pl.BlockSpec(memory_space=pl.ANY),
        scratch_shapes=[
            pltpu.VMEM((D_MODEL, 2 * D_FF), F32),
            pltpu.VMEM((D_FF, D_MODEL), F32),
            pltpu.VMEM((D_MODEL, 2 * D_FF), BF16),
            pltpu.VMEM((D_FF, D_MODEL), BF16),
            pltpu.VMEM((GATHER_DEPTH, ROW_BLOCK, D_MODEL), BF16),
            pltpu.VMEM((2, ROW_BLOCK, D_MODEL), BF16),
            pltpu.VMEM((FREE_ROWS, D_MODEL), BF16),
            pltpu.SemaphoreType.DMA((2,)),
            pltpu.SemaphoreType.DMA((GATHER_DEPTH,)),
            pltpu.SemaphoreType.DMA((2,)),
            pltpu.SemaphoreType.DMA(()),
        ],
    )
    return pl.pallas_call(
        _expert_kernel,
        grid_spec=grid_spec,
        out_shape=jax.ShapeDtypeStruct((DUMP_BASE + DUMP_ROWS, D_MODEL), BF16),
        compiler_params=pltpu.CompilerParams(
            dimension_semantics=("arbitrary",), vmem_limit_bytes=VMEM_LIMIT),
        name="experts",
    )(chunk_src, chunk_dst, block_expert, block_first, block_next, n_used, sorted_rows, w1, b1p, w2, b2)


def _combine_kernel(slot_ref, gate_ref, xp_ref, xs_ref, pp_ref, ps_ref, eo_ref, ln2g_ref, ln2b_ref,
                    wpg_ref, wple_ref, yp_ref, ys_ref):
    i = pl.program_id(0)
    T = TOK_TILE

    slot_rows = [slot_ref[0, j:j + 1, :] for j in range(TOP_K)]
    gate_rows = [gate_ref[0, j:j + 1, :] for j in range(TOP_K)]
    pad = jnp.zeros((LANES - SUBLANES, T), F32)
    slots_t = jnp.concatenate([slot_ref[0].astype(F32), pad], axis=0).T
    slot_cols = [slots_t[:, j:j + 1].astype(I32) for j in range(TOP_K)]

    ff = jnp.zeros((T, D_MODEL), F32)
    for grp in range(GROUPS):
        m = _placement(slot_rows, grp)
        weighted = jnp.where(m[0], gate_rows[0], jnp.where(m[1], gate_rows[1], jnp.where(
            m[2], gate_rows[2], jnp.where(m[3], gate_rows[3], 0.0))))
        g_col = jnp.sum(weighted, axis=1, keepdims=True)
        z = (eo_ref[grp * T:(grp + 1) * T, :].astype(F32) * g_col).astype(BF16)
        r = grp * T + lax.broadcasted_iota(I32, (T, T), 1)
        hit = jnp.logical_or(jnp.logical_or(r == slot_cols[0], r == slot_cols[1]),
                             jnp.logical_or(r == slot_cols[2], r == slot_cols[3]))
        ff = ff + _dot(jnp.where(hit, 1.0, 0.0).astype(BF16), z)

    x1 = _pick_tile(i, xp_ref, xs_ref)
    x2 = _layer_norm(DN_ALPHA * x1 + ff, ln2g_ref[...], ln2b_ref[...])
    p = _pick_tile(i, pp_ref, ps_ref)
    y = x2 + jax.nn.sigmoid(_dot(x2.astype(BF16), wpg_ref[...])) * _dot(p.astype(BF16), wple_ref[...])

    @pl.when(i < N_PROMPT_TILES)
    def _():
        yp_ref[...] = y

    @pl.when(i == N_PROMPT_TILES)
    def _():
        ys_ref[...] = y[0:DEC_BATCH, :]


def _combine(slots, gates, x1p, x1s, pp, ps, expert_out, ln2g, ln2b, w_pg_b, w_ple_b):
    tile_idx = lambda i: (jnp.minimum(i, N_PROMPT_TILES - 1), 0)
    const2 = lambda i: (0, 0)
    return pl.pallas_call(
        _combine_kernel,
        grid=(N_TILES,),
        in_specs=[
            pl.BlockSpec((1, SUBLANES, TOK_TILE), lambda i: (i, 0, 0)),
            pl.BlockSpec((1, SUBLANES, TOK_TILE), lambda i: (i, 0, 0)),
            pl.BlockSpec((TOK_TILE, D_MODEL), tile_idx),
            pl.BlockSpec((TOK_TILE, D_MODEL), const2),
            pl.BlockSpec((TOK_TILE, PLE_DIM), tile_idx),
            pl.BlockSpec((TOK_TILE, PLE_DIM), const2),
            pl.BlockSpec((LOCAL_ROWS, D_MODEL), lambda i: (i, 0)),
            pl.BlockSpec(ln2g.shape, const2),
            pl.BlockSpec(ln2b.shape, const2),
            pl.BlockSpec(w_pg_b.shape, const2),
            pl.BlockSpec(w_ple_b.shape, const2),
        ],
        out_specs=[
            pl.BlockSpec((TOK_TILE, D_MODEL), tile_idx),
            pl.BlockSpec((DEC_BATCH, D_MODEL), const2),
        ],
        out_shape=[
            jax.ShapeDtypeStruct((N_PROMPT, D_MODEL), F32),
            jax.ShapeDtypeStruct((DEC_BATCH, D_MODEL), F32),
        ],
        compiler_params=pltpu.CompilerParams(
            dimension_semantics=("arbitrary",), vmem_limit_bytes=VMEM_LIMIT),
        name="combine",
    )(slots, gates, x1p, x1s, pp, ps, expert_out, ln2g, ln2b, w_pg_b, w_ple_b)


def _block_tables(nch):
    seg_start = (jnp.cumsum(nch, axis=1) - nch) * CHUNK_ROWS
    tot = jnp.sum(nch, axis=0)
    nblk = (tot + BLOCK_CHUNKS - 1) // BLOCK_CHUNKS
    blk_end = jnp.cumsum(nblk)
    blk_start = blk_end - nblk
    n_used = blk_end[-1:].astype(I32)
    blk_ids = jnp.arange(N_BLOCKS, dtype=I32)

    e_ids = jnp.arange(N_EXPERTS, dtype=I32)
    used = nblk > 0
    blk_clamped = jnp.minimum(blk_ids, n_used[0] - 1)
    block_expert = jnp.minimum(jnp.sum(blk_end[:, None] <= blk_clamped[None, :], axis=0), N_EXPERTS - 1).astype(I32)
    starts_here = jnp.logical_and(blk_start[:, None] == blk_ids[None, :], used[:, None])
    block_first = jnp.any(starts_here, axis=0).astype(I32)
    later_used = jnp.logical_and(e_ids[None, :] > e_ids[:, None], used[None, :])
    next_used = jnp.min(jnp.where(later_used, e_ids[None, :], N_EXPERTS), axis=1)
    next_used = jnp.where(next_used < N_EXPERTS, next_used, -1)
    block_next = jnp.sum(jnp.where(block_expert[None, :] == e_ids[:, None], next_used[:, None], 0), axis=0).astype(I32)

    nch_t = nch.T
    seg_first = (blk_start[:, None] * BLOCK_CHUNKS + jnp.cumsum(nch_t, axis=1) - nch_t).reshape(-1)
    seg_count = nch_t.reshape(-1)
    seg_row = (jnp.arange(N_TILES, dtype=I32)[None, :] * LOCAL_ROWS + seg_start.T).reshape(-1)
    ent = jnp.arange(N_BLOCKS * BLOCK_CHUNKS, dtype=I32)
    d = ent[None, :] - seg_first[:, None]
    inside = jnp.logical_and(d >= 0, d < seg_count[:, None])
    row_plus_1 = jnp.sum(jnp.where(inside, seg_row[:, None] + d * CHUNK_ROWS + 1, 0), axis=0)
    real = row_plus_1 > 0
    row = row_plus_1 - 1
    dump = DUMP_BASE + (((ent // BLOCK_CHUNKS) % 2) * BLOCK_CHUNKS + ent % BLOCK_CHUNKS) * CHUNK_ROWS
    chunk_src = jnp.where(real, row, ZERO_CHUNK_ROW).astype(I32)
    chunk_dst = jnp.where(real, row, dump).astype(I32)
    return chunk_src, chunk_dst, block_expert, block_first, block_next, n_used


def kernel(x_prompt, x_sample, state_pool, state_mlstm_C, state_mlstm_n, state_mlstm_m, p_prompt, p_sample, w_in, b_i, b_f, w_pool, pool_scale, mh_g, w_out, ln1_g, ln1_b, w_router, b_router, w_mlp1, b_mlp1, w_mlp2, b_mlp2, ln2_g, ln2_b, w_ple, w_ple_gate):
    n_main = POOL_WIDTH + 4 * MLSTM_WIDTH
    w_in_b = w_in[0, :, 0:n_main].astype(BF16)
    w_g_b = jnp.pad(w_in[0, :, n_main:], ((0, 0), (0, LANES - 2 * HEADS))).astype(BF16)
    gbias = jnp.pad(jnp.concatenate([b_i[0], b_f[0]]), (0, LANES - 2 * HEADS)).reshape(1, LANES)
    w_pool_b = w_pool[0].astype(BF16)
    pscale = pool_scale[0].reshape(1, POOL_WIDTH)
    mhg = mh_g[0].reshape(1, MLSTM_WIDTH)
    w_out_b = w_out[0].astype(BF16)
    ln1g = ln1_g[0].reshape(1, D_MODEL)
    ln1b = ln1_b[0].reshape(1, D_MODEL)
    ln2g = ln2_g[0].reshape(1, D_MODEL)
    ln2b = ln2_b[0].reshape(1, D_MODEL)
    w_router_t = w_router[0].T
    b_router_col = jnp.broadcast_to(b_router[0].reshape(N_EXPERTS, 1), (N_EXPERTS, LANES))
    b1 = b_mlp1[0]
    b1p = jnp.concatenate([b1[:, 0::2], b1[:, 1::2]], axis=-1).reshape(N_EXPERTS, 1, 2 * D_FF)
    b2 = b_mlp2[0].reshape(N_EXPERTS, 1, D_MODEL)
    w_pg_b = w_ple_gate[0].astype(BF16)
    w_ple_b = w_ple[0].astype(BF16)

    x1p, pool_p, c_p, n_p, m_p = _prompt_mixer(
        x_prompt, w_in_b, w_g_b, gbias, w_pool_b, pscale, mhg, w_out_b, ln1g, ln1b)
    x1s, pool_s, c_s, n_s, m_s = _sample_mixer(
        x_sample.reshape(DEC_BATCH, D_MODEL),
        state_pool[0].reshape(DEC_BATCH, POOL_HIST * POOL_WIDTH),
        state_mlstm_C[0], state_mlstm_n[0].reshape(DEC_BATCH, MLSTM_WIDTH), state_mlstm_m[0],
        w_in_b, w_g_b, gbias, w_pool_b, pscale, mhg, w_out_b, ln1g, ln1b)

    slots, gates, nch, sorted_rows = _route(x1p, x1s, w_router_t, b_router_col)
    tables = _block_tables(nch[:, :, 0].astype(I32))
    expert_out = _experts(*tables, sorted_rows, w_mlp1[0], b1p, w_mlp2[0], b2)

    pp = p_prompt[0].reshape(N_PROMPT, PLE_DIM)
    ps = jnp.pad(p_sample[0].reshape(DEC_BATCH, PLE_DIM), ((0, TOK_TILE - DEC_BATCH), (0, 0)))
    yp, ys = _combine(slots, gates, x1p, x1s, pp, ps, expert_out, ln2g, ln2b, w_pg_b, w_ple_b)

    return (
        yp.reshape(BATCH, SEQ, D_MODEL),
        ys.reshape(DEC_BATCH, 1, D_MODEL),
        pool_p.reshape(1, BATCH, POOL_HIST, POOL_WIDTH),
        c_p.reshape(1, BATCH, HEADS, HEAD_DIM, HEAD_DIM),
        n_p.reshape(1, BATCH, HEADS, HEAD_DIM),
        m_p[:, 0:HEADS, 0].reshape(1, BATCH, HEADS),
        pool_s.reshape(1, DEC_BATCH, POOL_HIST, POOL_WIDTH),
        c_s.reshape(1, DEC_BATCH, HEADS, HEAD_DIM, HEAD_DIM),
        n_s.reshape(1, DEC_BATCH, HEADS, HEAD_DIM),
        m_s[:, HEADS:2 * HEADS].reshape(1, DEC_BATCH, HEADS),
    )
```

```python
import jax
import jax.numpy as jnp
from jax import lax
from jax.experimental import pallas as pl
from jax.experimental.pallas import tpu as pltpu

F32 = jnp.float32
BF16 = jnp.bfloat16
I32 = jnp.int32

D_MODEL = 1024
BATCH = 8
SEQ = 2048
DEC_BATCH = 128
PAST_LEN = 16384
POOL_WIDTH = 512
POOL_GROUP_DIM = 128
POOL_WINDOWS = (2, 4, 8, 16)
POOL_HIST = 15
MLSTM_WIDTH = 512
HEADS = 4
HEAD_DIM = 128
CHUNK = 128
N_EXPERTS = 32
TOP_K = 4
D_FF = 1024
SWIGLU_ALPHA = 1.702
SWIGLU_LIMIT = 7.0
PLE_DIM = 256
DN_ALPHA = 2.0 ** 0.25
LN_EPS = 1e-5

LANES = 128
SUBLANES = 8
BF16_ROWS = 16
MXU_COLS = 256
VMEM_LIMIT = 56 * 1024 * 1024

MIX_TILE = 256
MIX_SEQS = 2
HIST_PAD = 16
TOK_TILE = 512
N_PROMPT = BATCH * SEQ
N_PROMPT_TILES = N_PROMPT // TOK_TILE
N_TILES = N_PROMPT_TILES + 1
SAMPLE_BT = 16

CHUNK_ROWS = BF16_ROWS
LOCAL_ROWS = TOK_TILE * TOP_K + N_EXPERTS * CHUNK_ROWS
GROUPS = LOCAL_ROWS // TOK_TILE
ROW_BLOCK = 256
BLOCK_CHUNKS = ROW_BLOCK // CHUNK_ROWS
MAX_CHUNKS = N_TILES * (TOK_TILE * TOP_K // CHUNK_ROWS + N_EXPERTS)
N_BLOCKS = -(-MAX_CHUNKS // BLOCK_CHUNKS) + N_EXPERTS
ZERO_CHUNK_ROW = LOCAL_ROWS - CHUNK_ROWS
FREE_ROWS = LOCAL_ROWS - TOK_TILE * TOP_K
DUMP_BASE = N_TILES * LOCAL_ROWS
DUMP_ROWS = 2 * ROW_BLOCK
assert DUMP_ROWS <= FREE_ROWS
WEIGHT_PIECES = 8
PIECES_PER_BLOCK = 2
GATHER_DEPTH = 3


def _dot(a, b):
    return jnp.dot(a, b, preferred_element_type=F32)


def _dot_nt(a, b):
    return lax.dot_general(a, b, (((1,), (1,)), ((), ())), preferred_element_type=F32)


def _dot_tn(a, b):
    return lax.dot_general(a, b, (((0,), (0,)), ((), ())), preferred_element_type=F32)


def _split3(a):
    a0 = a.astype(BF16)
    r1 = a - a0.astype(F32)
    a1 = r1.astype(BF16)
    r2 = r1 - a1.astype(F32)
    return a0, a1, r2.astype(BF16)


def _log_sigmoid(x):
    return jnp.minimum(x, 0.0) - jnp.log1p(jnp.exp(-jnp.abs(x)))


def _layer_norm(x, g, b):
    mu = jnp.mean(x, axis=-1, keepdims=True)
    xc = x - mu
    var = jnp.mean(xc * xc, axis=-1, keepdims=True)
    return xc * lax.rsqrt(var + LN_EPS) * g + b


def _gate_values(g, gbias):
    lane = lax.broadcasted_iota(I32, g.shape, 1)
    z = g + gbias
    return jnp.where(lane < HEADS, z, _log_sigmoid(z))


def _head_out(hh, o_h, gain):
    mu = jnp.mean(hh, axis=-1, keepdims=True)
    hc = hh - mu
    var = jnp.mean(hc * hc, axis=-1, keepdims=True)
    return jax.nn.sigmoid(o_h) * (hc * lax.rsqrt(var + LN_EPS) * gain)


def _prompt_mixer_kernel(x_ref, xn_ref, win_ref, wg_ref, gb_ref, wpool_ref, pscale_ref, mhg_ref, wout_ref,
                         ln1g_ref, ln1b_ref,
                         x1_ref, pool_ref, c_out_ref, n_out_ref, m_out_ref,
                         ubuf, mixbuf, pbuf, gbuf, c_s, n_s, m_s):
    ti = pl.program_id(1)
    nt = pl.num_programs(1)
    TT = MIX_TILE
    S = MIX_SEQS

    @pl.when(ti == 0)
    def _():
        for s in range(S):
            ubuf[s, 0:HIST_PAD, :] = jnp.zeros((HIST_PAD, POOL_WIDTH), F32)
        c_s[...] = jnp.zeros_like(c_s)
        n_s[...] = jnp.zeros_like(n_s)
        m_s[...] = jnp.zeros_like(m_s)

    x = jnp.concatenate([x_ref[s, 0] for s in range(S)], axis=0)
    step = pl.program_id(0) * nt + ti
    cur = lax.rem(step, 2)
    nxt = 1 - cur
    n_main = POOL_WIDTH + 4 * MLSTM_WIDTH

    @pl.when(step == 0)
    def _():
        xb0 = x.astype(BF16)
        pbuf[0] = _dot(xb0, win_ref[...])
        gbuf[0] = _dot(xb0, wg_ref[...])

    xnb = jnp.concatenate([xn_ref[s, 0] for s in range(S)], axis=0).astype(BF16)

    def slab(j):
        def run():
            pbuf[nxt, :, j * MXU_COLS:(j + 1) * MXU_COLS] = _dot(xnb, win_ref[:, j * MXU_COLS:(j + 1) * MXU_COLS])
        return run

    def gate_slab():
        gbuf[nxt] = _dot(xnb, wg_ref[...])

    pending = [slab(j) for j in range(n_main // MXU_COLS)] + [gate_slab]

    def ahead(n=1):
        for _ in range(n):
            if pending:
                pending.pop(0)()

    proj = pbuf.at[cur]
    g = gbuf[cur]

    L = CHUNK
    row = lax.broadcasted_iota(I32, (L, L), 0)
    col = lax.broadcasted_iota(I32, (L, L), 1)
    causal = row >= col
    tril = jnp.where(causal, 1.0, 0.0).astype(BF16)
    pos = ti * TT + lax.broadcasted_iota(I32, (TT, 1), 0)

    for s in range(S):
        base = s * TT
        u = proj[base:base + TT, 0:POOL_WIDTH]

        ubuf[s, HIST_PAD:HIST_PAD + TT, :] = u
        for gi, w in enumerate(POOL_WINDOWS):
            sl = slice(gi * POOL_GROUP_DIM, (gi + 1) * POOL_GROUP_DIM)
            ug = u[:, sl]
            acc = ug
            for i in range(1, w):
                acc = acc + ubuf[s, HIST_PAD - i:HIST_PAD - i + TT, sl]
            cnt = jnp.minimum(pos + 1, w).astype(F32)
            z = acc / cnt - ug
            mixbuf[base:base + TT, sl] = _dot(z.astype(BF16), wpool_ref[gi]) * pscale_ref[:, sl]

        @pl.when(ti == nt - 1)
        def _():
            pool_ref[s, 0] = ubuf[s, TT + 1:TT + HIST_PAD, :]

        ubuf[s, 0:HIST_PAD, :] = ubuf[s, TT:TT + HIST_PAD, :]

    NC = TT // L
    chains = [(s, h) for s in range(S) for h in range(HEADS)]
    units = [(s, c, h) for c in range(NC) for s in range(S) for h in range(HEADS)]
    U = range(len(units))

    def rows(s, c):
        return slice(s * TT + c * L, s * TT + (c + 1) * L)

    def head_cols(part, h):
        return slice(part * POOL_WIDTH + h * HEAD_DIM, part * POOL_WIDTH + (h + 1) * HEAD_DIM)

    gate, cum, gate_t, cum_t = {}, {}, {}, {}
    for c in range(NC):
        for s in range(S):
            val = _gate_values(g[rows(s, c), :], gb_ref[...])
            v0, v1, v2 = _split3(val)
            gate[s, c] = val
            cum[s, c] = _dot(tril, v0) + _dot(tril, v1) + _dot(tril, v2)
    for key in gate:
        gate_t[key] = gate[key].T
        cum_t[key] = cum[key].T
    ahead()
    qf = [proj[rows(s, c), head_cols(1, h)] for s, c, h in units]
    kf = [proj[rows(s, c), head_cols(2, h)] * (HEAD_DIM ** -0.5) for s, c, h in units]
    vf = [proj[rows(s, c), head_cols(3, h)] for s, c, h in units]
    qb = [a.astype(BF16) for a in qf]
    kb = [a.astype(BF16) for a in kf]
    f_col = [cum[s, c][:, HEADS + h:HEADS + h + 1] for s, c, h in units]
    ahead()
    log_d = [jnp.where(causal, f_col[u] - cum_t[s, c][HEADS + h:HEADS + h + 1, :] + gate_t[s, c][h:h + 1, :],
                       -jnp.inf) for u, (s, c, h) in enumerate(units)]
    ahead()
    row_max = [jnp.max(log_d[u], axis=-1, keepdims=True) for u in U]
    ahead()
    qk_raw = [_dot_nt(qb[u], kb[u]) for u in U]

    m_prev, m_t, inter = [None] * len(units), [None] * len(units), [None] * len(units)
    m_run = {(s, h): m_s[s, h:h + 1, 0:1] for s, h in chains}
    for u, (s, c, h) in enumerate(units):
        m_prev[u] = m_run[s, h]
        inter[u] = m_prev[u] + f_col[u]
        m_t[u] = jnp.maximum(inter[u], row_max[u])
        m_run[s, h] = m_t[u][L - 1:L, :]
    m_new = [m_t[u][L - 1:L, :] for u in U]

    ahead()
    dw = [jnp.exp(log_d[u] - m_t[u]) for u in U]
    sc = [jnp.exp(inter[u] - m_t[u]) for u in U]
    ahead()
    qk = [qk_raw[u] * dw[u] for u in U]
    ahead()
    intra = [_dot(qk[u].astype(BF16), vf[u].astype(BF16)) for u in U]
    ahead()
    row_sum = [jnp.sum(qk[u], axis=-1, keepdims=True) for u in U]
    floor = [jnp.exp(-m_t[u]) for u in U]
    f_last = [f_col[u][L - 1:L, :] for u in U]
    ahead()
    wk = [jnp.exp(gate[s, c][:, h:h + 1] + f_last[u] - f_col[u] - m_new[u]) for u, (s, c, h) in enumerate(units)]
    decay = [jnp.exp(m_prev[u] + f_last[u] - m_new[u]) for u in U]
    ahead()
    upd = [_dot_tn((vf[u] * wk[u]).astype(BF16), kb[u]) for u in U]
    ahead()
    n_upd = [jnp.sum(wk[u] * kf[u], axis=0, keepdims=True) for u in U]

    c_run = {(s, h): c_s[s, h] for s, h in chains}
    n_run = {(s, h): n_s[s, h:h + 1, :] for s, h in chains}
    hh = [None] * len(units)
    for c in range(NC):
        cu = [u for u in U if units[u][1] == c]
        inter_term = {u: _dot_nt(qb[u], c_run[units[u][0], units[u][2]].astype(BF16)) for u in cu}
        n_term = {u: jnp.sum(qf[u] * n_run[units[u][0], units[u][2]], axis=-1, keepdims=True) for u in cu}
        for u in cu:
            s, _, h = units[u]
            num = intra[u] + sc[u] * inter_term[u]
            den = row_sum[u] + sc[u] * n_term[u]
            hh[u] = num / jnp.maximum(jnp.abs(den), floor[u])
            c_run[s, h] = decay[u] * c_run[s, h] + upd[u]
            n_run[s, h] = decay[u] * n_run[s, h] + n_upd[u]
    ahead(len(pending))
    for s, h in chains:
        c_s[s, h] = c_run[s, h]
        n_s[s, h:h + 1, :] = n_run[s, h]
        m_s[s, h:h + 1, :] = jnp.broadcast_to(m_run[s, h], (1, LANES))
    for u, (s, c, h) in enumerate(units):
        mixbuf[rows(s, c), head_cols(1, h)] = _head_out(
            hh[u], proj[rows(s, c), head_cols(4, h)], mhg_ref[:, h * HEAD_DIM:(h + 1) * HEAD_DIM])

    @pl.when(ti == nt - 1)
    def _():
        for s in range(S):
            c_out_ref[s, 0] = c_s[s]
            n_out_ref[s, 0] = n_s[s, 0:HEADS, :]
            m_out_ref[s, 0] = m_s[s]

    mix = _dot(mixbuf[...].astype(BF16), wout_ref[...])
    x1 = _layer_norm(DN_ALPHA * x + mix, ln1g_ref[...], ln1b_ref[...])
    for s in range(S):
        x1_ref[s] = x1[s * TT:(s + 1) * TT, :]


def _prompt_mixer(x, w_in_b, w_g_b, gbias, w_pool_b, pscale, mhg, w_out_b, ln1g, ln1b):
    nt = SEQ // MIX_TILE
    S = MIX_SEQS
    G = BATCH // S
    const2 = lambda b, t: (0, 0)
    const3 = lambda b, t: (0, 0, 0)

    def next_tile(b, t):
        nxt = jnp.minimum(b * nt + t + 1, G * nt - 1)
        return (0, nxt // nt, nxt % nt, 0)

    outs = pl.pallas_call(
        _prompt_mixer_kernel,
        grid=(G, nt),
        in_specs=[
            pl.BlockSpec((S, 1, MIX_TILE, D_MODEL), lambda b, t: (0, b, t, 0)),
            pl.BlockSpec((S, 1, MIX_TILE, D_MODEL), next_tile),
            pl.BlockSpec(w_in_b.shape, const2),
            pl.BlockSpec(w_g_b.shape, const2),
            pl.BlockSpec(gbias.shape, const2),
            pl.BlockSpec(w_pool_b.shape, const3),
            pl.BlockSpec(pscale.shape, const2),
            pl.BlockSpec(mhg.shape, const2),
            pl.BlockSpec(w_out_b.shape, const2),
            pl.BlockSpec(ln1g.shape, const2),
            pl.BlockSpec(ln1b.shape, const2),
        ],
        out_specs=[
            pl.BlockSpec((S, MIX_TILE, D_MODEL), lambda b, t: (0, b * nt + t, 0)),
            pl.BlockSpec((S, 1, POOL_HIST, POOL_WIDTH), lambda b, t: (0, b, 0, 0)),
            pl.BlockSpec((S, 1, HEADS, HEAD_DIM, HEAD_DIM), lambda b, t: (0, b, 0, 0, 0)),
            pl.BlockSpec((S, 1, HEADS, HEAD_DIM), lambda b, t: (0, b, 0, 0)),
            pl.BlockSpec((S, 1, SUBLANES, LANES), lambda b, t: (0, b, 0, 0)),
        ],
        out_shape=[
            jax.ShapeDtypeStruct((S, G * SEQ, D_MODEL), F32),
            jax.ShapeDtypeStruct((S, G, POOL_HIST, POOL_WIDTH), F32),
            jax.ShapeDtypeStruct((S, G, HEADS, HEAD_DIM, HEAD_DIM), F32),
            jax.ShapeDtypeStruct((S, G, HEADS, HEAD_DIM), F32),
            jax.ShapeDtypeStruct((S, G, SUBLANES, LANES), F32),
        ],
        scratch_shapes=[
            pltpu.VMEM((S, HIST_PAD + MIX_TILE, POOL_WIDTH), F32),
            pltpu.VMEM((S * MIX_TILE, D_MODEL), F32),
            pltpu.VMEM((2, S * MIX_TILE, POOL_WIDTH + 4 * MLSTM_WIDTH), F32),
            pltpu.VMEM((2, S * MIX_TILE, LANES), F32),
            pltpu.VMEM((S, HEADS, HEAD_DIM, HEAD_DIM), F32),
            pltpu.VMEM((S, SUBLANES, HEAD_DIM), F32),
            pltpu.VMEM((S, SUBLANES, LANES), F32),
        ],
        compiler_params=pltpu.CompilerParams(
            dimension_semantics=("arbitrary", "arbitrary"), vmem_limit_bytes=VMEM_LIMIT),
        name="prompt_mixer",
    )(x.reshape(S, G, SEQ, D_MODEL), x.reshape(S, G, SEQ, D_MODEL), w_in_b, w_g_b, gbias, w_pool_b, pscale, mhg, w_out_b, ln1g, ln1b)
    x1, pool, c, n, m = outs
    return (x1.reshape(N_PROMPT, D_MODEL), pool.reshape(BATCH, POOL_HIST, POOL_WIDTH),
            c.reshape(BATCH, HEADS, HEAD_DIM, HEAD_DIM), n.reshape(BATCH, HEADS, HEAD_DIM),
            m.reshape(BATCH, SUBLANES, LANES))


def _sample_mixer_kernel(x_ref, hist_ref, c_ref, n_ref, m_ref, win_ref, wg_ref, gb_ref, wpool_ref,
                         pscale_ref, mhg_ref, wout_ref, ln1g_ref, ln1b_ref,
                         x1_ref, pool_out_ref, c_out_ref, n_out_ref, m_out_ref,
                         q_s, k_s, vw_s, v_s, o_s, mixbuf, h_s, coef_s):
    i = pl.program_id(0)
    nsteps = pl.num_programs(0)
    B = DEC_BATCH

    @pl.when(i == 0)
    def _():
        x = x_ref[...]
        xb = x.astype(BF16)
        proj = _dot(xb, win_ref[...])
        g = _dot(xb, wg_ref[...])
        u = proj[:, 0:POOL_WIDTH]
        for gi, w in enumerate(POOL_WINDOWS):
            sl = slice(gi * POOL_GROUP_DIM, (gi + 1) * POOL_GROUP_DIM)
            ug = u[:, sl]
            s = ug
            for j in range(1, w):
                r = POOL_HIST - j
                s = s + hist_ref[:, r * POOL_WIDTH + gi * POOL_GROUP_DIM:r * POOL_WIDTH + (gi + 1) * POOL_GROUP_DIM]
            cnt = float(min(PAST_LEN + 1, w))
            z = s / cnt - ug
            mixbuf[:, sl] = _dot(z.astype(BF16), wpool_ref[gi]) * pscale_ref[:, sl]
        pool_out_ref[:, 0:(POOL_HIST - 1) * POOL_WIDTH] = hist_ref[:, POOL_WIDTH:POOL_HIST * POOL_WIDTH]
        pool_out_ref[:, (POOL_HIST - 1) * POOL_WIDTH:POOL_HIST * POOL_WIDTH] = u

        val = _gate_values(g, gb_ref[...])
        lane = lax.broadcasted_iota(I32, (B, LANES), 1)
        qk_all = jnp.zeros((B, LANES), F32)
        sc_all = jnp.zeros((B, LANES), F32)
        den_all = jnp.zeros((B, LANES), F32)
        floor_all = jnp.zeros((B, LANES), F32)
        m_all = jnp.zeros((B, LANES), F32)
        for h in range(HEADS):
            hs = slice(h * HEAD_DIM, (h + 1) * HEAD_DIM)
            qf = proj[:, POOL_WIDTH + h * HEAD_DIM:POOL_WIDTH + (h + 1) * HEAD_DIM]
            kf = proj[:, 2 * POOL_WIDTH + h * HEAD_DIM:2 * POOL_WIDTH + (h + 1) * HEAD_DIM] * (HEAD_DIM ** -0.5)
            vf = proj[:, 3 * POOL_WIDTH + h * HEAD_DIM:3 * POOL_WIDTH + (h + 1) * HEAD_DIM]
            ig = val[:, h:h + 1]
            lf = val[:, HEADS + h:HEADS + h + 1]
            m0 = m_ref[:, h:h + 1]
            n0 = n_ref[:, hs]
            inter = m0 + lf
            m_t = jnp.maximum(inter, ig)
            dw = jnp.exp(ig - m_t)
            sc = jnp.exp(inter - m_t)
            qk = jnp.sum(qf * kf, axis=-1, keepdims=True) * dw
            den = qk + sc * jnp.sum(qf * n0, axis=-1, keepdims=True)
            n_out_ref[:, hs] = sc * n0 + dw * kf
            q_s[0:B, hs] = qf
            k_s[0:B, hs] = kf
            v_s[0:B, hs] = vf
            vw_s[0:B, hs] = vf * dw
            sel = lane == h
            qk_all = jnp.where(sel, qk, qk_all)
            sc_all = jnp.where(sel, sc, sc_all)
            den_all = jnp.where(sel, den, den_all)
            floor_all = jnp.where(sel, jnp.exp(-m_t), floor_all)
            m_all = jnp.where(lane == HEADS + h, m_t, m_all)
        o_s[...] = proj[:, 4 * POOL_WIDTH:5 * POOL_WIDTH]
        coef_s[0] = qk_all
        coef_s[1] = sc_all
        coef_s[2] = den_all
        coef_s[3] = floor_all
        m_out_ref[...] = m_all

    rows = pl.ds(pl.multiple_of(i * SAMPLE_BT, SAMPLE_BT), SAMPLE_BT)
    q_t, k_t, v_t, vw_t = q_s[rows, :], k_s[rows, :], v_s[rows, :], vw_s[rows, :]
    qk_t, sc_t, den_t, floor_t = coef_s[0, rows, :], coef_s[1, rows, :], coef_s[2, rows, :], coef_s[3, rows, :]
    h_rows = []
    for bl in range(SAMPLE_BT):
        heads = []
        for h in range(HEADS):
            hs = slice(h * HEAD_DIM, (h + 1) * HEAD_DIM)
            c_prev = c_ref[bl, h]
            q8 = jnp.broadcast_to(q_t[bl:bl + 1, hs], (SUBLANES, HEAD_DIM))
            cq = _dot_nt(q8.astype(BF16), c_prev.astype(BF16))[0:1, :]
            qk = qk_t[bl:bl + 1, h:h + 1]
            sc = sc_t[bl:bl + 1, h:h + 1]
            num = qk * v_t[bl:bl + 1, hs] + sc * cq
            heads.append(num / jnp.maximum(jnp.abs(den_t[bl:bl + 1, h:h + 1]), floor_t[bl:bl + 1, h:h + 1]))
            v_col = jnp.broadcast_to(vw_t[bl:bl + 1, hs], (HEAD_DIM, HEAD_DIM)).T
            c_out_ref[bl, h] = sc * c_prev + v_col * k_t[bl:bl + 1, hs]
        h_rows.append(jnp.concatenate(heads, axis=1))
    h_s[rows, :] = jnp.concatenate(h_rows, axis=0)

    @pl.when(i == nsteps - 1)
    def _():
        for h in range(HEADS):
            hs = slice(h * HEAD_DIM, (h + 1) * HEAD_DIM)
            mixbuf[:, POOL_WIDTH + h * HEAD_DIM:POOL_WIDTH + (h + 1) * HEAD_DIM] = _head_out(
                h_s[:, hs], o_s[:, hs], mhg_ref[:, hs])
        mix = _dot(mixbuf[...].astype(BF16), wout_ref[...])
        x1 = _layer_norm(DN_ALPHA * x_ref[...] + mix, ln1g_ref[...], ln1b_ref[...])
        x1_ref[0:B, :] = x1
        x1_ref[B:TOK_TILE, :] = jnp.zeros((TOK_TILE - B, D_MODEL), F32)


def _sample_mixer(x, hist2, c0, n0, m0, w_in_b, w_g_b, gbias, w_pool_b, pscale, mhg, w_out_b, ln1g, ln1b):
    B = DEC_BATCH
    steps = B // SAMPLE_BT
    full = lambda a: pl.BlockSpec(a.shape, lambda i: (0,) * a.ndim)
    c_spec = pl.BlockSpec((SAMPLE_BT, HEADS, HEAD_DIM, HEAD_DIM), lambda i: (i, 0, 0, 0))
    return pl.pallas_call(
        _sample_mixer_kernel,
        grid=(steps,),
        in_specs=[full(x), full(hist2), c_spec, full(n0), full(m0), full(w_in_b), full(w_g_b), full(gbias),
                  full(w_pool_b), full(pscale), full(mhg), full(w_out_b), full(ln1g), full(ln1b)],
        out_specs=[
            pl.BlockSpec((TOK_TILE, D_MODEL), lambda i: (0, 0)),
            pl.BlockSpec((B, POOL_HIST * POOL_WIDTH), lambda i: (0, 0)),
            c_spec,
            pl.BlockSpec((B, MLSTM_WIDTH), lambda i: (0, 0)),
            pl.BlockSpec((B, LANES), lambda i: (0, 0)),
        ],
        out_shape=[
            jax.ShapeDtypeStruct((TOK_TILE, D_MODEL), F32),
            jax.ShapeDtypeStruct((B, POOL_HIST * POOL_WIDTH), F32),
            jax.ShapeDtypeStruct((B, HEADS, HEAD_DIM, HEAD_DIM), F32),
            jax.ShapeDtypeStruct((B, MLSTM_WIDTH), F32),
            jax.ShapeDtypeStruct((B, LANES), F32),
        ],
        scratch_shapes=[
            pltpu.VMEM((B, MLSTM_WIDTH), F32),
            pltpu.VMEM((B, MLSTM_WIDTH), F32),
            pltpu.VMEM((B, MLSTM_WIDTH), F32),
            pltpu.VMEM((B, MLSTM_WIDTH), F32),
            pltpu.VMEM((B, MLSTM_WIDTH), F32),
            pltpu.VMEM((B, D_MODEL), F32),
            pltpu.VMEM((B, MLSTM_WIDTH), F32),
            pltpu.VMEM((4, B, LANES), F32),
        ],
        compiler_params=pltpu.CompilerParams(
            dimension_semantics=("arbitrary",), vmem_limit_bytes=VMEM_LIMIT),
        name="sample_mixer",
    )(x, hist2, c0, n0, m0, w_in_b, w_g_b, gbias, w_pool_b, pscale, mhg, w_out_b, ln1g, ln1b)


def _pick_tile(i, prompt_ref, sample_ref):
    return jnp.where(i < N_PROMPT_TILES, prompt_ref[...], sample_ref[...])


def _placement(slot_rows, group):
    r = group * TOK_TILE + lax.broadcasted_iota(I32, (TOK_TILE, TOK_TILE), 0)
    return [r == s for s in slot_rows]


def _route_kernel(xp_ref, xs_ref, wrt_ref, br_ref, slot_ref, gate_ref, nch_ref, sorted_ref):
    i = pl.program_id(0)
    T = TOK_TILE
    E = N_EXPERTS

    x = _pick_tile(i, xp_ref, xs_ref)
    xh = x.astype(BF16)
    xl = (x - xh.astype(F32)).astype(BF16)
    w = wrt_ref[...]
    wh = w.astype(BF16)
    wl = (w - wh.astype(F32)).astype(BF16)
    logits = _dot_nt(wh, xh) + (_dot_nt(wh, xl) + _dot_nt(wl, xh)) + br_ref[:, 0:1]

    erow = lax.broadcasted_iota(I32, (E, T), 0).astype(F32)
    work = logits
    vals, sels = [], []
    for _ in range(TOP_K):
        mx = jnp.max(work, axis=0, keepdims=True)
        idx = jnp.min(jnp.where(work == mx, erow, float(E)), axis=0, keepdims=True)
        sel = erow == idx
        work = jnp.where(sel, -jnp.inf, work)
        vals.append(mx)
        sels.append(sel)
    chosen = jnp.logical_or(jnp.logical_or(sels[0], sels[1]), jnp.logical_or(sels[2], sels[3]))
    es = [jnp.exp(v - vals[0]) for v in vals]
    tot = es[0] + es[1] + es[2] + es[3]

    n_valid = jnp.where(i < N_PROMPT_TILES, T, DEC_BATCH)
    valid = lax.broadcasted_iota(I32, (1, T), 1) < n_valid
    onehot = jnp.where(jnp.logical_and(chosen, valid), 1.0, 0.0)
    trow = lax.broadcasted_iota(I32, (T, T), 0)
    tcol = lax.broadcasted_iota(I32, (T, T), 1)
    before = jnp.where(trow < tcol, 1.0, 0.0).astype(BF16)
    rank = _dot(onehot.astype(BF16), before)
    cnt = jnp.sum(onehot, axis=1, keepdims=True)
    nch = jnp.floor((cnt + (CHUNK_ROWS - 1)) * (1.0 / CHUNK_ROWS))
    lower = jnp.where(lax.broadcasted_iota(I32, (E, E), 0) > lax.broadcasted_iota(I32, (E, E), 1), 1.0, 0.0)
    nch_b = jnp.broadcast_to(nch, (E, LANES))
    seg_start = _dot(lower.astype(BF16), nch_b.astype(BF16))[:, 0:1] * CHUNK_ROWS
    base = seg_start + rank

    r8 = lax.broadcasted_iota(I32, (SUBLANES, T), 0)
    s_out = jnp.zeros((SUBLANES, T), I32)
    g_out = jnp.zeros((SUBLANES, T), F32)
    slot_rows = []
    for j in range(TOP_K):
        slot_j = jnp.sum(jnp.where(sels[j], base, 0.0), axis=0, keepdims=True).astype(I32)
        slot_j = jnp.where(valid, slot_j, -1)
        slot_rows.append(slot_j)
        s_out = jnp.where(r8 == j, slot_j, s_out)
        g_out = jnp.where(r8 == j, es[j] / tot, g_out)
    slot_ref[0] = s_out
    gate_ref[0] = g_out
    nch_ref[0] = nch_b

    for grp in range(GROUPS):
        m = _placement(slot_rows, grp)
        hit = jnp.logical_or(jnp.logical_or(m[0], m[1]), jnp.logical_or(m[2], m[3]))
        place = jnp.where(hit, 1.0, 0.0).astype(BF16)
        sorted_ref[grp * T:(grp + 1) * T, :] = _dot(place, xh).astype(BF16)


def _route(x1p, x1s, w_router_t, b_router_col):
    tile_spec = pl.BlockSpec((1, SUBLANES, TOK_TILE), lambda i: (i, 0, 0))
    return pl.pallas_call(
        _route_kernel,
        grid=(N_TILES,),
        in_specs=[
            pl.BlockSpec((TOK_TILE, D_MODEL), lambda i: (jnp.minimum(i, N_PROMPT_TILES - 1), 0)),
            pl.BlockSpec((TOK_TILE, D_MODEL), lambda i: (0, 0)),
            pl.BlockSpec(w_router_t.shape, lambda i: (0, 0)),
            pl.BlockSpec(b_router_col.shape, lambda i: (0, 0)),
        ],
        out_specs=[tile_spec, tile_spec,
                   pl.BlockSpec((1, N_EXPERTS, LANES), lambda i: (i, 0, 0)),
                   pl.BlockSpec((LOCAL_ROWS, D_MODEL), lambda i: (i, 0))],
        out_shape=[
            jax.ShapeDtypeStruct((N_TILES, SUBLANES, TOK_TILE), I32),
            jax.ShapeDtypeStruct((N_TILES, SUBLANES, TOK_TILE), F32),
            jax.ShapeDtypeStruct((N_TILES, N_EXPERTS, LANES), F32),
            jax.ShapeDtypeStruct((N_TILES * LOCAL_ROWS, D_MODEL), BF16),
        ],
        compiler_params=pltpu.CompilerParams(
            dimension_semantics=("arbitrary",), vmem_limit_bytes=VMEM_LIMIT),
        name="route",
    )(x1p, x1s, w_router_t, b_router_col)


def _expert_kernel(src_ref, dst_ref, bexp_ref, first_ref, next_ref, half_ref, nused_ref,
                   sorted_hbm, w1_hbm, b1_ref, w2_hbm, b2_ref, out_hbm,
                   w1_stage, w2_stage, w1_b, w2_b, xbuf, obuf, zbuf, wsem, gsem, ssem, zsem):
    nused = nused_ref[0]
    obuf[...] = jnp.zeros_like(obuf)

    def fetch_piece(e, p):
        r1 = pl.ds(pl.multiple_of(p * (D_MODEL // WEIGHT_PIECES), SUBLANES), D_MODEL // WEIGHT_PIECES)
        r2 = pl.ds(pl.multiple_of(p * (D_FF // WEIGHT_PIECES), SUBLANES), D_FF // WEIGHT_PIECES)
        return (pltpu.make_async_copy(w1_hbm.at[e, r1, :], w1_stage.at[r1, :], wsem.at[0]),
                pltpu.make_async_copy(w2_hbm.at[e, r2, :], w2_stage.at[r2, :], wsem.at[1]))

    def start_pieces(e, lo, hi):
        def body(p, c):
            for cp in fetch_piece(e, p):
                cp.start()
            return c
        lax.fori_loop(lo, hi, body, 0)

    def gather(b, q):
        slot = lax.rem(b, GATHER_DEPTH)
        row = pl.multiple_of(src_ref[b * BLOCK_CHUNKS + q], CHUNK_ROWS)
        return pltpu.make_async_copy(sorted_hbm.at[pl.ds(row, CHUNK_ROWS), :],
                                     xbuf.at[slot, pl.ds(q * CHUNK_ROWS, CHUNK_ROWS), :], gsem.at[slot])

    def scatter(b, q):
        slot = lax.rem(b, 2)
        row = pl.multiple_of(dst_ref[b * BLOCK_CHUNKS + q], CHUNK_ROWS)
        return pltpu.make_async_copy(obuf.at[slot, pl.ds(q * CHUNK_ROWS, CHUNK_ROWS), :],
                                     out_hbm.at[pl.ds(row, CHUNK_ROWS), :], ssem.at[slot])

    def zero_rows(start, n_rows):
        start = pl.multiple_of(start, CHUNK_ROWS)
        return pltpu.make_async_copy(zbuf.at[pl.ds(0, n_rows), :], out_hbm.at[pl.ds(start, n_rows), :], zsem)

    def zero_tail(k):
        return zero_rows(k * LOCAL_ROWS + TOK_TILE * TOP_K, FREE_ROWS)

    def zero_last_tile(part):
        return zero_rows((N_TILES - 1) * LOCAL_ROWS + part * FREE_ROWS, FREE_ROWS)

    zbuf[...] = jnp.zeros_like(zbuf)
    lax.fori_loop(0, N_TILES, lambda k, c: (zero_tail(k).start(), c)[1], 0)
    for part in range(TOK_TILE * TOP_K // FREE_ROWS):
        zero_last_tile(part).start()
    zero_rows(DUMP_BASE, DUMP_ROWS).start()
    for ahead in range(GATHER_DEPTH - 1):
        @pl.when(ahead < nused)
        def _():
            for q in range(BLOCK_CHUNKS):
                gather(ahead, q).start()
    lax.fori_loop(0, N_TILES, lambda k, c: (zero_tail(k).wait(), c)[1], 0)
    for part in range(TOK_TILE * TOP_K // FREE_ROWS):
        zero_last_tile(part).wait()
    zero_rows(DUMP_BASE, DUMP_ROWS).wait()

    half = MXU_COLS // 2
    k_io = lax.broadcasted_iota(I32, (MXU_COLS, MXU_COLS), 0)
    j_io = lax.broadcasted_iota(I32, (MXU_COLS, MXU_COLS), 1)
    src_col = jnp.where(j_io < half, 2 * j_io, 2 * (j_io - half) + 1)
    perm = jnp.where(k_io == src_col, 1.0, 0.0).astype(BF16)

    def block(i, fetched):
        e = bexp_ref[i]
        slot = lax.rem(i, 2)
        is_first = first_ref[i] == 1

        @pl.when(is_first)
        def _():
            start_pieces(e, fetched, WEIGHT_PIECES)

            def wait_piece(p, c):
                for cp in fetch_piece(e, p):
                    cp.wait()
                return c
            lax.fori_loop(0, WEIGHT_PIECES, wait_piece, 0)
            for c in range(2 * D_FF // MXU_COLS):
                blk = w1_stage[:, c * MXU_COLS:(c + 1) * MXU_COLS].astype(BF16)
                sep = _dot(blk, perm).astype(BF16)
                w1_b[:, c * half:(c + 1) * half] = sep[:, 0:half]
                w1_b[:, D_FF + c * half:D_FF + (c + 1) * half] = sep[:, half:MXU_COLS]
            w2_b[...] = w2_stage[...].astype(BF16)

        fetched = jnp.where(is_first, 0, fetched)

        for q in range(BLOCK_CHUNKS):
            gather(i, q).wait()

        @pl.when(i + GATHER_DEPTH - 1 < nused)
        def _():
            for q in range(BLOCK_CHUNKS):
                gather(i + GATHER_DEPTH - 1, q).start()

        @pl.when(i >= 2)
        def _():
            for q in range(BLOCK_CHUNKS):
                scatter(i - 2, q).wait()

        def ffn(n_rows):
            h = _dot(xbuf[lax.rem(i, GATHER_DEPTH), 0:n_rows, :], w1_b[...]) + b1_ref[e]
            glu = jnp.minimum(h[:, 0:D_FF], SWIGLU_LIMIT)
            lin = jnp.clip(h[:, D_FF:2 * D_FF], -SWIGLU_LIMIT, SWIGLU_LIMIT)
            a = glu * jax.nn.sigmoid(SWIGLU_ALPHA * glu) * (lin + 1.0)
            obuf[slot, 0:n_rows, :] = (_dot(a.astype(BF16), w2_b[...]) + b2_ref[e]).astype(BF16)

        @pl.when(half_ref[i] == 0)
        def _():
            ffn(ROW_BLOCK)

        @pl.when(half_ref[i] == 1)
        def _():
            ffn(ROW_BLOCK // 2)

        for q in range(BLOCK_CHUNKS):
            scatter(i, q).start()

        more = jnp.where(next_ref[i] >= 0, jnp.minimum(fetched + PIECES_PER_BLOCK, WEIGHT_PIECES), fetched)
        start_pieces(next_ref[i], fetched, more)
        return more

    lax.fori_loop(0, nused, block, jnp.int32(0))

    @pl.when(nused >= 2)
    def _():
        for q in range(BLOCK_CHUNKS):
            scatter(nused - 2, q).wait()
    for q in range(BLOCK_CHUNKS):
        scatter(nused - 1, q).wait()


def _experts(chunk_src, chunk_dst, block_expert, block_first, block_next, block_half, n_used,
             sorted_rows, w1, b1p, w2, b2):
    whole3 = lambda i, *_: (0, 0, 0)
    grid_spec = pltpu.PrefetchScalarGridSpec(
        num_scalar_prefetch=7,
        grid=(1,),
        in_specs=[
            pl.BlockSpec(memory_space=pl.ANY),
            pl.BlockSpec(memory_space=pl.ANY),
            pl.BlockSpec(b1p.shape, whole3),
            pl.BlockSpec(memory_space=pl.ANY),
            pl.BlockSpec(b2.shape, whole3),
        ],
        out_specs=pl.BlockSpec(memory_space=pl.ANY),
        scratch_shapes=[
            pltpu.VMEM((D_MODEL, 2 * D_FF), F32),
            pltpu.VMEM((D_FF, D_MODEL), F32),
            pltpu.VMEM((D_MODEL, 2 * D_FF), BF16),
            pltpu.VMEM((D_FF, D_MODEL), BF16),
            pltpu.VMEM((GATHER_DEPTH, ROW_BLOCK, D_MODEL), BF16),
            pltpu.VMEM((2, ROW_BLOCK, D_MODEL), BF16),
            pltpu.VMEM((FREE_ROWS, D_MODEL), BF16),
            pltpu.SemaphoreType.DMA((2,)),
            pltpu.SemaphoreType.DMA((GATHER_DEPTH,)),
            pltpu.SemaphoreType.DMA((2,)),
            pltpu.SemaphoreType.DMA(()),
        ],
    )
    return pl.pallas_call(
        _expert_kernel,
        grid_spec=grid_spec,
        out_shape=jax.ShapeDtypeStruct((DUMP_BASE + DUMP_ROWS, D_MODEL), BF16),
        compiler_params=pltpu.CompilerParams(
            dimension_semantics=("arbitrary",), vmem_limit_bytes=VMEM_LIMIT),
        name="experts",
    )(chunk_src, chunk_dst, block_expert, block_first, block_next, block_half, n_used,
      sorted_rows, w1, b1p, w2, b2)


def _combine_kernel(slot_ref, gate_ref, xp_ref, xs_ref, pp_ref, ps_ref, eo_ref, ln2g_ref, ln2b_ref,
                    wpg_ref, wple_ref, yp_ref, ys_ref):
    i = pl.program_id(0)
    T = TOK_TILE

    slot_rows = [slot_ref[0, j:j + 1, :] for j in range(TOP_K)]
    gate_rows = [gate_ref[0, j:j + 1, :] for j in range(TOP_K)]
    pad = jnp.zeros((LANES - SUBLANES, T), F32)
    slots_t = jnp.concatenate([slot_ref[0].astype(F32), pad], axis=0).T
    slot_cols = [slots_t[:, j:j + 1].astype(I32) for j in range(TOP_K)]

    ff = jnp.zeros((T, D_MODEL), F32)
    for grp in range(GROUPS):
        m = _placement(slot_rows, grp)
        weighted = jnp.where(m[0], gate_rows[0], jnp.where(m[1], gate_rows[1], jnp.where(
            m[2], gate_rows[2], jnp.where(m[3], gate_rows[3], 0.0))))
        g_col = jnp.sum(weighted, axis=1, keepdims=True)
        z = (eo_ref[grp * T:(grp + 1) * T, :].astype(F32) * g_col).astype(BF16)
        r = grp * T + lax.broadcasted_iota(I32, (T, T), 1)
        hit = jnp.logical_or(jnp.logical_or(r == slot_cols[0], r == slot_cols[1]),
                             jnp.logical_or(r == slot_cols[2], r == slot_cols[3]))
        ff = ff + _dot(jnp.where(hit, 1.0, 0.0).astype(BF16), z)

    x1 = _pick_tile(i, xp_ref, xs_ref)
    x2 = _layer_norm(DN_ALPHA * x1 + ff, ln2g_ref[...], ln2b_ref[...])
    p = _pick_tile(i, pp_ref, ps_ref)
    y = x2 + jax.nn.sigmoid(_dot(x2.astype(BF16), wpg_ref[...])) * _dot(p.astype(BF16), wple_ref[...])

    @pl.when(i < N_PROMPT_TILES)
    def _():
        yp_ref[...] = y

    @pl.when(i == N_PROMPT_TILES)
    def _():
        ys_ref[...] = y[0:DEC_BATCH, :]


def _combine(slots, gates, x1p, x1s, pp, ps, expert_out, ln2g, ln2b, w_pg_b, w_ple_b):
    tile_idx = lambda i: (jnp.minimum(i, N_PROMPT_TILES - 1), 0)
    const2 = lambda i: (0, 0)
    return pl.pallas_call(
        _combine_kernel,
        grid=(N_TILES,),
        in_specs=[
            pl.BlockSpec((1, SUBLANES, TOK_TILE), lambda i: (i, 0, 0)),
            pl.BlockSpec((1, SUBLANES, TOK_TILE), lambda i: (i, 0, 0)),
            pl.BlockSpec((TOK_TILE, D_MODEL), tile_idx),
            pl.BlockSpec((TOK_TILE, D_MODEL), const2),
            pl.BlockSpec((TOK_TILE, PLE_DIM), tile_idx),
            pl.BlockSpec((TOK_TILE, PLE_DIM), const2),
            pl.BlockSpec((LOCAL_ROWS, D_MODEL), lambda i: (i, 0)),
            pl.BlockSpec(ln2g.shape, const2),
            pl.BlockSpec(ln2b.shape, const2),
            pl.BlockSpec(w_pg_b.shape, const2),
            pl.BlockSpec(w_ple_b.shape, const2),
        ],
        out_specs=[
            pl.BlockSpec((TOK_TILE, D_MODEL), tile_idx),
            pl.BlockSpec((DEC_BATCH, D_MODEL), const2),
        ],
        out_shape=[
            jax.ShapeDtypeStruct((N_PROMPT, D_MODEL), F32),
            jax.ShapeDtypeStruct((DEC_BATCH, D_MODEL), F32),
        ],
        compiler_params=pltpu.CompilerParams(
            dimension_semantics=("arbitrary",), vmem_limit_bytes=VMEM_LIMIT),
        name="combine",
    )(slots, gates, x1p, x1s, pp, ps, expert_out, ln2g, ln2b, w_pg_b, w_ple_b)


def _block_tables(nch):
    seg_start = (jnp.cumsum(nch, axis=1) - nch) * CHUNK_ROWS
    tot = jnp.sum(nch, axis=0)
    nblk = (tot + BLOCK_CHUNKS - 1) // BLOCK_CHUNKS
    blk_end = jnp.cumsum(nblk)
    blk_start = blk_end - nblk
    n_used = blk_end[-1:].astype(I32)
    blk_ids = jnp.arange(N_BLOCKS, dtype=I32)

    e_ids = jnp.arange(N_EXPERTS, dtype=I32)
    used = nblk > 0
    blk_clamped = jnp.minimum(blk_ids, n_used[0] - 1)
    block_expert = jnp.minimum(jnp.sum(blk_end[:, None] <= blk_clamped[None, :], axis=0), N_EXPERTS - 1).astype(I32)
    starts_here = jnp.logical_and(blk_start[:, None] == blk_ids[None, :], used[:, None])
    block_first = jnp.any(starts_here, axis=0).astype(I32)
    later_used = jnp.logical_and(e_ids[None, :] > e_ids[:, None], used[None, :])
    next_used = jnp.min(jnp.where(later_used, e_ids[None, :], N_EXPERTS), axis=1)
    next_used = jnp.where(next_used < N_EXPERTS, next_used, -1)
    owner = block_expert[None, :] == e_ids[:, None]
    block_next = jnp.sum(jnp.where(owner, next_used[:, None], 0), axis=0).astype(I32)
    real_chunks = jnp.sum(jnp.where(owner, tot[:, None] - (blk_ids[None, :] - blk_start[:, None]) * BLOCK_CHUNKS, 0),
                          axis=0)
    block_half = (real_chunks <= BLOCK_CHUNKS // 2).astype(I32)

    nch_t = nch.T
    seg_first = (blk_start[:, None] * BLOCK_CHUNKS + jnp.cumsum(nch_t, axis=1) - nch_t).reshape(-1)
    seg_count = nch_t.reshape(-1)
    seg_row = (jnp.arange(N_TILES, dtype=I32)[None, :] * LOCAL_ROWS + seg_start.T).reshape(-1)
    ent = jnp.arange(N_BLOCKS * BLOCK_CHUNKS, dtype=I32)
    d = ent[None, :] - seg_first[:, None]
    inside = jnp.logical_and(d >= 0, d < seg_count[:, None])
    row_plus_1 = jnp.sum(jnp.where(inside, seg_row[:, None] + d * CHUNK_ROWS + 1, 0), axis=0)
    real = row_plus_1 > 0
    row = row_plus_1 - 1
    dump = DUMP_BASE + (((ent // BLOCK_CHUNKS) % 2) * BLOCK_CHUNKS + ent % BLOCK_CHUNKS) * CHUNK_ROWS
    chunk_src = jnp.where(real, row, ZERO_CHUNK_ROW).astype(I32)
    chunk_dst = jnp.where(real, row, dump).astype(I32)
    return chunk_src, chunk_dst, block_expert, block_first, block_next, block_half, n_used


def kernel(x_prompt, x_sample, state_pool, state_mlstm_C, state_mlstm_n, state_mlstm_m, p_prompt, p_sample, w_in, b_i, b_f, w_pool, pool_scale, mh_g, w_out, ln1_g, ln1_b, w_router, b_router, w_mlp1, b_mlp1, w_mlp2, b_mlp2, ln2_g, ln2_b, w_ple, w_ple_gate):
    n_main = POOL_WIDTH + 4 * MLSTM_WIDTH
    w_in_b = w_in[0, :, 0:n_main].astype(BF16)
    w_g_b = jnp.pad(w_in[0, :, n_main:], ((0, 0), (0, LANES - 2 * HEADS))).astype(BF16)
    gbias = jnp.pad(jnp.concatenate([b_i[0], b_f[0]]), (0, LANES - 2 * HEADS)).reshape(1, LANES)
    w_pool_b = w_pool[0].astype(BF16)
    pscale = pool_scale[0].reshape(1, POOL_WIDTH)
    mhg = mh_g[0].reshape(1, MLSTM_WIDTH)
    w_out_b = w_out[0].astype(BF16)
    ln1g = ln1_g[0].reshape(1, D_MODEL)
    ln1b = ln1_b[0].reshape(1, D_MODEL)
    ln2g = ln2_g[0].reshape(1, D_MODEL)
    ln2b = ln2_b[0].reshape(1, D_MODEL)
    w_router_t = w_router[0].T
    b_router_col = jnp.broadcast_to(b_router[0].reshape(N_EXPERTS, 1), (N_EXPERTS, LANES))
    b1 = b_mlp1[0]
    b1p = jnp.concatenate([b1[:, 0::2], b1[:, 1::2]], axis=-1).reshape(N_EXPERTS, 1, 2 * D_FF)
    b2 = b_mlp2[0].reshape(N_EXPERTS, 1, D_MODEL)
    w_pg_b = w_ple_gate[0].astype(BF16)
    w_ple_b = w_ple[0].astype(BF16)

    x1p, pool_p, c_p, n_p, m_p = _prompt_mixer(
        x_prompt, w_in_b, w_g_b, gbias, w_pool_b, pscale, mhg, w_out_b, ln1g, ln1b)
    x1s, pool_s, c_s, n_s, m_s = _sample_mixer(
        x_sample.reshape(DEC_BATCH, D_MODEL),
        state_pool[0].reshape(DEC_BATCH, POOL_HIST * POOL_WIDTH),
        state_mlstm_C[0], state_mlstm_n[0].reshape(DEC_BATCH, MLSTM_WIDTH), state_mlstm_m[0],
        w_in_b, w_g_b, gbias, w_pool_b, pscale, mhg, w_out_b, ln1g, ln1b)

    slots, gates, nch, sorted_rows = _route(x1p, x1s, w_router_t, b_router_col)
    tables = _block_tables(nch[:, :, 0].astype(I32))
    expert_out = _experts(*tables, sorted_rows, w_mlp1[0], b1p, w_mlp2[0], b2)

    pp = p_prompt[0].reshape(N_PROMPT, PLE_DIM)
    ps = jnp.pad(p_sample[0].reshape(DEC_BATCH, PLE_DIM), ((0, TOK_TILE - DEC_BATCH), (0, 0)))
    yp, ys = _combine(slots, gates, x1p, x1s, pp, ps, expert_out, ln2g, ln2b, w_pg_b, w_ple_b)

    return (
        yp.reshape(BATCH, SEQ, D_MODEL),
        ys.reshape(DEC_BATCH, 1, D_MODEL),
        pool_p.reshape(1, BATCH, POOL_HIST, POOL_WIDTH),
        c_p.reshape(1, BATCH, HEADS, HEAD_DIM, HEAD_DIM),
        n_p.reshape(1, BATCH, HEADS, HEAD_DIM),
        m_p[:, 0:HEADS, 0].reshape(1, BATCH, HEADS),
        pool_s.reshape(1, DEC_BATCH, POOL_HIST, POOL_WIDTH),
        c_s.reshape(1, DEC_BATCH, HEADS, HEAD_DIM, HEAD_DIM),
        n_s.reshape(1, DEC_BATCH, HEADS, HEAD_DIM),
        m_s[:, HEADS:2 * HEADS].reshape(1, DEC_BATCH, HEADS),
    )
```

```python
import jax
import jax.numpy as jnp
from jax import lax
from jax.experimental import pallas as pl
from jax.experimental.pallas import tpu as pltpu

F32 = jnp.float32
BF16 = jnp.bfloat16
I32 = jnp.int32

D_MODEL = 1024
BATCH = 8
SEQ = 2048
DEC_BATCH = 128
PAST_LEN = 16384
POOL_WIDTH = 512
POOL_GROUP_DIM = 128
POOL_WINDOWS = (2, 4, 8, 16)
POOL_HIST = 15
MLSTM_WIDTH = 512
HEADS = 4
HEAD_DIM = 128
CHUNK = 128
N_EXPERTS = 32
TOP_K = 4
D_FF = 1024
SWIGLU_ALPHA = 1.702
SWIGLU_LIMIT = 7.0
PLE_DIM = 256
DN_ALPHA = 2.0 ** 0.25
LN_EPS = 1e-5

LANES = 128
SUBLANES = 8
BF16_ROWS = 16
MXU_COLS = 256
VMEM_LIMIT = 56 * 1024 * 1024

MIX_TILE = 256
MIX_SEQS = 2
HIST_PAD = 16
TOK_TILE = 512
N_PROMPT = BATCH * SEQ
N_PROMPT_TILES = N_PROMPT // TOK_TILE
N_TILES = N_PROMPT_TILES + 1
SAMPLE_BT = 16

CHUNK_ROWS = BF16_ROWS
LOCAL_ROWS = TOK_TILE * TOP_K + N_EXPERTS * CHUNK_ROWS
GROUPS = LOCAL_ROWS // TOK_TILE
ROW_BLOCK = 512
BLOCK_CHUNKS = ROW_BLOCK // CHUNK_ROWS
MAX_CHUNKS = N_TILES * (TOK_TILE * TOP_K // CHUNK_ROWS + N_EXPERTS)
N_BLOCKS = -(-MAX_CHUNKS // BLOCK_CHUNKS) + N_EXPERTS
ZERO_CHUNK_ROW = LOCAL_ROWS - CHUNK_ROWS
FREE_ROWS = LOCAL_ROWS - TOK_TILE * TOP_K
DUMP_BASE = N_TILES * LOCAL_ROWS
DUMP_ROWS = 2 * ROW_BLOCK
assert DUMP_ROWS % FREE_ROWS == 0
WEIGHT_PIECES = 8
PIECES_PER_BLOCK = 4
GATHER_DEPTH = 3


def _dot(a, b):
    return jnp.dot(a, b, preferred_element_type=F32)


def _dot_nt(a, b):
    return lax.dot_general(a, b, (((1,), (1,)), ((), ())), preferred_element_type=F32)


def _dot_tn(a, b):
    return lax.dot_general(a, b, (((0,), (0,)), ((), ())), preferred_element_type=F32)


def _split3(a):
    a0 = a.astype(BF16)
    r1 = a - a0.astype(F32)
    a1 = r1.astype(BF16)
    r2 = r1 - a1.astype(F32)
    return a0, a1, r2.astype(BF16)


def _log_sigmoid(x):
    return jnp.minimum(x, 0.0) - jnp.log1p(jnp.exp(-jnp.abs(x)))


def _layer_norm(x, g, b):
    mu = jnp.mean(x, axis=-1, keepdims=True)
    xc = x - mu
    var = jnp.mean(xc * xc, axis=-1, keepdims=True)
    return xc * lax.rsqrt(var + LN_EPS) * g + b


def _gate_values(g, gbias):
    lane = lax.broadcasted_iota(I32, g.shape, 1)
    z = g + gbias
    return jnp.where(lane < HEADS, z, _log_sigmoid(z))


def _head_out(hh, o_h, gain):
    mu = jnp.mean(hh, axis=-1, keepdims=True)
    hc = hh - mu
    var = jnp.mean(hc * hc, axis=-1, keepdims=True)
    return jax.nn.sigmoid(o_h) * (hc * lax.rsqrt(var + LN_EPS) * gain)


def _prompt_mixer_kernel(x_ref, xn_ref, win_ref, wg_ref, gb_ref, wpool_ref, pscale_ref, mhg_ref, wout_ref,
                         ln1g_ref, ln1b_ref,
                         x1_ref, pool_ref, c_out_ref, n_out_ref, m_out_ref,
                         ubuf, mixbuf, pbuf, gbuf, c_s, n_s, m_s):
    ti = pl.program_id(1)
    nt = pl.num_programs(1)
    TT = MIX_TILE
    S = MIX_SEQS

    @pl.when(ti == 0)
    def _():
        for s in range(S):
            ubuf[s, 0:HIST_PAD, :] = jnp.zeros((HIST_PAD, POOL_WIDTH), F32)
        c_s[...] = jnp.zeros_like(c_s)
        n_s[...] = jnp.zeros_like(n_s)
        m_s[...] = jnp.zeros_like(m_s)

    x = jnp.concatenate([x_ref[s, 0] for s in range(S)], axis=0)
    step = pl.program_id(0) * nt + ti
    cur = lax.rem(step, 2)
    nxt = 1 - cur
    n_main = POOL_WIDTH + 4 * MLSTM_WIDTH

    @pl.when(step == 0)
    def _():
        xb0 = x.astype(BF16)
        pbuf[0] = _dot(xb0, win_ref[...])
        gbuf[0] = _dot(xb0, wg_ref[...])

    xnb = jnp.concatenate([xn_ref[s, 0] for s in range(S)], axis=0).astype(BF16)

    def slab(j):
        def run():
            pbuf[nxt, :, j * MXU_COLS:(j + 1) * MXU_COLS] = _dot(xnb, win_ref[:, j * MXU_COLS:(j + 1) * MXU_COLS])
        return run

    def gate_slab():
        gbuf[nxt] = _dot(xnb, wg_ref[...])

    pending = [slab(j) for j in range(n_main // MXU_COLS)] + [gate_slab]

    def ahead(n=1):
        for _ in range(n):
            if pending:
                pending.pop(0)()

    proj = pbuf.at[cur]
    g = gbuf[cur]

    L = CHUNK
    row = lax.broadcasted_iota(I32, (L, L), 0)
    col = lax.broadcasted_iota(I32, (L, L), 1)
    causal = row >= col
    tril = jnp.where(causal, 1.0, 0.0).astype(BF16)
    pos = ti * TT + lax.broadcasted_iota(I32, (TT, 1), 0)

    for s in range(S):
        base = s * TT
        u = proj[base:base + TT, 0:POOL_WIDTH]

        ubuf[s, HIST_PAD:HIST_PAD + TT, :] = u
        for gi, w in enumerate(POOL_WINDOWS):
            sl = slice(gi * POOL_GROUP_DIM, (gi + 1) * POOL_GROUP_DIM)
            ug = u[:, sl]
            acc = ug
            for i in range(1, w):
                acc = acc + ubuf[s, HIST_PAD - i:HIST_PAD - i + TT, sl]
            cnt = jnp.minimum(pos + 1, w).astype(F32)
            z = acc / cnt - ug
            mixbuf[base:base + TT, sl] = _dot(z.astype(BF16), wpool_ref[gi]) * pscale_ref[:, sl]

        @pl.when(ti == nt - 1)
        def _():
            pool_ref[s, 0] = ubuf[s, TT + 1:TT + HIST_PAD, :]

        ubuf[s, 0:HIST_PAD, :] = ubuf[s, TT:TT + HIST_PAD, :]

    NC = TT // L
    chains = [(s, h) for s in range(S) for h in range(HEADS)]
    units = [(s, c, h) for c in range(NC) for s in range(S) for h in range(HEADS)]
    U = range(len(units))

    def rows(s, c):
        return slice(s * TT + c * L, s * TT + (c + 1) * L)

    def head_cols(part, h):
        return slice(part * POOL_WIDTH + h * HEAD_DIM, part * POOL_WIDTH + (h + 1) * HEAD_DIM)

    gate, cum, gate_t, cum_t = {}, {}, {}, {}
    for c in range(NC):
        for s in range(S):
            val = _gate_values(g[rows(s, c), :], gb_ref[...])
            v0, v1, v2 = _split3(val)
            gate[s, c] = val
            cum[s, c] = _dot(tril, v0) + _dot(tril, v1) + _dot(tril, v2)
    for key in gate:
        gate_t[key] = gate[key].T
        cum_t[key] = cum[key].T
    ahead()
    qf = [proj[rows(s, c), head_cols(1, h)] for s, c, h in units]
    kf = [proj[rows(s, c), head_cols(2, h)] * (HEAD_DIM ** -0.5) for s, c, h in units]
    vf = [proj[rows(s, c), head_cols(3, h)] for s, c, h in units]
    qb = [a.astype(BF16) for a in qf]
    kb = [a.astype(BF16) for a in kf]
    f_col = [cum[s, c][:, HEADS + h:HEADS + h + 1] for s, c, h in units]
    ahead()
    log_d = [jnp.where(causal, f_col[u] - cum_t[s, c][HEADS + h:HEADS + h + 1, :] + gate_t[s, c][h:h + 1, :],
                       -jnp.inf) for u, (s, c, h) in enumerate(units)]
    ahead()
    row_max = [jnp.max(log_d[u], axis=-1, keepdims=True) for u in U]
    ahead()
    qk_raw = [_dot_nt(qb[u], kb[u]) for u in U]

    m_prev, m_t, inter = [None] * len(units), [None] * len(units), [None] * len(units)
    m_run = {(s, h): m_s[s, h:h + 1, 0:1] for s, h in chains}
    for u, (s, c, h) in enumerate(units):
        m_prev[u] = m_run[s, h]
        inter[u] = m_prev[u] + f_col[u]
        m_t[u] = jnp.maximum(inter[u], row_max[u])
        m_run[s, h] = m_t[u][L - 1:L, :]
    m_new = [m_t[u][L - 1:L, :] for u in U]

    ahead()
    dw = [jnp.exp(log_d[u] - m_t[u]) for u in U]
    sc = [jnp.exp(inter[u] - m_t[u]) for u in U]
    ahead()
    qk = [qk_raw[u] * dw[u] for u in U]
    ahead()
    intra = [_dot(qk[u].astype(BF16), vf[u].astype(BF16)) for u in U]
    ahead()
    row_sum = [jnp.sum(qk[u], axis=-1, keepdims=True) for u in U]
    floor = [jnp.exp(-m_t[u]) for u in U]
    f_last = [f_col[u][L - 1:L, :] for u in U]
    ahead()
    wk = [jnp.exp(gate[s, c][:, h:h + 1] + f_last[u] - f_col[u] - m_new[u]) for u, (s, c, h) in enumerate(units)]
    decay = [jnp.exp(m_prev[u] + f_last[u] - m_new[u]) for u in U]
    ahead()
    upd = [_dot_tn((vf[u] * wk[u]).astype(BF16), kb[u]) for u in U]
    ahead()
    n_upd = [jnp.sum(wk[u] * kf[u], axis=0, keepdims=True) for u in U]

    c_run = {(s, h): c_s[s, h] for s, h in chains}
    n_run = {(s, h): n_s[s, h:h + 1, :] for s, h in chains}
    hh = [None] * len(units)
    for c in range(NC):
        cu = [u for u in U if units[u][1] == c]
        inter_term = {u: _dot_nt(qb[u], c_run[units[u][0], units[u][2]].astype(BF16)) for u in cu}
        n_term = {u: jnp.sum(qf[u] * n_run[units[u][0], units[u][2]], axis=-1, keepdims=True) for u in cu}
        for u in cu:
            s, _, h = units[u]
            num = intra[u] + sc[u] * inter_term[u]
            den = row_sum[u] + sc[u] * n_term[u]
            hh[u] = num / jnp.maximum(jnp.abs(den), floor[u])
            c_run[s, h] = decay[u] * c_run[s, h] + upd[u]
            n_run[s, h] = decay[u] * n_run[s, h] + n_upd[u]
    ahead(len(pending))
    for s, h in chains:
        c_s[s, h] = c_run[s, h]
        n_s[s, h:h + 1, :] = n_run[s, h]
        m_s[s, h:h + 1, :] = jnp.broadcast_to(m_run[s, h], (1, LANES))
    for u, (s, c, h) in enumerate(units):
        mixbuf[rows(s, c), head_cols(1, h)] = _head_out(
            hh[u], proj[rows(s, c), head_cols(4, h)], mhg_ref[:, h * HEAD_DIM:(h + 1) * HEAD_DIM])

    @pl.when(ti == nt - 1)
    def _():
        for s in range(S):
            c_out_ref[s, 0] = c_s[s]
            n_out_ref[s, 0] = n_s[s, 0:HEADS, :]
            m_out_ref[s, 0] = m_s[s]

    mix = _dot(mixbuf[...].astype(BF16), wout_ref[...])
    x1 = _layer_norm(DN_ALPHA * x + mix, ln1g_ref[...], ln1b_ref[...])
    for s in range(S):
        x1_ref[s] = x1[s * TT:(s + 1) * TT, :]


def _prompt_mixer(x, w_in_b, w_g_b, gbias, w_pool_b, pscale, mhg, w_out_b, ln1g, ln1b):
    nt = SEQ // MIX_TILE
    S = MIX_SEQS
    G = BATCH // S
    const2 = lambda b, t: (0, 0)
    const3 = lambda b, t: (0, 0, 0)

    def next_tile(b, t):
        nxt = jnp.minimum(b * nt + t + 1, G * nt - 1)
        return (0, nxt // nt, nxt % nt, 0)

    outs = pl.pallas_call(
        _prompt_mixer_kernel,
        grid=(G, nt),
        in_specs=[
            pl.BlockSpec((S, 1, MIX_TILE, D_MODEL), lambda b, t: (0, b, t, 0)),
            pl.BlockSpec((S, 1, MIX_TILE, D_MODEL), next_tile),
            pl.BlockSpec(w_in_b.shape, const2),
            pl.BlockSpec(w_g_b.shape, const2),
            pl.BlockSpec(gbias.shape, const2),
            pl.BlockSpec(w_pool_b.shape, const3),
            pl.BlockSpec(pscale.shape, const2),
            pl.BlockSpec(mhg.shape, const2),
            pl.BlockSpec(w_out_b.shape, const2),
            pl.BlockSpec(ln1g.shape, const2),
            pl.BlockSpec(ln1b.shape, const2),
        ],
        out_specs=[
            pl.BlockSpec((S, MIX_TILE, D_MODEL), lambda b, t: (0, b * nt + t, 0)),
            pl.BlockSpec((S, 1, POOL_HIST, POOL_WIDTH), lambda b, t: (0, b, 0, 0)),
            pl.BlockSpec((S, 1, HEADS, HEAD_DIM, HEAD_DIM), lambda b, t: (0, b, 0, 0, 0)),
            pl.BlockSpec((S, 1, HEADS, HEAD_DIM), lambda b, t: (0, b, 0, 0)),
            pl.BlockSpec((S, 1, SUBLANES, LANES), lambda b, t: (0, b, 0, 0)),
        ],
        out_shape=[
            jax.ShapeDtypeStruct((S, G * SEQ, D_MODEL), F32),
            jax.ShapeDtypeStruct((S, G, POOL_HIST, POOL_WIDTH), F32),
            jax.ShapeDtypeStruct((S, G, HEADS, HEAD_DIM, HEAD_DIM), F32),
            jax.ShapeDtypeStruct((S, G, HEADS, HEAD_DIM), F32),
            jax.ShapeDtypeStruct((S, G, SUBLANES, LANES), F32),
        ],
        scratch_shapes=[
            pltpu.VMEM((S, HIST_PAD + MIX_TILE, POOL_WIDTH), F32),
            pltpu.VMEM((S * MIX_TILE, D_MODEL), F32),
            pltpu.VMEM((2, S * MIX_TILE, POOL_WIDTH + 4 * MLSTM_WIDTH), F32),
            pltpu.VMEM((2, S * MIX_TILE, LANES), F32),
            pltpu.VMEM((S, HEADS, HEAD_DIM, HEAD_DIM), F32),
            pltpu.VMEM((S, SUBLANES, HEAD_DIM), F32),
            pltpu.VMEM((S, SUBLANES, LANES), F32),
        ],
        compiler_params=pltpu.CompilerParams(
            dimension_semantics=("arbitrary", "arbitrary"), vmem_limit_bytes=VMEM_LIMIT),
        name="prompt_mixer",
    )(x.reshape(S, G, SEQ, D_MODEL), x.reshape(S, G, SEQ, D_MODEL), w_in_b, w_g_b, gbias, w_pool_b, pscale, mhg, w_out_b, ln1g, ln1b)
    x1, pool, c, n, m = outs
    return (x1.reshape(N_PROMPT, D_MODEL), pool.reshape(BATCH, POOL_HIST, POOL_WIDTH),
            c.reshape(BATCH, HEADS, HEAD_DIM, HEAD_DIM), n.reshape(BATCH, HEADS, HEAD_DIM),
            m.reshape(BATCH, SUBLANES, LANES))


def _sample_mixer_kernel(x_ref, hist_ref, c_ref, n_ref, m_ref, win_ref, wg_ref, gb_ref, wpool_ref,
                         pscale_ref, mhg_ref, wout_ref, ln1g_ref, ln1b_ref,
                         x1_ref, pool_out_ref, c_out_ref, n_out_ref, m_out_ref,
                         q_s, k_s, vw_s, v_s, o_s, mixbuf, h_s, coef_s):
    i = pl.program_id(0)
    nsteps = pl.num_programs(0)
    B = DEC_BATCH

    @pl.when(i == 0)
    def _():
        x = x_ref[...]
        xb = x.astype(BF16)
        proj = _dot(xb, win_ref[...])
        g = _dot(xb, wg_ref[...])
        u = proj[:, 0:POOL_WIDTH]
        for gi, w in enumerate(POOL_WINDOWS):
            sl = slice(gi * POOL_GROUP_DIM, (gi + 1) * POOL_GROUP_DIM)
            ug = u[:, sl]
            s = ug
            for j in range(1, w):
                r = POOL_HIST - j
                s = s + hist_ref[:, r * POOL_WIDTH + gi * POOL_GROUP_DIM:r * POOL_WIDTH + (gi + 1) * POOL_GROUP_DIM]
            cnt = float(min(PAST_LEN + 1, w))
            z = s / cnt - ug
            mixbuf[:, sl] = _dot(z.astype(BF16), wpool_ref[gi]) * pscale_ref[:, sl]
        pool_out_ref[:, 0:(POOL_HIST - 1) * POOL_WIDTH] = hist_ref[:, POOL_WIDTH:POOL_HIST * POOL_WIDTH]
        pool_out_ref[:, (POOL_HIST - 1) * POOL_WIDTH:POOL_HIST * POOL_WIDTH] = u

        val = _gate_values(g, gb_ref[...])
        lane = lax.broadcasted_iota(I32, (B, LANES), 1)
        qk_all = jnp.zeros((B, LANES), F32)
        sc_all = jnp.zeros((B, LANES), F32)
        den_all = jnp.zeros((B, LANES), F32)
        floor_all = jnp.zeros((B, LANES), F32)
        m_all = jnp.zeros((B, LANES), F32)
        for h in range(HEADS):
            hs = slice(h * HEAD_DIM, (h + 1) * HEAD_DIM)
            qf = proj[:, POOL_WIDTH + h * HEAD_DIM:POOL_WIDTH + (h + 1) * HEAD_DIM]
            kf = proj[:, 2 * POOL_WIDTH + h * HEAD_DIM:2 * POOL_WIDTH + (h + 1) * HEAD_DIM] * (HEAD_DIM ** -0.5)
            vf = proj[:, 3 * POOL_WIDTH + h * HEAD_DIM:3 * POOL_WIDTH + (h + 1) * HEAD_DIM]
            ig = val[:, h:h + 1]
            lf = val[:, HEADS + h:HEADS + h + 1]
            m0 = m_ref[:, h:h + 1]
            n0 = n_ref[:, hs]
            inter = m0 + lf
            m_t = jnp.maximum(inter, ig)
            dw = jnp.exp(ig - m_t)
            sc = jnp.exp(inter - m_t)
            qk = jnp.sum(qf * kf, axis=-1, keepdims=True) * dw
            den = qk + sc * jnp.sum(qf * n0, axis=-1, keepdims=True)
            n_out_ref[:, hs] = sc * n0 + dw * kf
            q_s[0:B, hs] = qf
            k_s[0:B, hs] = kf
            v_s[0:B, hs] = vf
            vw_s[0:B, hs] = vf * dw
            sel = lane == h
            qk_all = jnp.where(sel, qk, qk_all)
            sc_all = jnp.where(sel, sc, sc_all)
            den_all = jnp.where(sel, den, den_all)
            floor_all = jnp.where(sel, jnp.exp(-m_t), floor_all)
            m_all = jnp.where(lane == HEADS + h, m_t, m_all)
        o_s[...] = proj[:, 4 * POOL_WIDTH:5 * POOL_WIDTH]
        coef_s[0] = qk_all
        coef_s[1] = sc_all
        coef_s[2] = den_all
        coef_s[3] = floor_all
        m_out_ref[...] = m_all

    rows = pl.ds(pl.multiple_of(i * SAMPLE_BT, SAMPLE_BT), SAMPLE_BT)
    q_t, k_t, v_t, vw_t = q_s[rows, :], k_s[rows, :], v_s[rows, :], vw_s[rows, :]
    qk_t, sc_t, den_t, floor_t = coef_s[0, rows, :], coef_s[1, rows, :], coef_s[2, rows, :], coef_s[3, rows, :]
    h_rows = []
    for bl in range(SAMPLE_BT):
        heads = []
        for h in range(HEADS):
            hs = slice(h * HEAD_DIM, (h + 1) * HEAD_DIM)
            c_prev = c_ref[bl, h]
            q8 = jnp.broadcast_to(q_t[bl:bl + 1, hs], (SUBLANES, HEAD_DIM))
            cq = _dot_nt(q8.astype(BF16), c_prev.astype(BF16))[0:1, :]
            qk = qk_t[bl:bl + 1, h:h + 1]
            sc = sc_t[bl:bl + 1, h:h + 1]
            num = qk * v_t[bl:bl + 1, hs] + sc * cq
            heads.append(num / jnp.maximum(jnp.abs(den_t[bl:bl + 1, h:h + 1]), floor_t[bl:bl + 1, h:h + 1]))
            v_col = jnp.broadcast_to(vw_t[bl:bl + 1, hs], (HEAD_DIM, HEAD_DIM)).T
            c_out_ref[bl, h] = sc * c_prev + v_col * k_t[bl:bl + 1, hs]
        h_rows.append(jnp.concatenate(heads, axis=1))
    h_s[rows, :] = jnp.concatenate(h_rows, axis=0)

    @pl.when(i == nsteps - 1)
    def _():
        for h in range(HEADS):
            hs = slice(h * HEAD_DIM, (h + 1) * HEAD_DIM)
            mixbuf[:, POOL_WIDTH + h * HEAD_DIM:POOL_WIDTH + (h + 1) * HEAD_DIM] = _head_out(
                h_s[:, hs], o_s[:, hs], mhg_ref[:, hs])
        mix = _dot(mixbuf[...].astype(BF16), wout_ref[...])
        x1 = _layer_norm(DN_ALPHA * x_ref[...] + mix, ln1g_ref[...], ln1b_ref[...])
        x1_ref[0:B, :] = x1
        x1_ref[B:TOK_TILE, :] = jnp.zeros((TOK_TILE - B, D_MODEL), F32)


def _sample_mixer(x, hist2, c0, n0, m0, w_in_b, w_g_b, gbias, w_pool_b, pscale, mhg, w_out_b, ln1g, ln1b):
    B = DEC_BATCH
    steps = B // SAMPLE_BT
    full = lambda a: pl.BlockSpec(a.shape, lambda i: (0,) * a.ndim)
    c_spec = pl.BlockSpec((SAMPLE_BT, HEADS, HEAD_DIM, HEAD_DIM), lambda i: (i, 0, 0, 0))
    return pl.pallas_call(
        _sample_mixer_kernel,
        grid=(steps,),
        in_specs=[full(x), full(hist2), c_spec, full(n0), full(m0), full(w_in_b), full(w_g_b), full(gbias),
                  full(w_pool_b), full(pscale), full(mhg), full(w_out_b), full(ln1g), full(ln1b)],
        out_specs=[
            pl.BlockSpec((TOK_TILE, D_MODEL), lambda i: (0, 0)),
            pl.BlockSpec((B, POOL_HIST * POOL_WIDTH), lambda i: (0, 0)),
            c_spec,
            pl.BlockSpec((B, MLSTM_WIDTH), lambda i: (0, 0)),
            pl.BlockSpec((B, LANES), lambda i: (0, 0)),
        ],
        out_shape=[
            jax.ShapeDtypeStruct((TOK_TILE, D_MODEL), F32),
            jax.ShapeDtypeStruct((B, POOL_HIST * POOL_WIDTH), F32),
            jax.ShapeDtypeStruct((B, HEADS, HEAD_DIM, HEAD_DIM), F32),
            jax.ShapeDtypeStruct((B, MLSTM_WIDTH), F32),
            jax.ShapeDtypeStruct((B, LANES), F32),
        ],
        scratch_shapes=[
            pltpu.VMEM((B, MLSTM_WIDTH), F32),
            pltpu.VMEM((B, MLSTM_WIDTH), F32),
            pltpu.VMEM((B, MLSTM_WIDTH), F32),
            pltpu.VMEM((B, MLSTM_WIDTH), F32),
            pltpu.VMEM((B, MLSTM_WIDTH), F32),
            pltpu.VMEM((B, D_MODEL), F32),
            pltpu.VMEM((B, MLSTM_WIDTH), F32),
            pltpu.VMEM((4, B, LANES), F32),
        ],
        compiler_params=pltpu.CompilerParams(
            dimension_semantics=("arbitrary",), vmem_limit_bytes=VMEM_LIMIT),
        name="sample_mixer",
    )(x, hist2, c0, n0, m0, w_in_b, w_g_b, gbias, w_pool_b, pscale, mhg, w_out_b, ln1g, ln1b)


def _pick_tile(i, prompt_ref, sample_ref):
    return jnp.where(i < N_PROMPT_TILES, prompt_ref[...], sample_ref[...])


def _placement(slot_rows, group):
    r = group * TOK_TILE + lax.broadcasted_iota(I32, (TOK_TILE, TOK_TILE), 0)
    return [r == s for s in slot_rows]


def _route_kernel(xp_ref, xs_ref, wrt_ref, br_ref, slot_ref, gate_ref, nch_ref, sorted_ref):
    i = pl.program_id(0)
    T = TOK_TILE
    E = N_EXPERTS

    x = _pick_tile(i, xp_ref, xs_ref)
    xh = x.astype(BF16)
    xl = (x - xh.astype(F32)).astype(BF16)
    w = wrt_ref[...]
    wh = w.astype(BF16)
    wl = (w - wh.astype(F32)).astype(BF16)
    logits = _dot_nt(wh, xh) + (_dot_nt(wh, xl) + _dot_nt(wl, xh)) + br_ref[:, 0:1]

    erow = lax.broadcasted_iota(I32, (E, T), 0).astype(F32)
    work = logits
    vals, sels = [], []
    for _ in range(TOP_K):
        mx = jnp.max(work, axis=0, keepdims=True)
        idx = jnp.min(jnp.where(work == mx, erow, float(E)), axis=0, keepdims=True)
        sel = erow == idx
        work = jnp.where(sel, -jnp.inf, work)
        vals.append(mx)
        sels.append(sel)
    chosen = jnp.logical_or(jnp.logical_or(sels[0], sels[1]), jnp.logical_or(sels[2], sels[3]))
    es = [jnp.exp(v - vals[0]) for v in vals]
    tot = es[0] + es[1] + es[2] + es[3]

    n_valid = jnp.where(i < N_PROMPT_TILES, T, DEC_BATCH)
    valid = lax.broadcasted_iota(I32, (1, T), 1) < n_valid
    onehot = jnp.where(jnp.logical_and(chosen, valid), 1.0, 0.0)
    trow = lax.broadcasted_iota(I32, (T, T), 0)
    tcol = lax.broadcasted_iota(I32, (T, T), 1)
    before = jnp.where(trow < tcol, 1.0, 0.0).astype(BF16)
    rank = _dot(onehot.astype(BF16), before)
    cnt = jnp.sum(onehot, axis=1, keepdims=True)
    nch = jnp.floor((cnt + (CHUNK_ROWS - 1)) * (1.0 / CHUNK_ROWS))
    lower = jnp.where(lax.broadcasted_iota(I32, (E, E), 0) > lax.broadcasted_iota(I32, (E, E), 1), 1.0, 0.0)
    nch_b = jnp.broadcast_to(nch, (E, LANES))
    seg_start = _dot(lower.astype(BF16), nch_b.astype(BF16))[:, 0:1] * CHUNK_ROWS
    base = seg_start + rank

    r8 = lax.broadcasted_iota(I32, (SUBLANES, T), 0)
    s_out = jnp.zeros((SUBLANES, T), I32)
    g_out = jnp.zeros((SUBLANES, T), F32)
    slot_rows = []
    for j in range(TOP_K):
        slot_j = jnp.sum(jnp.where(sels[j], base, 0.0), axis=0, keepdims=True).astype(I32)
        slot_j = jnp.where(valid, slot_j, -1)
        slot_rows.append(slot_j)
        s_out = jnp.where(r8 == j, slot_j, s_out)
        g_out = jnp.where(r8 == j, es[j] / tot, g_out)
    slot_ref[0] = s_out
    gate_ref[0] = g_out
    nch_ref[0] = nch_b

    for grp in range(GROUPS):
        m = _placement(slot_rows, grp)
        hit = jnp.logical_or(jnp.logical_or(m[0], m[1]), jnp.logical_or(m[2], m[3]))
        place = jnp.where(hit, 1.0, 0.0).astype(BF16)
        sorted_ref[grp * T:(grp + 1) * T, :] = _dot(place, xh).astype(BF16)


def _route(x1p, x1s, w_router_t, b_router_col):
    tile_spec = pl.BlockSpec((1, SUBLANES, TOK_TILE), lambda i: (i, 0, 0))
    return pl.pallas_call(
        _route_kernel,
        grid=(N_TILES,),
        in_specs=[
            pl.BlockSpec((TOK_TILE, D_MODEL), lambda i: (jnp.minimum(i, N_PROMPT_TILES - 1), 0)),
            pl.BlockSpec((TOK_TILE, D_MODEL), lambda i: (0, 0)),
            pl.BlockSpec(w_router_t.shape, lambda i: (0, 0)),
            pl.BlockSpec(b_router_col.shape, lambda i: (0, 0)),
        ],
        out_specs=[tile_spec, tile_spec,
                   pl.BlockSpec((1, N_EXPERTS, LANES), lambda i: (i, 0, 0)),
                   pl.BlockSpec((LOCAL_ROWS, D_MODEL), lambda i: (i, 0))],
        out_shape=[
            jax.ShapeDtypeStruct((N_TILES, SUBLANES, TOK_TILE), I32),
            jax.ShapeDtypeStruct((N_TILES, SUBLANES, TOK_TILE), F32),
            jax.ShapeDtypeStruct((N_TILES, N_EXPERTS, LANES), F32),
            jax.ShapeDtypeStruct((N_TILES * LOCAL_ROWS, D_MODEL), BF16),
        ],
        compiler_params=pltpu.CompilerParams(
            dimension_semantics=("arbitrary",), vmem_limit_bytes=VMEM_LIMIT),
        name="route",
    )(x1p, x1s, w_router_t, b_router_col)


def _expert_kernel(src_ref, dst_ref, bexp_ref, first_ref, next_ref, half_ref, nused_ref,
                   sorted_hbm, w1_hbm, b1_ref, w2_hbm, b2_ref, out_hbm,
                   w1_stage, w2_stage, w1_b, w2_b, xbuf, obuf, zbuf, wsem, gsem, ssem, zsem):
    nused = nused_ref[0]
    obuf[...] = jnp.zeros_like(obuf)

    def fetch_piece(e, p):
        r1 = pl.ds(pl.multiple_of(p * (D_MODEL // WEIGHT_PIECES), SUBLANES), D_MODEL // WEIGHT_PIECES)
        r2 = pl.ds(pl.multiple_of(p * (D_FF // WEIGHT_PIECES), SUBLANES), D_FF // WEIGHT_PIECES)
        return (pltpu.make_async_copy(w1_hbm.at[e, r1, :], w1_stage.at[r1, :], wsem.at[0]),
                pltpu.make_async_copy(w2_hbm.at[e, r2, :], w2_stage.at[r2, :], wsem.at[1]))

    def start_pieces(e, lo, hi):
        def body(p, c):
            for cp in fetch_piece(e, p):
                cp.start()
            return c
        lax.fori_loop(lo, hi, body, 0)

    def gather(b, q):
        slot = lax.rem(b, GATHER_DEPTH)
        row = pl.multiple_of(src_ref[b * BLOCK_CHUNKS + q], CHUNK_ROWS)
        return pltpu.make_async_copy(sorted_hbm.at[pl.ds(row, CHUNK_ROWS), :],
                                     xbuf.at[slot, pl.ds(q * CHUNK_ROWS, CHUNK_ROWS), :], gsem.at[slot])

    def scatter(b, q):
        slot = lax.rem(b, 2)
        row = pl.multiple_of(dst_ref[b * BLOCK_CHUNKS + q], CHUNK_ROWS)
        return pltpu.make_async_copy(obuf.at[slot, pl.ds(q * CHUNK_ROWS, CHUNK_ROWS), :],
                                     out_hbm.at[pl.ds(row, CHUNK_ROWS), :], ssem.at[slot])

    def zero_rows(start, n_rows):
        start = pl.multiple_of(start, CHUNK_ROWS)
        return pltpu.make_async_copy(zbuf.at[pl.ds(0, n_rows), :], out_hbm.at[pl.ds(start, n_rows), :], zsem)

    def zero_tail(k):
        return zero_rows(k * LOCAL_ROWS + TOK_TILE * TOP_K, FREE_ROWS)

    def zero_last_tile(part):
        return zero_rows((N_TILES - 1) * LOCAL_ROWS + part * FREE_ROWS, FREE_ROWS)

    zbuf[...] = jnp.zeros_like(zbuf)
    lax.fori_loop(0, N_TILES, lambda k, c: (zero_tail(k).start(), c)[1], 0)
    for part in range(TOK_TILE * TOP_K // FREE_ROWS):
        zero_last_tile(part).start()
    for part in range(DUMP_ROWS // FREE_ROWS):
        zero_rows(DUMP_BASE + part * FREE_ROWS, FREE_ROWS).start()
    for ahead in range(GATHER_DEPTH - 1):
        @pl.when(ahead < nused)
        def _():
            for q in range(BLOCK_CHUNKS):
                gather(ahead, q).start()
    lax.fori_loop(0, N_TILES, lambda k, c: (zero_tail(k).wait(), c)[1], 0)
    for part in range(TOK_TILE * TOP_K // FREE_ROWS):
        zero_last_tile(part).wait()
    for part in range(DUMP_ROWS // FREE_ROWS):
        zero_rows(DUMP_BASE + part * FREE_ROWS, FREE_ROWS).wait()

    half = MXU_COLS // 2
    k_io = lax.broadcasted_iota(I32, (MXU_COLS, MXU_COLS), 0)
    j_io = lax.broadcasted_iota(I32, (MXU_COLS, MXU_COLS), 1)
    src_col = jnp.where(j_io < half, 2 * j_io, 2 * (j_io - half) + 1)
    perm = jnp.where(k_io == src_col, 1.0, 0.0).astype(BF16)

    def block(i, fetched):
        e = bexp_ref[i]
        slot = lax.rem(i, 2)
        is_first = first_ref[i] == 1

        @pl.when(is_first)
        def _():
            start_pieces(e, fetched, WEIGHT_PIECES)

            def wait_piece(p, c):
                for cp in fetch_piece(e, p):
                    cp.wait()
                return c
            lax.fori_loop(0, WEIGHT_PIECES, wait_piece, 0)
            for c in range(2 * D_FF // MXU_COLS):
                blk = w1_stage[:, c * MXU_COLS:(c + 1) * MXU_COLS].astype(BF16)
                sep = _dot(blk, perm).astype(BF16)
                w1_b[:, c * half:(c + 1) * half] = sep[:, 0:half]
                w1_b[:, D_FF + c * half:D_FF + (c + 1) * half] = sep[:, half:MXU_COLS]
            w2_b[...] = w2_stage[...].astype(BF16)

        fetched = jnp.where(is_first, 0, fetched)

        for q in range(BLOCK_CHUNKS):
            gather(i, q).wait()

        @pl.when(i + GATHER_DEPTH - 1 < nused)
        def _():
            for q in range(BLOCK_CHUNKS):
                gather(i + GATHER_DEPTH - 1, q).start()

        @pl.when(i >= 2)
        def _():
            for q in range(BLOCK_CHUNKS):
                scatter(i - 2, q).wait()

        def ffn(n_rows):
            h = _dot(xbuf[lax.rem(i, GATHER_DEPTH), 0:n_rows, :], w1_b[...]) + b1_ref[e]
            glu = jnp.minimum(h[:, 0:D_FF], SWIGLU_LIMIT)
            lin = jnp.clip(h[:, D_FF:2 * D_FF], -SWIGLU_LIMIT, SWIGLU_LIMIT)
            a = glu * jax.nn.sigmoid(SWIGLU_ALPHA * glu) * (lin + 1.0)
            obuf[slot, 0:n_rows, :] = (_dot(a.astype(BF16), w2_b[...]) + b2_ref[e]).astype(BF16)

        @pl.when(half_ref[i] == 0)
        def _():
            ffn(ROW_BLOCK)

        @pl.when(half_ref[i] == 1)
        def _():
            ffn(ROW_BLOCK // 2)

        for q in range(BLOCK_CHUNKS):
            scatter(i, q).start()

        more = jnp.where(next_ref[i] >= 0, jnp.minimum(fetched + PIECES_PER_BLOCK, WEIGHT_PIECES), fetched)
        start_pieces(next_ref[i], fetched, more)
        return more

    lax.fori_loop(0, nused, block, jnp.int32(0))

    @pl.when(nused >= 2)
    def _():
        for q in range(BLOCK_CHUNKS):
            scatter(nused - 2, q).wait()
    for q in range(BLOCK_CHUNKS):
        scatter(nused - 1, q).wait()


def _experts(chunk_src, chunk_dst, block_expert, block_first, block_next, block_half, n_used,
             sorted_rows, w1, b1p, w2, b2):
    whole3 = lambda i, *_: (0, 0, 0)
    grid_spec = pltpu.PrefetchScalarGridSpec(
        num_scalar_prefetch=7,
        grid=(1,),
        in_specs=[
            pl.BlockSpec(memory_space=pl.ANY),
            pl.BlockSpec(memory_space=pl.ANY),
            pl.BlockSpec(b1p.shape, whole3),
            pl.BlockSpec(memory_space=pl.ANY),
            pl.BlockSpec(b2.shape, whole3),
        ],
        out_specs=pl.BlockSpec(memory_space=pl.ANY),
        scratch_shapes=[
            pltpu.VMEM((D_MODEL, 2 * D_FF), F32),
            pltpu.VMEM((D_FF, D_MODEL), F32),
            pltpu.VMEM((D_MODEL, 2 * D_FF), BF16),
            pltpu.VMEM((D_FF, D_MODEL), BF16),
            pltpu.VMEM((GATHER_DEPTH, ROW_BLOCK, D_MODEL), BF16),
            pltpu.VMEM((2, ROW_BLOCK, D_MODEL), BF16),
            pltpu.VMEM((FREE_ROWS, D_MODEL), BF16),
            pltpu.SemaphoreType.DMA((2,)),
            pltpu.SemaphoreType.DMA((GATHER_DEPTH,)),
            pltpu.SemaphoreType.DMA((2,)),
            pltpu.SemaphoreType.DMA(()),
        ],
    )
    return pl.pallas_call(
        _expert_kernel,
        grid_spec=grid_spec,
        out_shape=jax.ShapeDtypeStruct((DUMP_BASE + DUMP_ROWS, D_MODEL), BF16),
        compiler_params=pltpu.CompilerParams(
            dimension_semantics=("arbitrary",), vmem_limit_bytes=VMEM_LIMIT),
        name="experts",
    )(chunk_src, chunk_dst, block_expert, block_first, block_next, block_half, n_used,
      sorted_rows, w1, b1p, w2, b2)


def _combine_kernel(slot_ref, gate_ref, xp_ref, xs_ref, pp_ref, ps_ref, eo_ref, ln2g_ref, ln2b_ref,
                    wpg_ref, wple_ref, yp_ref, ys_ref):
    i = pl.program_id(0)
    T = TOK_TILE

    slot_rows = [slot_ref[0, j:j + 1, :] for j in range(TOP_K)]
    gate_rows = [gate_ref[0, j:j + 1, :] for j in range(TOP_K)]
    pad = jnp.zeros((LANES - SUBLANES, T), F32)
    slots_t = jnp.concatenate([slot_ref[0].astype(F32), pad], axis=0).T
    slot_cols = [slots_t[:, j:j + 1].astype(I32) for j in range(TOP_K)]

    ff = jnp.zeros((T, D_MODEL), F32)
    for grp in range(GROUPS):
        m = _placement(slot_rows, grp)
        weighted = jnp.where(m[0], gate_rows[0], jnp.where(m[1], gate_rows[1], jnp.where(
            m[2], gate_rows[2], jnp.where(m[3], gate_rows[3], 0.0))))
        g_col = jnp.sum(weighted, axis=1, keepdims=True)
        z = (eo_ref[grp * T:(grp + 1) * T, :].astype(F32) * g_col).astype(BF16)
        r = grp * T + lax.broadcasted_iota(I32, (T, T), 1)
        hit = jnp.logical_or(jnp.logical_or(r == slot_cols[0], r == slot_cols[1]),
                             jnp.logical_or(r == slot_cols[2], r == slot_cols[3]))
        ff = ff + _dot(jnp.where(hit, 1.0, 0.0).astype(BF16), z)

    x1 = _pick_tile(i, xp_ref, xs_ref)
    x2 = _layer_norm(DN_ALPHA * x1 + ff, ln2g_ref[...], ln2b_ref[...])
    p = _pick_tile(i, pp_ref, ps_ref)
    y = x2 + jax.nn.sigmoid(_dot(x2.astype(BF16), wpg_ref[...])) * _dot(p.astype(BF16), wple_ref[...])

    @pl.when(i < N_PROMPT_TILES)
    def _():
        yp_ref[...] = y

    @pl.when(i == N_PROMPT_TILES)
    def _():
        ys_ref[...] = y[0:DEC_BATCH, :]


def _combine(slots, gates, x1p, x1s, pp, ps, expert_out, ln2g, ln2b, w_pg_b, w_ple_b):
    tile_idx = lambda i: (jnp.minimum(i, N_PROMPT_TILES - 1), 0)
    const2 = lambda i: (0, 0)
    return pl.pallas_call(
        _combine_kernel,
        grid=(N_TILES,),
        in_specs=[
            pl.BlockSpec((1, SUBLANES, TOK_TILE), lambda i: (i, 0, 0)),
            pl.BlockSpec((1, SUBLANES, TOK_TILE), lambda i: (i, 0, 0)),
            pl.BlockSpec((TOK_TILE, D_MODEL), tile_idx),
            pl.BlockSpec((TOK_TILE, D_MODEL), const2),
            pl.BlockSpec((TOK_TILE, PLE_DIM), tile_idx),
            pl.BlockSpec((TOK_TILE, PLE_DIM), const2),
            pl.BlockSpec((LOCAL_ROWS, D_MODEL), lambda i: (i, 0)),
            pl.BlockSpec(ln2g.shape, const2),
            pl.BlockSpec(ln2b.shape, const2),
            pl.BlockSpec(w_pg_b.shape, const2),
            pl.BlockSpec(w_ple_b.shape, const2),
        ],
        out_specs=[
            pl.BlockSpec((TOK_TILE, D_MODEL), tile_idx),
            pl.BlockSpec((DEC_BATCH, D_MODEL), const2),
        ],
        out_shape=[
            jax.ShapeDtypeStruct((N_PROMPT, D_MODEL), F32),
            jax.ShapeDtypeStruct((DEC_BATCH, D_MODEL), F32),
        ],
        compiler_params=pltpu.CompilerParams(
            dimension_semantics=("arbitrary",), vmem_limit_bytes=VMEM_LIMIT),
        name="combine",
    )(slots, gates, x1p, x1s, pp, ps, expert_out, ln2g, ln2b, w_pg_b, w_ple_b)


def _block_tables(nch):
    seg_start = (jnp.cumsum(nch, axis=1) - nch) * CHUNK_ROWS
    tot = jnp.sum(nch, axis=0)
    nblk = (tot + BLOCK_CHUNKS - 1) // BLOCK_CHUNKS
    blk_end = jnp.cumsum(nblk)
    blk_start = blk_end - nblk
    n_used = blk_end[-1:].astype(I32)
    blk_ids = jnp.arange(N_BLOCKS, dtype=I32)

    e_ids = jnp.arange(N_EXPERTS, dtype=I32)
    used = nblk > 0
    blk_clamped = jnp.minimum(blk_ids, n_used[0] - 1)
    block_expert = jnp.minimum(jnp.sum(blk_end[:, None] <= blk_clamped[None, :], axis=0), N_EXPERTS - 1).astype(I32)
    starts_here = jnp.logical_and(blk_start[:, None] == blk_ids[None, :], used[:, None])
    block_first = jnp.any(starts_here, axis=0).astype(I32)
    later_used = jnp.logical_and(e_ids[None, :] > e_ids[:, None], used[None, :])
    next_used = jnp.min(jnp.where(later_used, e_ids[None, :], N_EXPERTS), axis=1)
    next_used = jnp.where(next_used < N_EXPERTS, next_used, -1)
    owner = block_expert[None, :] == e_ids[:, None]
    block_next = jnp.sum(jnp.where(owner, next_used[:, None], 0), axis=0).astype(I32)
    real_chunks = jnp.sum(jnp.where(owner, tot[:, None] - (blk_ids[None, :] - blk_start[:, None]) * BLOCK_CHUNKS, 0),
                          axis=0)
    block_half = (real_chunks <= BLOCK_CHUNKS // 2).astype(I32)

    nch_t = nch.T
    seg_first = (blk_start[:, None] * BLOCK_CHUNKS + jnp.cumsum(nch_t, axis=1) - nch_t).reshape(-1)
    seg_count = nch_t.reshape(-1)
    seg_row = (jnp.arange(N_TILES, dtype=I32)[None, :] * LOCAL_ROWS + seg_start.T).reshape(-1)
    ent = jnp.arange(N_BLOCKS * BLOCK_CHUNKS, dtype=I32)
    d = ent[None, :] - seg_first[:, None]
    inside = jnp.logical_and(d >= 0, d < seg_count[:, None])
    row_plus_1 = jnp.sum(jnp.where(inside, seg_row[:, None] + d * CHUNK_ROWS + 1, 0), axis=0)
    real = row_plus_1 > 0
    row = row_plus_1 - 1
    dump = DUMP_BASE + (((ent // BLOCK_CHUNKS) % 2) * BLOCK_CHUNKS + ent % BLOCK_CHUNKS) * CHUNK_ROWS
    chunk_src = jnp.where(real, row, ZERO_CHUNK_ROW).astype(I32)
    chunk_dst = jnp.where(real, row, dump).astype(I32)
    return chunk_src, chunk_dst, block_expert, block_first, block_next, block_half, n_used


def kernel(x_prompt, x_sample, state_pool, state_mlstm_C, state_mlstm_n, state_mlstm_m, p_prompt, p_sample, w_in, b_i, b_f, w_pool, pool_scale, mh_g, w_out, ln1_g, ln1_b, w_router, b_router, w_mlp1, b_mlp1, w_mlp2, b_mlp2, ln2_g, ln2_b, w_ple, w_ple_gate):
    n_main = POOL_WIDTH + 4 * MLSTM_WIDTH
    w_in_b = w_in[0, :, 0:n_main].astype(BF16)
    w_g_b = jnp.pad(w_in[0, :, n_main:], ((0, 0), (0, LANES - 2 * HEADS))).astype(BF16)
    gbias = jnp.pad(jnp.concatenate([b_i[0], b_f[0]]), (0, LANES - 2 * HEADS)).reshape(1, LANES)
    w_pool_b = w_pool[0].astype(BF16)
    pscale = pool_scale[0].reshape(1, POOL_WIDTH)
    mhg = mh_g[0].reshape(1, MLSTM_WIDTH)
    w_out_b = w_out[0].astype(BF16)
    ln1g = ln1_g[0].reshape(1, D_MODEL)
    ln1b = ln1_b[0].reshape(1, D_MODEL)
    ln2g = ln2_g[0].reshape(1, D_MODEL)
    ln2b = ln2_b[0].reshape(1, D_MODEL)
    w_router_t = w_router[0].T
    b_router_col = jnp.broadcast_to(b_router[0].reshape(N_EXPERTS, 1), (N_EXPERTS, LANES))
    b1 = b_mlp1[0]
    b1p = jnp.concatenate([b1[:, 0::2], b1[:, 1::2]], axis=-1).reshape(N_EXPERTS, 1, 2 * D_FF)
    b2 = b_mlp2[0].reshape(N_EXPERTS, 1, D_MODEL)
    w_pg_b = w_ple_gate[0].astype(BF16)
    w_ple_b = w_ple[0].astype(BF16)

    x1p, pool_p, c_p, n_p, m_p = _prompt_mixer(
        x_prompt, w_in_b, w_g_b, gbias, w_pool_b, pscale, mhg, w_out_b, ln1g, ln1b)
    x1s, pool_s, c_s, n_s, m_s = _sample_mixer(
        x_sample.reshape(DEC_BATCH, D_MODEL),
        state_pool[0].reshape(DEC_BATCH, POOL_HIST * POOL_WIDTH),
        state_mlstm_C[0], state_mlstm_n[0].reshape(DEC_BATCH, MLSTM_WIDTH), state_mlstm_m[0],
        w_in_b, w_g_b, gbias, w_pool_b, pscale, mhg, w_out_b, ln1g, ln1b)

    slots, gates, nch, sorted_rows = _route(x1p, x1s, w_router_t, b_router_col)
    tables = _block_tables(nch[:, :, 0].astype(I32))
    expert_out = _experts(*tables, sorted_rows, w_mlp1[0], b1p, w_mlp2[0], b2)

    pp = p_prompt[0].reshape(N_PROMPT, PLE_DIM)
    ps = jnp.pad(p_sample[0].reshape(DEC_BATCH, PLE_DIM), ((0, TOK_TILE - DEC_BATCH), (0, 0)))
    yp, ys = _combine(slots, gates, x1p, x1s, pp, ps, expert_out, ln2g, ln2b, w_pg_b, w_ple_b)

    return (
        yp.reshape(BATCH, SEQ, D_MODEL),
        ys.reshape(DEC_BATCH, 1, D_MODEL),
        pool_p.reshape(1, BATCH, POOL_HIST, POOL_WIDTH),
        c_p.reshape(1, BATCH, HEADS, HEAD_DIM, HEAD_DIM),
        n_p.reshape(1, BATCH, HEADS, HEAD_DIM),
        m_p[:, 0:HEADS, 0].reshape(1, BATCH, HEADS),
        pool_s.reshape(1, DEC_BATCH, POOL_HIST, POOL_WIDTH),
        c_s.reshape(1, DEC_BATCH, HEADS, HEAD_DIM, HEAD_DIM),
        n_s.reshape(1, DEC_BATCH, HEADS, HEAD_DIM),
        m_s[:, HEADS:2 * HEADS].reshape(1, DEC_BATCH, HEADS),
    )
```

```python
import jax
import jax.numpy as jnp
from jax import lax
from jax.experimental import pallas as pl
from jax.experimental.pallas import tpu as pltpu

F32 = jnp.float32
BF16 = jnp.bfloat16
I32 = jnp.int32

D_MODEL = 1024
BATCH = 8
SEQ = 2048
DEC_BATCH = 128
PAST_LEN = 16384
POOL_WIDTH = 512
POOL_GROUP_DIM = 128
POOL_WINDOWS = (2, 4, 8, 16)
POOL_HIST = 15
MLSTM_WIDTH = 512
HEADS = 4
HEAD_DIM = 128
CHUNK = 128
N_EXPERTS = 32
TOP_K = 4
D_FF = 1024
SWIGLU_ALPHA = 1.702
SWIGLU_LIMIT = 7.0
PLE_DIM = 256
DN_ALPHA = 2.0 ** 0.25
LN_EPS = 1e-5

LANES = 128
SUBLANES = 8
BF16_ROWS = 16
MXU_COLS = 256
VMEM_LIMIT = 56 * 1024 * 1024

MIX_TILE = 256
MIX_SEQS = 2
HIST_PAD = 16
TOK_TILE = 512
N_PROMPT = BATCH * SEQ
N_PROMPT_TILES = N_PROMPT // TOK_TILE
N_TILES = N_PROMPT_TILES + 1
SAMPLE_BT = 16

CHUNK_ROWS = BF16_ROWS
LOCAL_ROWS = TOK_TILE * TOP_K + N_EXPERTS * CHUNK_ROWS
GROUPS = LOCAL_ROWS // TOK_TILE
ROW_BLOCK = 512
BLOCK_QUARTERS = 8
BLOCK_CHUNKS = ROW_BLOCK // CHUNK_ROWS
MAX_CHUNKS = N_TILES * (TOK_TILE * TOP_K // CHUNK_ROWS + N_EXPERTS)
N_BLOCKS = -(-MAX_CHUNKS // BLOCK_CHUNKS) + N_EXPERTS
ZERO_CHUNK_ROW = LOCAL_ROWS - CHUNK_ROWS
FREE_ROWS = LOCAL_ROWS - TOK_TILE * TOP_K
DUMP_BASE = N_TILES * LOCAL_ROWS
DUMP_ROWS = 2 * ROW_BLOCK
assert DUMP_ROWS % FREE_ROWS == 0
WEIGHT_PIECES = 8
PIECES_PER_BLOCK = 4
GATHER_DEPTH = 3


def _dot(a, b):
    return jnp.dot(a, b, preferred_element_type=F32)


def _dot_nt(a, b):
    return lax.dot_general(a, b, (((1,), (1,)), ((), ())), preferred_element_type=F32)


def _dot_tn(a, b):
    return lax.dot_general(a, b, (((0,), (0,)), ((), ())), preferred_element_type=F32)


def _split3(a):
    a0 = a.astype(BF16)
    r1 = a - a0.astype(F32)
    a1 = r1.astype(BF16)
    r2 = r1 - a1.astype(F32)
    return a0, a1, r2.astype(BF16)


def _log_sigmoid(x):
    return jnp.minimum(x, 0.0) - jnp.log1p(jnp.exp(-jnp.abs(x)))


def _layer_norm(x, g, b):
    mu = jnp.mean(x, axis=-1, keepdims=True)
    xc = x - mu
    var = jnp.mean(xc * xc, axis=-1, keepdims=True)
    return xc * lax.rsqrt(var + LN_EPS) * g + b


def _gate_values(g, gbias):
    lane = lax.broadcasted_iota(I32, g.shape, 1)
    z = g + gbias
    return jnp.where(lane < HEADS, z, _log_sigmoid(z))


def _head_out(hh, o_h, gain):
    mu = jnp.mean(hh, axis=-1, keepdims=True)
    hc = hh - mu
    var = jnp.mean(hc * hc, axis=-1, keepdims=True)
    return jax.nn.sigmoid(o_h) * (hc * lax.rsqrt(var + LN_EPS) * gain)


def _prompt_mixer_kernel(x_ref, xn_ref, win_ref, wg_ref, gb_ref, wpool_ref, pscale_ref, mhg_ref, wout_ref,
                         ln1g_ref, ln1b_ref,
                         x1_ref, pool_ref, c_out_ref, n_out_ref, m_out_ref,
                         ubuf, mixbuf, pbuf, gbuf, c_s, n_s, m_s):
    ti = pl.program_id(1)
    nt = pl.num_programs(1)
    TT = MIX_TILE
    S = MIX_SEQS

    @pl.when(ti == 0)
    def _():
        for s in range(S):
            ubuf[s, 0:HIST_PAD, :] = jnp.zeros((HIST_PAD, POOL_WIDTH), F32)
        c_s[...] = jnp.zeros_like(c_s)
        n_s[...] = jnp.zeros_like(n_s)
        m_s[...] = jnp.zeros_like(m_s)

    x = jnp.concatenate([x_ref[s, 0] for s in range(S)], axis=0)
    step = pl.program_id(0) * nt + ti
    cur = lax.rem(step, 2)
    nxt = 1 - cur
    n_main = POOL_WIDTH + 4 * MLSTM_WIDTH

    @pl.when(step == 0)
    def _():
        xb0 = x.astype(BF16)
        pbuf[0] = _dot(xb0, win_ref[...])
        gbuf[0] = _dot(xb0, wg_ref[...])

    xnb = jnp.concatenate([xn_ref[s, 0] for s in range(S)], axis=0).astype(BF16)

    def slab(j):
        def run():
            pbuf[nxt, :, j * MXU_COLS:(j + 1) * MXU_COLS] = _dot(xnb, win_ref[:, j * MXU_COLS:(j + 1) * MXU_COLS])
        return run

    def gate_slab():
        gbuf[nxt] = _dot(xnb, wg_ref[...])

    pending = [slab(j) for j in range(n_main // MXU_COLS)] + [gate_slab]

    def ahead(n=1):
        for _ in range(n):
            if pending:
                pending.pop(0)()

    proj = pbuf.at[cur]
    g = gbuf[cur]

    L = CHUNK
    row = lax.broadcasted_iota(I32, (L, L), 0)
    col = lax.broadcasted_iota(I32, (L, L), 1)
    causal = row >= col
    tril = jnp.where(causal, 1.0, 0.0).astype(BF16)
    pos = ti * TT + lax.broadcasted_iota(I32, (TT, 1), 0)

    for s in range(S):
        base = s * TT
        u = proj[base:base + TT, 0:POOL_WIDTH]

        ubuf[s, HIST_PAD:HIST_PAD + TT, :] = u
        for gi, w in enumerate(POOL_WINDOWS):
            sl = slice(gi * POOL_GROUP_DIM, (gi + 1) * POOL_GROUP_DIM)
            ug = u[:, sl]
            acc = ug
            for i in range(1, w):
                acc = acc + ubuf[s, HIST_PAD - i:HIST_PAD - i + TT, sl]
            cnt = jnp.minimum(pos + 1, w).astype(F32)
            z = acc / cnt - ug
            mixbuf[base:base + TT, sl] = _dot(z.astype(BF16), wpool_ref[gi]) * pscale_ref[:, sl]

        @pl.when(ti == nt - 1)
        def _():
            pool_ref[s, 0] = ubuf[s, TT + 1:TT + HIST_PAD, :]

        ubuf[s, 0:HIST_PAD, :] = ubuf[s, TT:TT + HIST_PAD, :]

    NC = TT // L
    chains = [(s, h) for s in range(S) for h in range(HEADS)]
    units = [(s, c, h) for c in range(NC) for s in range(S) for h in range(HEADS)]
    U = range(len(units))

    def rows(s, c):
        return slice(s * TT + c * L, s * TT + (c + 1) * L)

    def head_cols(part, h):
        return slice(part * POOL_WIDTH + h * HEAD_DIM, part * POOL_WIDTH + (h + 1) * HEAD_DIM)

    gate, cum, gate_t, cum_t = {}, {}, {}, {}
    for c in range(NC):
        for s in range(S):
            val = _gate_values(g[rows(s, c), :], gb_ref[...])
            v0, v1, v2 = _split3(val)
            gate[s, c] = val
            cum[s, c] = _dot(tril, v0) + _dot(tril, v1) + _dot(tril, v2)
    for key in gate:
        gate_t[key] = gate[key].T
        cum_t[key] = cum[key].T
    ahead()
    qf = [proj[rows(s, c), head_cols(1, h)] for s, c, h in units]
    kf = [proj[rows(s, c), head_cols(2, h)] * (HEAD_DIM ** -0.5) for s, c, h in units]
    vf = [proj[rows(s, c), head_cols(3, h)] for s, c, h in units]
    qb = [a.astype(BF16) for a in qf]
    kb = [a.astype(BF16) for a in kf]
    f_col = [cum[s, c][:, HEADS + h:HEADS + h + 1] for s, c, h in units]
    ahead()
    log_d = [jnp.where(causal, f_col[u] - cum_t[s, c][HEADS + h:HEADS + h + 1, :] + gate_t[s, c][h:h + 1, :],
                       -jnp.inf) for u, (s, c, h) in enumerate(units)]
    ahead()
    row_max = [jnp.max(log_d[u], axis=-1, keepdims=True) for u in U]
    ahead()
    qk_raw = [_dot_nt(qb[u], kb[u]) for u in U]

    m_prev, m_t, inter = [None] * len(units), [None] * len(units), [None] * len(units)
    m_run = {(s, h): m_s[s, h:h + 1, 0:1] for s, h in chains}
    for u, (s, c, h) in enumerate(units):
        m_prev[u] = m_run[s, h]
        inter[u] = m_prev[u] + f_col[u]
        m_t[u] = jnp.maximum(inter[u], row_max[u])
        m_run[s, h] = m_t[u][L - 1:L, :]
    m_new = [m_t[u][L - 1:L, :] for u in U]

    ahead()
    dw = [jnp.exp(log_d[u] - m_t[u]) for u in U]
    sc = [jnp.exp(inter[u] - m_t[u]) for u in U]
    ahead()
    qk = [qk_raw[u] * dw[u] for u in U]
    ahead()
    intra = [_dot(qk[u].astype(BF16), vf[u].astype(BF16)) for u in U]
    ahead()
    row_sum = [jnp.sum(qk[u], axis=-1, keepdims=True) for u in U]
    floor = [jnp.exp(-m_t[u]) for u in U]
    f_last = [f_col[u][L - 1:L, :] for u in U]
    ahead()
    wk = [jnp.exp(gate[s, c][:, h:h + 1] + f_last[u] - f_col[u] - m_new[u]) for u, (s, c, h) in enumerate(units)]
    decay = [jnp.exp(m_prev[u] + f_last[u] - m_new[u]) for u in U]
    ahead()
    upd = [_dot_tn((vf[u] * wk[u]).astype(BF16), kb[u]) for u in U]
    ahead()
    n_upd = [jnp.sum(wk[u] * kf[u], axis=0, keepdims=True) for u in U]

    c_run = {(s, h): c_s[s, h] for s, h in chains}
    n_run = {(s, h): n_s[s, h:h + 1, :] for s, h in chains}
    hh = [None] * len(units)
    for c in range(NC):
        cu = [u for u in U if units[u][1] == c]
        inter_term = {u: _dot_nt(qb[u], c_run[units[u][0], units[u][2]].astype(BF16)) for u in cu}
        n_term = {u: jnp.sum(qf[u] * n_run[units[u][0], units[u][2]], axis=-1, keepdims=True) for u in cu}
        for u in cu:
            s, _, h = units[u]
            num = intra[u] + sc[u] * inter_term[u]
            den = row_sum[u] + sc[u] * n_term[u]
            hh[u] = num / jnp.maximum(jnp.abs(den), floor[u])
            c_run[s, h] = decay[u] * c_run[s, h] + upd[u]
            n_run[s, h] = decay[u] * n_run[s, h] + n_upd[u]
    ahead(len(pending))
    for s, h in chains:
        c_s[s, h] = c_run[s, h]
        n_s[s, h:h + 1, :] = n_run[s, h]
        m_s[s, h:h + 1, :] = jnp.broadcast_to(m_run[s, h], (1, LANES))
    for u, (s, c, h) in enumerate(units):
        mixbuf[rows(s, c), head_cols(1, h)] = _head_out(
            hh[u], proj[rows(s, c), head_cols(4, h)], mhg_ref[:, h * HEAD_DIM:(h + 1) * HEAD_DIM])

    @pl.when(ti == nt - 1)
    def _():
        for s in range(S):
            c_out_ref[s, 0] = c_s[s]
            n_out_ref[s, 0] = n_s[s, 0:HEADS, :]
            m_out_ref[s, 0] = m_s[s]

    mix = _dot(mixbuf[...].astype(BF16), wout_ref[...])
    x1 = _layer_norm(DN_ALPHA * x + mix, ln1g_ref[...], ln1b_ref[...])
    for s in range(S):
        x1_ref[s] = x1[s * TT:(s + 1) * TT, :]


def _prompt_mixer(x, w_in_b, w_g_b, gbias, w_pool_b, pscale, mhg, w_out_b, ln1g, ln1b):
    nt = SEQ // MIX_TILE
    S = MIX_SEQS
    G = BATCH // S
    const2 = lambda b, t: (0, 0)
    const3 = lambda b, t: (0, 0, 0)

    def next_tile(b, t):
        nxt = jnp.minimum(b * nt + t + 1, G * nt - 1)
        return (0, nxt // nt, nxt % nt, 0)

    outs = pl.pallas_call(
        _prompt_mixer_kernel,
        grid=(G, nt),
        in_specs=[
            pl.BlockSpec((S, 1, MIX_TILE, D_MODEL), lambda b, t: (0, b, t, 0)),
            pl.BlockSpec((S, 1, MIX_TILE, D_MODEL), next_tile),
            pl.BlockSpec(w_in_b.shape, const2),
            pl.BlockSpec(w_g_b.shape, const2),
            pl.BlockSpec(gbias.shape, const2),
            pl.BlockSpec(w_pool_b.shape, const3),
            pl.BlockSpec(pscale.shape, const2),
            pl.BlockSpec(mhg.shape, const2),
            pl.BlockSpec(w_out_b.shape, const2),
            pl.BlockSpec(ln1g.shape, const2),
            pl.BlockSpec(ln1b.shape, const2),
        ],
        out_specs=[
            pl.BlockSpec((S, MIX_TILE, D_MODEL), lambda b, t: (0, b * nt + t, 0)),
            pl.BlockSpec((S, 1, POOL_HIST, POOL_WIDTH), lambda b, t: (0, b, 0, 0)),
            pl.BlockSpec((S, 1, HEADS, HEAD_DIM, HEAD_DIM), lambda b, t: (0, b, 0, 0, 0)),
            pl.BlockSpec((S, 1, HEADS, HEAD_DIM), lambda b, t: (0, b, 0, 0)),
            pl.BlockSpec((S, 1, SUBLANES, LANES), lambda b, t: (0, b, 0, 0)),
        ],
        out_shape=[
            jax.ShapeDtypeStruct((S, G * SEQ, D_MODEL), F32),
            jax.ShapeDtypeStruct((S, G, POOL_HIST, POOL_WIDTH), F32),
            jax.ShapeDtypeStruct((S, G, HEADS, HEAD_DIM, HEAD_DIM), F32),
            jax.ShapeDtypeStruct((S, G, HEADS, HEAD_DIM), F32),
            jax.ShapeDtypeStruct((S, G, SUBLANES, LANES), F32),
        ],
        scratch_shapes=[
            pltpu.VMEM((S, HIST_PAD + MIX_TILE, POOL_WIDTH), F32),
            pltpu.VMEM((S * MIX_TILE, D_MODEL), F32),
            pltpu.VMEM((2, S * MIX_TILE, POOL_WIDTH + 4 * MLSTM_WIDTH), F32),
            pltpu.VMEM((2, S * MIX_TILE, LANES), F32),
            pltpu.VMEM((S, HEADS, HEAD_DIM, HEAD_DIM), F32),
            pltpu.VMEM((S, SUBLANES, HEAD_DIM), F32),
            pltpu.VMEM((S, SUBLANES, LANES), F32),
        ],
        compiler_params=pltpu.CompilerParams(
            dimension_semantics=("arbitrary", "arbitrary"), vmem_limit_bytes=VMEM_LIMIT),
        name="prompt_mixer",
    )(x.reshape(S, G, SEQ, D_MODEL), x.reshape(S, G, SEQ, D_MODEL), w_in_b, w_g_b, gbias, w_pool_b, pscale, mhg, w_out_b, ln1g, ln1b)
    x1, pool, c, n, m = outs
    return (x1.reshape(N_PROMPT, D_MODEL), pool.reshape(BATCH, POOL_HIST, POOL_WIDTH),
            c.reshape(BATCH, HEADS, HEAD_DIM, HEAD_DIM), n.reshape(BATCH, HEADS, HEAD_DIM),
            m.reshape(BATCH, SUBLANES, LANES))


def _sample_mixer_kernel(x_ref, hist_ref, c_ref, n_ref, m_ref, win_ref, wg_ref, gb_ref, wpool_ref,
                         pscale_ref, mhg_ref, wout_ref, ln1g_ref, ln1b_ref,
                         x1_ref, pool_out_ref, c_out_ref, n_out_ref, m_out_ref,
                         q_s, k_s, vw_s, v_s, o_s, mixbuf, h_s, coef_s):
    i = pl.program_id(0)
    nsteps = pl.num_programs(0)
    B = DEC_BATCH

    @pl.when(i == 0)
    def _():
        x = x_ref[...]
        xb = x.astype(BF16)
        proj = _dot(xb, win_ref[...])
        g = _dot(xb, wg_ref[...])
        u = proj[:, 0:POOL_WIDTH]
        for gi, w in enumerate(POOL_WINDOWS):
            sl = slice(gi * POOL_GROUP_DIM, (gi + 1) * POOL_GROUP_DIM)
            ug = u[:, sl]
            s = ug
            for j in range(1, w):
                r = POOL_HIST - j
                s = s + hist_ref[:, r * POOL_WIDTH + gi * POOL_GROUP_DIM:r * POOL_WIDTH + (gi + 1) * POOL_GROUP_DIM]
            cnt = float(min(PAST_LEN + 1, w))
            z = s / cnt - ug
            mixbuf[:, sl] = _dot(z.astype(BF16), wpool_ref[gi]) * pscale_ref[:, sl]
        pool_out_ref[:, 0:(POOL_HIST - 1) * POOL_WIDTH] = hist_ref[:, POOL_WIDTH:POOL_HIST * POOL_WIDTH]
        pool_out_ref[:, (POOL_HIST - 1) * POOL_WIDTH:POOL_HIST * POOL_WIDTH] = u

        val = _gate_values(g, gb_ref[...])
        lane = lax.broadcasted_iota(I32, (B, LANES), 1)
        qk_all = jnp.zeros((B, LANES), F32)
        sc_all = jnp.zeros((B, LANES), F32)
        den_all = jnp.zeros((B, LANES), F32)
        floor_all = jnp.zeros((B, LANES), F32)
        m_all = jnp.zeros((B, LANES), F32)
        for h in range(HEADS):
            hs = slice(h * HEAD_DIM, (h + 1) * HEAD_DIM)
            qf = proj[:, POOL_WIDTH + h * HEAD_DIM:POOL_WIDTH + (h + 1) * HEAD_DIM]
            kf = proj[:, 2 * POOL_WIDTH + h * HEAD_DIM:2 * POOL_WIDTH + (h + 1) * HEAD_DIM] * (HEAD_DIM ** -0.5)
            vf = proj[:, 3 * POOL_WIDTH + h * HEAD_DIM:3 * POOL_WIDTH + (h + 1) * HEAD_DIM]
            ig = val[:, h:h + 1]
            lf = val[:, HEADS + h:HEADS + h + 1]
            m0 = m_ref[:, h:h + 1]
            n0 = n_ref[:, hs]
            inter = m0 + lf
            m_t = jnp.maximum(inter, ig)
            dw = jnp.exp(ig - m_t)
            sc = jnp.exp(inter - m_t)
            qk = jnp.sum(qf * kf, axis=-1, keepdims=True) * dw
            den = qk + sc * jnp.sum(qf * n0, axis=-1, keepdims=True)
            n_out_ref[:, hs] = sc * n0 + dw * kf
            q_s[0:B, hs] = qf
            k_s[0:B, hs] = kf
            v_s[0:B, hs] = vf
            vw_s[0:B, hs] = vf * dw
            sel = lane == h
            qk_all = jnp.where(sel, qk, qk_all)
            sc_all = jnp.where(sel, sc, sc_all)
            den_all = jnp.where(sel, den, den_all)
            floor_all = jnp.where(sel, jnp.exp(-m_t), floor_all)
            m_all = jnp.where(lane == HEADS + h, m_t, m_all)
        o_s[...] = proj[:, 4 * POOL_WIDTH:5 * POOL_WIDTH]
        coef_s[0] = qk_all
        coef_s[1] = sc_all
        coef_s[2] = den_all
        coef_s[3] = floor_all
        m_out_ref[...] = m_all

    rows = pl.ds(pl.multiple_of(i * SAMPLE_BT, SAMPLE_BT), SAMPLE_BT)
    q_t, k_t, v_t, vw_t = q_s[rows, :], k_s[rows, :], v_s[rows, :], vw_s[rows, :]
    qk_t, sc_t, den_t, floor_t = coef_s[0, rows, :], coef_s[1, rows, :], coef_s[2, rows, :], coef_s[3, rows, :]
    h_rows = []
    for bl in range(SAMPLE_BT):
        heads = []
        for h in range(HEADS):
            hs = slice(h * HEAD_DIM, (h + 1) * HEAD_DIM)
            c_prev = c_ref[bl, h]
            q8 = jnp.broadcast_to(q_t[bl:bl + 1, hs], (SUBLANES, HEAD_DIM))
            cq = _dot_nt(q8.astype(BF16), c_prev.astype(BF16))[0:1, :]
            qk = qk_t[bl:bl + 1, h:h + 1]
            sc = sc_t[bl:bl + 1, h:h + 1]
            num = qk * v_t[bl:bl + 1, hs] + sc * cq
            heads.append(num / jnp.maximum(jnp.abs(den_t[bl:bl + 1, h:h + 1]), floor_t[bl:bl + 1, h:h + 1]))
            v_col = jnp.broadcast_to(vw_t[bl:bl + 1, hs], (HEAD_DIM, HEAD_DIM)).T
            c_out_ref[bl, h] = sc * c_prev + v_col * k_t[bl:bl + 1, hs]
        h_rows.append(jnp.concatenate(heads, axis=1))
    h_s[rows, :] = jnp.concatenate(h_rows, axis=0)

    @pl.when(i == nsteps - 1)
    def _():
        for h in range(HEADS):
            hs = slice(h * HEAD_DIM, (h + 1) * HEAD_DIM)
            mixbuf[:, POOL_WIDTH + h * HEAD_DIM:POOL_WIDTH + (h + 1) * HEAD_DIM] = _head_out(
                h_s[:, hs], o_s[:, hs], mhg_ref[:, hs])
        mix = _dot(mixbuf[...].astype(BF16), wout_ref[...])
        x1 = _layer_norm(DN_ALPHA * x_ref[...] + mix, ln1g_ref[...], ln1b_ref[...])
        x1_ref[0:B, :] = x1
        x1_ref[B:TOK_TILE, :] = jnp.zeros((TOK_TILE - B, D_MODEL), F32)


def _sample_mixer(x, hist2, c0, n0, m0, w_in_b, w_g_b, gbias, w_pool_b, pscale, mhg, w_out_b, ln1g, ln1b):
    B = DEC_BATCH
    steps = B // SAMPLE_BT
    full = lambda a: pl.BlockSpec(a.shape, lambda i: (0,) * a.ndim)
    c_spec = pl.BlockSpec((SAMPLE_BT, HEADS, HEAD_DIM, HEAD_DIM), lambda i: (i, 0, 0, 0))
    return pl.pallas_call(
        _sample_mixer_kernel,
        grid=(steps,),
        in_specs=[full(x), full(hist2), c_spec, full(n0), full(m0), full(w_in_b), full(w_g_b), full(gbias),
                  full(w_pool_b), full(pscale), full(mhg), full(w_out_b), full(ln1g), full(ln1b)],
        out_specs=[
            pl.BlockSpec((TOK_TILE, D_MODEL), lambda i: (0, 0)),
            pl.BlockSpec((B, POOL_HIST * POOL_WIDTH), lambda i: (0, 0)),
            c_spec,
            pl.BlockSpec((B, MLSTM_WIDTH), lambda i: (0, 0)),
            pl.BlockSpec((B, LANES), lambda i: (0, 0)),
        ],
        out_shape=[
            jax.ShapeDtypeStruct((TOK_TILE, D_MODEL), F32),
            jax.ShapeDtypeStruct((B, POOL_HIST * POOL_WIDTH), F32),
            jax.ShapeDtypeStruct((B, HEADS, HEAD_DIM, HEAD_DIM), F32),
            jax.ShapeDtypeStruct((B, MLSTM_WIDTH), F32),
            jax.ShapeDtypeStruct((B, LANES), F32),
        ],
        scratch_shapes=[
            pltpu.VMEM((B, MLSTM_WIDTH), F32),
            pltpu.VMEM((B, MLSTM_WIDTH), F32),
            pltpu.VMEM((B, MLSTM_WIDTH), F32),
            pltpu.VMEM((B, MLSTM_WIDTH), F32),
            pltpu.VMEM((B, MLSTM_WIDTH), F32),
            pltpu.VMEM((B, D_MODEL), F32),
            pltpu.VMEM((B, MLSTM_WIDTH), F32),
            pltpu.VMEM((4, B, LANES), F32),
        ],
        compiler_params=pltpu.CompilerParams(
            dimension_semantics=("arbitrary",), vmem_limit_bytes=VMEM_LIMIT),
        name="sample_mixer",
    )(x, hist2, c0, n0, m0, w_in_b, w_g_b, gbias, w_pool_b, pscale, mhg, w_out_b, ln1g, ln1b)


def _pick_tile(i, prompt_ref, sample_ref):
    return jnp.where(i < N_PROMPT_TILES, prompt_ref[...], sample_ref[...])


def _placement(slot_rows, group):
    r = group * TOK_TILE + lax.broadcasted_iota(I32, (TOK_TILE, TOK_TILE), 0)
    return [r == s for s in slot_rows]


def _route_kernel(xp_ref, xs_ref, wrt_ref, br_ref, slot_ref, gate_ref, nch_ref, sorted_ref):
    i = pl.program_id(0)
    T = TOK_TILE
    E = N_EXPERTS

    x = _pick_tile(i, xp_ref, xs_ref)
    xh = x.astype(BF16)
    xl = (x - xh.astype(F32)).astype(BF16)
    w = wrt_ref[...]
    wh = w.astype(BF16)
    wl = (w - wh.astype(F32)).astype(BF16)
    logits = _dot_nt(wh, xh) + (_dot_nt(wh, xl) + _dot_nt(wl, xh)) + br_ref[:, 0:1]

    erow = lax.broadcasted_iota(I32, (E, T), 0).astype(F32)
    work = logits
    vals, sels = [], []
    for _ in range(TOP_K):
        mx = jnp.max(work, axis=0, keepdims=True)
        idx = jnp.min(jnp.where(work == mx, erow, float(E)), axis=0, keepdims=True)
        sel = erow == idx
        work = jnp.where(sel, -jnp.inf, work)
        vals.append(mx)
        sels.append(sel)
    chosen = jnp.logical_or(jnp.logical_or(sels[0], sels[1]), jnp.logical_or(sels[2], sels[3]))
    es = [jnp.exp(v - vals[0]) for v in vals]
    tot = es[0] + es[1] + es[2] + es[3]

    n_valid = jnp.where(i < N_PROMPT_TILES, T, DEC_BATCH)
    valid = lax.broadcasted_iota(I32, (1, T), 1) < n_valid
    onehot = jnp.where(jnp.logical_and(chosen, valid), 1.0, 0.0)
    trow = lax.broadcasted_iota(I32, (T, T), 0)
    tcol = lax.broadcasted_iota(I32, (T, T), 1)
    before = jnp.where(trow < tcol, 1.0, 0.0).astype(BF16)
    rank = _dot(onehot.astype(BF16), before)
    cnt = jnp.sum(onehot, axis=1, keepdims=True)
    nch = jnp.floor((cnt + (CHUNK_ROWS - 1)) * (1.0 / CHUNK_ROWS))
    lower = jnp.where(lax.broadcasted_iota(I32, (E, E), 0) > lax.broadcasted_iota(I32, (E, E), 1), 1.0, 0.0)
    nch_b = jnp.broadcast_to(nch, (E, LANES))
    seg_start = _dot(lower.astype(BF16), nch_b.astype(BF16))[:, 0:1] * CHUNK_ROWS
    base = seg_start + rank

    r8 = lax.broadcasted_iota(I32, (SUBLANES, T), 0)
    s_out = jnp.zeros((SUBLANES, T), I32)
    g_out = jnp.zeros((SUBLANES, T), F32)
    slot_rows = []
    for j in range(TOP_K):
        slot_j = jnp.sum(jnp.where(sels[j], base, 0.0), axis=0, keepdims=True).astype(I32)
        slot_j = jnp.where(valid, slot_j, -1)
        slot_rows.append(slot_j)
        s_out = jnp.where(r8 == j, slot_j, s_out)
        g_out = jnp.where(r8 == j, es[j] / tot, g_out)
    slot_ref[0] = s_out
    gate_ref[0] = g_out
    nch_ref[0] = nch_b

    for grp in range(GROUPS):
        m = _placement(slot_rows, grp)
        hit = jnp.logical_or(jnp.logical_or(m[0], m[1]), jnp.logical_or(m[2], m[3]))
        place = jnp.where(hit, 1.0, 0.0).astype(BF16)
        sorted_ref[grp * T:(grp + 1) * T, :] = _dot(place, xh).astype(BF16)


def _route(x1p, x1s, w_router_t, b_router_col):
    tile_spec = pl.BlockSpec((1, SUBLANES, TOK_TILE), lambda i: (i, 0, 0))
    return pl.pallas_call(
        _route_kernel,
        grid=(N_TILES,),
        in_specs=[
            pl.BlockSpec((TOK_TILE, D_MODEL), lambda i: (jnp.minimum(i, N_PROMPT_TILES - 1), 0)),
            pl.BlockSpec((TOK_TILE, D_MODEL), lambda i: (0, 0)),
            pl.BlockSpec(w_router_t.shape, lambda i: (0, 0)),
            pl.BlockSpec(b_router_col.shape, lambda i: (0, 0)),
        ],
        out_specs=[tile_spec, tile_spec,
                   pl.BlockSpec((1, N_EXPERTS, LANES), lambda i: (i, 0, 0)),
                   pl.BlockSpec((LOCAL_ROWS, D_MODEL), lambda i: (i, 0))],
        out_shape=[
            jax.ShapeDtypeStruct((N_TILES, SUBLANES, TOK_TILE), I32),
            jax.ShapeDtypeStruct((N_TILES, SUBLANES, TOK_TILE), F32),
            jax.ShapeDtypeStruct((N_TILES, N_EXPERTS, LANES), F32),
            jax.ShapeDtypeStruct((N_TILES * LOCAL_ROWS, D_MODEL), BF16),
        ],
        compiler_params=pltpu.CompilerParams(
            dimension_semantics=("arbitrary",), vmem_limit_bytes=VMEM_LIMIT),
        name="route",
    )(x1p, x1s, w_router_t, b_router_col)


def _expert_kernel(src_ref, dst_ref, bexp_ref, first_ref, next_ref, quarters_ref, nused_ref,
                   sorted_hbm, w1_hbm, b1_ref, w2_hbm, b2_ref, out_hbm,
                   w1_stage, w2_stage, w1_b, w2_b, xbuf, obuf, zbuf, wsem, gsem, ssem, zsem):
    nused = nused_ref[0]
    obuf[...] = jnp.zeros_like(obuf)

    def fetch_piece(e, p):
        r1 = pl.ds(pl.multiple_of(p * (D_MODEL // WEIGHT_PIECES), SUBLANES), D_MODEL // WEIGHT_PIECES)
        r2 = pl.ds(pl.multiple_of(p * (D_FF // WEIGHT_PIECES), SUBLANES), D_FF // WEIGHT_PIECES)
        return (pltpu.make_async_copy(w1_hbm.at[e, r1, :], w1_stage.at[r1, :], wsem.at[0]),
                pltpu.make_async_copy(w2_hbm.at[e, r2, :], w2_stage.at[r2, :], wsem.at[1]))

    def start_pieces(e, lo, hi):
        def body(p, c):
            for cp in fetch_piece(e, p):
                cp.start()
            return c
        lax.fori_loop(lo, hi, body, 0)

    def gather(b, q):
        slot = lax.rem(b, GATHER_DEPTH)
        row = pl.multiple_of(src_ref[b * BLOCK_CHUNKS + q], CHUNK_ROWS)
        return pltpu.make_async_copy(sorted_hbm.at[pl.ds(row, CHUNK_ROWS), :],
                                     xbuf.at[slot, pl.ds(q * CHUNK_ROWS, CHUNK_ROWS), :], gsem.at[slot])

    def scatter(b, q):
        slot = lax.rem(b, 2)
        row = pl.multiple_of(dst_ref[b * BLOCK_CHUNKS + q], CHUNK_ROWS)
        return pltpu.make_async_copy(obuf.at[slot, pl.ds(q * CHUNK_ROWS, CHUNK_ROWS), :],
                                     out_hbm.at[pl.ds(row, CHUNK_ROWS), :], ssem.at[slot])

    def zero_rows(start, n_rows):
        start = pl.multiple_of(start, CHUNK_ROWS)
        return pltpu.make_async_copy(zbuf.at[pl.ds(0, n_rows), :], out_hbm.at[pl.ds(start, n_rows), :], zsem)

    def zero_tail(k):
        return zero_rows(k * LOCAL_ROWS + TOK_TILE * TOP_K, FREE_ROWS)

    def zero_last_tile(part):
        return zero_rows((N_TILES - 1) * LOCAL_ROWS + part * FREE_ROWS, FREE_ROWS)

    zbuf[...] = jnp.zeros_like(zbuf)
    lax.fori_loop(0, N_TILES, lambda k, c: (zero_tail(k).start(), c)[1], 0)
    for part in range(TOK_TILE * TOP_K // FREE_ROWS):
        zero_last_tile(part).start()
    for part in range(DUMP_ROWS // FREE_ROWS):
        zero_rows(DUMP_BASE + part * FREE_ROWS, FREE_ROWS).start()
    for ahead in range(GATHER_DEPTH - 1):
        @pl.when(ahead < nused)
        def _():
            for q in range(BLOCK_CHUNKS):
                gather(ahead, q).start()
    lax.fori_loop(0, N_TILES, lambda k, c: (zero_tail(k).wait(), c)[1], 0)
    for part in range(TOK_TILE * TOP_K // FREE_ROWS):
        zero_last_tile(part).wait()
    for part in range(DUMP_ROWS // FREE_ROWS):
        zero_rows(DUMP_BASE + part * FREE_ROWS, FREE_ROWS).wait()

    half = MXU_COLS // 2
    k_io = lax.broadcasted_iota(I32, (MXU_COLS, MXU_COLS), 0)
    j_io = lax.broadcasted_iota(I32, (MXU_COLS, MXU_COLS), 1)
    src_col = jnp.where(j_io < half, 2 * j_io, 2 * (j_io - half) + 1)
    perm = jnp.where(k_io == src_col, 1.0, 0.0).astype(BF16)

    def block(i, fetched):
        e = bexp_ref[i]
        slot = lax.rem(i, 2)
        is_first = first_ref[i] == 1

        @pl.when(is_first)
        def _():
            start_pieces(e, fetched, WEIGHT_PIECES)

            def wait_piece(p, c):
                for cp in fetch_piece(e, p):
                    cp.wait()
                return c
            lax.fori_loop(0, WEIGHT_PIECES, wait_piece, 0)
            for c in range(2 * D_FF // MXU_COLS):
                blk = w1_stage[:, c * MXU_COLS:(c + 1) * MXU_COLS].astype(BF16)
                sep = _dot(blk, perm).astype(BF16)
                w1_b[:, c * half:(c + 1) * half] = sep[:, 0:half]
                w1_b[:, D_FF + c * half:D_FF + (c + 1) * half] = sep[:, half:MXU_COLS]
            w2_b[...] = w2_stage[...].astype(BF16)

        fetched = jnp.where(is_first, 0, fetched)

        for q in range(BLOCK_CHUNKS):
            gather(i, q).wait()

        @pl.when(i + GATHER_DEPTH - 1 < nused)
        def _():
            for q in range(BLOCK_CHUNKS):
                gather(i + GATHER_DEPTH - 1, q).start()

        @pl.when(i >= 2)
        def _():
            for q in range(BLOCK_CHUNKS):
                scatter(i - 2, q).wait()

        def ffn(n_rows):
            h = _dot(xbuf[lax.rem(i, GATHER_DEPTH), 0:n_rows, :], w1_b[...]) + b1_ref[e]
            glu = jnp.minimum(h[:, 0:D_FF], SWIGLU_LIMIT)
            lin = jnp.clip(h[:, D_FF:2 * D_FF], -SWIGLU_LIMIT, SWIGLU_LIMIT)
            a = glu * jax.nn.sigmoid(SWIGLU_ALPHA * glu) * (lin + 1.0)
            obuf[slot, 0:n_rows, :] = (_dot(a.astype(BF16), w2_b[...]) + b2_ref[e]).astype(BF16)

        for quarters in range(1, BLOCK_QUARTERS + 1):
            @pl.when(quarters_ref[i] == quarters)
            def _():
                ffn(quarters * (ROW_BLOCK // BLOCK_QUARTERS))

        for q in range(BLOCK_CHUNKS):
            scatter(i, q).start()

        more = jnp.where(next_ref[i] >= 0, jnp.minimum(fetched + PIECES_PER_BLOCK, WEIGHT_PIECES), fetched)
        start_pieces(next_ref[i], fetched, more)
        return more

    lax.fori_loop(0, nused, block, jnp.int32(0))

    @pl.when(nused >= 2)
    def _():
        for q in range(BLOCK_CHUNKS):
            scatter(nused - 2, q).wait()
    for q in range(BLOCK_CHUNKS):
        scatter(nused - 1, q).wait()


def _experts(chunk_src, chunk_dst, block_expert, block_first, block_next, block_quarters, n_used,
             sorted_rows, w1, b1p, w2, b2):
    whole3 = lambda i, *_: (0, 0, 0)
    grid_spec = pltpu.PrefetchScalarGridSpec(
        num_scalar_prefetch=7,
        grid=(1,),
        in_specs=[
            pl.BlockSpec(memory_space=pl.ANY),
            pl.BlockSpec(memory_space=pl.ANY),
            pl.BlockSpec(b1p.shape, whole3),
            pl.BlockSpec(memory_space=pl.ANY),
            pl.BlockSpec(b2.shape, whole3),
        ],
        out_specs=pl.BlockSpec(memory_space=pl.ANY),
        scratch_shapes=[
            pltpu.VMEM((D_MODEL, 2 * D_FF), F32),
            pltpu.VMEM((D_FF, D_MODEL), F32),
            pltpu.VMEM((D_MODEL, 2 * D_FF), BF16),
            pltpu.VMEM((D_FF, D_MODEL), BF16),
            pltpu.VMEM((GATHER_DEPTH, ROW_BLOCK, D_MODEL), BF16),
            pltpu.VMEM((2, ROW_BLOCK, D_MODEL), BF16),
            pltpu.VMEM((FREE_ROWS, D_MODEL), BF16),
            pltpu.SemaphoreType.DMA((2,)),
            pltpu.SemaphoreType.DMA((GATHER_DEPTH,)),
            pltpu.SemaphoreType.DMA((2,)),
            pltpu.SemaphoreType.DMA(()),
        ],
    )
    return pl.pallas_call(
        _expert_kernel,
        grid_spec=grid_spec,
        out_shape=jax.ShapeDtypeStruct((DUMP_BASE + DUMP_ROWS, D_MODEL), BF16),
        compiler_params=pltpu.CompilerParams(
            dimension_semantics=("arbitrary",), vmem_limit_bytes=VMEM_LIMIT),
        name="experts",
    )(chunk_src, chunk_dst, block_expert, block_first, block_next, block_quarters, n_used,
      sorted_rows, w1, b1p, w2, b2)


def _combine_kernel(slot_ref, gate_ref, xp_ref, xs_ref, pp_ref, ps_ref, eo_ref, ln2g_ref, ln2b_ref,
                    wpg_ref, wple_ref, yp_ref, ys_ref):
    i = pl.program_id(0)
    T = TOK_TILE

    slot_rows = [slot_ref[0, j:j + 1, :] for j in range(TOP_K)]
    gate_rows = [gate_ref[0, j:j + 1, :] for j in range(TOP_K)]
    pad = jnp.zeros((LANES - SUBLANES, T), F32)
    slots_t = jnp.concatenate([slot_ref[0].astype(F32), pad], axis=0).T
    slot_cols = [slots_t[:, j:j + 1].astype(I32) for j in range(TOP_K)]

    ff = jnp.zeros((T, D_MODEL), F32)
    for grp in range(GROUPS):
        m = _placement(slot_rows, grp)
        weighted = jnp.where(m[0], gate_rows[0], jnp.where(m[1], gate_rows[1], jnp.where(
            m[2], gate_rows[2], jnp.where(m[3], gate_rows[3], 0.0))))
        g_col = jnp.sum(weighted, axis=1, keepdims=True)
        z = (eo_ref[grp * T:(grp + 1) * T, :].astype(F32) * g_col).astype(BF16)
        r = grp * T + lax.broadcasted_iota(I32, (T, T), 1)
        hit = jnp.logical_or(jnp.logical_or(r == slot_cols[0], r == slot_cols[1]),
                             jnp.logical_or(r == slot_cols[2], r == slot_cols[3]))
        ff = ff + _dot(jnp.where(hit, 1.0, 0.0).astype(BF16), z)

    x1 = _pick_tile(i, xp_ref, xs_ref)
    x2 = _layer_norm(DN_ALPHA * x1 + ff, ln2g_ref[...], ln2b_ref[...])
    p = _pick_tile(i, pp_ref, ps_ref)
    y = x2 + jax.nn.sigmoid(_dot(x2.astype(BF16), wpg_ref[...])) * _dot(p.astype(BF16), wple_ref[...])

    @pl.when(i < N_PROMPT_TILES)
    def _():
        yp_ref[...] = y

    @pl.when(i == N_PROMPT_TILES)
    def _():
        ys_ref[...] = y[0:DEC_BATCH, :]


def _combine(slots, gates, x1p, x1s, pp, ps, expert_out, ln2g, ln2b, w_pg_b, w_ple_b):
    tile_idx = lambda i: (jnp.minimum(i, N_PROMPT_TILES - 1), 0)
    const2 = lambda i: (0, 0)
    return pl.pallas_call(
        _combine_kernel,
        grid=(N_TILES,),
        in_specs=[
            pl.BlockSpec((1, SUBLANES, TOK_TILE), lambda i: (i, 0, 0)),
            pl.BlockSpec((1, SUBLANES, TOK_TILE), lambda i: (i, 0, 0)),
            pl.BlockSpec((TOK_TILE, D_MODEL), tile_idx),
            pl.BlockSpec((TOK_TILE, D_MODEL), const2),
            pl.BlockSpec((TOK_TILE, PLE_DIM), tile_idx),
            pl.BlockSpec((TOK_TILE, PLE_DIM), const2),
            pl.BlockSpec((LOCAL_ROWS, D_MODEL), lambda i: (i, 0)),
            pl.BlockSpec(ln2g.shape, const2),
            pl.BlockSpec(ln2b.shape, const2),
            pl.BlockSpec(w_pg_b.shape, const2),
            pl.BlockSpec(w_ple_b.shape, const2),
        ],
        out_specs=[
            pl.BlockSpec((TOK_TILE, D_MODEL), tile_idx),
            pl.BlockSpec((DEC_BATCH, D_MODEL), const2),
        ],
        out_shape=[
            jax.ShapeDtypeStruct((N_PROMPT, D_MODEL), F32),
            jax.ShapeDtypeStruct((DEC_BATCH, D_MODEL), F32),
        ],
        compiler_params=pltpu.CompilerParams(
            dimension_semantics=("arbitrary",), vmem_limit_bytes=VMEM_LIMIT),
        name="combine",
    )(slots, gates, x1p, x1s, pp, ps, expert_out, ln2g, ln2b, w_pg_b, w_ple_b)


def _block_tables(nch):
    seg_start = (jnp.cumsum(nch, axis=1) - nch) * CHUNK_ROWS
    tot = jnp.sum(nch, axis=0)
    nblk = (tot + BLOCK_CHUNKS - 1) // BLOCK_CHUNKS
    blk_end = jnp.cumsum(nblk)
    blk_start = blk_end - nblk
    n_used = blk_end[-1:].astype(I32)
    blk_ids = jnp.arange(N_BLOCKS, dtype=I32)

    e_ids = jnp.arange(N_EXPERTS, dtype=I32)
    used = nblk > 0
    blk_clamped = jnp.minimum(blk_ids, n_used[0] - 1)
    block_expert = jnp.minimum(jnp.sum(blk_end[:, None] <= blk_clamped[None, :], axis=0), N_EXPERTS - 1).astype(I32)
    starts_here = jnp.logical_and(blk_start[:, None] == blk_ids[None, :], used[:, None])
    block_first = jnp.any(starts_here, axis=0).astype(I32)
    later_used = jnp.logical_and(e_ids[None, :] > e_ids[:, None], used[None, :])
    next_used = jnp.min(jnp.where(later_used, e_ids[None, :], N_EXPERTS), axis=1)
    next_used = jnp.where(next_used < N_EXPERTS, next_used, -1)
    owner = block_expert[None, :] == e_ids[:, None]
    block_next = jnp.sum(jnp.where(owner, next_used[:, None], 0), axis=0).astype(I32)
    real_chunks = jnp.sum(jnp.where(owner, tot[:, None] - (blk_ids[None, :] - blk_start[:, None]) * BLOCK_CHUNKS, 0),
                          axis=0)
    quarter_chunks = BLOCK_CHUNKS // BLOCK_QUARTERS
    block_quarters = jnp.clip((real_chunks + quarter_chunks - 1) // quarter_chunks, 1, BLOCK_QUARTERS).astype(I32)

    nch_t = nch.T
    seg_first = (blk_start[:, None] * BLOCK_CHUNKS + jnp.cumsum(nch_t, axis=1) - nch_t).reshape(-1)
    seg_count = nch_t.reshape(-1)
    seg_row = (jnp.arange(N_TILES, dtype=I32)[None, :] * LOCAL_ROWS + seg_start.T).reshape(-1)
    ent = jnp.arange(N_BLOCKS * BLOCK_CHUNKS, dtype=I32)
    d = ent[None, :] - seg_first[:, None]
    inside = jnp.logical_and(d >= 0, d < seg_count[:, None])
    row_plus_1 = jnp.sum(jnp.where(inside, seg_row[:, None] + d * CHUNK_ROWS + 1, 0), axis=0)
    real = row_plus_1 > 0
    row = row_plus_1 - 1
    dump = DUMP_BASE + (((ent // BLOCK_CHUNKS) % 2) * BLOCK_CHUNKS + ent % BLOCK_CHUNKS) * CHUNK_ROWS
    chunk_src = jnp.where(real, row, ZERO_CHUNK_ROW).astype(I32)
    chunk_dst = jnp.where(real, row, dump).astype(I32)
    return chunk_src, chunk_dst, block_expert, block_first, block_next, block_quarters, n_used


def kernel(x_prompt, x_sample, state_pool, state_mlstm_C, state_mlstm_n, state_mlstm_m, p_prompt, p_sample, w_in, b_i, b_f, w_pool, pool_scale, mh_g, w_out, ln1_g, ln1_b, w_router, b_router, w_mlp1, b_mlp1, w_mlp2, b_mlp2, ln2_g, ln2_b, w_ple, w_ple_gate):
    n_main = POOL_WIDTH + 4 * MLSTM_WIDTH
    w_in_b = w_in[0, :, 0:n_main].astype(BF16)
    w_g_b = jnp.pad(w_in[0, :, n_main:], ((0, 0), (0, LANES - 2 * HEADS))).astype(BF16)
    gbias = jnp.pad(jnp.concatenate([b_i[0], b_f[0]]), (0, LANES - 2 * HEADS)).reshape(1, LANES)
    w_pool_b = w_pool[0].astype(BF16)
    pscale = pool_scale[0].reshape(1, POOL_WIDTH)
    mhg = mh_g[0].reshape(1, MLSTM_WIDTH)
    w_out_b = w_out[0].astype(BF16)
    ln1g = ln1_g[0].reshape(1, D_MODEL)
    ln1b = ln1_b[0].reshape(1, D_MODEL)
    ln2g = ln2_g[0].reshape(1, D_MODEL)
    ln2b = ln2_b[0].reshape(1, D_MODEL)
    w_router_t = w_router[0].T
    b_router_col = jnp.broadcast_to(b_router[0].reshape(N_EXPERTS, 1), (N_EXPERTS, LANES))
    b1 = b_mlp1[0]
    b1p = jnp.concatenate([b1[:, 0::2], b1[:, 1::2]], axis=-1).reshape(N_EXPERTS, 1, 2 * D_FF)
    b2 = b_mlp2[0].reshape(N_EXPERTS, 1, D_MODEL)
    w_pg_b = w_ple_gate[0].astype(BF16)
    w_ple_b = w_ple[0].astype(BF16)

    x1p, pool_p, c_p, n_p, m_p = _prompt_mixer(
        x_prompt, w_in_b, w_g_b, gbias, w_pool_b, pscale, mhg, w_out_b, ln1g, ln1b)
    x1s, pool_s, c_s, n_s, m_s = _sample_mixer(
        x_sample.reshape(DEC_BATCH, D_MODEL),
        state_pool[0].reshape(DEC_BATCH, POOL_HIST * POOL_WIDTH),
        state_mlstm_C[0], state_mlstm_n[0].reshape(DEC_BATCH, MLSTM_WIDTH), state_mlstm_m[0],
        w_in_b, w_g_b, gbias, w_pool_b, pscale, mhg, w_out_b, ln1g, ln1b)

    slots, gates, nch, sorted_rows = _route(x1p, x1s, w_router_t, b_router_col)
    tables = _block_tables(nch[:, :, 0].astype(I32))
    expert_out = _experts(*tables, sorted_rows, w_mlp1[0], b1p, w_mlp2[0], b2)

    pp = p_prompt[0].reshape(N_PROMPT, PLE_DIM)
    ps = jnp.pad(p_sample[0].reshape(DEC_BATCH, PLE_DIM), ((0, TOK_TILE - DEC_BATCH), (0, 0)))
    yp, ys = _combine(slots, gates, x1p, x1s, pp, ps, expert_out, ln2g, ln2b, w_pg_b, w_ple_b)

    return (
        yp.reshape(BATCH, SEQ, D_MODEL),
        ys.reshape(DEC_BATCH, 1, D_MODEL),
        pool_p.reshape(1, BATCH, POOL_HIST, POOL_WIDTH),
        c_p.reshape(1, BATCH, HEADS, HEAD_DIM, HEAD_DIM),
        n_p.reshape(1, BATCH, HEADS, HEAD_DIM),
        m_p[:, 0:HEADS, 0].reshape(1, BATCH, HEADS),
        pool_s.reshape(1, DEC_BATCH, POOL_HIST, POOL_WIDTH),
        c_s.reshape(1, DEC_BATCH, HEADS, HEAD_DIM, HEAD_DIM),
        n_s.reshape(1, DEC_BATCH, HEADS, HEAD_DIM),
        m_s[:, HEADS:2 * HEADS].reshape(1, DEC_BATCH, HEADS),
    )
```

```python
import jax
import jax.numpy as jnp
from jax import lax
from jax.experimental import pallas as pl
from jax.experimental.pallas import tpu as pltpu

F32 = jnp.float32
BF16 = jnp.bfloat16
I32 = jnp.int32

D_MODEL = 1024
BATCH = 8
SEQ = 2048
DEC_BATCH = 128
PAST_LEN = 16384
POOL_WIDTH = 512
POOL_GROUP_DIM = 128
POOL_WINDOWS = (2, 4, 8, 16)
POOL_HIST = 15
MLSTM_WIDTH = 512
HEADS = 4
HEAD_DIM = 128
CHUNK = 128
N_EXPERTS = 32
TOP_K = 4
D_FF = 1024
SWIGLU_ALPHA = 1.702
SWIGLU_LIMIT = 7.0
PLE_DIM = 256
DN_ALPHA = 2.0 ** 0.25
LN_EPS = 1e-5

LANES = 128
SUBLANES = 8
BF16_ROWS = 16
MXU_COLS = 256
VMEM_LIMIT = 56 * 1024 * 1024

MIX_TILE = 256
MIX_SEQS = 2
HIST_PAD = 16
TOK_TILE = 512
N_PROMPT = BATCH * SEQ
N_PROMPT_TILES = N_PROMPT // TOK_TILE
N_TILES = N_PROMPT_TILES + 1
SAMPLE_BT = 16

CHUNK_ROWS = BF16_ROWS
LOCAL_ROWS = TOK_TILE * TOP_K + N_EXPERTS * CHUNK_ROWS
GROUPS = LOCAL_ROWS // TOK_TILE
ROW_BLOCK = 1024
BLOCK_QUARTERS = 8
BLOCK_CHUNKS = ROW_BLOCK // CHUNK_ROWS
MAX_CHUNKS = N_TILES * (TOK_TILE * TOP_K // CHUNK_ROWS + N_EXPERTS)
N_BLOCKS = -(-MAX_CHUNKS // BLOCK_CHUNKS) + N_EXPERTS
ZERO_CHUNK_ROW = LOCAL_ROWS - CHUNK_ROWS
FREE_ROWS = LOCAL_ROWS - TOK_TILE * TOP_K
DUMP_BASE = N_TILES * LOCAL_ROWS
DUMP_ROWS = 2 * ROW_BLOCK
assert DUMP_ROWS % FREE_ROWS == 0
WEIGHT_PIECES = 8
PIECES_PER_BLOCK = 8
GATHER_DEPTH = 3


def _dot(a, b):
    return jnp.dot(a, b, preferred_element_type=F32)


def _dot_nt(a, b):
    return lax.dot_general(a, b, (((1,), (1,)), ((), ())), preferred_element_type=F32)


def _dot_tn(a, b):
    return lax.dot_general(a, b, (((0,), (0,)), ((), ())), preferred_element_type=F32)


def _split3(a):
    a0 = a.astype(BF16)
    r1 = a - a0.astype(F32)
    a1 = r1.astype(BF16)
    r2 = r1 - a1.astype(F32)
    return a0, a1, r2.astype(BF16)


def _log_sigmoid(x):
    return jnp.minimum(x, 0.0) - jnp.log1p(jnp.exp(-jnp.abs(x)))


def _layer_norm(x, g, b):
    mu = jnp.mean(x, axis=-1, keepdims=True)
    xc = x - mu
    var = jnp.mean(xc * xc, axis=-1, keepdims=True)
    return xc * lax.rsqrt(var + LN_EPS) * g + b


def _gate_values(g, gbias):
    lane = lax.broadcasted_iota(I32, g.shape, 1)
    z = g + gbias
    return jnp.where(lane < HEADS, z, _log_sigmoid(z))


def _head_out(hh, o_h, gain):
    mu = jnp.mean(hh, axis=-1, keepdims=True)
    hc = hh - mu
    var = jnp.mean(hc * hc, axis=-1, keepdims=True)
    return jax.nn.sigmoid(o_h) * (hc * lax.rsqrt(var + LN_EPS) * gain)


def _prompt_mixer_kernel(x_ref, xn_ref, win_ref, wg_ref, gb_ref, wpool_ref, pscale_ref, mhg_ref, wout_ref,
                         ln1g_ref, ln1b_ref,
                         x1_ref, pool_ref, c_out_ref, n_out_ref, m_out_ref,
                         ubuf, mixbuf, pbuf, gbuf, c_s, n_s, m_s):
    ti = pl.program_id(1)
    nt = pl.num_programs(1)
    TT = MIX_TILE
    S = MIX_SEQS

    @pl.when(ti == 0)
    def _():
        for s in range(S):
            ubuf[s, 0:HIST_PAD, :] = jnp.zeros((HIST_PAD, POOL_WIDTH), F32)
        c_s[...] = jnp.zeros_like(c_s)
        n_s[...] = jnp.zeros_like(n_s)
        m_s[...] = jnp.zeros_like(m_s)

    x = jnp.concatenate([x_ref[s, 0] for s in range(S)], axis=0)
    step = pl.program_id(0) * nt + ti
    cur = lax.rem(step, 2)
    nxt = 1 - cur
    n_main = POOL_WIDTH + 4 * MLSTM_WIDTH

    @pl.when(step == 0)
    def _():
        xb0 = x.astype(BF16)
        pbuf[0] = _dot(xb0, win_ref[...])
        gbuf[0] = _dot(xb0, wg_ref[...])

    xnb = jnp.concatenate([xn_ref[s, 0] for s in range(S)], axis=0).astype(BF16)

    def slab(j):
        def run():
            pbuf[nxt, :, j * MXU_COLS:(j + 1) * MXU_COLS] = _dot(xnb, win_ref[:, j * MXU_COLS:(j + 1) * MXU_COLS])
        return run

    def gate_slab():
        gbuf[nxt] = _dot(xnb, wg_ref[...])

    pending = [slab(j) for j in range(n_main // MXU_COLS)] + [gate_slab]

    def ahead(n=1):
        for _ in range(n):
            if pending:
                pending.pop(0)()

    proj = pbuf.at[cur]
    g = gbuf[cur]

    L = CHUNK
    row = lax.broadcasted_iota(I32, (L, L), 0)
    col = lax.broadcasted_iota(I32, (L, L), 1)
    causal = row >= col
    tril = jnp.where(causal, 1.0, 0.0).astype(BF16)
    pos = ti * TT + lax.broadcasted_iota(I32, (TT, 1), 0)

    for s in range(S):
        base = s * TT
        u = proj[base:base + TT, 0:POOL_WIDTH]

        ubuf[s, HIST_PAD:HIST_PAD + TT, :] = u
        for gi, w in enumerate(POOL_WINDOWS):
            sl = slice(gi * POOL_GROUP_DIM, (gi + 1) * POOL_GROUP_DIM)
            ug = u[:, sl]
            acc = ug
            for i in range(1, w):
                acc = acc + ubuf[s, HIST_PAD - i:HIST_PAD - i + TT, sl]
            cnt = jnp.minimum(pos + 1, w).astype(F32)
            z = acc / cnt - ug
            mixbuf[base:base + TT, sl] = _dot(z.astype(BF16), wpool_ref[gi]) * pscale_ref[:, sl]

        @pl.when(ti == nt - 1)
        def _():
            pool_ref[s, 0] = ubuf[s, TT + 1:TT + HIST_PAD, :]

        ubuf[s, 0:HIST_PAD, :] = ubuf[s, TT:TT + HIST_PAD, :]

    NC = TT // L
    chains = [(s, h) for s in range(S) for h in range(HEADS)]
    units = [(s, c, h) for c in range(NC) for s in range(S) for h in range(HEADS)]
    U = range(len(units))

    def rows(s, c):
        return slice(s * TT + c * L, s * TT + (c + 1) * L)

    def head_cols(part, h):
        return slice(part * POOL_WIDTH + h * HEAD_DIM, part * POOL_WIDTH + (h + 1) * HEAD_DIM)

    gate, cum, gate_t, cum_t = {}, {}, {}, {}
    for c in range(NC):
        for s in range(S):
            val = _gate_values(g[rows(s, c), :], gb_ref[...])
            v0, v1, v2 = _split3(val)
            gate[s, c] = val
            cum[s, c] = _dot(tril, v0) + _dot(tril, v1) + _dot(tril, v2)
    for key in gate:
        gate_t[key] = gate[key].T
        cum_t[key] = cum[key].T
    ahead()
    qf = [proj[rows(s, c), head_cols(1, h)] for s, c, h in units]
    kf = [proj[rows(s, c), head_cols(2, h)] * (HEAD_DIM ** -0.5) for s, c, h in units]
    vf = [proj[rows(s, c), head_cols(3, h)] for s, c, h in units]
    qb = [a.astype(BF16) for a in qf]
    kb = [a.astype(BF16) for a in kf]
    f_col = [cum[s, c][:, HEADS + h:HEADS + h + 1] for s, c, h in units]
    ahead()
    log_d = [jnp.where(causal, f_col[u] - cum_t[s, c][HEADS + h:HEADS + h + 1, :] + gate_t[s, c][h:h + 1, :],
                       -jnp.inf) for u, (s, c, h) in enumerate(units)]
    ahead()
    row_max = [jnp.max(log_d[u], axis=-1, keepdims=True) for u in U]
    ahead()
    qk_raw = [_dot_nt(qb[u], kb[u]) for u in U]

    m_prev, m_t, inter = [None] * len(units), [None] * len(units), [None] * len(units)
    m_run = {(s, h): m_s[s, h:h + 1, 0:1] for s, h in chains}
    for u, (s, c, h) in enumerate(units):
        m_prev[u] = m_run[s, h]
        inter[u] = m_prev[u] + f_col[u]
        m_t[u] = jnp.maximum(inter[u], row_max[u])
        m_run[s, h] = m_t[u][L - 1:L, :]
    m_new = [m_t[u][L - 1:L, :] for u in U]

    ahead()
    dw = [jnp.exp(log_d[u] - m_t[u]) for u in U]
    sc = [jnp.exp(inter[u] - m_t[u]) for u in U]
    ahead()
    qk = [qk_raw[u] * dw[u] for u in U]
    ahead()
    intra = [_dot(qk[u].astype(BF16), vf[u].astype(BF16)) for u in U]
    ahead()
    row_sum = [jnp.sum(qk[u], axis=-1, keepdims=True) for u in U]
    floor = [jnp.exp(-m_t[u]) for u in U]
    f_last = [f_col[u][L - 1:L, :] for u in U]
    ahead()
    wk = [jnp.exp(gate[s, c][:, h:h + 1] + f_last[u] - f_col[u] - m_new[u]) for u, (s, c, h) in enumerate(units)]
    decay = [jnp.exp(m_prev[u] + f_last[u] - m_new[u]) for u in U]
    ahead()
    upd = [_dot_tn((vf[u] * wk[u]).astype(BF16), kb[u]) for u in U]
    ahead()
    n_upd = [jnp.sum(wk[u] * kf[u], axis=0, keepdims=True) for u in U]

    c_run = {(s, h): c_s[s, h] for s, h in chains}
    n_run = {(s, h): n_s[s, h:h + 1, :] for s, h in chains}
    hh = [None] * len(units)
    for c in range(NC):
        cu = [u for u in U if units[u][1] == c]
        inter_term = {u: _dot_nt(qb[u], c_run[units[u][0], units[u][2]].astype(BF16)) for u in cu}
        n_term = {u: jnp.sum(qf[u] * n_run[units[u][0], units[u][2]], axis=-1, keepdims=True) for u in cu}
        for u in cu:
            s, _, h = units[u]
            num = intra[u] + sc[u] * inter_term[u]
            den = row_sum[u] + sc[u] * n_term[u]
            hh[u] = num / jnp.maximum(jnp.abs(den), floor[u])
            c_run[s, h] = decay[u] * c_run[s, h] + upd[u]
            n_run[s, h] = decay[u] * n_run[s, h] + n_upd[u]
    ahead(len(pending))
    for s, h in chains:
        c_s[s, h] = c_run[s, h]
        n_s[s, h:h + 1, :] = n_run[s, h]
        m_s[s, h:h + 1, :] = jnp.broadcast_to(m_run[s, h], (1, LANES))
    for u, (s, c, h) in enumerate(units):
        mixbuf[rows(s, c), head_cols(1, h)] = _head_out(
            hh[u], proj[rows(s, c), head_cols(4, h)], mhg_ref[:, h * HEAD_DIM:(h + 1) * HEAD_DIM])

    @pl.when(ti == nt - 1)
    def _():
        for s in range(S):
            c_out_ref[s, 0] = c_s[s]
            n_out_ref[s, 0] = n_s[s, 0:HEADS, :]
            m_out_ref[s, 0] = m_s[s]

    mix = _dot(mixbuf[...].astype(BF16), wout_ref[...])
    x1 = _layer_norm(DN_ALPHA * x + mix, ln1g_ref[...], ln1b_ref[...])
    for s in range(S):
        x1_ref[s] = x1[s * TT:(s + 1) * TT, :]


def _prompt_mixer(x, w_in_b, w_g_b, gbias, w_pool_b, pscale, mhg, w_out_b, ln1g, ln1b):
    nt = SEQ // MIX_TILE
    S = MIX_SEQS
    G = BATCH // S
    const2 = lambda b, t: (0, 0)
    const3 = lambda b, t: (0, 0, 0)

    def next_tile(b, t):
        nxt = jnp.minimum(b * nt + t + 1, G * nt - 1)
        return (0, nxt // nt, nxt % nt, 0)

    outs = pl.pallas_call(
        _prompt_mixer_kernel,
        grid=(G, nt),
        in_specs=[
            pl.BlockSpec((S, 1, MIX_TILE, D_MODEL), lambda b, t: (0, b, t, 0)),
            pl.BlockSpec((S, 1, MIX_TILE, D_MODEL), next_tile),
            pl.BlockSpec(w_in_b.shape, const2),
            pl.BlockSpec(w_g_b.shape, const2),
            pl.BlockSpec(gbias.shape, const2),
            pl.BlockSpec(w_pool_b.shape, const3),
            pl.BlockSpec(pscale.shape, const2),
            pl.BlockSpec(mhg.shape, const2),
            pl.BlockSpec(w_out_b.shape, const2),
            pl.BlockSpec(ln1g.shape, const2),
            pl.BlockSpec(ln1b.shape, const2),
        ],
        out_specs=[
            pl.BlockSpec((S, MIX_TILE, D_MODEL), lambda b, t: (0, b * nt + t, 0)),
            pl.BlockSpec((S, 1, POOL_HIST, POOL_WIDTH), lambda b, t: (0, b, 0, 0)),
            pl.BlockSpec((S, 1, HEADS, HEAD_DIM, HEAD_DIM), lambda b, t: (0, b, 0, 0, 0)),
            pl.BlockSpec((S, 1, HEADS, HEAD_DIM), lambda b, t: (0, b, 0, 0)),
            pl.BlockSpec((S, 1, SUBLANES, LANES), lambda b, t: (0, b, 0, 0)),
        ],
        out_shape=[
            jax.ShapeDtypeStruct((S, G * SEQ, D_MODEL), F32),
            jax.ShapeDtypeStruct((S, G, POOL_HIST, POOL_WIDTH), F32),
            jax.ShapeDtypeStruct((S, G, HEADS, HEAD_DIM, HEAD_DIM), F32),
            jax.ShapeDtypeStruct((S, G, HEADS, HEAD_DIM), F32),
            jax.ShapeDtypeStruct((S, G, SUBLANES, LANES), F32),
        ],
        scratch_shapes=[
            pltpu.VMEM((S, HIST_PAD + MIX_TILE, POOL_WIDTH), F32),
            pltpu.VMEM((S * MIX_TILE, D_MODEL), F32),
            pltpu.VMEM((2, S * MIX_TILE, POOL_WIDTH + 4 * MLSTM_WIDTH), F32),
            pltpu.VMEM((2, S * MIX_TILE, LANES), F32),
            pltpu.VMEM((S, HEADS, HEAD_DIM, HEAD_DIM), F32),
            pltpu.VMEM((S, SUBLANES, HEAD_DIM), F32),
            pltpu.VMEM((S, SUBLANES, LANES), F32),
        ],
        compiler_params=pltpu.CompilerParams(
            dimension_semantics=("arbitrary", "arbitrary"), vmem_limit_bytes=VMEM_LIMIT),
        name="prompt_mixer",
    )(x.reshape(S, G, SEQ, D_MODEL), x.reshape(S, G, SEQ, D_MODEL), w_in_b, w_g_b, gbias, w_pool_b, pscale, mhg, w_out_b, ln1g, ln1b)
    x1, pool, c, n, m = outs
    return (x1.reshape(N_PROMPT, D_MODEL), pool.reshape(BATCH, POOL_HIST, POOL_WIDTH),
            c.reshape(BATCH, HEADS, HEAD_DIM, HEAD_DIM), n.reshape(BATCH, HEADS, HEAD_DIM),
            m.reshape(BATCH, SUBLANES, LANES))


def _sample_mixer_kernel(x_ref, hist_ref, c_ref, n_ref, m_ref, win_ref, wg_ref, gb_ref, wpool_ref,
                         pscale_ref, mhg_ref, wout_ref, ln1g_ref, ln1b_ref,
                         x1_ref, pool_out_ref, c_out_ref, n_out_ref, m_out_ref,
                         q_s, k_s, vw_s, v_s, o_s, mixbuf, h_s, coef_s):
    i = pl.program_id(0)
    nsteps = pl.num_programs(0)
    B = DEC_BATCH

    @pl.when(i == 0)
    def _():
        x = x_ref[...]
        xb = x.astype(BF16)
        proj = _dot(xb, win_ref[...])
        g = _dot(xb, wg_ref[...])
        u = proj[:, 0:POOL_WIDTH]
        for gi, w in enumerate(POOL_WINDOWS):
            sl = slice(gi * POOL_GROUP_DIM, (gi + 1) * POOL_GROUP_DIM)
            ug = u[:, sl]
            s = ug
            for j in range(1, w):
                r = POOL_HIST - j
                s = s + hist_ref[:, r * POOL_WIDTH + gi * POOL_GROUP_DIM:r * POOL_WIDTH + (gi + 1) * POOL_GROUP_DIM]
            cnt = float(min(PAST_LEN + 1, w))
            z = s / cnt - ug
            mixbuf[:, sl] = _dot(z.astype(BF16), wpool_ref[gi]) * pscale_ref[:, sl]
        pool_out_ref[:, 0:(POOL_HIST - 1) * POOL_WIDTH] = hist_ref[:, POOL_WIDTH:POOL_HIST * POOL_WIDTH]
        pool_out_ref[:, (POOL_HIST - 1) * POOL_WIDTH:POOL_HIST * POOL_WIDTH] = u

        val = _gate_values(g, gb_ref[...])
        lane = lax.broadcasted_iota(I32, (B, LANES), 1)
        qk_all = jnp.zeros((B, LANES), F32)
        sc_all = jnp.zeros((B, LANES), F32)
        den_all = jnp.zeros((B, LANES), F32)
        floor_all = jnp.zeros((B, LANES), F32)
        m_all = jnp.zeros((B, LANES), F32)
        for h in range(HEADS):
            hs = slice(h * HEAD_DIM, (h + 1) * HEAD_DIM)
            qf = proj[:, POOL_WIDTH + h * HEAD_DIM:POOL_WIDTH + (h + 1) * HEAD_DIM]
            kf = proj[:, 2 * POOL_WIDTH + h * HEAD_DIM:2 * POOL_WIDTH + (h + 1) * HEAD_DIM] * (HEAD_DIM ** -0.5)
            vf = proj[:, 3 * POOL_WIDTH + h * HEAD_DIM:3 * POOL_WIDTH + (h + 1) * HEAD_DIM]
            ig = val[:, h:h + 1]
            lf = val[:, HEADS + h:HEADS + h + 1]
            m0 = m_ref[:, h:h + 1]
            n0 = n_ref[:, hs]
            inter = m0 + lf
            m_t = jnp.maximum(inter, ig)
            dw = jnp.exp(ig - m_t)
            sc = jnp.exp(inter - m_t)
            qk = jnp.sum(qf * kf, axis=-1, keepdims=True) * dw
            den = qk + sc * jnp.sum(qf * n0, axis=-1, keepdims=True)
            n_out_ref[:, hs] = sc * n0 + dw * kf
            q_s[0:B, hs] = qf
            k_s[0:B, hs] = kf
            v_s[0:B, hs] = vf
            vw_s[0:B, hs] = vf * dw
            sel = lane == h
            qk_all = jnp.where(sel, qk, qk_all)
            sc_all = jnp.where(sel, sc, sc_all)
            den_all = jnp.where(sel, den, den_all)
            floor_all = jnp.where(sel, jnp.exp(-m_t), floor_all)
            m_all = jnp.where(lane == HEADS + h, m_t, m_all)
        o_s[...] = proj[:, 4 * POOL_WIDTH:5 * POOL_WIDTH]
        coef_s[0] = qk_all
        coef_s[1] = sc_all
        coef_s[2] = den_all
        coef_s[3] = floor_all
        m_out_ref[...] = m_all

    rows = pl.ds(pl.multiple_of(i * SAMPLE_BT, SAMPLE_BT), SAMPLE_BT)
    q_t, k_t, v_t, vw_t = q_s[rows, :], k_s[rows, :], v_s[rows, :], vw_s[rows, :]
    qk_t, sc_t, den_t, floor_t = coef_s[0, rows, :], coef_s[1, rows, :], coef_s[2, rows, :], coef_s[3, rows, :]
    h_rows = []
    for bl in range(SAMPLE_BT):
        heads = []
        for h in range(HEADS):
            hs = slice(h * HEAD_DIM, (h + 1) * HEAD_DIM)
            c_prev = c_ref[bl, h]
            q8 = jnp.broadcast_to(q_t[bl:bl + 1, hs], (SUBLANES, HEAD_DIM))
            cq = _dot_nt(q8.astype(BF16), c_prev.astype(BF16))[0:1, :]
            qk = qk_t[bl:bl + 1, h:h + 1]
            sc = sc_t[bl:bl + 1, h:h + 1]
            num = qk * v_t[bl:bl + 1, hs] + sc * cq
            heads.append(num / jnp.maximum(jnp.abs(den_t[bl:bl + 1, h:h + 1]), floor_t[bl:bl + 1, h:h + 1]))
            v_col = jnp.broadcast_to(vw_t[bl:bl + 1, hs], (HEAD_DIM, HEAD_DIM)).T
            c_out_ref[bl, h] = sc * c_prev + v_col * k_t[bl:bl + 1, hs]
        h_rows.append(jnp.concatenate(heads, axis=1))
    h_s[rows, :] = jnp.concatenate(h_rows, axis=0)

    @pl.when(i == nsteps - 1)
    def _():
        for h in range(HEADS):
            hs = slice(h * HEAD_DIM, (h + 1) * HEAD_DIM)
            mixbuf[:, POOL_WIDTH + h * HEAD_DIM:POOL_WIDTH + (h + 1) * HEAD_DIM] = _head_out(
                h_s[:, hs], o_s[:, hs], mhg_ref[:, hs])
        mix = _dot(mixbuf[...].astype(BF16), wout_ref[...])
        x1 = _layer_norm(DN_ALPHA * x_ref[...] + mix, ln1g_ref[...], ln1b_ref[...])
        x1_ref[0:B, :] = x1
        x1_ref[B:TOK_TILE, :] = jnp.zeros((TOK_TILE - B, D_MODEL), F32)


def _sample_mixer(x, hist2, c0, n0, m0, w_in_b, w_g_b, gbias, w_pool_b, pscale, mhg, w_out_b, ln1g, ln1b):
    B = DEC_BATCH
    steps = B // SAMPLE_BT
    full = lambda a: pl.BlockSpec(a.shape, lambda i: (0,) * a.ndim)
    c_spec = pl.BlockSpec((SAMPLE_BT, HEADS, HEAD_DIM, HEAD_DIM), lambda i: (i, 0, 0, 0))
    return pl.pallas_call(
        _sample_mixer_kernel,
        grid=(steps,),
        in_specs=[full(x), full(hist2), c_spec, full(n0), full(m0), full(w_in_b), full(w_g_b), full(gbias),
                  full(w_pool_b), full(pscale), full(mhg), full(w_out_b), full(ln1g), full(ln1b)],
        out_specs=[
            pl.BlockSpec((TOK_TILE, D_MODEL), lambda i: (0, 0)),
            pl.BlockSpec((B, POOL_HIST * POOL_WIDTH), lambda i: (0, 0)),
            c_spec,
            pl.BlockSpec((B, MLSTM_WIDTH), lambda i: (0, 0)),
            pl.BlockSpec((B, LANES), lambda i: (0, 0)),
        ],
        out_shape=[
            jax.ShapeDtypeStruct((TOK_TILE, D_MODEL), F32),
            jax.ShapeDtypeStruct((B, POOL_HIST * POOL_WIDTH), F32),
            jax.ShapeDtypeStruct((B, HEADS, HEAD_DIM, HEAD_DIM), F32),
            jax.ShapeDtypeStruct((B, MLSTM_WIDTH), F32),
            jax.ShapeDtypeStruct((B, LANES), F32),
        ],
        scratch_shapes=[
            pltpu.VMEM((B, MLSTM_WIDTH), F32),
            pltpu.VMEM((B, MLSTM_WIDTH), F32),
            pltpu.VMEM((B, MLSTM_WIDTH), F32),
            pltpu.VMEM((B, MLSTM_WIDTH), F32),
            pltpu.VMEM((B, MLSTM_WIDTH), F32),
            pltpu.VMEM((B, D_MODEL), F32),
            pltpu.VMEM((B, MLSTM_WIDTH), F32),
            pltpu.VMEM((4, B, LANES), F32),
        ],
        compiler_params=pltpu.CompilerParams(
            dimension_semantics=("arbitrary",), vmem_limit_bytes=VMEM_LIMIT),
        name="sample_mixer",
    )(x, hist2, c0, n0, m0, w_in_b, w_g_b, gbias, w_pool_b, pscale, mhg, w_out_b, ln1g, ln1b)


def _pick_tile(i, prompt_ref, sample_ref):
    return jnp.where(i < N_PROMPT_TILES, prompt_ref[...], sample_ref[...])


def _placement(slot_rows, group):
    r = group * TOK_TILE + lax.broadcasted_iota(I32, (TOK_TILE, TOK_TILE), 0)
    return [r == s for s in slot_rows]


def _route_kernel(xp_ref, xs_ref, wrt_ref, br_ref, slot_ref, gate_ref, nch_ref, sorted_ref):
    i = pl.program_id(0)
    T = TOK_TILE
    E = N_EXPERTS

    x = _pick_tile(i, xp_ref, xs_ref)
    xh = x.astype(BF16)
    xl = (x - xh.astype(F32)).astype(BF16)
    w = wrt_ref[...]
    wh = w.astype(BF16)
    wl = (w - wh.astype(F32)).astype(BF16)
    logits = _dot_nt(wh, xh) + (_dot_nt(wh, xl) + _dot_nt(wl, xh)) + br_ref[:, 0:1]

    erow = lax.broadcasted_iota(I32, (E, T), 0).astype(F32)
    work = logits
    vals, sels = [], []
    for _ in range(TOP_K):
        mx = jnp.max(work, axis=0, keepdims=True)
        idx = jnp.min(jnp.where(work == mx, erow, float(E)), axis=0, keepdims=True)
        sel = erow == idx
        work = jnp.where(sel, -jnp.inf, work)
        vals.append(mx)
        sels.append(sel)
    chosen = jnp.logical_or(jnp.logical_or(sels[0], sels[1]), jnp.logical_or(sels[2], sels[3]))
    es = [jnp.exp(v - vals[0]) for v in vals]
    tot = es[0] + es[1] + es[2] + es[3]

    n_valid = jnp.where(i < N_PROMPT_TILES, T, DEC_BATCH)
    valid = lax.broadcasted_iota(I32, (1, T), 1) < n_valid
    onehot = jnp.where(jnp.logical_and(chosen, valid), 1.0, 0.0)
    trow = lax.broadcasted_iota(I32, (T, T), 0)
    tcol = lax.broadcasted_iota(I32, (T, T), 1)
    before = jnp.where(trow < tcol, 1.0, 0.0).astype(BF16)
    rank = _dot(onehot.astype(BF16), before)
    cnt = jnp.sum(onehot, axis=1, keepdims=True)
    nch = jnp.floor((cnt + (CHUNK_ROWS - 1)) * (1.0 / CHUNK_ROWS))
    lower = jnp.where(lax.broadcasted_iota(I32, (E, E), 0) > lax.broadcasted_iota(I32, (E, E), 1), 1.0, 0.0)
    nch_b = jnp.broadcast_to(nch, (E, LANES))
    seg_start = _dot(lower.astype(BF16), nch_b.astype(BF16))[:, 0:1] * CHUNK_ROWS
    base = seg_start + rank

    r8 = lax.broadcasted_iota(I32, (SUBLANES, T), 0)
    s_out = jnp.zeros((SUBLANES, T), I32)
    g_out = jnp.zeros((SUBLANES, T), F32)
    slot_rows = []
    for j in range(TOP_K):
        slot_j = jnp.sum(jnp.where(sels[j], base, 0.0), axis=0, keepdims=True).astype(I32)
        slot_j = jnp.where(valid, slot_j, -1)
        slot_rows.append(slot_j)
        s_out = jnp.where(r8 == j, slot_j, s_out)
        g_out = jnp.where(r8 == j, es[j] / tot, g_out)
    slot_ref[0] = s_out
    gate_ref[0] = g_out
    nch_ref[0] = nch_b

    for grp in range(GROUPS):
        m = _placement(slot_rows, grp)
        hit = jnp.logical_or(jnp.logical_or(m[0], m[1]), jnp.logical_or(m[2], m[3]))
        place = jnp.where(hit, 1.0, 0.0).astype(BF16)
        sorted_ref[grp * T:(grp + 1) * T, :] = _dot(place, xh).astype(BF16)


def _route(x1p, x1s, w_router_t, b_router_col):
    tile_spec = pl.BlockSpec((1, SUBLANES, TOK_TILE), lambda i: (i, 0, 0))
    return pl.pallas_call(
        _route_kernel,
        grid=(N_TILES,),
        in_specs=[
            pl.BlockSpec((TOK_TILE, D_MODEL), lambda i: (jnp.minimum(i, N_PROMPT_TILES - 1), 0)),
            pl.BlockSpec((TOK_TILE, D_MODEL), lambda i: (0, 0)),
            pl.BlockSpec(w_router_t.shape, lambda i: (0, 0)),
            pl.BlockSpec(b_router_col.shape, lambda i: (0, 0)),
        ],
        out_specs=[tile_spec, tile_spec,
                   pl.BlockSpec((1, N_EXPERTS, LANES), lambda i: (i, 0, 0)),
                   pl.BlockSpec((LOCAL_ROWS, D_MODEL), lambda i: (i, 0))],
        out_shape=[
            jax.ShapeDtypeStruct((N_TILES, SUBLANES, TOK_TILE), I32),
            jax.ShapeDtypeStruct((N_TILES, SUBLANES, TOK_TILE), F32),
            jax.ShapeDtypeStruct((N_TILES, N_EXPERTS, LANES), F32),
            jax.ShapeDtypeStruct((N_TILES * LOCAL_ROWS, D_MODEL), BF16),
        ],
        compiler_params=pltpu.CompilerParams(
            dimension_semantics=("arbitrary",), vmem_limit_bytes=VMEM_LIMIT),
        name="route",
    )(x1p, x1s, w_router_t, b_router_col)


def _expert_kernel(src_ref, dst_ref, bexp_ref, first_ref, next_ref, quarters_ref, nused_ref,
                   sorted_hbm, w1_hbm, b1_ref, w2_hbm, b2_ref, out_hbm,
                   w1_stage, w2_stage, w1_b, w2_b, xbuf, obuf, zbuf, wsem, gsem, ssem, zsem):
    nused = nused_ref[0]
    obuf[...] = jnp.zeros_like(obuf)

    def fetch_piece(e, p):
        r1 = pl.ds(pl.multiple_of(p * (D_MODEL // WEIGHT_PIECES), SUBLANES), D_MODEL // WEIGHT_PIECES)
        r2 = pl.ds(pl.multiple_of(p * (D_FF // WEIGHT_PIECES), SUBLANES), D_FF // WEIGHT_PIECES)
        return (pltpu.make_async_copy(w1_hbm.at[e, r1, :], w1_stage.at[r1, :], wsem.at[0]),
                pltpu.make_async_copy(w2_hbm.at[e, r2, :], w2_stage.at[r2, :], wsem.at[1]))

    def start_pieces(e, lo, hi):
        def body(p, c):
            for cp in fetch_piece(e, p):
                cp.start()
            return c
        lax.fori_loop(lo, hi, body, 0)

    def gather(b, q):
        slot = lax.rem(b, GATHER_DEPTH)
        row = pl.multiple_of(src_ref[b * BLOCK_CHUNKS + q], CHUNK_ROWS)
        return pltpu.make_async_copy(sorted_hbm.at[pl.ds(row, CHUNK_ROWS), :],
                                     xbuf.at[slot, pl.ds(q * CHUNK_ROWS, CHUNK_ROWS), :], gsem.at[slot])

    def scatter(b, q):
        slot = lax.rem(b, 2)
        row = pl.multiple_of(dst_ref[b * BLOCK_CHUNKS + q], CHUNK_ROWS)
        return pltpu.make_async_copy(obuf.at[slot, pl.ds(q * CHUNK_ROWS, CHUNK_ROWS), :],
                                     out_hbm.at[pl.ds(row, CHUNK_ROWS), :], ssem.at[slot])

    def zero_rows(start, n_rows):
        start = pl.multiple_of(start, CHUNK_ROWS)
        return pltpu.make_async_copy(zbuf.at[pl.ds(0, n_rows), :], out_hbm.at[pl.ds(start, n_rows), :], zsem)

    def zero_tail(k):
        return zero_rows(k * LOCAL_ROWS + TOK_TILE * TOP_K, FREE_ROWS)

    def zero_last_tile(part):
        return zero_rows((N_TILES - 1) * LOCAL_ROWS + part * FREE_ROWS, FREE_ROWS)

    zbuf[...] = jnp.zeros_like(zbuf)
    lax.fori_loop(0, N_TILES, lambda k, c: (zero_tail(k).start(), c)[1], 0)
    for part in range(TOK_TILE * TOP_K // FREE_ROWS):
        zero_last_tile(part).start()
    for part in range(DUMP_ROWS // FREE_ROWS):
        zero_rows(DUMP_BASE + part * FREE_ROWS, FREE_ROWS).start()
    for ahead in range(GATHER_DEPTH - 1):
        @pl.when(ahead < nused)
        def _():
            for q in range(BLOCK_CHUNKS):
                gather(ahead, q).start()
    lax.fori_loop(0, N_TILES, lambda k, c: (zero_tail(k).wait(), c)[1], 0)
    for part in range(TOK_TILE * TOP_K // FREE_ROWS):
        zero_last_tile(part).wait()
    for part in range(DUMP_ROWS // FREE_ROWS):
        zero_rows(DUMP_BASE + part * FREE_ROWS, FREE_ROWS).wait()

    half = MXU_COLS // 2
    k_io = lax.broadcasted_iota(I32, (MXU_COLS, MXU_COLS), 0)
    j_io = lax.broadcasted_iota(I32, (MXU_COLS, MXU_COLS), 1)
    src_col = jnp.where(j_io < half, 2 * j_io, 2 * (j_io - half) + 1)
    perm = jnp.where(k_io == src_col, 1.0, 0.0).astype(BF16)

    def block(i, fetched):
        e = bexp_ref[i]
        slot = lax.rem(i, 2)
        is_first = first_ref[i] == 1

        @pl.when(is_first)
        def _():
            start_pieces(e, fetched, WEIGHT_PIECES)

            def wait_piece(p, c):
                for cp in fetch_piece(e, p):
                    cp.wait()
                return c
            lax.fori_loop(0, WEIGHT_PIECES, wait_piece, 0)
            for c in range(2 * D_FF // MXU_COLS):
                blk = w1_stage[:, c * MXU_COLS:(c + 1) * MXU_COLS].astype(BF16)
                sep = _dot(blk, perm).astype(BF16)
                w1_b[:, c * half:(c + 1) * half] = sep[:, 0:half]
                w1_b[:, D_FF + c * half:D_FF + (c + 1) * half] = sep[:, half:MXU_COLS]
            w2_b[...] = w2_stage[...].astype(BF16)

        fetched = jnp.where(is_first, 0, fetched)

        for q in range(BLOCK_CHUNKS):
            gather(i, q).wait()

        @pl.when(i + GATHER_DEPTH - 1 < nused)
        def _():
            for q in range(BLOCK_CHUNKS):
                gather(i + GATHER_DEPTH - 1, q).start()

        @pl.when(i >= 2)
        def _():
            for q in range(BLOCK_CHUNKS):
                scatter(i - 2, q).wait()

        def ffn(n_rows):
            h = _dot(xbuf[lax.rem(i, GATHER_DEPTH), 0:n_rows, :], w1_b[...]) + b1_ref[e]
            glu = jnp.minimum(h[:, 0:D_FF], SWIGLU_LIMIT)
            lin = jnp.clip(h[:, D_FF:2 * D_FF], -SWIGLU_LIMIT, SWIGLU_LIMIT)
            a = glu * jax.nn.sigmoid(SWIGLU_ALPHA * glu) * (lin + 1.0)
            obuf[slot, 0:n_rows, :] = (_dot(a.astype(BF16), w2_b[...]) + b2_ref[e]).astype(BF16)

        for quarters in range(1, BLOCK_QUARTERS + 1):
            @pl.when(quarters_ref[i] == quarters)
            def _():
                ffn(quarters * (ROW_BLOCK // BLOCK_QUARTERS))

        for q in range(BLOCK_CHUNKS):
            scatter(i, q).start()

        more = jnp.where(next_ref[i] >= 0, jnp.minimum(fetched + PIECES_PER_BLOCK, WEIGHT_PIECES), fetched)
        start_pieces(next_ref[i], fetched, more)
        return more

    lax.fori_loop(0, nused, block, jnp.int32(0))

    @pl.when(nused >= 2)
    def _():
        for q in range(BLOCK_CHUNKS):
            scatter(nused - 2, q).wait()
    for q in range(BLOCK_CHUNKS):
        scatter(nused - 1, q).wait()


def _experts(chunk_src, chunk_dst, block_expert, block_first, block_next, block_quarters, n_used,
             sorted_rows, w1, b1p, w2, b2):
    whole3 = lambda i, *_: (0, 0, 0)
    grid_spec = pltpu.PrefetchScalarGridSpec(
        num_scalar_prefetch=7,
        grid=(1,),
        in_specs=[
            pl.BlockSpec(memory_space=pl.ANY),
            pl.BlockSpec(memory_space=pl.ANY),
            pl.BlockSpec(b1p.shape, whole3),
            pl.BlockSpec(memory_space=pl.ANY),
            pl.BlockSpec(b2.shape, whole3),
        ],
        out_specs=pl.BlockSpec(memory_space=pl.ANY),
        scratch_shapes=[
            pltpu.VMEM((D_MODEL, 2 * D_FF), F32),
            pltpu.VMEM((D_FF, D_MODEL), F32),
            pltpu.VMEM((D_MODEL, 2 * D_FF), BF16),
            pltpu.VMEM((D_FF, D_MODEL), BF16),
            pltpu.VMEM((GATHER_DEPTH, ROW_BLOCK, D_MODEL), BF16),
            pltpu.VMEM((2, ROW_BLOCK, D_MODEL), BF16),
            pltpu.VMEM((FREE_ROWS, D_MODEL), BF16),
            pltpu.SemaphoreType.DMA((2,)),
            pltpu.SemaphoreType.DMA((GATHER_DEPTH,)),
            pltpu.SemaphoreType.DMA((2,)),
            pltpu.SemaphoreType.DMA(()),
        ],
    )
    return pl.pallas_call(
        _expert_kernel,
        grid_spec=grid_spec,
        out_shape=jax.ShapeDtypeStruct((DUMP_BASE + DUMP_ROWS, D_MODEL), BF16),
        compiler_params=pltpu.CompilerParams(
            dimension_semantics=("arbitrary",), vmem_limit_bytes=VMEM_LIMIT),
        name="experts",
    )(chunk_src, chunk_dst, block_expert, block_first, block_next, block_quarters, n_used,
      sorted_rows, w1, b1p, w2, b2)


def _combine_kernel(slot_ref, gate_ref, xp_ref, xs_ref, pp_ref, ps_ref, eo_ref, ln2g_ref, ln2b_ref,
                    wpg_ref, wple_ref, yp_ref, ys_ref):
    i = pl.program_id(0)
    T = TOK_TILE

    slot_rows = [slot_ref[0, j:j + 1, :] for j in range(TOP_K)]
    gate_rows = [gate_ref[0, j:j + 1, :] for j in range(TOP_K)]
    pad = jnp.zeros((LANES - SUBLANES, T), F32)
    slots_t = jnp.concatenate([slot_ref[0].astype(F32), pad], axis=0).T
    slot_cols = [slots_t[:, j:j + 1].astype(I32) for j in range(TOP_K)]

    ff = jnp.zeros((T, D_MODEL), F32)
    for grp in range(GROUPS):
        m = _placement(slot_rows, grp)
        weighted = jnp.where(m[0], gate_rows[0], jnp.where(m[1], gate_rows[1], jnp.where(
            m[2], gate_rows[2], jnp.where(m[3], gate_rows[3], 0.0))))
        g_col = jnp.sum(weighted, axis=1, keepdims=True)
        z = (eo_ref[grp * T:(grp + 1) * T, :].astype(F32) * g_col).astype(BF16)
        r = grp * T + lax.broadcasted_iota(I32, (T, T), 1)
        hit = jnp.logical_or(jnp.logical_or(r == slot_cols[0], r == slot_cols[1]),
                             jnp.logical_or(r == slot_cols[2], r == slot_cols[3]))
        ff = ff + _dot(jnp.where(hit, 1.0, 0.0).astype(BF16), z)

    x1 = _pick_tile(i, xp_ref, xs_ref)
    x2 = _layer_norm(DN_ALPHA * x1 + ff, ln2g_ref[...], ln2b_ref[...])
    p = _pick_tile(i, pp_ref, ps_ref)
    y = x2 + jax.nn.sigmoid(_dot(x2.astype(BF16), wpg_ref[...])) * _dot(p.astype(BF16), wple_ref[...])

    @pl.when(i < N_PROMPT_TILES)
    def _():
        yp_ref[...] = y

    @pl.when(i == N_PROMPT_TILES)
    def _():
        ys_ref[...] = y[0:DEC_BATCH, :]


def _combine(slots, gates, x1p, x1s, pp, ps, expert_out, ln2g, ln2b, w_pg_b, w_ple_b):
    tile_idx = lambda i: (jnp.minimum(i, N_PROMPT_TILES - 1), 0)
    const2 = lambda i: (0, 0)
    return pl.pallas_call(
        _combine_kernel,
        grid=(N_TILES,),
        in_specs=[
            pl.BlockSpec((1, SUBLANES, TOK_TILE), lambda i: (i, 0, 0)),
            pl.BlockSpec((1, SUBLANES, TOK_TILE), lambda i: (i, 0, 0)),
            pl.BlockSpec((TOK_TILE, D_MODEL), tile_idx),
            pl.BlockSpec((TOK_TILE, D_MODEL), const2),
            pl.BlockSpec((TOK_TILE, PLE_DIM), tile_idx),
            pl.BlockSpec((TOK_TILE, PLE_DIM), const2),
            pl.BlockSpec((LOCAL_ROWS, D_MODEL), lambda i: (i, 0)),
            pl.BlockSpec(ln2g.shape, const2),
            pl.BlockSpec(ln2b.shape, const2),
            pl.BlockSpec(w_pg_b.shape, const2),
            pl.BlockSpec(w_ple_b.shape, const2),
        ],
        out_specs=[
            pl.BlockSpec((TOK_TILE, D_MODEL), tile_idx),
            pl.BlockSpec((DEC_BATCH, D_MODEL), const2),
        ],
        out_shape=[
            jax.ShapeDtypeStruct((N_PROMPT, D_MODEL), F32),
            jax.ShapeDtypeStruct((DEC_BATCH, D_MODEL), F32),
        ],
        compiler_params=pltpu.CompilerParams(
            dimension_semantics=("arbitrary",), vmem_limit_bytes=VMEM_LIMIT),
        name="combine",
    )(slots, gates, x1p, x1s, pp, ps, expert_out, ln2g, ln2b, w_pg_b, w_ple_b)


def _block_tables(nch):
    seg_start = (jnp.cumsum(nch, axis=1) - nch) * CHUNK_ROWS
    tot = jnp.sum(nch, axis=0)
    nblk = (tot + BLOCK_CHUNKS - 1) // BLOCK_CHUNKS
    blk_end = jnp.cumsum(nblk)
    blk_start = blk_end - nblk
    n_used = blk_end[-1:].astype(I32)
    blk_ids = jnp.arange(N_BLOCKS, dtype=I32)

    e_ids = jnp.arange(N_EXPERTS, dtype=I32)
    used = nblk > 0
    blk_clamped = jnp.minimum(blk_ids, n_used[0] - 1)
    block_expert = jnp.minimum(jnp.sum(blk_end[:, None] <= blk_clamped[None, :], axis=0), N_EXPERTS - 1).astype(I32)
    starts_here = jnp.logical_and(blk_start[:, None] == blk_ids[None, :], used[:, None])
    block_first = jnp.any(starts_here, axis=0).astype(I32)
    later_used = jnp.logical_and(e_ids[None, :] > e_ids[:, None], used[None, :])
    next_used = jnp.min(jnp.where(later_used, e_ids[None, :], N_EXPERTS), axis=1)
    next_used = jnp.where(next_used < N_EXPERTS, next_used, -1)
    owner = block_expert[None, :] == e_ids[:, None]
    block_next = jnp.sum(jnp.where(owner, next_used[:, None], 0), axis=0).astype(I32)
    real_chunks = jnp.sum(jnp.where(owner, tot[:, None] - (blk_ids[None, :] - blk_start[:, None]) * BLOCK_CHUNKS, 0),
                          axis=0)
    quarter_chunks = BLOCK_CHUNKS // BLOCK_QUARTERS
    block_quarters = jnp.clip((real_chunks + quarter_chunks - 1) // quarter_chunks, 1, BLOCK_QUARTERS).astype(I32)

    nch_t = nch.T
    seg_first = (blk_start[:, None] * BLOCK_CHUNKS + jnp.cumsum(nch_t, axis=1) - nch_t).reshape(-1)
    seg_count = nch_t.reshape(-1)
    seg_row = (jnp.arange(N_TILES, dtype=I32)[None, :] * LOCAL_ROWS + seg_start.T).reshape(-1)
    ent = jnp.arange(N_BLOCKS * BLOCK_CHUNKS, dtype=I32)
    d = ent[None, :] - seg_first[:, None]
    inside = jnp.logical_and(d >= 0, d < seg_count[:, None])
    row_plus_1 = jnp.sum(jnp.where(inside, seg_row[:, None] + d * CHUNK_ROWS + 1, 0), axis=0)
    real = row_plus_1 > 0
    row = row_plus_1 - 1
    dump = DUMP_BASE + (((ent // BLOCK_CHUNKS) % 2) * BLOCK_CHUNKS + ent % BLOCK_CHUNKS) * CHUNK_ROWS
    chunk_src = jnp.where(real, row, ZERO_CHUNK_ROW).astype(I32)
    chunk_dst = jnp.where(real, row, dump).astype(I32)
    return chunk_src, chunk_dst, block_expert, block_first, block_next, block_quarters, n_used


def kernel(x_prompt, x_sample, state_pool, state_mlstm_C, state_mlstm_n, state_mlstm_m, p_prompt, p_sample, w_in, b_i, b_f, w_pool, pool_scale, mh_g, w_out, ln1_g, ln1_b, w_router, b_router, w_mlp1, b_mlp1, w_mlp2, b_mlp2, ln2_g, ln2_b, w_ple, w_ple_gate):
    n_main = POOL_WIDTH + 4 * MLSTM_WIDTH
    w_in_b = w_in[0, :, 0:n_main].astype(BF16)
    w_g_b = jnp.pad(w_in[0, :, n_main:], ((0, 0), (0, LANES - 2 * HEADS))).astype(BF16)
    gbias = jnp.pad(jnp.concatenate([b_i[0], b_f[0]]), (0, LANES - 2 * HEADS)).reshape(1, LANES)
    w_pool_b = w_pool[0].astype(BF16)
    pscale = pool_scale[0].reshape(1, POOL_WIDTH)
    mhg = mh_g[0].reshape(1, MLSTM_WIDTH)
    w_out_b = w_out[0].astype(BF16)
    ln1g = ln1_g[0].reshape(1, D_MODEL)
    ln1b = ln1_b[0].reshape(1, D_MODEL)
    ln2g = ln2_g[0].reshape(1, D_MODEL)
    ln2b = ln2_b[0].reshape(1, D_MODEL)
    w_router_t = w_router[0].T
    b_router_col = jnp.broadcast_to(b_router[0].reshape(N_EXPERTS, 1), (N_EXPERTS, LANES))
    b1 = b_mlp1[0]
    b1p = jnp.concatenate([b1[:, 0::2], b1[:, 1::2]], axis=-1).reshape(N_EXPERTS, 1, 2 * D_FF)
    b2 = b_mlp2[0].reshape(N_EXPERTS, 1, D_MODEL)
    w_pg_b = w_ple_gate[0].astype(BF16)
    w_ple_b = w_ple[0].astype(BF16)

    x1p, pool_p, c_p, n_p, m_p = _prompt_mixer(
        x_prompt, w_in_b, w_g_b, gbias, w_pool_b, pscale, mhg, w_out_b, ln1g, ln1b)
    x1s, pool_s, c_s, n_s, m_s = _sample_mixer(
        x_sample.reshape(DEC_BATCH, D_MODEL),
        state_pool[0].reshape(DEC_BATCH, POOL_HIST * POOL_WIDTH),
        state_mlstm_C[0], state_mlstm_n[0].reshape(DEC_BATCH, MLSTM_WIDTH), state_mlstm_m[0],
        w_in_b, w_g_b, gbias, w_pool_b, pscale, mhg, w_out_b, ln1g, ln1b)

    slots, gates, nch, sorted_rows = _route(x1p, x1s, w_router_t, b_router_col)
    tables = _block_tables(nch[:, :, 0].astype(I32))
    expert_out = _experts(*tables, sorted_rows, w_mlp1[0], b1p, w_mlp2[0], b2)

    pp = p_prompt[0].reshape(N_PROMPT, PLE_DIM)
    ps = jnp.pad(p_sample[0].reshape(DEC_BATCH, PLE_DIM), ((0, TOK_TILE - DEC_BATCH), (0, 0)))
    yp, ys = _combine(slots, gates, x1p, x1s, pp, ps, expert_out, ln2g, ln2b, w_pg_b, w_ple_b)

    return (
        yp.reshape(BATCH, SEQ, D_MODEL),
        ys.reshape(DEC_BATCH, 1, D_MODEL),
        pool_p.reshape(1, BATCH, POOL_HIST, POOL_WIDTH),
        c_p.reshape(1, BATCH, HEADS, HEAD_DIM, HEAD_DIM),
        n_p.reshape(1, BATCH, HEADS, HEAD_DIM),
        m_p[:, 0:HEADS, 0].reshape(1, BATCH, HEADS),
        pool_s.reshape(1, DEC_BATCH, POOL_HIST, POOL_WIDTH),
        c_s.reshape(1, DEC_BATCH, HEADS, HEAD_DIM, HEAD_DIM),
        n_s.reshape(1, DEC_BATCH, HEADS, HEAD_DIM),
        m_s[:, HEADS:2 * HEADS].reshape(1, DEC_BATCH, HEADS),
    )
```

```python
import jax
import jax.numpy as jnp
from jax import lax
from jax.experimental import pallas as pl
from jax.experimental.pallas import tpu as pltpu

F32 = jnp.float32
BF16 = jnp.bfloat16
I32 = jnp.int32

D_MODEL = 1024
BATCH = 8
SEQ = 2048
DEC_BATCH = 128
PAST_LEN = 16384
POOL_WIDTH = 512
POOL_GROUP_DIM = 128
POOL_WINDOWS = (2, 4, 8, 16)
POOL_HIST = 15
MLSTM_WIDTH = 512
HEADS = 4
HEAD_DIM = 128
CHUNK = 128
N_EXPERTS = 32
TOP_K = 4
D_FF = 1024
SWIGLU_ALPHA = 1.702
SWIGLU_LIMIT = 7.0
PLE_DIM = 256
DN_ALPHA = 2.0 ** 0.25
LN_EPS = 1e-5

LANES = 128
SUBLANES = 8
BF16_ROWS = 16
MXU_COLS = 256
VMEM_LIMIT = 56 * 1024 * 1024

MIX_TILE = 256
MIX_SEQS = 2
HIST_PAD = 16
TOK_TILE = 512
N_PROMPT = BATCH * SEQ
N_PROMPT_TILES = N_PROMPT // TOK_TILE
N_TILES = N_PROMPT_TILES + 1
SAMPLE_BT = 16

CHUNK_ROWS = BF16_ROWS
LOCAL_ROWS = TOK_TILE * TOP_K + N_EXPERTS * CHUNK_ROWS
GROUPS = LOCAL_ROWS // TOK_TILE
ROW_BLOCK = 1024
BLOCK_PARTS = 8
BLOCK_CHUNKS = ROW_BLOCK // CHUNK_ROWS
MAX_CHUNKS = N_TILES * (TOK_TILE * TOP_K // CHUNK_ROWS + N_EXPERTS)
N_BLOCKS = -(-MAX_CHUNKS // BLOCK_CHUNKS) + N_EXPERTS
ZERO_CHUNK_ROW = LOCAL_ROWS - CHUNK_ROWS
FREE_ROWS = LOCAL_ROWS - TOK_TILE * TOP_K
DUMP_BASE = N_TILES * LOCAL_ROWS
DUMP_ROWS = 2 * ROW_BLOCK
assert DUMP_ROWS % FREE_ROWS == 0
WEIGHT_PIECES = 8
PIECES_PER_BLOCK = 8
GATHER_DEPTH = 3


def _dot(a, b):
    return jnp.dot(a, b, preferred_element_type=F32)


def _dot_nt(a, b):
    return lax.dot_general(a, b, (((1,), (1,)), ((), ())), preferred_element_type=F32)


def _dot_tn(a, b):
    return lax.dot_general(a, b, (((0,), (0,)), ((), ())), preferred_element_type=F32)


def _split3(a):
    a0 = a.astype(BF16)
    r1 = a - a0.astype(F32)
    a1 = r1.astype(BF16)
    r2 = r1 - a1.astype(F32)
    return a0, a1, r2.astype(BF16)


def _log_sigmoid(x):
    return jnp.minimum(x, 0.0) - jnp.log1p(jnp.exp(-jnp.abs(x)))


def _layer_norm(x, g, b):
    mu = jnp.mean(x, axis=-1, keepdims=True)
    xc = x - mu
    var = jnp.mean(xc * xc, axis=-1, keepdims=True)
    return xc * lax.rsqrt(var + LN_EPS) * g + b


def _gate_values(g, gbias):
    lane = lax.broadcasted_iota(I32, g.shape, 1)
    z = g + gbias
    return jnp.where(lane < HEADS, z, _log_sigmoid(z))


def _head_out(hh, o_h, gain):
    mu = jnp.mean(hh, axis=-1, keepdims=True)
    hc = hh - mu
    var = jnp.mean(hc * hc, axis=-1, keepdims=True)
    return jax.nn.sigmoid(o_h) * (hc * lax.rsqrt(var + LN_EPS) * gain)


def _prompt_mixer_kernel(x_ref, xn_ref, win_ref, wg_ref, gb_ref, wpool_ref, pscale_ref, mhg_ref, wout_ref,
                         ln1g_ref, ln1b_ref,
                         x1_ref, pool_ref, c_out_ref, n_out_ref, m_out_ref,
                         ubuf, mixbuf, pbuf, gbuf, c_s, n_s, m_s):
    ti = pl.program_id(1)
    nt = pl.num_programs(1)
    TT = MIX_TILE
    S = MIX_SEQS

    @pl.when(ti == 0)
    def _():
        for s in range(S):
            ubuf[s, 0:HIST_PAD, :] = jnp.zeros((HIST_PAD, POOL_WIDTH), F32)
        c_s[...] = jnp.zeros_like(c_s)
        n_s[...] = jnp.zeros_like(n_s)
        m_s[...] = jnp.zeros_like(m_s)

    x = jnp.concatenate([x_ref[s, 0] for s in range(S)], axis=0)
    step = pl.program_id(0) * nt + ti
    cur = lax.rem(step, 2)
    nxt = 1 - cur
    n_main = POOL_WIDTH + 4 * MLSTM_WIDTH

    @pl.when(step == 0)
    def _():
        xb0 = x.astype(BF16)
        pbuf[0] = _dot(xb0, win_ref[...])
        gbuf[0] = _dot(xb0, wg_ref[...])

    xnb = jnp.concatenate([xn_ref[s, 0] for s in range(S)], axis=0).astype(BF16)

    def slab(j):
        def run():
            pbuf[nxt, :, j * MXU_COLS:(j + 1) * MXU_COLS] = _dot(xnb, win_ref[:, j * MXU_COLS:(j + 1) * MXU_COLS])
        return run

    def gate_slab():
        gbuf[nxt] = _dot(xnb, wg_ref[...])

    pending = [slab(j) for j in range(n_main // MXU_COLS)] + [gate_slab]

    def ahead(n=1):
        for _ in range(n):
            if pending:
                pending.pop(0)()

    proj = pbuf.at[cur]
    g = gbuf[cur]

    L = CHUNK
    row = lax.broadcasted_iota(I32, (L, L), 0)
    col = lax.broadcasted_iota(I32, (L, L), 1)
    causal = row >= col
    tril = jnp.where(causal, 1.0, 0.0).astype(BF16)
    pos = ti * TT + lax.broadcasted_iota(I32, (TT, 1), 0)

    for s in range(S):
        base = s * TT
        u = proj[base:base + TT, 0:POOL_WIDTH]

        ubuf[s, HIST_PAD:HIST_PAD + TT, :] = u
        for gi, w in enumerate(POOL_WINDOWS):
            sl = slice(gi * POOL_GROUP_DIM, (gi + 1) * POOL_GROUP_DIM)
            ug = u[:, sl]
            acc = ug
            for i in range(1, w):
                acc = acc + ubuf[s, HIST_PAD - i:HIST_PAD - i + TT, sl]
            cnt = jnp.minimum(pos + 1, w).astype(F32)
            z = acc / cnt - ug
            mixbuf[base:base + TT, sl] = _dot(z.astype(BF16), wpool_ref[gi]) * pscale_ref[:, sl]

        @pl.when(ti == nt - 1)
        def _():
            pool_ref[s, 0] = ubuf[s, TT + 1:TT + HIST_PAD, :]

        ubuf[s, 0:HIST_PAD, :] = ubuf[s, TT:TT + HIST_PAD, :]

    NC = TT // L
    chains = [(s, h) for s in range(S) for h in range(HEADS)]
    units = [(s, c, h) for c in range(NC) for s in range(S) for h in range(HEADS)]
    U = range(len(units))

    def rows(s, c):
        return slice(s * TT + c * L, s * TT + (c + 1) * L)

    def head_cols(part, h):
        return slice(part * POOL_WIDTH + h * HEAD_DIM, part * POOL_WIDTH + (h + 1) * HEAD_DIM)

    gate, cum, gate_t, cum_t = {}, {}, {}, {}
    for c in range(NC):
        for s in range(S):
            val = _gate_values(g[rows(s, c), :], gb_ref[...])
            v0, v1, v2 = _split3(val)
            gate[s, c] = val
            cum[s, c] = _dot(tril, v0) + _dot(tril, v1) + _dot(tril, v2)
    for key in gate:
        gate_t[key] = gate[key].T
        cum_t[key] = cum[key].T
    ahead()
    qf = [proj[rows(s, c), head_cols(1, h)] for s, c, h in units]
    kf = [proj[rows(s, c), head_cols(2, h)] * (HEAD_DIM ** -0.5) for s, c, h in units]
    vf = [proj[rows(s, c), head_cols(3, h)] for s, c, h in units]
    qb = [a.astype(BF16) for a in qf]
    kb = [a.astype(BF16) for a in kf]
    f_col = [cum[s, c][:, HEADS + h:HEADS + h + 1] for s, c, h in units]
    ahead()
    log_d = [jnp.where(causal, f_col[u] - cum_t[s, c][HEADS + h:HEADS + h + 1, :] + gate_t[s, c][h:h + 1, :],
                       -jnp.inf) for u, (s, c, h) in enumerate(units)]
    ahead()
    row_max = [jnp.max(log_d[u], axis=-1, keepdims=True) for u in U]
    ahead()
    qk_raw = [_dot_nt(qb[u], kb[u]) for u in U]

    m_prev, m_t, inter = [None] * len(units), [None] * len(units), [None] * len(units)
    m_run = {(s, h): m_s[s, h:h + 1, 0:1] for s, h in chains}
    for u, (s, c, h) in enumerate(units):
        m_prev[u] = m_run[s, h]
        inter[u] = m_prev[u] + f_col[u]
        m_t[u] = jnp.maximum(inter[u], row_max[u])
        m_run[s, h] = m_t[u][L - 1:L, :]
    m_new = [m_t[u][L - 1:L, :] for u in U]

    ahead()
    dw = [jnp.exp(log_d[u] - m_t[u]) for u in U]
    sc = [jnp.exp(inter[u] - m_t[u]) for u in U]
    ahead()
    qk = [qk_raw[u] * dw[u] for u in U]
    ahead()
    intra = [_dot(qk[u].astype(BF16), vf[u].astype(BF16)) for u in U]
    ahead()
    row_sum = [jnp.sum(qk[u], axis=-1, keepdims=True) for u in U]
    floor = [jnp.exp(-m_t[u]) for u in U]
    f_last = [f_col[u][L - 1:L, :] for u in U]
    ahead()
    wk = [jnp.exp(gate[s, c][:, h:h + 1] + f_last[u] - f_col[u] - m_new[u]) for u, (s, c, h) in enumerate(units)]
    decay = [jnp.exp(m_prev[u] + f_last[u] - m_new[u]) for u in U]
    ahead()
    upd = [_dot_tn((vf[u] * wk[u]).astype(BF16), kb[u]) for u in U]
    ahead()
    n_upd = [jnp.sum(wk[u] * kf[u], axis=0, keepdims=True) for u in U]

    c_run = {(s, h): c_s[s, h] for s, h in chains}
    n_run = {(s, h): n_s[s, h:h + 1, :] for s, h in chains}
    hh = [None] * len(units)
    for c in range(NC):
        cu = [u for u in U if units[u][1] == c]
        inter_term = {u: _dot_nt(qb[u], c_run[units[u][0], units[u][2]].astype(BF16)) for u in cu}
        n_term = {u: jnp.sum(qf[u] * n_run[units[u][0], units[u][2]], axis=-1, keepdims=True) for u in cu}
        for u in cu:
            s, _, h = units[u]
            num = intra[u] + sc[u] * inter_term[u]
            den = row_sum[u] + sc[u] * n_term[u]
            hh[u] = num / jnp.maximum(jnp.abs(den), floor[u])
            c_run[s, h] = decay[u] * c_run[s, h] + upd[u]
            n_run[s, h] = decay[u] * n_run[s, h] + n_upd[u]
    ahead(len(pending))
    for s, h in chains:
        c_s[s, h] = c_run[s, h]
        n_s[s, h:h + 1, :] = n_run[s, h]
        m_s[s, h:h + 1, :] = jnp.broadcast_to(m_run[s, h], (1, LANES))
    for u, (s, c, h) in enumerate(units):
        mixbuf[rows(s, c), head_cols(1, h)] = _head_out(
            hh[u], proj[rows(s, c), head_cols(4, h)], mhg_ref[:, h * HEAD_DIM:(h + 1) * HEAD_DIM])

    @pl.when(ti == nt - 1)
    def _():
        for s in range(S):
            c_out_ref[s, 0] = c_s[s]
            n_out_ref[s, 0] = n_s[s, 0:HEADS, :]
            m_out_ref[s, 0] = m_s[s]

    mix = _dot(mixbuf[...].astype(BF16), wout_ref[...])
    x1 = _layer_norm(DN_ALPHA * x + mix, ln1g_ref[...], ln1b_ref[...])
    for s in range(S):
        x1_ref[s] = x1[s * TT:(s + 1) * TT, :]


def _prompt_mixer(x, w_in_b, w_g_b, gbias, w_pool_b, pscale, mhg, w_out_b, ln1g, ln1b):
    nt = SEQ // MIX_TILE
    S = MIX_SEQS
    G = BATCH // S
    const2 = lambda b, t: (0, 0)
    const3 = lambda b, t: (0, 0, 0)

    def next_tile(b, t):
        nxt = jnp.minimum(b * nt + t + 1, G * nt - 1)
        return (0, nxt // nt, nxt % nt, 0)

    outs = pl.pallas_call(
        _prompt_mixer_kernel,
        grid=(G, nt),
        in_specs=[
            pl.BlockSpec((S, 1, MIX_TILE, D_MODEL), lambda b, t: (0, b, t, 0)),
            pl.BlockSpec((S, 1, MIX_TILE, D_MODEL), next_tile),
            pl.BlockSpec(w_in_b.shape, const2),
            pl.BlockSpec(w_g_b.shape, const2),
            pl.BlockSpec(gbias.shape, const2),
            pl.BlockSpec(w_pool_b.shape, const3),
            pl.BlockSpec(pscale.shape, const2),
            pl.BlockSpec(mhg.shape, const2),
            pl.BlockSpec(w_out_b.shape, const2),
            pl.BlockSpec(ln1g.shape, const2),
            pl.BlockSpec(ln1b.shape, const2),
        ],
        out_specs=[
            pl.BlockSpec((S, MIX_TILE, D_MODEL), lambda b, t: (0, b * nt + t, 0)),
            pl.BlockSpec((S, 1, POOL_HIST, POOL_WIDTH), lambda b, t: (0, b, 0, 0)),
            pl.BlockSpec((S, 1, HEADS, HEAD_DIM, HEAD_DIM), lambda b, t: (0, b, 0, 0, 0)),
            pl.BlockSpec((S, 1, HEADS, HEAD_DIM), lambda b, t: (0, b, 0, 0)),
            pl.BlockSpec((S, 1, SUBLANES, LANES), lambda b, t: (0, b, 0, 0)),
        ],
        out_shape=[
            jax.ShapeDtypeStruct((S, G * SEQ, D_MODEL), F32),
            jax.ShapeDtypeStruct((S, G, POOL_HIST, POOL_WIDTH), F32),
            jax.ShapeDtypeStruct((S, G, HEADS, HEAD_DIM, HEAD_DIM), F32),
            jax.ShapeDtypeStruct((S, G, HEADS, HEAD_DIM), F32),
            jax.ShapeDtypeStruct((S, G, SUBLANES, LANES), F32),
        ],
        scratch_shapes=[
            pltpu.VMEM((S, HIST_PAD + MIX_TILE, POOL_WIDTH), F32),
            pltpu.VMEM((S * MIX_TILE, D_MODEL), F32),
            pltpu.VMEM((2, S * MIX_TILE, POOL_WIDTH + 4 * MLSTM_WIDTH), F32),
            pltpu.VMEM((2, S * MIX_TILE, LANES), F32),
            pltpu.VMEM((S, HEADS, HEAD_DIM, HEAD_DIM), F32),
            pltpu.VMEM((S, SUBLANES, HEAD_DIM), F32),
            pltpu.VMEM((S, SUBLANES, LANES), F32),
        ],
        compiler_params=pltpu.CompilerParams(
            dimension_semantics=("arbitrary", "arbitrary"), vmem_limit_bytes=VMEM_LIMIT),
        name="prompt_mixer",
    )(x.reshape(S, G, SEQ, D_MODEL), x.reshape(S, G, SEQ, D_MODEL), w_in_b, w_g_b, gbias, w_pool_b, pscale, mhg, w_out_b, ln1g, ln1b)
    x1, pool, c, n, m = outs
    return (x1.reshape(N_PROMPT, D_MODEL), pool.reshape(BATCH, POOL_HIST, POOL_WIDTH),
            c.reshape(BATCH, HEADS, HEAD_DIM, HEAD_DIM), n.reshape(BATCH, HEADS, HEAD_DIM),
            m.reshape(BATCH, SUBLANES, LANES))


def _sample_mixer_kernel(x_ref, hist_ref, c_ref, n_ref, m_ref, win_ref, wg_ref, gb_ref, wpool_ref,
                         pscale_ref, mhg_ref, wout_ref, ln1g_ref, ln1b_ref,
                         x1_ref, pool_out_ref, c_out_ref, n_out_ref, m_out_ref,
                         q_s, k_s, vw_s, v_s, o_s, mixbuf, h_s, coef_s):
    i = pl.program_id(0)
    nsteps = pl.num_programs(0)
    B = DEC_BATCH

    @pl.when(i == 0)
    def _():
        x = x_ref[...]
        xb = x.astype(BF16)
        proj = _dot(xb, win_ref[...])
        g = _dot(xb, wg_ref[...])
        u = proj[:, 0:POOL_WIDTH]
        for gi, w in enumerate(POOL_WINDOWS):
            sl = slice(gi * POOL_GROUP_DIM, (gi + 1) * POOL_GROUP_DIM)
            ug = u[:, sl]
            s = ug
            for j in range(1, w):
                r = POOL_HIST - j
                s = s + hist_ref[:, r * POOL_WIDTH + gi * POOL_GROUP_DIM:r * POOL_WIDTH + (gi + 1) * POOL_GROUP_DIM]
            cnt = float(min(PAST_LEN + 1, w))
            z = s / cnt - ug
            mixbuf[:, sl] = _dot(z.astype(BF16), wpool_ref[gi]) * pscale_ref[:, sl]
        pool_out_ref[:, 0:(POOL_HIST - 1) * POOL_WIDTH] = hist_ref[:, POOL_WIDTH:POOL_HIST * POOL_WIDTH]
        pool_out_ref[:, (POOL_HIST - 1) * POOL_WIDTH:POOL_HIST * POOL_WIDTH] = u

        val = _gate_values(g, gb_ref[...])
        lane = lax.broadcasted_iota(I32, (B, LANES), 1)
        qk_all = jnp.zeros((B, LANES), F32)
        sc_all = jnp.zeros((B, LANES), F32)
        den_all = jnp.zeros((B, LANES), F32)
        floor_all = jnp.zeros((B, LANES), F32)
        m_all = jnp.zeros((B, LANES), F32)
        for h in range(HEADS):
            hs = slice(h * HEAD_DIM, (h + 1) * HEAD_DIM)
            qf = proj[:, POOL_WIDTH + h * HEAD_DIM:POOL_WIDTH + (h + 1) * HEAD_DIM]
            kf = proj[:, 2 * POOL_WIDTH + h * HEAD_DIM:2 * POOL_WIDTH + (h + 1) * HEAD_DIM] * (HEAD_DIM ** -0.5)
            vf = proj[:, 3 * POOL_WIDTH + h * HEAD_DIM:3 * POOL_WIDTH + (h + 1) * HEAD_DIM]
            ig = val[:, h:h + 1]
            lf = val[:, HEADS + h:HEADS + h + 1]
            m0 = m_ref[:, h:h + 1]
            n0 = n_ref[:, hs]
            inter = m0 + lf
            m_t = jnp.maximum(inter, ig)
            dw = jnp.exp(ig - m_t)
            sc = jnp.exp(inter - m_t)
            qk = jnp.sum(qf * kf, axis=-1, keepdims=True) * dw
            den = qk + sc * jnp.sum(qf * n0, axis=-1, keepdims=True)
            n_out_ref[:, hs] = sc * n0 + dw * kf
            q_s[0:B, hs] = qf
            k_s[0:B, hs] = kf
            v_s[0:B, hs] = vf
            vw_s[0:B, hs] = vf * dw
            sel = lane == h
            qk_all = jnp.where(sel, qk, qk_all)
            sc_all = jnp.where(sel, sc, sc_all)
            den_all = jnp.where(sel, den, den_all)
            floor_all = jnp.where(sel, jnp.exp(-m_t), floor_all)
            m_all = jnp.where(lane == HEADS + h, m_t, m_all)
        o_s[...] = proj[:, 4 * POOL_WIDTH:5 * POOL_WIDTH]
        coef_s[0] = qk_all
        coef_s[1] = sc_all
        coef_s[2] = den_all
        coef_s[3] = floor_all
        m_out_ref[...] = m_all

    rows = pl.ds(pl.multiple_of(i * SAMPLE_BT, SAMPLE_BT), SAMPLE_BT)
    q_t, k_t, v_t, vw_t = q_s[rows, :], k_s[rows, :], v_s[rows, :], vw_s[rows, :]
    qk_t, sc_t, den_t, floor_t = coef_s[0, rows, :], coef_s[1, rows, :], coef_s[2, rows, :], coef_s[3, rows, :]
    h_rows = []
    for bl in range(SAMPLE_BT):
        heads = []
        for h in range(HEADS):
            hs = slice(h * HEAD_DIM, (h + 1) * HEAD_DIM)
            c_prev = c_ref[bl, h]
            q8 = jnp.broadcast_to(q_t[bl:bl + 1, hs], (SUBLANES, HEAD_DIM))
            cq = _dot_nt(q8.astype(BF16), c_prev.astype(BF16))[0:1, :]
            qk = qk_t[bl:bl + 1, h:h + 1]
            sc = sc_t[bl:bl + 1, h:h + 1]
            num = qk * v_t[bl:bl + 1, hs] + sc * cq
            heads.append(num / jnp.maximum(jnp.abs(den_t[bl:bl + 1, h:h + 1]), floor_t[bl:bl + 1, h:h + 1]))
            v_col = jnp.broadcast_to(vw_t[bl:bl + 1, hs], (HEAD_DIM, HEAD_DIM)).T
            c_out_ref[bl, h] = sc * c_prev + v_col * k_t[bl:bl + 1, hs]
        h_rows.append(jnp.concatenate(heads, axis=1))
    h_s[rows, :] = jnp.concatenate(h_rows, axis=0)

    @pl.when(i == nsteps - 1)
    def _():
        for h in range(HEADS):
            hs = slice(h * HEAD_DIM, (h + 1) * HEAD_DIM)
            mixbuf[:, POOL_WIDTH + h * HEAD_DIM:POOL_WIDTH + (h + 1) * HEAD_DIM] = _head_out(
                h_s[:, hs], o_s[:, hs], mhg_ref[:, hs])
        mix = _dot(mixbuf[...].astype(BF16), wout_ref[...])
        x1 = _layer_norm(DN_ALPHA * x_ref[...] + mix, ln1g_ref[...], ln1b_ref[...])
        x1_ref[0:B, :] = x1
        x1_ref[B:TOK_TILE, :] = jnp.zeros((TOK_TILE - B, D_MODEL), F32)


def _sample_mixer(x, hist2, c0, n0, m0, w_in_b, w_g_b, gbias, w_pool_b, pscale, mhg, w_out_b, ln1g, ln1b):
    B = DEC_BATCH
    steps = B // SAMPLE_BT
    full = lambda a: pl.BlockSpec(a.shape, lambda i: (0,) * a.ndim)
    c_spec = pl.BlockSpec((SAMPLE_BT, HEADS, HEAD_DIM, HEAD_DIM), lambda i: (i, 0, 0, 0))
    return pl.pallas_call(
        _sample_mixer_kernel,
        grid=(steps,),
        in_specs=[full(x), full(hist2), c_spec, full(n0), full(m0), full(w_in_b), full(w_g_b), full(gbias),
                  full(w_pool_b), full(pscale), full(mhg), full(w_out_b), full(ln1g), full(ln1b)],
        out_specs=[
            pl.BlockSpec((TOK_TILE, D_MODEL), lambda i: (0, 0)),
            pl.BlockSpec((B, POOL_HIST * POOL_WIDTH), lambda i: (0, 0)),
            c_spec,
            pl.BlockSpec((B, MLSTM_WIDTH), lambda i: (0, 0)),
            pl.BlockSpec((B, LANES), lambda i: (0, 0)),
        ],
        out_shape=[
            jax.ShapeDtypeStruct((TOK_TILE, D_MODEL), F32),
            jax.ShapeDtypeStruct((B, POOL_HIST * POOL_WIDTH), F32),
            jax.ShapeDtypeStruct((B, HEADS, HEAD_DIM, HEAD_DIM), F32),
            jax.ShapeDtypeStruct((B, MLSTM_WIDTH), F32),
            jax.ShapeDtypeStruct((B, LANES), F32),
        ],
        scratch_shapes=[
            pltpu.VMEM((B, MLSTM_WIDTH), F32),
            pltpu.VMEM((B, MLSTM_WIDTH), F32),
            pltpu.VMEM((B, MLSTM_WIDTH), F32),
            pltpu.VMEM((B, MLSTM_WIDTH), F32),
            pltpu.VMEM((B, MLSTM_WIDTH), F32),
            pltpu.VMEM((B, D_MODEL), F32),
            pltpu.VMEM((B, MLSTM_WIDTH), F32),
            pltpu.VMEM((4, B, LANES), F32),
        ],
        compiler_params=pltpu.CompilerParams(
            dimension_semantics=("arbitrary",), vmem_limit_bytes=VMEM_LIMIT),
        name="sample_mixer",
    )(x, hist2, c0, n0, m0, w_in_b, w_g_b, gbias, w_pool_b, pscale, mhg, w_out_b, ln1g, ln1b)


def _pick_tile(i, prompt_ref, sample_ref):
    return jnp.where(i < N_PROMPT_TILES, prompt_ref[...], sample_ref[...])


def _placement(slot_rows, group):
    r = group * TOK_TILE + lax.broadcasted_iota(I32, (TOK_TILE, TOK_TILE), 0)
    return [r == s for s in slot_rows]


def _route_kernel(xp_ref, xs_ref, wrt_ref, br_ref, slot_ref, gate_ref, nch_ref, sorted_ref):
    i = pl.program_id(0)
    T = TOK_TILE
    E = N_EXPERTS

    x = _pick_tile(i, xp_ref, xs_ref)
    xh = x.astype(BF16)
    xl = (x - xh.astype(F32)).astype(BF16)
    w = wrt_ref[...]
    wh = w.astype(BF16)
    wl = (w - wh.astype(F32)).astype(BF16)
    logits = _dot_nt(wh, xh) + (_dot_nt(wh, xl) + _dot_nt(wl, xh)) + br_ref[:, 0:1]

    erow = lax.broadcasted_iota(I32, (E, T), 0).astype(F32)
    work = logits
    vals, sels = [], []
    for _ in range(TOP_K):
        mx = jnp.max(work, axis=0, keepdims=True)
        idx = jnp.min(jnp.where(work == mx, erow, float(E)), axis=0, keepdims=True)
        sel = erow == idx
        work = jnp.where(sel, -jnp.inf, work)
        vals.append(mx)
        sels.append(sel)
    chosen = jnp.logical_or(jnp.logical_or(sels[0], sels[1]), jnp.logical_or(sels[2], sels[3]))
    es = [jnp.exp(v - vals[0]) for v in vals]
    tot = es[0] + es[1] + es[2] + es[3]

    n_valid = jnp.where(i < N_PROMPT_TILES, T, DEC_BATCH)
    valid = lax.broadcasted_iota(I32, (1, T), 1) < n_valid
    onehot = jnp.where(jnp.logical_and(chosen, valid), 1.0, 0.0)
    trow = lax.broadcasted_iota(I32, (T, T), 0)
    tcol = lax.broadcasted_iota(I32, (T, T), 1)
    before = jnp.where(trow < tcol, 1.0, 0.0).astype(BF16)
    rank = _dot(onehot.astype(BF16), before)
    cnt = jnp.sum(onehot, axis=1, keepdims=True)
    nch = jnp.floor((cnt + (CHUNK_ROWS - 1)) * (1.0 / CHUNK_ROWS))
    lower = jnp.where(lax.broadcasted_iota(I32, (E, E), 0) > lax.broadcasted_iota(I32, (E, E), 1), 1.0, 0.0)
    nch_b = jnp.broadcast_to(nch, (E, LANES))
    seg_start = _dot(lower.astype(BF16), nch_b.astype(BF16))[:, 0:1] * CHUNK_ROWS
    base = seg_start + rank

    r8 = lax.broadcasted_iota(I32, (SUBLANES, T), 0)
    s_out = jnp.zeros((SUBLANES, T), I32)
    g_out = jnp.zeros((SUBLANES, T), F32)
    slot_rows = []
    for j in range(TOP_K):
        slot_j = jnp.sum(jnp.where(sels[j], base, 0.0), axis=0, keepdims=True).astype(I32)
        slot_j = jnp.where(valid, slot_j, -1)
        slot_rows.append(slot_j)
        s_out = jnp.where(r8 == j, slot_j, s_out)
        g_out = jnp.where(r8 == j, es[j] / tot, g_out)
    slot_ref[0] = s_out
    gate_ref[0] = g_out
    nch_ref[0] = nch_b

    for grp in range(GROUPS):
        m = _placement(slot_rows, grp)
        hit = jnp.logical_or(jnp.logical_or(m[0], m[1]), jnp.logical_or(m[2], m[3]))
        place = jnp.where(hit, 1.0, 0.0).astype(BF16)
        sorted_ref[grp * T:(grp + 1) * T, :] = _dot(place, xh).astype(BF16)


def _route(x1p, x1s, w_router_t, b_router_col):
    tile_spec = pl.BlockSpec((1, SUBLANES, TOK_TILE), lambda i: (i, 0, 0))
    return pl.pallas_call(
        _route_kernel,
        grid=(N_TILES,),
        in_specs=[
            pl.BlockSpec((TOK_TILE, D_MODEL), lambda i: (jnp.minimum(i, N_PROMPT_TILES - 1), 0)),
            pl.BlockSpec((TOK_TILE, D_MODEL), lambda i: (0, 0)),
            pl.BlockSpec(w_router_t.shape, lambda i: (0, 0)),
            pl.BlockSpec(b_router_col.shape, lambda i: (0, 0)),
        ],
        out_specs=[tile_spec, tile_spec,
                   pl.BlockSpec((1, N_EXPERTS, LANES), lambda i: (i, 0, 0)),
                   pl.BlockSpec((LOCAL_ROWS, D_MODEL), lambda i: (i, 0))],
        out_shape=[
            jax.ShapeDtypeStruct((N_TILES, SUBLANES, TOK_TILE), I32),
            jax.ShapeDtypeStruct((N_TILES, SUBLANES, TOK_TILE), F32),
            jax.ShapeDtypeStruct((N_TILES, N_EXPERTS, LANES), F32),
            jax.ShapeDtypeStruct((N_TILES * LOCAL_ROWS, D_MODEL), BF16),
        ],
        compiler_params=pltpu.CompilerParams(
            dimension_semantics=("arbitrary",), vmem_limit_bytes=VMEM_LIMIT),
        name="route",
    )(x1p, x1s, w_router_t, b_router_col)


def _expert_kernel(src_ref, dst_ref, bexp_ref, first_ref, next_ref, parts_ref, nused_ref,
                   sorted_hbm, w1_hbm, b1_ref, w2_hbm, b2_ref, out_hbm,
                   w1_stage, w2_stage, w1_b, w2_b, xbuf, obuf, zbuf, wsem, gsem, ssem, zsem):
    nused = nused_ref[0]
    obuf[...] = jnp.zeros_like(obuf)

    def fetch_piece(e, p):
        r1 = pl.ds(pl.multiple_of(p * (D_MODEL // WEIGHT_PIECES), SUBLANES), D_MODEL // WEIGHT_PIECES)
        r2 = pl.ds(pl.multiple_of(p * (D_FF // WEIGHT_PIECES), SUBLANES), D_FF // WEIGHT_PIECES)
        return (pltpu.make_async_copy(w1_hbm.at[e, r1, :], w1_stage.at[r1, :], wsem.at[0]),
                pltpu.make_async_copy(w2_hbm.at[e, r2, :], w2_stage.at[r2, :], wsem.at[1]))

    def start_pieces(e, lo, hi):
        def body(p, c):
            for cp in fetch_piece(e, p):
                cp.start()
            return c
        lax.fori_loop(lo, hi, body, 0)

    def gather(b, q):
        slot = lax.rem(b, GATHER_DEPTH)
        row = pl.multiple_of(src_ref[b * BLOCK_CHUNKS + q], CHUNK_ROWS)
        return pltpu.make_async_copy(sorted_hbm.at[pl.ds(row, CHUNK_ROWS), :],
                                     xbuf.at[slot, pl.ds(q * CHUNK_ROWS, CHUNK_ROWS), :], gsem.at[slot])

    def scatter(b, q):
        slot = lax.rem(b, 2)
        row = pl.multiple_of(dst_ref[b * BLOCK_CHUNKS + q], CHUNK_ROWS)
        return pltpu.make_async_copy(obuf.at[slot, pl.ds(q * CHUNK_ROWS, CHUNK_ROWS), :],
                                     out_hbm.at[pl.ds(row, CHUNK_ROWS), :], ssem.at[slot])

    def zero_rows(start, n_rows):
        start = pl.multiple_of(start, CHUNK_ROWS)
        return pltpu.make_async_copy(zbuf.at[pl.ds(0, n_rows), :], out_hbm.at[pl.ds(start, n_rows), :], zsem)

    def zero_tail(k):
        return zero_rows(k * LOCAL_ROWS + TOK_TILE * TOP_K, FREE_ROWS)

    def zero_last_tile(part):
        return zero_rows((N_TILES - 1) * LOCAL_ROWS + part * FREE_ROWS, FREE_ROWS)

    zbuf[...] = jnp.zeros_like(zbuf)
    lax.fori_loop(0, N_TILES, lambda k, c: (zero_tail(k).start(), c)[1], 0)
    for part in range(TOK_TILE * TOP_K // FREE_ROWS):
        zero_last_tile(part).start()
    for part in range(DUMP_ROWS // FREE_ROWS):
        zero_rows(DUMP_BASE + part * FREE_ROWS, FREE_ROWS).start()
    for ahead in range(GATHER_DEPTH - 1):
        @pl.when(ahead < nused)
        def _():
            for q in range(BLOCK_CHUNKS):
                gather(ahead, q).start()
    lax.fori_loop(0, N_TILES, lambda k, c: (zero_tail(k).wait(), c)[1], 0)
    for part in range(TOK_TILE * TOP_K // FREE_ROWS):
        zero_last_tile(part).wait()
    for part in range(DUMP_ROWS // FREE_ROWS):
        zero_rows(DUMP_BASE + part * FREE_ROWS, FREE_ROWS).wait()

    half = MXU_COLS // 2
    k_io = lax.broadcasted_iota(I32, (MXU_COLS, MXU_COLS), 0)
    j_io = lax.broadcasted_iota(I32, (MXU_COLS, MXU_COLS), 1)
    src_col = jnp.where(j_io < half, 2 * j_io, 2 * (j_io - half) + 1)
    perm = jnp.where(k_io == src_col, 1.0, 0.0).astype(BF16)

    def block(i, fetched):
        e = bexp_ref[i]
        slot = lax.rem(i, 2)
        is_first = first_ref[i] == 1

        @pl.when(is_first)
        def _():
            start_pieces(e, fetched, WEIGHT_PIECES)

            def wait_piece(p, c):
                for cp in fetch_piece(e, p):
                    cp.wait()
                return c
            lax.fori_loop(0, WEIGHT_PIECES, wait_piece, 0)
            for c in range(2 * D_FF // MXU_COLS):
                blk = w1_stage[:, c * MXU_COLS:(c + 1) * MXU_COLS].astype(BF16)
                sep = _dot(blk, perm).astype(BF16)
                w1_b[:, c * half:(c + 1) * half] = sep[:, 0:half]
                w1_b[:, D_FF + c * half:D_FF + (c + 1) * half] = sep[:, half:MXU_COLS]
            w2_b[...] = w2_stage[...].astype(BF16)

        fetched = jnp.where(is_first, 0, fetched)

        for q in range(BLOCK_CHUNKS):
            gather(i, q).wait()

        @pl.when(i + GATHER_DEPTH - 1 < nused)
        def _():
            for q in range(BLOCK_CHUNKS):
                gather(i + GATHER_DEPTH - 1, q).start()

        @pl.when(i >= 2)
        def _():
            for q in range(BLOCK_CHUNKS):
                scatter(i - 2, q).wait()

        def ffn(n_rows):
            h = _dot(xbuf[lax.rem(i, GATHER_DEPTH), 0:n_rows, :], w1_b[...]) + b1_ref[e]
            glu = jnp.minimum(h[:, 0:D_FF], SWIGLU_LIMIT)
            lin = jnp.clip(h[:, D_FF:2 * D_FF], -SWIGLU_LIMIT, SWIGLU_LIMIT)
            a = glu * jax.nn.sigmoid(SWIGLU_ALPHA * glu) * (lin + 1.0)
            obuf[slot, 0:n_rows, :] = (_dot(a.astype(BF16), w2_b[...]) + b2_ref[e]).astype(BF16)

        for parts in range(1, BLOCK_PARTS + 1):
            @pl.when(parts_ref[i] == parts)
            def _():
                ffn(parts * (ROW_BLOCK // BLOCK_PARTS))

        for q in range(BLOCK_CHUNKS):
            scatter(i, q).start()

        more = jnp.where(next_ref[i] >= 0, jnp.minimum(fetched + PIECES_PER_BLOCK, WEIGHT_PIECES), fetched)
        start_pieces(next_ref[i], fetched, more)
        return more

    lax.fori_loop(0, nused, block, jnp.int32(0))

    @pl.when(nused >= 2)
    def _():
        for q in range(BLOCK_CHUNKS):
            scatter(nused - 2, q).wait()
    for q in range(BLOCK_CHUNKS):
        scatter(nused - 1, q).wait()


def _experts(chunk_src, chunk_dst, block_expert, block_first, block_next, block_parts, n_used,
             sorted_rows, w1, b1p, w2, b2):
    whole3 = lambda i, *_: (0, 0, 0)
    grid_spec = pltpu.PrefetchScalarGridSpec(
        num_scalar_prefetch=7,
        grid=(1,),
        in_specs=[
            pl.BlockSpec(memory_space=pl.ANY),
            pl.BlockSpec(memory_space=pl.ANY),
            pl.BlockSpec(b1p.shape, whole3),
            pl.BlockSpec(memory_space=pl.ANY),
            pl.BlockSpec(b2.shape, whole3),
        ],
        out_specs=pl.BlockSpec(memory_space=pl.ANY),
        scratch_shapes=[
            pltpu.VMEM((D_MODEL, 2 * D_FF), F32),
            pltpu.VMEM((D_FF, D_MODEL), F32),
            pltpu.VMEM((D_MODEL, 2 * D_FF), BF16),
            pltpu.VMEM((D_FF, D_MODEL), BF16),
            pltpu.VMEM((GATHER_DEPTH, ROW_BLOCK, D_MODEL), BF16),
            pltpu.VMEM((2, ROW_BLOCK, D_MODEL), BF16),
            pltpu.VMEM((FREE_ROWS, D_MODEL), BF16),
            pltpu.SemaphoreType.DMA((2,)),
            pltpu.SemaphoreType.DMA((GATHER_DEPTH,)),
            pltpu.SemaphoreType.DMA((2,)),
            pltpu.SemaphoreType.DMA(()),
        ],
    )
    return pl.pallas_call(
        _expert_kernel,
        grid_spec=grid_spec,
        out_shape=jax.ShapeDtypeStruct((DUMP_BASE + DUMP_ROWS, D_MODEL), BF16),
        compiler_params=pltpu.CompilerParams(
            dimension_semantics=("arbitrary",), vmem_limit_bytes=VMEM_LIMIT),
        name="experts",
    )(chunk_src, chunk_dst, block_expert, block_first, block_next, block_parts, n_used,
      sorted_rows, w1, b1p, w2, b2)


def _combine_kernel(slot_ref, gate_ref, xp_ref, xs_ref, pp_ref, ps_ref, eo_ref, ln2g_ref, ln2b_ref,
                    wpg_ref, wple_ref, yp_ref, ys_ref):
    i = pl.program_id(0)
    T = TOK_TILE

    slot_rows = [slot_ref[0, j:j + 1, :] for j in range(TOP_K)]
    gate_rows = [gate_ref[0, j:j + 1, :] for j in range(TOP_K)]
    pad = jnp.zeros((LANES - SUBLANES, T), F32)
    slots_t = jnp.concatenate([slot_ref[0].astype(F32), pad], axis=0).T
    slot_cols = [slots_t[:, j:j + 1].astype(I32) for j in range(TOP_K)]

    ff = jnp.zeros((T, D_MODEL), F32)
    for grp in range(GROUPS):
        m = _placement(slot_rows, grp)
        weighted = jnp.where(m[0], gate_rows[0], jnp.where(m[1], gate_rows[1], jnp.where(
            m[2], gate_rows[2], jnp.where(m[3], gate_rows[3], 0.0))))
        g_col = jnp.sum(weighted, axis=1, keepdims=True)
        z = (eo_ref[grp * T:(grp + 1) * T, :].astype(F32) * g_col).astype(BF16)
        r = grp * T + lax.broadcasted_iota(I32, (T, T), 1)
        hit = jnp.logical_or(jnp.logical_or(r == slot_cols[0], r == slot_cols[1]),
                             jnp.logical_or(r == slot_cols[2], r == slot_cols[3]))
        ff = ff + _dot(jnp.where(hit, 1.0, 0.0).astype(BF16), z)

    x1 = _pick_tile(i, xp_ref, xs_ref)
    x2 = _layer_norm(DN_ALPHA * x1 + ff, ln2g_ref[...], ln2b_ref[...])
    p = _pick_tile(i, pp_ref, ps_ref)
    y = x2 + jax.nn.sigmoid(_dot(x2.astype(BF16), wpg_ref[...])) * _dot(p.astype(BF16), wple_ref[...])

    @pl.when(i < N_PROMPT_TILES)
    def _():
        yp_ref[...] = y

    @pl.when(i == N_PROMPT_TILES)
    def _():
        ys_ref[...] = y[0:DEC_BATCH, :]


def _combine(slots, gates, x1p, x1s, pp, ps, expert_out, ln2g, ln2b, w_pg_b, w_ple_b):
    tile_idx = lambda i: (jnp.minimum(i, N_PROMPT_TILES - 1), 0)
    const2 = lambda i: (0, 0)
    return pl.pallas_call(
        _combine_kernel,
        grid=(N_TILES,),
        in_specs=[
            pl.BlockSpec((1, SUBLANES, TOK_TILE), lambda i: (i, 0, 0)),
            pl.BlockSpec((1, SUBLANES, TOK_TILE), lambda i: (i, 0, 0)),
            pl.BlockSpec((TOK_TILE, D_MODEL), tile_idx),
            pl.BlockSpec((TOK_TILE, D_MODEL), const2),
            pl.BlockSpec((TOK_TILE, PLE_DIM), tile_idx),
            pl.BlockSpec((TOK_TILE, PLE_DIM), const2),
            pl.BlockSpec((LOCAL_ROWS, D_MODEL), lambda i: (i, 0)),
            pl.BlockSpec(ln2g.shape, const2),
            pl.BlockSpec(ln2b.shape, const2),
            pl.BlockSpec(w_pg_b.shape, const2),
            pl.BlockSpec(w_ple_b.shape, const2),
        ],
        out_specs=[
            pl.BlockSpec((TOK_TILE, D_MODEL), tile_idx),
            pl.BlockSpec((DEC_BATCH, D_MODEL), const2),
        ],
        out_shape=[
            jax.ShapeDtypeStruct((N_PROMPT, D_MODEL), F32),
            jax.ShapeDtypeStruct((DEC_BATCH, D_MODEL), F32),
        ],
        compiler_params=pltpu.CompilerParams(
            dimension_semantics=("arbitrary",), vmem_limit_bytes=VMEM_LIMIT),
        name="combine",
    )(slots, gates, x1p, x1s, pp, ps, expert_out, ln2g, ln2b, w_pg_b, w_ple_b)


def _block_tables(nch):
    seg_start = (jnp.cumsum(nch, axis=1) - nch) * CHUNK_ROWS
    tot = jnp.sum(nch, axis=0)
    nblk = (tot + BLOCK_CHUNKS - 1) // BLOCK_CHUNKS
    blk_end = jnp.cumsum(nblk)
    blk_start = blk_end - nblk
    n_used = blk_end[-1:].astype(I32)
    blk_ids = jnp.arange(N_BLOCKS, dtype=I32)

    e_ids = jnp.arange(N_EXPERTS, dtype=I32)
    used = nblk > 0
    blk_clamped = jnp.minimum(blk_ids, n_used[0] - 1)
    block_expert = jnp.minimum(jnp.sum(blk_end[:, None] <= blk_clamped[None, :], axis=0), N_EXPERTS - 1).astype(I32)
    starts_here = jnp.logical_and(blk_start[:, None] == blk_ids[None, :], used[:, None])
    block_first = jnp.any(starts_here, axis=0).astype(I32)
    later_used = jnp.logical_and(e_ids[None, :] > e_ids[:, None], used[None, :])
    next_used = jnp.min(jnp.where(later_used, e_ids[None, :], N_EXPERTS), axis=1)
    next_used = jnp.where(next_used < N_EXPERTS, next_used, -1)
    owner = block_expert[None, :] == e_ids[:, None]
    block_next = jnp.sum(jnp.where(owner, next_used[:, None], 0), axis=0).astype(I32)
    real_chunks = jnp.sum(jnp.where(owner, tot[:, None] - (blk_ids[None, :] - blk_start[:, None]) * BLOCK_CHUNKS, 0),
                          axis=0)
    part_chunks = BLOCK_CHUNKS // BLOCK_PARTS
    block_parts = jnp.clip((real_chunks + part_chunks - 1) // part_chunks, 1, BLOCK_PARTS).astype(I32)

    nch_t = nch.T
    seg_first = (blk_start[:, None] * BLOCK_CHUNKS + jnp.cumsum(nch_t, axis=1) - nch_t).reshape(-1)
    seg_count = nch_t.reshape(-1)
    seg_row = (jnp.arange(N_TILES, dtype=I32)[None, :] * LOCAL_ROWS + seg_start.T).reshape(-1)
    ent = jnp.arange(N_BLOCKS * BLOCK_CHUNKS, dtype=I32)
    d = ent[None, :] - seg_first[:, None]
    inside = jnp.logical_and(d >= 0, d < seg_count[:, None])
    row_plus_1 = jnp.sum(jnp.where(inside, seg_row[:, None] + d * CHUNK_ROWS + 1, 0), axis=0)
    real = row_plus_1 > 0
    row = row_plus_1 - 1
    dump = DUMP_BASE + (((ent // BLOCK_CHUNKS) % 2) * BLOCK_CHUNKS + ent % BLOCK_CHUNKS) * CHUNK_ROWS
    chunk_src = jnp.where(real, row, ZERO_CHUNK_ROW).astype(I32)
    chunk_dst = jnp.where(real, row, dump).astype(I32)
    return chunk_src, chunk_dst, block_expert, block_first, block_next, block_parts, n_used


def kernel(x_prompt, x_sample, state_pool, state_mlstm_C, state_mlstm_n, state_mlstm_m, p_prompt, p_sample, w_in, b_i, b_f, w_pool, pool_scale, mh_g, w_out, ln1_g, ln1_b, w_router, b_router, w_mlp1, b_mlp1, w_mlp2, b_mlp2, ln2_g, ln2_b, w_ple, w_ple_gate):
    n_main = POOL_WIDTH + 4 * MLSTM_WIDTH
    w_in_b = w_in[0, :, 0:n_main].astype(BF16)
    w_g_b = jnp.pad(w_in[0, :, n_main:], ((0, 0), (0, LANES - 2 * HEADS))).astype(BF16)
    gbias = jnp.pad(jnp.concatenate([b_i[0], b_f[0]]), (0, LANES - 2 * HEADS)).reshape(1, LANES)
    w_pool_b = w_pool[0].astype(BF16)
    pscale = pool_scale[0].reshape(1, POOL_WIDTH)
    mhg = mh_g[0].reshape(1, MLSTM_WIDTH)
    w_out_b = w_out[0].astype(BF16)
    ln1g = ln1_g[0].reshape(1, D_MODEL)
    ln1b = ln1_b[0].reshape(1, D_MODEL)
    ln2g = ln2_g[0].reshape(1, D_MODEL)
    ln2b = ln2_b[0].reshape(1, D_MODEL)
    w_router_t = w_router[0].T
    b_router_col = jnp.broadcast_to(b_router[0].reshape(N_EXPERTS, 1), (N_EXPERTS, LANES))
    b1 = b_mlp1[0]
    b1p = jnp.concatenate([b1[:, 0::2], b1[:, 1::2]], axis=-1).reshape(N_EXPERTS, 1, 2 * D_FF)
    b2 = b_mlp2[0].reshape(N_EXPERTS, 1, D_MODEL)
    w_pg_b = w_ple_gate[0].astype(BF16)
    w_ple_b = w_ple[0].astype(BF16)

    x1p, pool_p, c_p, n_p, m_p = _prompt_mixer(
        x_prompt, w_in_b, w_g_b, gbias, w_pool_b, pscale, mhg, w_out_b, ln1g, ln1b)
    x1s, pool_s, c_s, n_s, m_s = _sample_mixer(
        x_sample.reshape(DEC_BATCH, D_MODEL),
        state_pool[0].reshape(DEC_BATCH, POOL_HIST * POOL_WIDTH),
        state_mlstm_C[0], state_mlstm_n[0].reshape(DEC_BATCH, MLSTM_WIDTH), state_mlstm_m[0],
        w_in_b, w_g_b, gbias, w_pool_b, pscale, mhg, w_out_b, ln1g, ln1b)

    slots, gates, nch, sorted_rows = _route(x1p, x1s, w_router_t, b_router_col)
    tables = _block_tables(nch[:, :, 0].astype(I32))
    expert_out = _experts(*tables, sorted_rows, w_mlp1[0], b1p, w_mlp2[0], b2)

    pp = p_prompt[0].reshape(N_PROMPT, PLE_DIM)
    ps = jnp.pad(p_sample[0].reshape(DEC_BATCH, PLE_DIM), ((0, TOK_TILE - DEC_BATCH), (0, 0)))
    yp, ys = _combine(slots, gates, x1p, x1s, pp, ps, expert_out, ln2g, ln2b, w_pg_b, w_ple_b)

    return (
        yp.reshape(BATCH, SEQ, D_MODEL),
        ys.reshape(DEC_BATCH, 1, D_MODEL),
        pool_p.reshape(1, BATCH, POOL_HIST, POOL_WIDTH),
        c_p.reshape(1, BATCH, HEADS, HEAD_DIM, HEAD_DIM),
        n_p.reshape(1, BATCH, HEADS, HEAD_DIM),
        m_p[:, 0:HEADS, 0].reshape(1, BATCH, HEADS),
        pool_s.reshape(1, DEC_BATCH, POOL_HIST, POOL_WIDTH),
        c_s.reshape(1, DEC_BATCH, HEADS, HEAD_DIM, HEAD_DIM),
        n_s.reshape(1, DEC_BATCH, HEADS, HEAD_DIM),
        m_s[:, HEADS:2 * HEADS].reshape(1, DEC_BATCH, HEADS),
    )
```
